```python
import math
import jax
import jax.numpy as jnp
from jax import lax
import numpy as np

D_MODEL = 1024
BATCH = 16
SEQ = 4096
DEPTH = 2

N_MIXERS = 2
N_CONV_LAYERS = (DEPTH + 1) // 2
N_GDN_LAYERS = DEPTH // 2
D_FF = 4 * D_MODEL
CONV_WIDTH = 31
GDN_HEADS = 8
GDN_HEAD_K = D_MODEL // GDN_HEADS
GDN_HEAD_V = D_MODEL // GDN_HEADS
GDN_KEY_DIM = GDN_HEADS * GDN_HEAD_K
GDN_VAL_DIM = GDN_HEADS * GDN_HEAD_V
GDN_QKV_DIM = 2 * GDN_KEY_DIM + GDN_VAL_DIM
GDN_IN_DIM = GDN_QKV_DIM + GDN_VAL_DIM + 2 * GDN_HEADS
SHORT_CONV_WIDTH = 4
CHUNK = 64
NORM_EPS = 1e-6

kernel_name = 'hybrid_conformer_conv_gated_deltanet_trunk'


def rms_norm(x, g, eps=NORM_EPS):
    xf = x.astype(jnp.float32)
    y = xf * lax.rsqrt(jnp.mean(xf * xf, axis=-1, keepdims=True) + eps)
    return (y * g.astype(jnp.float32)).astype(x.dtype)


def layer_norm(x, g, b, eps=NORM_EPS):
    xf = x.astype(jnp.float32)
    mu = jnp.mean(xf, axis=-1, keepdims=True)
    xc = xf - mu
    y = xc * lax.rsqrt(jnp.mean(xc * xc, axis=-1, keepdims=True) + eps)
    return (y * g.astype(jnp.float32) + b.astype(jnp.float32)).astype(x.dtype)


def l2norm(x, eps=1e-6):
    xf = x.astype(jnp.float32)
    return xf * lax.rsqrt(jnp.sum(xf * xf, axis=-1, keepdims=True) + eps)


def causal_depthwise_conv(x, w):
    K, C = w.shape
    return lax.conv_general_dilated(
        x, w[:, None, :].astype(x.dtype), window_strides=(1,),
        padding=[(K - 1, 0)], dimension_numbers=('NWC', 'WIO', 'NWC'),
        feature_group_count=C)


def conformer_conv(h, w_pw1, b_pw1, w_dw, b_dw, ln_g, ln_b, w_pw2, b_pw2):
    u = h @ w_pw1 + b_pw1
    u = jax.nn.glu(u, axis=-1)
    u = causal_depthwise_conv(u, w_dw) + b_dw
    u = jax.nn.silu(layer_norm(u, ln_g, ln_b))
    return u @ w_pw2 + b_pw2


def chunk_gated_delta_rule(q, k, v, g, beta):
    B, S, H, dk = q.shape
    dv = v.shape[-1]
    N = S // CHUNK
    q, k, v = [jnp.swapaxes(t, 1, 2).reshape(B, H, N, CHUNK, -1) for t in (q, k, v)]
    g, beta = [jnp.swapaxes(t, 1, 2).reshape(B, H, N, CHUNK) for t in (g, beta)]
    g = jnp.cumsum(g, axis=-1)
    idx = jnp.arange(CHUNK)
    causal = idx[:, None] >= idx[None, :]
    strict = idx[:, None] > idx[None, :]
    decay = jnp.exp(jnp.where(causal, g[..., :, None] - g[..., None, :], -jnp.inf))
    k_beta = k * beta[..., None]
    kk = jnp.einsum('bhnid,bhnjd->bhnij', k_beta, k) * decay
    m = jnp.where(strict, kk, 0.0) + jnp.eye(CHUNK, dtype=jnp.float32)
    rhs = jnp.concatenate([v * beta[..., None], k_beta * jnp.exp(g)[..., None]], axis=-1)
    sol = lax.linalg.triangular_solve(m, rhs, left_side=True, lower=True, unit_diagonal=True)
    u = sol[..., :dv]
    w = sol[..., dv:]
    qk = jnp.einsum('bhnid,bhnjd->bhnij', q, k) * decay

    def step(state, xs):
        q_c, k_c, u_c, w_c, qk_c, g_c = xs
        v_new = u_c - jnp.einsum('bhcd,bhde->bhce', w_c, state)
        o = (jnp.einsum('bhcd,bhde->bhce', q_c * jnp.exp(g_c)[..., None], state)
             + jnp.einsum('bhij,bhje->bhie', qk_c, v_new))
        g_last = g_c[..., -1]
        state = (state * jnp.exp(g_last)[..., None, None]
                 + jnp.einsum('bhcd,bhce->bhde',
                              k_c * jnp.exp(g_last[..., None] - g_c)[..., None], v_new))
        return state, o

    xs = tuple(jnp.moveaxis(t, 2, 0) for t in (q, k, u, w, qk, g))
    state0 = jnp.zeros((B, H, dk, dv), jnp.float32)
    _, o = lax.scan(step, state0, xs)
    o = jnp.moveaxis(o, 0, 2).reshape(B, H, S, dv)
    return jnp.swapaxes(o, 1, 2)


def gated_deltanet(h, w_in, conv_w, a_log, dt_bias, norm_g, w_out):
    B, S, _ = h.shape
    proj = h @ w_in
    qkv = proj[..., :GDN_QKV_DIM]
    z = proj[..., GDN_QKV_DIM:GDN_QKV_DIM + GDN_VAL_DIM]
    a_raw = proj[..., GDN_QKV_DIM + GDN_VAL_DIM:GDN_QKV_DIM + GDN_VAL_DIM + GDN_HEADS]
    b_raw = proj[..., GDN_QKV_DIM + GDN_VAL_DIM + GDN_HEADS:]
    qkv = jax.nn.silu(causal_depthwise_conv(qkv, conv_w))
    q = qkv[..., :GDN_KEY_DIM].reshape(B, S, GDN_HEADS, GDN_HEAD_K)
    k = qkv[..., GDN_KEY_DIM:2 * GDN_KEY_DIM].reshape(B, S, GDN_HEADS, GDN_HEAD_K)
    v = qkv[..., 2 * GDN_KEY_DIM:].reshape(B, S, GDN_HEADS, GDN_HEAD_V).astype(jnp.float32)
    q = l2norm(q) * (GDN_HEAD_K ** -0.5)
    k = l2norm(k)
    beta = jax.nn.sigmoid(b_raw.astype(jnp.float32))
    g = -jnp.exp(a_log.astype(jnp.float32)) * jax.nn.softplus(
        a_raw.astype(jnp.float32) + dt_bias.astype(jnp.float32))
    o = chunk_gated_delta_rule(q, k, v, g, beta)
    zf = z.reshape(B, S, GDN_HEADS, GDN_HEAD_V).astype(jnp.float32)
    o = rms_norm(o, norm_g) * jax.nn.silu(zf)
    return o.reshape(B, S, GDN_VAL_DIM).astype(h.dtype) @ w_out


def sqrelu_mlp(h, w1, w2):
    return jnp.square(jax.nn.relu(h @ w1)) @ w2


def _normal(key, shape, fan_in):
    return jax.random.normal(key, shape, jnp.float32) * (fan_in ** -0.5)


def _fwd_setup_inputs(seed: int = 0) -> dict:
    key = jax.random.key(seed)
    ks = jax.random.split(key, 24)
    D = D_MODEL
    Nc, Ng = N_CONV_LAYERS, N_GDN_LAYERS
    x = jax.random.normal(ks[0], (BATCH, SEQ, D), jnp.float32)
    norm_mix_g = 1.0 + 0.02 * jax.random.normal(ks[1], (DEPTH, D), jnp.float32)
    norm_ffn_g = 1.0 + 0.02 * jax.random.normal(ks[2], (DEPTH, D), jnp.float32)
    final_norm_g = 1.0 + 0.02 * jax.random.normal(ks[3], (D,), jnp.float32)
    cv_w_pw1 = _normal(ks[4], (Nc, D, 2 * D), D)
    cv_b_pw1 = 0.01 * jax.random.normal(ks[5], (Nc, 2 * D), jnp.float32)
    cv_w_dw = _normal(ks[6], (Nc, CONV_WIDTH, D), CONV_WIDTH)
    cv_b_dw = 0.01 * jax.random.normal(ks[7], (Nc, D), jnp.float32)
    cv_ln_g = 1.0 + 0.02 * jax.random.normal(ks[8], (Nc, D), jnp.float32)
    cv_ln_b = 0.01 * jax.random.normal(ks[9], (Nc, D), jnp.float32)
    cv_w_pw2 = _normal(ks[10], (Nc, D, D), D)
    cv_b_pw2 = 0.01 * jax.random.normal(ks[11], (Nc, D), jnp.float32)
    gdn_w_in = _normal(ks[12], (Ng, D, GDN_IN_DIM), D)
    gdn_conv_w = _normal(ks[13], (Ng, SHORT_CONV_WIDTH, GDN_QKV_DIM), SHORT_CONV_WIDTH)
    gdn_a_log = jnp.log(jax.random.uniform(ks[14], (Ng, GDN_HEADS), jnp.float32, 1.0, 16.0))
    dt = jnp.exp(jax.random.uniform(ks[15], (Ng, GDN_HEADS), jnp.float32,
                                    math.log(1e-3), math.log(1e-1)))
    gdn_dt_bias = dt + jnp.log(-jnp.expm1(-dt))
    gdn_norm_g = 1.0 + 0.02 * jax.random.normal(ks[16], (Ng, GDN_HEAD_V), jnp.float32)
    gdn_w_out = _normal(ks[17], (Ng, GDN_VAL_DIM, D), GDN_VAL_DIM)
    mlp_w1 = _normal(ks[18], (DEPTH, D, D_FF), D)
    mlp_w2 = _normal(ks[19], (DEPTH, D_FF, D), D_FF)
    return {'x': x, 'norm_mix_g': norm_mix_g, 'norm_ffn_g': norm_ffn_g,
            'final_norm_g': final_norm_g,
            'cv_w_pw1': cv_w_pw1, 'cv_b_pw1': cv_b_pw1, 'cv_w_dw': cv_w_dw,
            'cv_b_dw': cv_b_dw, 'cv_ln_g': cv_ln_g, 'cv_ln_b': cv_ln_b,
            'cv_w_pw2': cv_w_pw2, 'cv_b_pw2': cv_b_pw2,
            'gdn_w_in': gdn_w_in, 'gdn_conv_w': gdn_conv_w, 'gdn_a_log': gdn_a_log,
            'gdn_dt_bias': gdn_dt_bias, 'gdn_norm_g': gdn_norm_g, 'gdn_w_out': gdn_w_out,
            'mlp_w1': mlp_w1, 'mlp_w2': mlp_w2}


def _fwd_reference(x, norm_mix_g, norm_ffn_g, final_norm_g,
              cv_w_pw1, cv_b_pw1, cv_w_dw, cv_b_dw, cv_ln_g, cv_ln_b, cv_w_pw2, cv_b_pw2,
              gdn_w_in, gdn_conv_w, gdn_a_log, gdn_dt_bias, gdn_norm_g, gdn_w_out,
              mlp_w1, mlp_w2):
    h = x
    for i in range(DEPTH):
        hn = rms_norm(h, norm_mix_g[i])
        j = i // N_MIXERS
        if i % N_MIXERS == 0:
            mix = conformer_conv(hn, cv_w_pw1[j], cv_b_pw1[j], cv_w_dw[j], cv_b_dw[j],
                                 cv_ln_g[j], cv_ln_b[j], cv_w_pw2[j], cv_b_pw2[j])
        else:
            mix = gated_deltanet(hn, gdn_w_in[j], gdn_conv_w[j], gdn_a_log[j],
                                 gdn_dt_bias[j], gdn_norm_g[j], gdn_w_out[j])
        h = h + mix
        h = h + sqrelu_mlp(rms_norm(h, norm_ffn_g[i]), mlp_w1[i], mlp_w2[i])
    return rms_norm(h, final_norm_g)


import jax as _jax
import jax.numpy as _jnp

TWIN_FORMAT = 'train_step'
FWD_PARAMS = ['x', 'norm_mix_g', 'norm_ffn_g', 'final_norm_g', 'cv_w_pw1', 'cv_b_pw1', 'cv_w_dw', 'cv_b_dw', 'cv_ln_g', 'cv_ln_b', 'cv_w_pw2', 'cv_b_pw2', 'gdn_w_in', 'gdn_conv_w', 'gdn_a_log', 'gdn_dt_bias', 'gdn_norm_g', 'gdn_w_out', 'mlp_w1', 'mlp_w2']
TWIN_WEIGHTS = ['norm_mix_g', 'norm_ffn_g', 'final_norm_g', 'cv_w_pw1', 'cv_b_pw1', 'cv_w_dw', 'cv_b_dw', 'cv_ln_g', 'cv_ln_b', 'cv_w_pw2', 'cv_b_pw2', 'gdn_w_in', 'gdn_conv_w', 'gdn_a_log', 'gdn_dt_bias', 'gdn_norm_g', 'gdn_w_out', 'mlp_w1', 'mlp_w2']
TWIN_DIFF_INPUT = 'x'
TWIN_INPUTS = ['x', 'norm_mix_g', 'norm_ffn_g', 'final_norm_g', 'cv_w_pw1', 'cv_b_pw1', 'cv_w_dw', 'cv_b_dw', 'cv_ln_g', 'cv_ln_b', 'cv_w_pw2', 'cv_b_pw2', 'gdn_w_in', 'gdn_conv_w', 'gdn_a_log', 'gdn_dt_bias', 'gdn_norm_g', 'gdn_w_out', 'mlp_w1', 'mlp_w2', 'loss_target', 'm_norm_mix_g', 'm_norm_ffn_g', 'm_final_norm_g', 'm_cv_w_pw1', 'm_cv_b_pw1', 'm_cv_w_dw', 'm_cv_b_dw', 'm_cv_ln_g', 'm_cv_ln_b', 'm_cv_w_pw2', 'm_cv_b_pw2', 'm_gdn_w_in', 'm_gdn_conv_w', 'm_gdn_a_log', 'm_gdn_dt_bias', 'm_gdn_norm_g', 'm_gdn_w_out', 'm_mlp_w1', 'm_mlp_w2', 'v_norm_mix_g', 'v_norm_ffn_g', 'v_final_norm_g', 'v_cv_w_pw1', 'v_cv_b_pw1', 'v_cv_w_dw', 'v_cv_b_dw', 'v_cv_ln_g', 'v_cv_ln_b', 'v_cv_w_pw2', 'v_cv_b_pw2', 'v_gdn_w_in', 'v_gdn_conv_w', 'v_gdn_a_log', 'v_gdn_dt_bias', 'v_gdn_norm_g', 'v_gdn_w_out', 'v_mlp_w1', 'v_mlp_w2']
TWIN_OUTPUTS = ['loss', 'grad_x', 'grad_norm_mix_g', 'grad_norm_ffn_g', 'grad_final_norm_g', 'grad_cv_w_pw1', 'grad_cv_b_pw1', 'grad_cv_w_dw', 'grad_cv_b_dw', 'grad_cv_ln_g', 'grad_cv_ln_b', 'grad_cv_w_pw2', 'grad_cv_b_pw2', 'grad_gdn_w_in', 'grad_gdn_conv_w', 'grad_gdn_a_log', 'grad_gdn_dt_bias', 'grad_gdn_norm_g', 'grad_gdn_w_out', 'grad_mlp_w1', 'grad_mlp_w2', 'delta_norm_mix_g', 'delta_norm_ffn_g', 'delta_final_norm_g', 'delta_cv_w_pw1', 'delta_cv_b_pw1', 'delta_cv_w_dw', 'delta_cv_b_dw', 'delta_cv_ln_g', 'delta_cv_ln_b', 'delta_cv_w_pw2', 'delta_cv_b_pw2', 'delta_gdn_w_in', 'delta_gdn_conv_w', 'delta_gdn_a_log', 'delta_gdn_dt_bias', 'delta_gdn_norm_g', 'delta_gdn_w_out', 'delta_mlp_w1', 'delta_mlp_w2', 'new_m_norm_mix_g', 'new_m_norm_ffn_g', 'new_m_final_norm_g', 'new_m_cv_w_pw1', 'new_m_cv_b_pw1', 'new_m_cv_w_dw', 'new_m_cv_b_dw', 'new_m_cv_ln_g', 'new_m_cv_ln_b', 'new_m_cv_w_pw2', 'new_m_cv_b_pw2', 'new_m_gdn_w_in', 'new_m_gdn_conv_w', 'new_m_gdn_a_log', 'new_m_gdn_dt_bias', 'new_m_gdn_norm_g', 'new_m_gdn_w_out', 'new_m_mlp_w1', 'new_m_mlp_w2', 'new_v_norm_mix_g', 'new_v_norm_ffn_g', 'new_v_final_norm_g', 'new_v_cv_w_pw1', 'new_v_cv_b_pw1', 'new_v_cv_w_dw', 'new_v_cv_b_dw', 'new_v_cv_ln_g', 'new_v_cv_ln_b', 'new_v_cv_w_pw2', 'new_v_cv_b_pw2', 'new_v_gdn_w_in', 'new_v_gdn_conv_w', 'new_v_gdn_a_log', 'new_v_gdn_dt_bias', 'new_v_gdn_norm_g', 'new_v_gdn_w_out', 'new_v_mlp_w1', 'new_v_mlp_w2']
TWIN_LEAF_KINDS = {'loss': 'loss', 'grad_x': 'grad_x', 'grad_norm_mix_g': 'grad_w', 'grad_norm_ffn_g': 'grad_w', 'grad_final_norm_g': 'grad_w', 'grad_cv_w_pw1': 'grad_w', 'grad_cv_b_pw1': 'grad_w', 'grad_cv_w_dw': 'grad_w', 'grad_cv_b_dw': 'grad_w', 'grad_cv_ln_g': 'grad_w', 'grad_cv_ln_b': 'grad_w', 'grad_cv_w_pw2': 'grad_w', 'grad_cv_b_pw2': 'grad_w', 'grad_gdn_w_in': 'grad_w', 'grad_gdn_conv_w': 'grad_w', 'grad_gdn_a_log': 'grad_w', 'grad_gdn_dt_bias': 'grad_w', 'grad_gdn_norm_g': 'grad_w', 'grad_gdn_w_out': 'grad_w', 'grad_mlp_w1': 'grad_w', 'grad_mlp_w2': 'grad_w', 'delta_norm_mix_g': 'delta_w', 'delta_norm_ffn_g': 'delta_w', 'delta_final_norm_g': 'delta_w', 'delta_cv_w_pw1': 'delta_w', 'delta_cv_b_pw1': 'delta_w', 'delta_cv_w_dw': 'delta_w', 'delta_cv_b_dw': 'delta_w', 'delta_cv_ln_g': 'delta_w', 'delta_cv_ln_b': 'delta_w', 'delta_cv_w_pw2': 'delta_w', 'delta_cv_b_pw2': 'delta_w', 'delta_gdn_w_in': 'delta_w', 'delta_gdn_conv_w': 'delta_w', 'delta_gdn_a_log': 'delta_w', 'delta_gdn_dt_bias': 'delta_w', 'delta_gdn_norm_g': 'delta_w', 'delta_gdn_w_out': 'delta_w', 'delta_mlp_w1': 'delta_w', 'delta_mlp_w2': 'delta_w', 'new_m_norm_mix_g': 'new_m', 'new_m_norm_ffn_g': 'new_m', 'new_m_final_norm_g': 'new_m', 'new_m_cv_w_pw1': 'new_m', 'new_m_cv_b_pw1': 'new_m', 'new_m_cv_w_dw': 'new_m', 'new_m_cv_b_dw': 'new_m', 'new_m_cv_ln_g': 'new_m', 'new_m_cv_ln_b': 'new_m', 'new_m_cv_w_pw2': 'new_m', 'new_m_cv_b_pw2': 'new_m', 'new_m_gdn_w_in': 'new_m', 'new_m_gdn_conv_w': 'new_m', 'new_m_gdn_a_log': 'new_m', 'new_m_gdn_dt_bias': 'new_m', 'new_m_gdn_norm_g': 'new_m', 'new_m_gdn_w_out': 'new_m', 'new_m_mlp_w1': 'new_m', 'new_m_mlp_w2': 'new_m', 'new_v_norm_mix_g': 'new_v', 'new_v_norm_ffn_g': 'new_v', 'new_v_final_norm_g': 'new_v', 'new_v_cv_w_pw1': 'new_v', 'new_v_cv_b_pw1': 'new_v', 'new_v_cv_w_dw': 'new_v', 'new_v_cv_b_dw': 'new_v', 'new_v_cv_ln_g': 'new_v', 'new_v_cv_ln_b': 'new_v', 'new_v_cv_w_pw2': 'new_v', 'new_v_cv_b_pw2': 'new_v', 'new_v_gdn_w_in': 'new_v', 'new_v_gdn_conv_w': 'new_v', 'new_v_gdn_a_log': 'new_v', 'new_v_gdn_dt_bias': 'new_v', 'new_v_gdn_norm_g': 'new_v', 'new_v_gdn_w_out': 'new_v', 'new_v_mlp_w1': 'new_v', 'new_v_mlp_w2': 'new_v'}


def _forward(args):
    return _fwd_reference(*[args[k] for k in FWD_PARAMS])


def _output_shape():
    out = _jax.eval_shape(lambda: _forward(_fwd_setup_inputs(0)))
    return out.shape, out.dtype

N_MICROBATCH = 1
ADAM_LR = 0.001
ADAM_B1 = 0.9
ADAM_B2 = 0.999
ADAM_EPS = 1e-08
ADAM_WD = 0.01
ADAM_STEP = 10
PER_EXAMPLE_BATCH_AXIS = {'x': 0, 'loss_target': 0}
SHARED_INPUTS = []
_WEIGHT_DTYPES = {'norm_mix_g': _jnp.float32, 'norm_ffn_g': _jnp.float32, 'final_norm_g': _jnp.float32, 'cv_w_pw1': _jnp.float32, 'cv_b_pw1': _jnp.float32, 'cv_w_dw': _jnp.float32, 'cv_b_dw': _jnp.float32, 'cv_ln_g': _jnp.float32, 'cv_ln_b': _jnp.float32, 'cv_w_pw2': _jnp.float32, 'cv_b_pw2': _jnp.float32, 'gdn_w_in': _jnp.float32, 'gdn_conv_w': _jnp.float32, 'gdn_a_log': _jnp.float32, 'gdn_dt_bias': _jnp.float32, 'gdn_norm_g': _jnp.float32, 'gdn_w_out': _jnp.float32, 'mlp_w1': _jnp.float32, 'mlp_w2': _jnp.float32}
MOMENT_SCALE = {'norm_mix_g': 1.605556e-01, 'norm_ffn_g': 2.089497e-01, 'final_norm_g': 6.493633e+01, 'cv_w_pw1': 1.260743e-01, 'cv_b_pw1': 1.708718e-01, 'cv_w_dw': 1.689018e-01, 'cv_b_dw': 3.771399e-01, 'cv_ln_g': 2.349739e-01, 'cv_ln_b': 2.160930e-01, 'cv_w_pw2': 1.689210e-01, 'cv_b_pw2': 3.927759e-01, 'gdn_w_in': 7.304870e-02, 'gdn_conv_w': 6.471791e-02, 'gdn_a_log': 2.945672e-01, 'gdn_dt_bias': 2.820885e-01, 'gdn_norm_g': 2.710729e-01, 'gdn_w_out': 9.261342e-02, 'mlp_w1': 1.047138e-01, 'mlp_w2': 2.027757e-01}


def _to_microbatches(a, axis):
    t = _jnp.moveaxis(a, axis, 0)
    t = t.reshape((N_MICROBATCH, t.shape[0] // N_MICROBATCH) + t.shape[1:])
    return _jnp.moveaxis(t, 1, axis + 1)


def setup_inputs(seed: int = 0) -> dict:
    inp = _fwd_setup_inputs(seed)
    key = _jax.random.fold_in(_jax.random.key(seed), 7919)
    shape, _ = _output_shape()
    out = dict(inp)
    out["loss_target"] = _jax.random.normal(_jax.random.fold_in(key, 0), shape, _jnp.float32)
    for i, name in enumerate(TWIN_WEIGHTS):
        w = inp[name].astype(_jnp.float32)
        if MOMENT_SCALE is None:
            s = _jnp.sqrt(_jnp.mean(_jnp.square(w)) + 1e-30)
        else:
            s = MOMENT_SCALE[name]
        km, kv = _jax.random.split(_jax.random.fold_in(key, i + 1))
        out[name] = w
        out["m_" + name] = s * _jax.random.normal(km, w.shape, _jnp.float32)
        out["v_" + name] = (s * s) * _jax.random.uniform(kv, w.shape, _jnp.float32, 0.5, 1.5)
    if N_MICROBATCH > 1:
        for name, axis in PER_EXAMPLE_BATCH_AXIS.items():
            out[name] = _to_microbatches(out[name], axis)
    return {'x': out['x'], 'norm_mix_g': out['norm_mix_g'], 'norm_ffn_g': out['norm_ffn_g'], 'final_norm_g': out['final_norm_g'], 'cv_w_pw1': out['cv_w_pw1'], 'cv_b_pw1': out['cv_b_pw1'], 'cv_w_dw': out['cv_w_dw'], 'cv_b_dw': out['cv_b_dw'], 'cv_ln_g': out['cv_ln_g'], 'cv_ln_b': out['cv_ln_b'], 'cv_w_pw2': out['cv_w_pw2'], 'cv_b_pw2': out['cv_b_pw2'], 'gdn_w_in': out['gdn_w_in'], 'gdn_conv_w': out['gdn_conv_w'], 'gdn_a_log': out['gdn_a_log'], 'gdn_dt_bias': out['gdn_dt_bias'], 'gdn_norm_g': out['gdn_norm_g'], 'gdn_w_out': out['gdn_w_out'], 'mlp_w1': out['mlp_w1'], 'mlp_w2': out['mlp_w2'], 'loss_target': out['loss_target'], 'm_norm_mix_g': out['m_norm_mix_g'], 'm_norm_ffn_g': out['m_norm_ffn_g'], 'm_final_norm_g': out['m_final_norm_g'], 'm_cv_w_pw1': out['m_cv_w_pw1'], 'm_cv_b_pw1': out['m_cv_b_pw1'], 'm_cv_w_dw': out['m_cv_w_dw'], 'm_cv_b_dw': out['m_cv_b_dw'], 'm_cv_ln_g': out['m_cv_ln_g'], 'm_cv_ln_b': out['m_cv_ln_b'], 'm_cv_w_pw2': out['m_cv_w_pw2'], 'm_cv_b_pw2': out['m_cv_b_pw2'], 'm_gdn_w_in': out['m_gdn_w_in'], 'm_gdn_conv_w': out['m_gdn_conv_w'], 'm_gdn_a_log': out['m_gdn_a_log'], 'm_gdn_dt_bias': out['m_gdn_dt_bias'], 'm_gdn_norm_g': out['m_gdn_norm_g'], 'm_gdn_w_out': out['m_gdn_w_out'], 'm_mlp_w1': out['m_mlp_w1'], 'm_mlp_w2': out['m_mlp_w2'], 'v_norm_mix_g': out['v_norm_mix_g'], 'v_norm_ffn_g': out['v_norm_ffn_g'], 'v_final_norm_g': out['v_final_norm_g'], 'v_cv_w_pw1': out['v_cv_w_pw1'], 'v_cv_b_pw1': out['v_cv_b_pw1'], 'v_cv_w_dw': out['v_cv_w_dw'], 'v_cv_b_dw': out['v_cv_b_dw'], 'v_cv_ln_g': out['v_cv_ln_g'], 'v_cv_ln_b': out['v_cv_ln_b'], 'v_cv_w_pw2': out['v_cv_w_pw2'], 'v_cv_b_pw2': out['v_cv_b_pw2'], 'v_gdn_w_in': out['v_gdn_w_in'], 'v_gdn_conv_w': out['v_gdn_conv_w'], 'v_gdn_a_log': out['v_gdn_a_log'], 'v_gdn_dt_bias': out['v_gdn_dt_bias'], 'v_gdn_norm_g': out['v_gdn_norm_g'], 'v_gdn_w_out': out['v_gdn_w_out'], 'v_mlp_w1': out['v_mlp_w1'], 'v_mlp_w2': out['v_mlp_w2']}


def _loss(weights, diff, rest, loss_target):
    with _jax.named_scope("forward"):
        args = {**rest, TWIN_DIFF_INPUT: diff, **{k: w.astype(_WEIGHT_DTYPES[k]) for k, w in weights.items()}}
        y = _forward(args)
    with _jax.named_scope("loss_head"):
        err = _jnp.square(y.astype(_jnp.float32) - loss_target)
        return 0.5 * _jnp.sum(_jnp.mean(err, axis=-1)) if err.ndim else 0.5 * err


def _adamw(w, g, m, v):
    m = ADAM_B1 * m + (1.0 - ADAM_B1) * g
    v = ADAM_B2 * v + (1.0 - ADAM_B2) * _jnp.square(g)
    m_hat = m / (1.0 - ADAM_B1 ** ADAM_STEP)
    v_hat = v / (1.0 - ADAM_B2 ** ADAM_STEP)
    delta = -ADAM_LR * (m_hat / (_jnp.sqrt(v_hat) + ADAM_EPS) + ADAM_WD * w)
    return delta, m, v


def reference(x, norm_mix_g, norm_ffn_g, final_norm_g, cv_w_pw1, cv_b_pw1, cv_w_dw, cv_b_dw, cv_ln_g, cv_ln_b, cv_w_pw2, cv_b_pw2, gdn_w_in, gdn_conv_w, gdn_a_log, gdn_dt_bias, gdn_norm_g, gdn_w_out, mlp_w1, mlp_w2, loss_target, m_norm_mix_g, m_norm_ffn_g, m_final_norm_g, m_cv_w_pw1, m_cv_b_pw1, m_cv_w_dw, m_cv_b_dw, m_cv_ln_g, m_cv_ln_b, m_cv_w_pw2, m_cv_b_pw2, m_gdn_w_in, m_gdn_conv_w, m_gdn_a_log, m_gdn_dt_bias, m_gdn_norm_g, m_gdn_w_out, m_mlp_w1, m_mlp_w2, v_norm_mix_g, v_norm_ffn_g, v_final_norm_g, v_cv_w_pw1, v_cv_b_pw1, v_cv_w_dw, v_cv_b_dw, v_cv_ln_g, v_cv_ln_b, v_cv_w_pw2, v_cv_b_pw2, v_gdn_w_in, v_gdn_conv_w, v_gdn_a_log, v_gdn_dt_bias, v_gdn_norm_g, v_gdn_w_out, v_mlp_w1, v_mlp_w2):
    given = dict(x=x, norm_mix_g=norm_mix_g, norm_ffn_g=norm_ffn_g, final_norm_g=final_norm_g, cv_w_pw1=cv_w_pw1, cv_b_pw1=cv_b_pw1, cv_w_dw=cv_w_dw, cv_b_dw=cv_b_dw, cv_ln_g=cv_ln_g, cv_ln_b=cv_ln_b, cv_w_pw2=cv_w_pw2, cv_b_pw2=cv_b_pw2, gdn_w_in=gdn_w_in, gdn_conv_w=gdn_conv_w, gdn_a_log=gdn_a_log, gdn_dt_bias=gdn_dt_bias, gdn_norm_g=gdn_norm_g, gdn_w_out=gdn_w_out, mlp_w1=mlp_w1, mlp_w2=mlp_w2, loss_target=loss_target, m_norm_mix_g=m_norm_mix_g, m_norm_ffn_g=m_norm_ffn_g, m_final_norm_g=m_final_norm_g, m_cv_w_pw1=m_cv_w_pw1, m_cv_b_pw1=m_cv_b_pw1, m_cv_w_dw=m_cv_w_dw, m_cv_b_dw=m_cv_b_dw, m_cv_ln_g=m_cv_ln_g, m_cv_ln_b=m_cv_ln_b, m_cv_w_pw2=m_cv_w_pw2, m_cv_b_pw2=m_cv_b_pw2, m_gdn_w_in=m_gdn_w_in, m_gdn_conv_w=m_gdn_conv_w, m_gdn_a_log=m_gdn_a_log, m_gdn_dt_bias=m_gdn_dt_bias, m_gdn_norm_g=m_gdn_norm_g, m_gdn_w_out=m_gdn_w_out, m_mlp_w1=m_mlp_w1, m_mlp_w2=m_mlp_w2, v_norm_mix_g=v_norm_mix_g, v_norm_ffn_g=v_norm_ffn_g, v_final_norm_g=v_final_norm_g, v_cv_w_pw1=v_cv_w_pw1, v_cv_b_pw1=v_cv_b_pw1, v_cv_w_dw=v_cv_w_dw, v_cv_b_dw=v_cv_b_dw, v_cv_ln_g=v_cv_ln_g, v_cv_ln_b=v_cv_ln_b, v_cv_w_pw2=v_cv_w_pw2, v_cv_b_pw2=v_cv_b_pw2, v_gdn_w_in=v_gdn_w_in, v_gdn_conv_w=v_gdn_conv_w, v_gdn_a_log=v_gdn_a_log, v_gdn_dt_bias=v_gdn_dt_bias, v_gdn_norm_g=v_gdn_norm_g, v_gdn_w_out=v_gdn_w_out, v_mlp_w1=v_mlp_w1, v_mlp_w2=v_mlp_w2)
    weights = {n: given[n] for n in TWIN_WEIGHTS}
    shared = {n: given[n] for n in SHARED_INPUTS}
    per_example = {n: given[n] for n in ['x']}
    grad_fn = _jax.value_and_grad(_loss, argnums=(0, 1))

    def one_microbatch(ex, loss_target):
        ex = dict(ex)
        diff = ex.pop(TWIN_DIFF_INPUT)
        return grad_fn(weights, diff, {**shared, **ex}, loss_target)

    if N_MICROBATCH == 1:
        loss, (grad_w, grad_x) = one_microbatch(per_example, given["loss_target"])
    else:
        def body(carry, xs):
            loss_sum, grad_sum = carry
            l_k, (gw_k, gx_k) = one_microbatch(xs[0], xs[1])
            with _jax.named_scope("update"):
                return (loss_sum + l_k, _jax.tree.map(_jnp.add, grad_sum, gw_k)), gx_k

        init = (_jnp.zeros((), _jnp.float32), _jax.tree.map(_jnp.zeros_like, weights))
        (loss, grad_w), grad_x = _jax.lax.scan(body, init, (per_example, given["loss_target"]))
    with _jax.named_scope("update"):
        delta_w, new_m, new_v = {}, {}, {}
        for n in TWIN_WEIGHTS:
            delta_w[n], new_m[n], new_v[n] = _adamw(weights[n], grad_w[n], given["m_" + n], given["v_" + n])
    return (loss, grad_x, *[grad_w[n] for n in TWIN_WEIGHTS], *[delta_w[n] for n in TWIN_WEIGHTS],
            *[new_m[n] for n in TWIN_WEIGHTS], *[new_v[n] for n in TWIN_WEIGHTS])
```

```python
import functools

import jax
import jax.numpy as jnp
from jax import lax
from jax.experimental import pallas as pl
from jax.experimental.pallas import tpu as pltpu

F32, BF16 = jnp.float32, jnp.bfloat16
D = 1024
H = 8
HD = 128
CH = 64
DFF = 4 * D
KCV, HB_CV = 31, 32
KSC, HB_SC = 4, 8
EPS = 1e-6
LR, B1, B2, EPS_A, WD, STEP = 0.001, 0.9, 0.999, 1e-08, 0.01, 10
VMEM_LIMIT = 56 * 1024 * 1024
SUB = 32
NSMALL = 64
MESH = pl.DeviceIdType.MESH


def _cp(*sem):
    return pltpu.CompilerParams(dimension_semantics=sem, vmem_limit_bytes=VMEM_LIMIT)


def f_rms(h, g):
    return h * lax.rsqrt(jnp.mean(h * h, axis=-1, keepdims=True) + EPS) * g


def f_silu(x):
    return x * jax.nn.sigmoid(x)


def f_glu(u):
    return u[:, :D] * jax.nn.sigmoid(u[:, D:])


def f_ln_silu(x, g, b):
    mu = jnp.mean(x, axis=-1, keepdims=True)
    xc = x - mu
    y = xc * lax.rsqrt(jnp.mean(xc * xc, axis=-1, keepdims=True) + EPS)
    return f_silu(y * g + b)


def f_relu2(z):
    r = jnp.maximum(z.astype(F32), 0.0)
    return r * r


def f_post(o, z, ng):
    outs = []
    for h in range(H):
        oh = o[:, h * HD:(h + 1) * HD]
        y = oh * lax.rsqrt(jnp.mean(oh * oh, axis=-1, keepdims=True) + EPS) * ng
        outs.append(y * f_silu(z[:, h * HD:(h + 1) * HD]))
    return jnp.concatenate(outs, axis=1)


def f_adamw(w, g, m, v):
    m2 = B1 * m + (1.0 - B1) * g
    v2 = B2 * v + (1.0 - B2) * (g * g)
    m_hat = m2 / (1.0 - B1 ** STEP)
    v_hat = v2 / (1.0 - B2 ** STEP)
    delta = -LR * (m_hat / (jnp.sqrt(v_hat) + EPS_A) + WD * w)
    return delta, m2, v2


def _dot_raw(a, b, mode):
    dims = {"NN": ((1,), (0,)), "NT": ((1,), (1,)), "TN": ((0,), (0,))}[mode]
    return lax.dot_general(a.astype(BF16), b.astype(BF16), (dims, ((), ())), preferred_element_type=F32)


@functools.partial(jax.custom_vjp, nondiff_argnums=(2,))
def _dot_vjp(a, b, mode):
    return _dot_raw(a, b, mode)


def _dot_fwd(a, b, mode):
    return _dot_raw(a, b, mode), (a, b)


def _dot_bwd(mode, res, dc):
    a, b = res
    if mode == "NN":
        return _dot_vjp(dc, b, "NT"), _dot_vjp(a, dc, "TN")
    if mode == "NT":
        return _dot_vjp(dc, b, "NN"), _dot_vjp(dc, a, "TN")
    return _dot_vjp(b, dc, "NT"), _dot_vjp(a, dc, "NN")


_dot_vjp.defvjp(_dot_fwd, _dot_bwd)


def _dot_split(x, y):
    xh, yh = x.astype(BF16), y.astype(BF16)
    xl, yl = x - xh.astype(F32), y - yh.astype(F32)
    return _dot_raw(xh, yh, "NN") + (_dot_raw(xh, yl, "NN") + _dot_raw(xl, yh, "NN"))


def _tril_inverse(a):
    ri = lax.broadcasted_iota(jnp.int32, (CH, CH), 0)
    ci = lax.broadcasted_iota(jnp.int32, (CH, CH), 1)
    t = (ri == ci).astype(F32)
    for lvl in range(CH.bit_length() - 1):
        same_pair = jnp.right_shift(ri, lvl + 1) == jnp.right_shift(ci, lvl + 1)
        quarter = (jnp.bitwise_and(jnp.right_shift(ri, lvl), 1) == 1) & (jnp.bitwise_and(jnp.right_shift(ci, lvl), 1) == 0)
        off = jnp.where(same_pair & quarter, a, 0.0)
        t = t - _dot_split(_dot_split(t, off), t)
    return t


@jax.custom_vjp
def _stored_solve(a, t, rhs):
    return _dot_raw(t, rhs, "NN")


def _stored_solve_fwd(a, t, rhs):
    sol = _dot_raw(t, rhs, "NN")
    return sol, (t, sol)


def _stored_solve_bwd(res, g):
    t, sol = res
    g_rhs = _dot_vjp(t, g, "TN")
    return -_dot_vjp(g_rhs, sol, "NT"), jnp.zeros_like(t), g_rhs


_stored_solve.defvjp(_stored_solve_fwd, _stored_solve_bwd)


def _lane_pick(row, idx, width):
    sel = lax.broadcasted_iota(jnp.int32, (1, width), 1) == idx
    return jnp.sum(jnp.where(sel, row, 0.0), axis=1, keepdims=True)


def f_prep_head(cq, ck, cv, araw, braw, alog, dtb, t_stored, dot):
    ri = lax.broadcasted_iota(jnp.int32, (CH, CH), 0)
    ci = lax.broadcasted_iota(jnp.int32, (CH, CH), 1)
    eye = (ri == ci).astype(F32)
    low = (ri >= ci).astype(F32)
    q = f_silu(cq)
    q = q * lax.rsqrt(jnp.sum(q * q, axis=-1, keepdims=True) + 1e-6) * (HD ** -0.5)
    k = f_silu(ck)
    k = k * lax.rsqrt(jnp.sum(k * k, axis=-1, keepdims=True) + 1e-6)
    v = f_silu(cv)
    beta = jax.nn.sigmoid(braw)
    sp_in = araw + dtb
    softplus = jnp.maximum(sp_in, 0.0) + jnp.log(1.0 + jnp.exp(-jnp.abs(sp_in)))
    g = -jnp.exp(alog) * softplus
    g_row = jnp.sum(eye * g, axis=0, keepdims=True)
    gc = jnp.sum(low * g_row, axis=1, keepdims=True)
    gc_row = jnp.sum(eye * gc, axis=0, keepdims=True)
    decay = jnp.exp(jnp.where(ri >= ci, gc - gc_row, -1e30))
    kb = k * beta
    a = jnp.where(ri > ci, dot(kb, k, "NT") * decay, 0.0)
    if t_stored is None:
        t = _tril_inverse(a)
        solve = lambda rhs: dot(t, rhs, "NN")
    else:
        t = t_stored
        solve = lambda rhs: _stored_solve(a, t_stored, rhs)
    egc = jnp.exp(gc)
    u = solve(v * beta)
    w = solve(kb * egc)
    qk = dot(q, k, "NT") * decay
    qg = q * egc
    last = lax.broadcasted_iota(jnp.int32, (CH, 1), 0) == CH - 1
    gl = jnp.sum(jnp.where(last, gc, 0.0), axis=0, keepdims=True)
    kg = k * jnp.exp(gl - gc)
    eg = jnp.exp(gl) * jnp.ones((1, HD), F32)
    return u, w, qk, qg, kg, eg, t


def f_scan_head(s, u, w, qg, kg, qk, eg, dot):
    vn = u - dot(w, s, "NN")
    o = dot(qg, s, "NN") + dot(qk, vn, "NN")
    s2 = s * eg + dot(kg, vn, "TN")
    return o, s2


def row_call(name, fn, rows, pars, out_rows, out_accs, tm):
    T = rows[0][0].shape[0]
    n_r, n_p, n_o = len(rows), len(pars), len(out_rows)
    in_specs = [pl.BlockSpec((tm, w), functools.partial(lambda i, cb: (i, cb), cb=cb)) for (_, w, cb) in rows]
    in_specs += [pl.BlockSpec(p.shape, functools.partial(lambda i, nd: (0,) * nd, nd=p.ndim)) for p in pars]
    out_specs = [pl.BlockSpec((tm, w), lambda i: (i, 0)) for (w, _) in out_rows]
    out_specs += [pl.BlockSpec(s, lambda i: (0, 0)) for s in out_accs]
    out_shape = [jax.ShapeDtypeStruct((T, w), dt) for (w, dt) in out_rows]
    out_shape += [jax.ShapeDtypeStruct(s, F32) for s in out_accs]

    def body(*refs):
        rin, pin = refs[:n_r], refs[n_r:n_r + n_p]
        rout, aout = refs[n_r + n_p:n_r + n_p + n_o], refs[n_r + n_p + n_o:]
        if aout:
            @pl.when(pl.program_id(0) == 0)
            def _():
                for a in aout:
                    a[...] = jnp.zeros(a.shape, F32)
        pv = [p[...] for p in pin]

        def step(r, carry):
            sl = pl.ds(pl.multiple_of(r * SUB, SUB), SUB)
            outs, accs = fn(*[x[sl, :] for x in rin], *pv)
            for o, val in zip(rout, outs):
                o[sl, :] = val.astype(o.dtype)
            for a, val in zip(aout, accs):
                a[...] += val
            return carry

        lax.fori_loop(0, tm // SUB, step, 0)

    return pl.pallas_call(body, name=name, grid=(T // tm,), in_specs=in_specs, out_specs=out_specs,
                          out_shape=out_shape, compiler_params=_cp("arbitrary"))(*[r[0] for r in rows], *pars)


EW_TILE_ELEMS = 256 * 1024


def _ew_rows(R, Cc):
    if R * Cc <= EW_TILE_ELEMS or R % 8:
        return R
    tr = 8
    while tr * 2 * Cc <= EW_TILE_ELEMS and R % (tr * 2) == 0:
        tr *= 2
    return tr


def sum4_call(name, p):
    _, R, Cc = p.shape
    tr = _ew_rows(R, Cc)

    def body(p_ref, o_ref):
        o_ref[...] = ((p_ref[0] + p_ref[1]) + p_ref[2]) + p_ref[3]

    return pl.pallas_call(body, name=name, grid=(R // tr,), in_specs=[pl.BlockSpec((4, tr, Cc), lambda i: (0, i, 0))],
                          out_specs=pl.BlockSpec((tr, Cc), lambda i: (i, 0)), out_shape=jax.ShapeDtypeStruct((R, Cc), F32),
                          compiler_params=_cp("arbitrary"))(p)


def ew_call(name, fn, ins, n_out):
    shape = ins[0].shape
    lead = shape[:-2]
    R, Cc = shape[-2:]
    tr = _ew_rows(R, Cc)
    grid = lead + (R // tr,)
    nl = len(lead)
    spec = pl.BlockSpec((None,) * nl + (tr, Cc), lambda *idx: idx + (0,))

    def body(*refs):
        outs = fn(*[r[...] for r in refs[:len(ins)]])
        for o, val in zip(refs[len(ins):], outs):
            o[...] = val

    return pl.pallas_call(body, name=name, grid=grid, in_specs=[spec] * len(ins), out_specs=[spec] * n_out,
                          out_shape=[jax.ShapeDtypeStruct(shape, F32)] * n_out,
                          compiler_params=_cp(*(("arbitrary",) * len(grid))))(*ins)


def mm(name, a, b, mode, out_dtype, a_fn=None, epi=None, epi_ins=(), tm=512):
    M, K = a.shape
    N = b.shape[1] if mode == "NN" else b.shape[0]
    tm = min(tm, M)
    tn = min(N, 1024)
    in_specs = [pl.BlockSpec((tm, K), lambda j, i: (i, 0)),
                pl.BlockSpec((K, tn), lambda j, i: (0, j)) if mode == "NN" else pl.BlockSpec((tn, K), lambda j, i: (j, 0))]
    for (_, kind) in epi_ins:
        in_specs.append(pl.BlockSpec((tm, tn), lambda j, i: (i, j)) if kind == "tile" else pl.BlockSpec((1, tn), lambda j, i: (0, j)))

    def body(a_ref, b_ref, *rest):
        av = a_ref[...]
        if a_fn is not None:
            av = a_fn(av)
        acc = _dot_raw(av, b_ref[...], mode)
        if epi is not None:
            acc = epi(acc, *[r[...] for r in rest[:-1]])
        rest[-1][...] = acc.astype(out_dtype)

    return pl.pallas_call(body, name=name, grid=(N // tn, M // tm), in_specs=in_specs,
                          out_specs=pl.BlockSpec((tm, tn), lambda j, i: (i, j)),
                          out_shape=jax.ShapeDtypeStruct((M, N), out_dtype),
                          compiler_params=_cp("arbitrary", "arbitrary"))(a, b, *[e[0] for e in epi_ins])


def mm_tn(name, a, g, a_fn=None, a_cols=None, tt=512):
    T = a.shape[0]
    ka, acb = (a.shape[1], 0) if a_cols is None else a_cols
    N = g.shape[1]
    tt = min(tt, T)
    tka, tn = min(ka, 1024), min(N, 1024)
    nkb = ka // tka

    def body(a_ref, g_ref, o_ref):
        @pl.when(pl.program_id(2) == 0)
        def _():
            o_ref[...] = jnp.zeros(o_ref.shape, F32)
        av = a_ref[...]
        if a_fn is not None:
            av = a_fn(av)
        o_ref[...] += _dot_raw(av, g_ref[...], "TN")

    return pl.pallas_call(body, name=name, grid=(nkb, N // tn, T // tt),
                          in_specs=[pl.BlockSpec((tt, tka), lambda ia, j, t: (t, acb * nkb + ia)),
                                    pl.BlockSpec((tt, tn), lambda ia, j, t: (t, j))],
                          out_specs=pl.BlockSpec((tka, tn), lambda ia, j, t: (ia, j)),
                          out_shape=jax.ShapeDtypeStruct((ka, N), F32),
                          compiler_params=_cp("arbitrary", "arbitrary", "arbitrary"))(a, g)


def dwconv_fwd(name, x, xw, w_pad, bias, S, K, HB, pre, post, tm):
    T = x.shape[0]
    C = w_pad.shape[1]
    nb, per_seq = tm // HB, S // tm
    has_b, has_post = bias is not None, post is not None

    def body(*refs):
        x_ref, xp_ref, w_ref = refs[:3]
        pos = 3
        b_ref = refs[pos] if has_b else None
        pos += has_b
        ppars = refs[pos:pos + (len(post[1]) if has_post else 0)]
        pos += len(ppars)
        c_ref = refs[pos]
        s_ref = refs[pos + 1] if has_post else None
        ext = refs[-1]
        first = (pl.program_id(0) % per_seq) == 0
        ext[0:HB, :] = jnp.where(first, 0.0, pre(xp_ref[...]))
        for r in range(tm // SUB):
            ext[HB + r * SUB:HB + (r + 1) * SUB, :] = pre(x_ref[r * SUB:(r + 1) * SUB, :])
        pv = [p[...] for p in ppars]
        for r in range(tm // SUB):
            acc = jnp.zeros((SUB, C), F32)
            if has_b:
                acc = acc + b_ref[...]
            for k in range(K):
                off = HB + r * SUB - (K - 1) + k
                acc = acc + w_ref[k:k + 1, :] * ext[off:off + SUB, :]
            c_ref[r * SUB:(r + 1) * SUB, :] = acc
            if has_post:
                s_ref[r * SUB:(r + 1) * SUB, :] = post[0](acc, *pv).astype(BF16)

    ins = [x, x, w_pad] + ([bias] if has_b else []) + (list(post[1]) if has_post else [])
    in_specs = [pl.BlockSpec((tm, xw), lambda i: (i, 0)),
                pl.BlockSpec((HB, xw), lambda i: (jnp.maximum(i * nb - 1, 0), 0)),
                pl.BlockSpec(w_pad.shape, lambda i: (0, 0))]
    in_specs += [pl.BlockSpec(p.shape, lambda i: (0, 0)) for p in ins[3:]]
    out_specs = [pl.BlockSpec((tm, C), lambda i: (i, 0))] * (1 + has_post)
    out_shape = [jax.ShapeDtypeStruct((T, C), F32)] + ([jax.ShapeDtypeStruct((T, C), BF16)] if has_post else [])
    return pl.pallas_call(body, name=name, grid=(T // tm,), in_specs=in_specs, out_specs=out_specs, out_shape=out_shape,
                          scratch_shapes=[pltpu.VMEM((HB + tm, C), F32)], compiler_params=_cp("arbitrary"))(*ins)


def dwconv_bwd(name, g, x, xw, w_pad, S, K, HB, pre, pre_bwd, tm):
    T = g.shape[0]
    C = w_pad.shape[1]
    nb, per_seq = tm // HB, S // tm
    nblk = T // HB

    def body(g_ref, gn_ref, x_ref, xp_ref, w_ref, dx_ref, dw_ref, dbx_ref, extg, exta):
        i = pl.program_id(0)
        first = (i % per_seq) == 0
        last = (i % per_seq) == per_seq - 1

        @pl.when(i == 0)
        def _():
            dw_ref[...] = jnp.zeros(dw_ref.shape, F32)
            dbx_ref[...] = jnp.zeros(dbx_ref.shape, F32)

        extg[tm:tm + HB, :] = jnp.where(last, 0.0, gn_ref[...])
        exta[0:HB, :] = jnp.where(first, 0.0, pre(xp_ref[...]))
        for r in range(tm // SUB):
            extg[r * SUB:(r + 1) * SUB, :] = g_ref[r * SUB:(r + 1) * SUB, :]
            exta[HB + r * SUB:HB + (r + 1) * SUB, :] = pre(x_ref[r * SUB:(r + 1) * SUB, :])
        for r in range(tm // SUB):
            acc = jnp.zeros((SUB, C), F32)
            for k in range(K):
                off = r * SUB + (K - 1) - k
                acc = acc + w_ref[k:k + 1, :] * extg[off:off + SUB, :]
            dx = pre_bwd(x_ref[r * SUB:(r + 1) * SUB, :], acc)
            dx_ref[r * SUB:(r + 1) * SUB, :] = dx
            dbx_ref[...] += jnp.sum(dx, axis=0, keepdims=True)
        for k in range(K):
            p = jnp.zeros((SUB, C), F32)
            for r in range(tm // SUB):
                off = HB + r * SUB - (K - 1) + k
                p = p + extg[r * SUB:(r + 1) * SUB, :] * exta[off:off + SUB, :]
            dw_ref[k:k + 1, :] += jnp.sum(p, axis=0, keepdims=True)

    in_specs = [pl.BlockSpec((tm, C), lambda i: (i, 0)),
                pl.BlockSpec((HB, C), lambda i: (jnp.minimum((i + 1) * nb, nblk - 1), 0)),
                pl.BlockSpec((tm, xw), lambda i: (i, 0)),
                pl.BlockSpec((HB, xw), lambda i: (jnp.maximum(i * nb - 1, 0), 0)),
                pl.BlockSpec(w_pad.shape, lambda i: (0, 0))]
    out_specs = [pl.BlockSpec((tm, xw), lambda i: (i, 0)), pl.BlockSpec((HB, C), lambda i: (0, 0)),
                 pl.BlockSpec((1, xw), lambda i: (0, 0))]
    out_shape = [jax.ShapeDtypeStruct((T, xw), F32), jax.ShapeDtypeStruct((HB, C), F32), jax.ShapeDtypeStruct((1, xw), F32)]
    return pl.pallas_call(body, name=name, grid=(T // tm,), in_specs=in_specs, out_specs=out_specs, out_shape=out_shape,
                          scratch_shapes=[pltpu.VMEM((tm + HB, C), F32), pltpu.VMEM((HB + tm, C), F32)],
                          compiler_params=_cp("arbitrary"))(g, g, x, x, w_pad)


def _glu_bwd(u, da):
    u1, sg = u[:, :D], jax.nn.sigmoid(u[:, D:])
    return jnp.concatenate([da * sg, da * u1 * sg * (1.0 - sg)], axis=1)


def _head_cols(ref, h, base=0):
    return ref[:, base + h * HD:base + (h + 1) * HD]


def gdn_prep_fwd(cpre, pab, alog, dtb):
    T = cpre.shape[0]
    nc = T // CH

    def body(c_ref, ab_ref, al_ref, dt_ref, u_ref, w_ref, qg_ref, kg_ref, qk_ref, t_ref, eg_ref):
        ab = ab_ref[...]
        for h in range(H):
            u, w, qk, qg, kg, eg, t = f_prep_head(
                _head_cols(c_ref, h), _head_cols(c_ref, h, D), _head_cols(c_ref, h, 2 * D),
                _lane_pick(ab, h, HD), _lane_pick(ab, H + h, HD),
                _lane_pick(al_ref[...], h, HD), _lane_pick(dt_ref[...], h, HD), None, _dot_raw)
            cols = slice(h * HD, (h + 1) * HD)
            u_ref[:, cols] = u
            w_ref[:, cols] = w.astype(BF16)
            qg_ref[:, cols] = qg.astype(BF16)
            kg_ref[:, cols] = kg.astype(BF16)
            qk_ref[0, h] = qk.astype(BF16)
            t_ref[0, h] = t.astype(BF16)
            eg_ref[0, h:h + 1, :] = eg

    row = lambda w: pl.BlockSpec((CH, w), lambda n: (n, 0))
    par = pl.BlockSpec((1, HD), lambda n: (0, 0))
    mat = pl.BlockSpec((1, H, CH, CH), lambda n: (n, 0, 0, 0))
    return pl.pallas_call(
        body, name="gdn_prep_fwd", grid=(nc,), in_specs=[row(3 * D), row(HD), par, par],
        out_specs=[row(D), row(D), row(D), row(D), mat, mat, pl.BlockSpec((1, H, HD), lambda n: (n, 0, 0))],
        out_shape=[jax.ShapeDtypeStruct((T, D), F32)] + [jax.ShapeDtypeStruct((T, D), BF16)] * 3
        + [jax.ShapeDtypeStruct((nc, H, CH, CH), BF16)] * 2 + [jax.ShapeDtypeStruct((nc, H, HD), F32)],
        compiler_params=_cp("arbitrary"))(cpre, pab, alog, dtb)


def gdn_prep_bwd(cpre, pab, alog, dtb, tmat, du, dw, dqg, dkg, dqk, deg):
    T = cpre.shape[0]
    nc = T // CH

    def body(c_ref, ab_ref, al_ref, dt_ref, t_ref, du_ref, dw_ref, dqg_ref, dkg_ref, dqk_ref, deg_ref,
             dc_ref, dab_ref, dal_ref, ddt_ref):
        @pl.when(pl.program_id(0) == 0)
        def _():
            dal_ref[...] = jnp.zeros(dal_ref.shape, F32)
            ddt_ref[...] = jnp.zeros(ddt_ref.shape, F32)

        ab, al, dt = ab_ref[...], al_ref[...], dt_ref[...]
        lane = lax.broadcasted_iota(jnp.int32, (1, HD), 1)
        dab = jnp.zeros((CH, HD), F32)
        dal = jnp.zeros((1, HD), F32)
        ddt = jnp.zeros((1, HD), F32)
        for h in range(H):
            t_st = t_ref[0, h].astype(F32)

            def fwd(cq, ck, cv, araw, braw, alh, dth):
                return f_prep_head(cq, ck, cv, araw, braw, alh, dth, t_st, _dot_vjp)[:6]

            _, vjp = jax.vjp(fwd, _head_cols(c_ref, h), _head_cols(c_ref, h, D), _head_cols(c_ref, h, 2 * D),
                             _lane_pick(ab, h, HD), _lane_pick(ab, H + h, HD), _lane_pick(al, h, HD), _lane_pick(dt, h, HD))
            dcq, dck, dcv, dar, dbr, dalh, ddth = vjp((
                _head_cols(du_ref, h), _head_cols(dw_ref, h), dqk_ref[0, h], _head_cols(dqg_ref, h),
                _head_cols(dkg_ref, h), deg_ref[0, h:h + 1, :]))
            dc_ref[:, h * HD:(h + 1) * HD] = dcq
            dc_ref[:, D + h * HD:D + (h + 1) * HD] = dck
            dc_ref[:, 2 * D + h * HD:2 * D + (h + 1) * HD] = dcv
            dab = dab + jnp.where(lane == h, dar, 0.0) + jnp.where(lane == H + h, dbr, 0.0)
            dal = dal + jnp.where(lane == h, dalh, 0.0)
            ddt = ddt + jnp.where(lane == h, ddth, 0.0)
        dab_ref[...] = dab
        dal_ref[...] += dal
        ddt_ref[...] += ddt

    row = lambda w: pl.BlockSpec((CH, w), lambda n: (n, 0))
    par = pl.BlockSpec((1, HD), lambda n: (0, 0))
    mat = pl.BlockSpec((1, H, CH, CH), lambda n: (n, 0, 0, 0))
    vec = pl.BlockSpec((1, H, HD), lambda n: (n, 0, 0))
    return pl.pallas_call(
        body, name="gdn_prep_bwd", grid=(nc,),
        in_specs=[row(3 * D), row(HD), par, par, mat, row(D), row(D), row(D), row(D), mat, vec],
        out_specs=[row(3 * D), row(HD), par, par],
        out_shape=[jax.ShapeDtypeStruct((T, 3 * D), F32), jax.ShapeDtypeStruct((T, HD), F32),
                   jax.ShapeDtypeStruct((1, HD), F32), jax.ShapeDtypeStruct((1, HD), F32)],
        compiler_params=_cp("arbitrary"))(cpre, pab, alog, dtb, tmat, du, dw, dqg, dkg, dqk, deg)


def gdn_scan_fwd(u, w, qg, kg, qk, eg, S):
    T = u.shape[0]
    nc, per_seq = T // CH, S // CH

    def body(u_ref, w_ref, qg_ref, kg_ref, qk_ref, eg_ref, o_ref, sall_ref, s_ref):
        @pl.when(pl.program_id(0) % per_seq == 0)
        def _():
            s_ref[...] = jnp.zeros(s_ref.shape, F32)

        for h in range(H):
            s = s_ref[h]
            sall_ref[0, h] = s
            o, s2 = f_scan_head(s, _head_cols(u_ref, h), _head_cols(w_ref, h), _head_cols(qg_ref, h),
                                _head_cols(kg_ref, h), qk_ref[0, h], eg_ref[0, h:h + 1, :], _dot_raw)
            o_ref[:, h * HD:(h + 1) * HD] = o
            s_ref[h] = s2

    row = pl.BlockSpec((CH, D), lambda n: (n, 0))
    return pl.pallas_call(
        body, name="gdn_scan_fwd", grid=(nc,),
        in_specs=[row, row, row, row, pl.BlockSpec((1, H, CH, CH), lambda n: (n, 0, 0, 0)),
                  pl.BlockSpec((1, H, HD), lambda n: (n, 0, 0))],
        out_specs=[row, pl.BlockSpec((1, H, HD, HD), lambda n: (n, 0, 0, 0))],
        out_shape=[jax.ShapeDtypeStruct((T, D), F32), jax.ShapeDtypeStruct((nc, H, HD, HD), F32)],
        scratch_shapes=[pltpu.VMEM((H, HD, HD), F32)], compiler_params=_cp("arbitrary"))(u, w, qg, kg, qk, eg)


def gdn_scan_bwd(do, u, w, qg, kg, qk, eg, sall, S):
    T = u.shape[0]
    nc, per_seq = T // CH, S // CH

    def body(do_ref, u_ref, w_ref, qg_ref, kg_ref, qk_ref, eg_ref, sall_ref,
             du_ref, dw_ref, dqg_ref, dkg_ref, dqk_ref, deg_ref, ds_ref):
        n = nc - 1 - pl.program_id(0)

        @pl.when(n % per_seq == per_seq - 1)
        def _():
            ds_ref[...] = jnp.zeros(ds_ref.shape, F32)

        for h in range(H):
            def fwd(s, uu, ww, qq, kk, mm_, ee):
                return f_scan_head(s, uu, ww, qq, kk, mm_, ee, _dot_vjp)

            _, vjp = jax.vjp(fwd, sall_ref[0, h], _head_cols(u_ref, h), _head_cols(w_ref, h).astype(F32),
                             _head_cols(qg_ref, h).astype(F32), _head_cols(kg_ref, h).astype(F32),
                             qk_ref[0, h].astype(F32), eg_ref[0, h:h + 1, :])
            ds, du, dw, dqg, dkg, dqk, deg = vjp((_head_cols(do_ref, h), ds_ref[h]))
            cols = slice(h * HD, (h + 1) * HD)
            du_ref[:, cols] = du
            dw_ref[:, cols] = dw
            dqg_ref[:, cols] = dqg
            dkg_ref[:, cols] = dkg
            dqk_ref[0, h] = dqk
            deg_ref[0, h:h + 1, :] = deg
            ds_ref[h] = ds

    rev = lambda n: (nc - 1 - n, 0)
    row = pl.BlockSpec((CH, D), rev)
    mat = pl.BlockSpec((1, H, CH, CH), lambda n: (nc - 1 - n, 0, 0, 0))
    vec = pl.BlockSpec((1, H, HD), lambda n: (nc - 1 - n, 0, 0))
    return pl.pallas_call(
        body, name="gdn_scan_bwd", grid=(nc,),
        in_specs=[row, row, row, row, row, mat, vec, pl.BlockSpec((1, H, HD, HD), lambda n: (nc - 1 - n, 0, 0, 0))],
        out_specs=[row, row, row, row, mat, vec],
        out_shape=[jax.ShapeDtypeStruct((T, D), F32)] * 4
        + [jax.ShapeDtypeStruct((nc, H, CH, CH), F32), jax.ShapeDtypeStruct((nc, H, HD), F32)],
        scratch_shapes=[pltpu.VMEM((H, HD, HD), F32)], compiler_params=_cp("arbitrary"))(do, u, w, qg, kg, qk, eg, sall)


def xor_exchange(name, ins, out_shapes, plan, n_remote, n_local):
    n_in = len(ins)

    def body(*refs):
        in_refs, out_refs = refs[:n_in], refs[n_in:n_in + len(out_shapes)]
        send_sems, recv_sems, loc_sems = refs[n_in + len(out_shapes):]
        x, y, c = lax.axis_index("x"), lax.axis_index("y"), lax.axis_index("c")
        remote, local = plan(in_refs, out_refs, (x, y, c))
        assert len(remote) == n_remote and len(local) == n_local
        copies = []
        for k, ((dx, dy, dc), src, dst) in enumerate(remote):
            peer = (1 - x if dx else x, 1 - y if dy else y, 1 - c if dc else c)
            copies.append(pltpu.make_async_remote_copy(src_ref=src, dst_ref=dst, send_sem=send_sems.at[k],
                                                       recv_sem=recv_sems.at[k], device_id=peer, device_id_type=MESH))
        for cp in copies:
            cp.start()
        locs = [pltpu.make_async_copy(src, dst, loc_sems.at[k]) for k, (src, dst) in enumerate(local)]
        for cp in locs:
            cp.start()
        for cp in copies:
            cp.wait()
        for cp in locs:
            cp.wait()

    anyspec = pl.BlockSpec(memory_space=pl.ANY)
    return pl.pallas_call(
        body, name=name, in_specs=[anyspec] * n_in, out_specs=[anyspec] * len(out_shapes), out_shape=out_shapes,
        scratch_shapes=[pltpu.SemaphoreType.DMA((n_remote,)), pltpu.SemaphoreType.DMA((n_remote,)),
                        pltpu.SemaphoreType.DMA((max(n_local, 1),))],
        )(*ins)


class WSpec:
    def __init__(self, name, full, sa, ha, group, layer=None, lead=False):
        self.name, self.full, self.sa, self.ha, self.group, self.layer, self.lead = name, full, sa, ha, group, layer, lead
        self.ws = 1 if lead else full[sa] // 4
        self.wh = full[ha] // 2

    def shard_shape(self):
        if self.lead:
            return tuple(n for a, n in enumerate(self.full) if a != self.sa)
        return tuple(self.ws if a == self.sa else n for a, n in enumerate(self.full))

    def half_full_shape(self):
        return tuple(self.wh if a == self.ha else n for a, n in enumerate(self.full))

    def shard_half_shape(self):
        s = list(self.half_full_shape())
        if self.lead:
            del s[self.sa]
        else:
            s[self.sa] = self.ws
        return tuple(s)

    def full_view(self, ref, q=None, h=None):
        idx = []
        for a in range(len(self.full)):
            if a == self.sa and q is not None:
                idx.append(q if self.lead else pl.ds(pl.multiple_of(q * self.ws, self.ws), self.ws))
            elif a == self.ha and h is not None:
                idx.append(pl.ds(pl.multiple_of(h * self.wh, self.wh), self.wh))
            else:
                idx.append(slice(None))
        return ref.at[tuple(idx)]

    def shard_view(self, ref, h):
        idx = [] if self.layer is None else [self.layer]
        for a in range(len(self.full)):
            if self.lead and a == self.sa:
                continue
            idx.append(pl.ds(pl.multiple_of(h * self.wh, self.wh), self.wh) if a == self.ha else slice(None))
        return ref.at[tuple(idx)]

    def shard_all(self, ref):
        return ref if self.layer is None else ref.at[self.layer]


WSPECS = [
    WSpec("cv_w_pw1", (D, 2 * D), 1, 0, 0),
    WSpec("cv_w_pw2", (D, D), 0, 1, 1),
    WSpec("gdn_w_in", (4, D, (4 * D + 2 * H) // 4), 0, 1, 2, lead=True),
    WSpec("gdn_w_out", (D, D), 0, 1, 3),
    WSpec("mlp_w1_0", (D, DFF), 1, 0, 4, layer=0),
    WSpec("mlp_w1_1", (D, DFF), 1, 0, 4, layer=1),
    WSpec("mlp_w2_0", (DFF, D), 0, 1, 5, layer=0),
    WSpec("mlp_w2_1", (DFF, D), 0, 1, 5, layer=1),
]
FLIPS = [(1, 0, 0), (0, 1, 0), (1, 1, 0)]
SIB = (0, 0, 1)


def _chip(x, y):
    return 2 * x + y


def gather_weights(shards_bf16, wdw_shard, wcv_shard):
    n = len(WSPECS)

    def plan_a(in_refs, out_refs, pos):
        x, y, c = pos
        q = _chip(x, y)
        remote, local = [], []
        for i, ws in enumerate(WSPECS):
            src_all = ws.shard_all(in_refs[ws.group])
            local.append((src_all, ws.full_view(out_refs[i], q)))
            src = ws.shard_view(in_refs[ws.group], c)
            for f in FLIPS:
                remote.append((f, src, ws.full_view(out_refs[i], q, c)))
        for j, (arr_i, width) in enumerate(((6, D // 4), (7, 3 * D // 4))):
            dst = out_refs[n + j].at[:, pl.ds(pl.multiple_of(q * width, 128), width)]
            local.append((in_refs[arr_i], dst))
            for f in FLIPS:
                remote.append((f, in_refs[arr_i], dst))
        return remote, local

    outs_a = [jax.ShapeDtypeStruct(ws.full, BF16) for ws in WSPECS]
    outs_a += [jax.ShapeDtypeStruct((KCV, D), F32), jax.ShapeDtypeStruct((KSC, 3 * D), F32)]
    got = xor_exchange("gather_chips", list(shards_bf16) + [wdw_shard, wcv_shard], outs_a, plan_a,
                       3 * (n + 2), n + 2)

    def plan_b(in_refs, out_refs, pos):
        x, y, c = pos
        q = _chip(x, y)
        remote = []
        for i, ws in enumerate(WSPECS):
            for (dx, dy, _) in FLIPS:
                qq = _chip(1 - x if dx else x, 1 - y if dy else y)
                view = ws.full_view(out_refs[i], qq, c)
                remote.append((SIB, ws.full_view(in_refs[i], qq, c), view))
        return remote, []

    outs_b = xor_exchange_alias("gather_cores", list(got[:n]), plan_b, 3 * n)
    return list(outs_b), got[n], got[n + 1]


def xor_exchange_alias(name, arrays, plan, n_remote):
    n = len(arrays)

    def body(*refs):
        out_refs = refs[n:2 * n]
        send_sems, recv_sems = refs[2 * n:]
        x, y, c = lax.axis_index("x"), lax.axis_index("y"), lax.axis_index("c")
        remote, _ = plan(out_refs, out_refs, (x, y, c))
        assert len(remote) == n_remote
        copies = []
        for k, ((dx, dy, dc), src, dst) in enumerate(remote):
            peer = (1 - x if dx else x, 1 - y if dy else y, 1 - c if dc else c)
            copies.append(pltpu.make_async_remote_copy(src_ref=src, dst_ref=dst, send_sem=send_sems.at[k],
                                                       recv_sem=recv_sems.at[k], device_id=peer, device_id_type=MESH))
        for cp in copies:
            cp.start()
        for cp in copies:
            cp.wait()

    anyspec = pl.BlockSpec(memory_space=pl.ANY)
    return pl.pallas_call(
        body, name=name, in_specs=[anyspec] * n, out_specs=[anyspec] * n,
        out_shape=[jax.ShapeDtypeStruct(a.shape, a.dtype) for a in arrays],
        input_output_aliases={i: i for i in range(n)},
        scratch_shapes=[pltpu.SemaphoreType.DMA((n_remote,)), pltpu.SemaphoreType.DMA((n_remote,))],
        )(*arrays)


def reduce_grads(grads):
    n = len(WSPECS)

    def plan1(in_refs, out_refs, pos):
        c = pos[2]
        remote, local = [], []
        for i, ws in enumerate(WSPECS):
            local.append((ws.full_view(in_refs[i], None, c), out_refs[i]))
            remote.append((SIB, ws.full_view(in_refs[i], None, 1 - c), out_refs[n + i]))
        return remote, local

    halves = [jax.ShapeDtypeStruct(ws.half_full_shape(), F32) for ws in WSPECS]
    got = xor_exchange("reduce_cores", grads, halves + halves, plan1, n, n)

    def as2d(a):
        return a.reshape((-1, a.shape[-1]))

    sums = [ew_call("reduce_add_" + ws.name, lambda a, b: (a + b,), [as2d(got[i]), as2d(got[n + i])], 1)[0]
            .reshape(ws.half_full_shape()) for i, ws in enumerate(WSPECS)]

    def plan2(in_refs, out_refs, pos):
        x, y, c = pos
        q = _chip(x, y)
        remote, local = [], []
        for i, ws in enumerate(WSPECS):
            local.append((ws.full_view(in_refs[i], q), out_refs[i].at[3]))
            for s, (dx, dy, _) in enumerate(FLIPS):
                qq = _chip(1 - x if dx else x, 1 - y if dy else y)
                remote.append(((dx, dy, 0), ws.full_view(in_refs[i], qq), out_refs[i].at[s]))
        return remote, local

    parts = [jax.ShapeDtypeStruct((4,) + ws.shard_half_shape(), F32) for ws in WSPECS]
    got2 = xor_exchange("reduce_chips", sums, parts, plan2, 3 * n, n)
    red = [sum4_call("reduce_sum_" + ws.name, got2[i]) for i, ws in enumerate(WSPECS)]

    groups = {}
    for ws in WSPECS:
        shape = ws.shard_shape() if ws.layer is None else (2,) + ws.shard_shape()
        groups[ws.group] = jax.ShapeDtypeStruct(shape, F32)

    def plan3(in_refs, out_refs, pos):
        c = pos[2]
        remote, local = [], []
        for i, ws in enumerate(WSPECS):
            dst = ws.shard_view(out_refs[ws.group], c)
            local.append((in_refs[i], dst))
            remote.append((SIB, in_refs[i], dst))
        return remote, local

    return xor_exchange("reduce_swap", red, [groups[g] for g in sorted(groups)], plan3, n, n)


def gather_small(buf):
    flips = [(dx, dy, dc) for dx in (0, 1) for dy in (0, 1) for dc in (0, 1)][1:]

    def plan(in_refs, out_refs, pos):
        x, y, c = pos
        me = 4 * x + 2 * y + c
        dst = out_refs[0].at[me]
        return [(f, in_refs[0], dst) for f in flips], [(in_refs[0], dst)]

    return xor_exchange("gather_small", [buf], [jax.ShapeDtypeStruct((8,) + buf.shape, F32)], plan, 7, 1)[0]


def _pad_rows(a, rows):
    return jnp.pad(a, ((0, rows - a.shape[0]), (0, 0)))


def _row1(v):
    v = v.reshape((1, -1))
    return jnp.pad(v, ((0, 0), (0, D - v.shape[1])))


def _rms_fwd(name, h, g, tm):
    return row_call(name, lambda hh, gg: ((f_rms(hh, gg),), ()), [(h, D, 0)], [g], [(D, BF16)], [], tm)[0]


def _rms_bwd(name, h, dhn, dres, g, tm):
    def fn(hh, dd, rr, gg):
        _, vjp = jax.vjp(f_rms, hh, gg)
        dh, dg = vjp(dd)
        dh = dh + rr
        return (dh,), (dg, jnp.sum(dh, axis=0, keepdims=True))

    return row_call(name, fn, [(h, D, 0), (dhn, D, 0), (dres, D, 0)], [g], [(D, F32)], [(1, D), (1, D)], tm)


def _mlp_fwd(tag, h, g, w1, w2, tm):
    hn = _rms_fwd("rms_ffn" + tag, h, g, tm)
    z1 = mm("mlp_up" + tag, hn, w1, "NN", BF16)
    h2 = mm("mlp_down" + tag, z1, w2, "NN", F32, a_fn=f_relu2, epi=lambda acc, r: acc + r, epi_ins=[(h, "tile")])
    return h2, (hn, z1)


def _mlp_bwd(tag, dh, h, g, w1, w2, saved, tm):
    hn, z1 = saved
    dz1 = mm("mlp_down_dx" + tag, dh, w2, "NT", BF16,
             epi=lambda acc, z: acc * (2.0 * jnp.maximum(z.astype(F32), 0.0)), epi_ins=[(z1, "tile")])
    dw2 = mm_tn("mlp_down_dw" + tag, z1, dh, a_fn=f_relu2)
    dhn = mm("mlp_up_dx" + tag, dz1, w1, "NT", F32)
    dw1 = mm_tn("mlp_up_dw" + tag, hn, dz1)
    dh_in, dg, colsum = _rms_bwd("rms_ffn_bwd" + tag, h, dhn, dh, g, tm)
    return dh_in, dg, colsum, dw1, dw2


def kernel(x, norm_mix_g, norm_ffn_g, final_norm_g, cv_w_pw1, cv_b_pw1, cv_w_dw, cv_b_dw, cv_ln_g, cv_ln_b, cv_w_pw2, cv_b_pw2, gdn_w_in, gdn_conv_w, gdn_a_log, gdn_dt_bias, gdn_norm_g, gdn_w_out, mlp_w1, mlp_w2, loss_target, m_norm_mix_g, m_norm_ffn_g, m_final_norm_g, m_cv_w_pw1, m_cv_b_pw1, m_cv_w_dw, m_cv_b_dw, m_cv_ln_g, m_cv_ln_b, m_cv_w_pw2, m_cv_b_pw2, m_gdn_w_in, m_gdn_conv_w, m_gdn_a_log, m_gdn_dt_bias, m_gdn_norm_g, m_gdn_w_out, m_mlp_w1, m_mlp_w2, v_norm_mix_g, v_norm_ffn_g, v_final_norm_g, v_cv_w_pw1, v_cv_b_pw1, v_cv_w_dw, v_cv_b_dw, v_cv_ln_g, v_cv_ln_b, v_cv_w_pw2, v_cv_b_pw2, v_gdn_w_in, v_gdn_conv_w, v_gdn_a_log, v_gdn_dt_bias, v_gdn_norm_g, v_gdn_w_out, v_mlp_w1, v_mlp_w2):
    env = dict(locals())
    bl, S, _ = x.shape
    T = bl * S
    tm = min(256, S)
    xf = x.reshape((T, D))
    tgt = loss_target.reshape((T, D))

    big = [cv_w_pw1[0], cv_w_pw2[0], gdn_w_in[0], gdn_w_out[0], mlp_w1, mlp_w2]
    nat, wdw, wcv = gather_weights([b.astype(BF16) for b in big], cv_w_dw[0], gdn_conv_w[0])
    w_pw1, w_pw2, w_in_sm, w_out, w1_0, w1_1, w2_0, w2_1 = nat
    w_in = jnp.transpose(w_in_sm, (1, 0, 2)).reshape((D, 4 * D + 2 * H))
    w_qkv, w_z = w_in[:, :3 * D], w_in[:, 3 * D:4 * D]
    w_qkvz = w_in[:, :4 * D]
    w_ab = jnp.pad(w_in[:, 4 * D:], ((0, 0), (0, HD - 2 * H)))
    wdw_p, wcv_p = _pad_rows(wdw, HB_CV), _pad_rows(wcv, HB_SC)
    alog_p = jnp.pad(gdn_a_log, ((0, 0), (0, HD - H)))
    dtb_p = jnp.pad(gdn_dt_bias, ((0, 0), (0, HD - H)))
    g_mix0, g_mix1 = norm_mix_g[0:1], norm_mix_g[1:2]
    g_ffn0, g_ffn1 = norm_ffn_g[0:1], norm_ffn_g[1:2]
    g_fin = final_norm_g.reshape((1, D))

    hn0 = _rms_fwd("rms_mix0", xf, g_mix0, tm)
    u = mm("cv_pw1", hn0, w_pw1, "NN", F32, epi=lambda acc, b: acc + b, epi_ins=[(cv_b_pw1, "row")])
    dwc, s_act = dwconv_fwd("cv_dwconv", u, 2 * D, wdw_p, cv_b_dw, S, KCV, HB_CV, f_glu, (f_ln_silu, (cv_ln_g, cv_ln_b)), tm)
    h1 = mm("cv_pw2", s_act, w_pw2, "NN", F32, epi=lambda acc, b, r: acc + b + r,
            epi_ins=[(cv_b_pw2, "row"), (xf, "tile")])
    h2, mlp0_saved = _mlp_fwd("0", h1, g_ffn0, w1_0, w2_0, tm)

    hn2 = _rms_fwd("rms_mix1", h2, g_mix1, tm)
    pqkvz = mm("gdn_in", hn2, w_qkvz, "NN", F32)
    pab = mm("gdn_in_ab", hn2, w_ab, "NN", F32)
    cpre = dwconv_fwd("gdn_conv", pqkvz, 3 * D, wcv_p, None, S, KSC, HB_SC, lambda v: v, None, tm)[0]
    gu, gw, gqg, gkg, gqk, gt, geg = gdn_prep_fwd(cpre, pab, alog_p, dtb_p)
    o, sall = gdn_scan_fwd(gu, gw, gqg, gkg, gqk, geg, S)
    on = row_call("gdn_post", lambda oo, zz, ng: ((f_post(oo, zz, ng),), ()), [(o, D, 0), (pqkvz, D, 3)],
                  [gdn_norm_g], [(D, BF16)], [], tm)[0]
    h3 = mm("gdn_out", on, w_out, "NN", F32, epi=lambda acc, r: acc + r, epi_ins=[(h2, "tile")])
    h4, mlp1_saved = _mlp_fwd("1", h3, g_ffn1, w1_1, w2_1, tm)

    def head(hh, tt, gg):
        def loss_of(h_, g_):
            return 0.5 * jnp.sum(jnp.mean(jnp.square(f_rms(h_, g_) - tt), axis=-1))
        lv, (dh_, dg_) = jax.value_and_grad(loss_of, (0, 1))(hh, gg)
        return (dh_,), (dg_, jnp.full((1, D), lv, F32))

    dh4, dg_fin, loss_row = row_call("loss_head", head, [(h4, D, 0), (tgt, D, 0)], [g_fin], [(D, F32)], [(1, D), (1, D)], tm)

    dh3, dg_ffn1, _, dw1_1, dw2_1 = _mlp_bwd("1", dh4, h3, g_ffn1, w1_1, w2_1, mlp1_saved, tm)
    don = mm("gdn_out_dx", dh3, w_out, "NT", F32)
    dw_out = mm_tn("gdn_out_dw", on, dh3)

    def post_bwd(oo, zz, dd, ng):
        _, vjp = jax.vjp(f_post, oo, zz, ng)
        do_, dz_, dng_ = vjp(dd)
        return (do_, dz_), (dng_,)

    do, dz, dng = row_call("gdn_post_bwd", post_bwd, [(o, D, 0), (pqkvz, D, 3), (don, D, 0)], [gdn_norm_g],
                           [(D, F32), (D, F32)], [(1, HD)], tm)
    du, dw, dqg, dkg, dqk, deg = gdn_scan_bwd(do, gu, gw, gqg, gkg, gqk, geg, sall, S)
    dcpre, dpab, dalog, ddtb = gdn_prep_bwd(cpre, pab, alog_p, dtb_p, gt, du, dw, dqg, dkg, dqk, deg)
    dqkv, dwcv, _ = dwconv_bwd("gdn_conv_bwd", dcpre, pqkvz, 3 * D, wcv_p, S, KSC, HB_SC, lambda v: v, lambda xx, da: da, tm)
    dhn2 = mm("gdn_in_dx_ab", dpab, w_ab, "NT", F32)
    dhn2 = mm("gdn_in_dx_z", dz, w_z, "NT", F32, epi=lambda acc, r: acc + r, epi_ins=[(dhn2, "tile")])
    dhn2 = mm("gdn_in_dx_qkv", dqkv, w_qkv, "NT", F32, epi=lambda acc, r: acc + r, epi_ins=[(dhn2, "tile")])
    dw_in = jnp.concatenate([mm_tn("gdn_in_dw_qkv", hn2, dqkv), mm_tn("gdn_in_dw_z", hn2, dz),
                             mm_tn("gdn_in_dw_ab", hn2, dpab)[:, :2 * H]], axis=1)
    dh2, dg_mix1, _ = _rms_bwd("rms_mix1_bwd", h2, dhn2, dh3, g_mix1, tm)

    dh1, dg_ffn0, db_pw2, dw1_0, dw2_0 = _mlp_bwd("0", dh2, h1, g_ffn0, w1_0, w2_0, mlp0_saved, tm)
    ds = mm("cv_pw2_dx", dh1, w_pw2, "NT", F32)
    dw_pw2 = mm_tn("cv_pw2_dw", s_act, dh1)

    def ln_bwd(xx, dd, gg, bb):
        _, vjp = jax.vjp(f_ln_silu, xx, gg, bb)
        dx_, dg_, db_ = vjp(dd)
        return (dx_,), (dg_, db_, jnp.sum(dx_, axis=0, keepdims=True))

    ddw, dln_g, dln_b, db_dw = row_call("cv_ln_bwd", ln_bwd, [(dwc, D, 0), (ds, D, 0)], [cv_ln_g, cv_ln_b],
                                        [(D, F32)], [(1, D), (1, D), (1, D)], tm)
    du_cv, dwdw, db_pw1 = dwconv_bwd("cv_dwconv_bwd", ddw, u, 2 * D, wdw_p, S, KCV, HB_CV, f_glu, _glu_bwd, tm)
    dhn0 = mm("cv_pw1_dx", du_cv, w_pw1, "NT", F32)
    dw_pw1 = mm_tn("cv_pw1_dw", hn0, du_cv)
    grad_x, dg_mix0, _ = _rms_bwd("rms_mix0_bwd", xf, dhn0, dh1, g_mix0, tm)

    dw_in_sm = jnp.transpose(dw_in.reshape((D, 4, D + 4)), (1, 0, 2))
    g_pw1, g_pw2, g_in, g_out, g_w1, g_w2 = reduce_grads([dw_pw1, dw_pw2, dw_in_sm, dw_out, dw1_0, dw1_1, dw2_0, dw2_1])

    small = jnp.concatenate([
        dg_mix0, dg_mix1, dg_ffn0, dg_ffn1, dg_fin, db_pw1.reshape((2, D)), db_dw, dln_g, dln_b, db_pw2,
        _row1(dalog[:, :H]), _row1(ddtb[:, :H]), _row1(dng), loss_row, jnp.zeros((1, D), F32),
        dwdw, dwcv[:KSC].reshape((3 * KSC, D)), jnp.zeros((NSMALL - 48 - 3 * KSC, D), F32)], axis=0)
    small_all = gather_small(small)

    def pack(a, b, c_, d, e, f, g_, h_, i_, j_, k_):
        return jnp.concatenate([a, b, c_.reshape((1, D)), d.reshape((2, D)), e, f, g_, h_, _row1(i_), _row1(j_), _row1(k_),
                                jnp.zeros((2, D), F32)], axis=0)

    order = lambda p: (p + "norm_mix_g", p + "norm_ffn_g", p + "final_norm_g", p + "cv_b_pw1", p + "cv_b_dw", p + "cv_ln_g",
                       p + "cv_ln_b", p + "cv_b_pw2", p + "gdn_a_log", p + "gdn_dt_bias", p + "gdn_norm_g")
    w16, m16, v16 = (pack(*[env[nm] for nm in order(p)]) for p in ("", "m_", "v_"))

    def small_step(ga, ww, mm_, vv):
        gsum = ga[0]
        for dev in range(1, 8):
            gsum = gsum + ga[dev]
        delta, m2, v2 = f_adamw(ww, gsum[:16], mm_, vv)
        return gsum, delta, m2, v2

    def small_body(ga_ref, w_ref, m_ref, v_ref, g_out, d_out, m_out, v_out):
        gsum, delta, m2, v2 = small_step(ga_ref[...], w_ref[...], m_ref[...], v_ref[...])
        g_out[...] = gsum
        d_out[...] = delta
        m_out[...] = m2
        v_out[...] = v2

    vm = pl.BlockSpec(memory_space=pltpu.VMEM)
    sg, sd, sm, sv = pl.pallas_call(
        small_body, name="adamw_small", in_specs=[vm] * 4, out_specs=[vm] * 4,
        out_shape=[jax.ShapeDtypeStruct((NSMALL, D), F32)] + [jax.ShapeDtypeStruct((16, D), F32)] * 3)(small_all, w16, m16, v16)

    def unpack(b):
        return (b[0:2], b[2:4], b[4], b[5:7].reshape((1, 2 * D)), b[7:8], b[8:9], b[9:10], b[10:11],
                b[11:12, :H], b[12:13, :H], b[13:14, :HD])

    loss = sg[14, 0]
    chip = 2 * lax.axis_index("x") + lax.axis_index("y")
    g_dw = lax.dynamic_slice(sg[16:16 + KCV], (0, chip * (D // 4)), (KCV, D // 4))
    g_cv = lax.dynamic_slice(sg[48:48 + 3 * KSC].reshape((KSC, 3 * D)), (0, chip * (3 * D // 4)), (KSC, 3 * D // 4))

    def adamw(name, w, g, m, v):
        lead = w.shape[:-2]
        if len(lead) == 1 and lead[0] == 1:
            d, m2, v2 = ew_call(name, f_adamw, [w[0], g.reshape(w.shape[1:]), m[0], v[0]], 3)
            return g.reshape(w.shape), d[None], m2[None], v2[None]
        return (g.reshape(w.shape),) + tuple(ew_call(name, f_adamw, [w, g.reshape(w.shape), m, v], 3))

    res = {
        "cv_w_pw1": adamw("adamw_pw1", cv_w_pw1, g_pw1, m_cv_w_pw1, v_cv_w_pw1),
        "cv_w_dw": adamw("adamw_dw", cv_w_dw, g_dw, m_cv_w_dw, v_cv_w_dw),
        "cv_w_pw2": adamw("adamw_pw2", cv_w_pw2, g_pw2, m_cv_w_pw2, v_cv_w_pw2),
        "gdn_w_in": adamw("adamw_win", gdn_w_in, g_in, m_gdn_w_in, v_gdn_w_in),
        "gdn_conv_w": adamw("adamw_cvw", gdn_conv_w, g_cv, m_gdn_conv_w, v_gdn_conv_w),
        "gdn_w_out": adamw("adamw_wout", gdn_w_out, g_out, m_gdn_w_out, v_gdn_w_out),
        "mlp_w1": adamw("adamw_w1", mlp_w1, g_w1, m_mlp_w1, v_mlp_w1),
        "mlp_w2": adamw("adamw_w2", mlp_w2, g_w2, m_mlp_w2, v_mlp_w2),
    }
    names = ("norm_mix_g", "norm_ffn_g", "final_norm_g", "cv_b_pw1", "cv_b_dw", "cv_ln_g", "cv_ln_b", "cv_b_pw2",
             "gdn_a_log", "gdn_dt_bias", "gdn_norm_g")
    for nm, gg, dd, mm_, vv in zip(names, unpack(sg), unpack(sd), unpack(sm), unpack(sv)):
        res[nm] = (gg, dd, mm_, vv)
    weights = ("norm_mix_g", "norm_ffn_g", "final_norm_g", "cv_w_pw1", "cv_b_pw1", "cv_w_dw", "cv_b_dw", "cv_ln_g",
               "cv_ln_b", "cv_w_pw2", "cv_b_pw2", "gdn_w_in", "gdn_conv_w", "gdn_a_log", "gdn_dt_bias", "gdn_norm_g",
               "gdn_w_out", "mlp_w1", "mlp_w2")
    outs = [loss, grad_x.reshape(x.shape)]
    for kind in range(4):
        outs += [res[nm][kind] for nm in weights]
    return tuple(outs)
```

```python
import functools

import jax
import jax.numpy as jnp
from jax import lax
from jax.experimental import pallas as pl
from jax.experimental.pallas import tpu as pltpu

F32, BF16 = jnp.float32, jnp.bfloat16
D = 1024
H = 8
HD = 128
CH = 64
DFF = 4 * D
KCV, HB_CV = 31, 32
KSC, HB_SC = 4, 8
EPS = 1e-6
LR, B1, B2, EPS_A, WD, STEP = 0.001, 0.9, 0.999, 1e-08, 0.01, 10
VMEM_LIMIT = 56 * 1024 * 1024
SUB = 32
NSMALL = 64
MESH = pl.DeviceIdType.MESH


def _cp(*sem):
    return pltpu.CompilerParams(dimension_semantics=sem, vmem_limit_bytes=VMEM_LIMIT)


def f_rms(h, g):
    return h * lax.rsqrt(jnp.mean(h * h, axis=-1, keepdims=True) + EPS) * g


def f_silu(x):
    return x * jax.nn.sigmoid(x)


def f_glu(u):
    return u[:, :D] * jax.nn.sigmoid(u[:, D:])


def f_ln_silu(x, g, b):
    mu = jnp.mean(x, axis=-1, keepdims=True)
    xc = x - mu
    y = xc * lax.rsqrt(jnp.mean(xc * xc, axis=-1, keepdims=True) + EPS)
    return f_silu(y * g + b)


def f_relu2(z):
    r = jnp.maximum(z.astype(F32), 0.0)
    return r * r


def f_post(o, z, ng):
    outs = []
    for h in range(H):
        oh = o[:, h * HD:(h + 1) * HD]
        y = oh * lax.rsqrt(jnp.mean(oh * oh, axis=-1, keepdims=True) + EPS) * ng
        outs.append(y * f_silu(z[:, h * HD:(h + 1) * HD]))
    return jnp.concatenate(outs, axis=1)


def f_adamw(w, g, m, v):
    m2 = B1 * m + (1.0 - B1) * g
    v2 = B2 * v + (1.0 - B2) * (g * g)
    m_hat = m2 / (1.0 - B1 ** STEP)
    v_hat = v2 / (1.0 - B2 ** STEP)
    delta = -LR * (m_hat / (jnp.sqrt(v_hat) + EPS_A) + WD * w)
    return delta, m2, v2


def _dot_raw(a, b, mode):
    dims = {"NN": ((1,), (0,)), "NT": ((1,), (1,)), "TN": ((0,), (0,))}[mode]
    return lax.dot_general(a.astype(BF16), b.astype(BF16), (dims, ((), ())), preferred_element_type=F32)


@functools.partial(jax.custom_vjp, nondiff_argnums=(2,))
def _dot_vjp(a, b, mode):
    return _dot_raw(a, b, mode)


def _dot_fwd(a, b, mode):
    return _dot_raw(a, b, mode), (a, b)


def _dot_bwd(mode, res, dc):
    a, b = res
    if mode == "NN":
        return _dot_vjp(dc, b, "NT"), _dot_vjp(a, dc, "TN")
    if mode == "NT":
        return _dot_vjp(dc, b, "NN"), _dot_vjp(dc, a, "TN")
    return _dot_vjp(b, dc, "NT"), _dot_vjp(a, dc, "NN")


_dot_vjp.defvjp(_dot_fwd, _dot_bwd)


def _dot_split(x, y):
    xh, yh = x.astype(BF16), y.astype(BF16)
    xl, yl = x - xh.astype(F32), y - yh.astype(F32)
    return _dot_raw(xh, yh, "NN") + (_dot_raw(xh, yl, "NN") + _dot_raw(xl, yh, "NN"))


def _tril_inverse(a_list):
    ri = lax.broadcasted_iota(jnp.int32, (CH, CH), 0)
    ci = lax.broadcasted_iota(jnp.int32, (CH, CH), 1)
    ts = [(ri == ci).astype(F32) for _ in a_list]
    for lvl in range(CH.bit_length() - 1):
        same_pair = jnp.right_shift(ri, lvl + 1) == jnp.right_shift(ci, lvl + 1)
        quarter = (jnp.bitwise_and(jnp.right_shift(ri, lvl), 1) == 1) & (jnp.bitwise_and(jnp.right_shift(ci, lvl), 1) == 0)
        mids = [_dot_split(t, jnp.where(same_pair & quarter, a, 0.0)) for t, a in zip(ts, a_list)]
        ts = [t - _dot_split(m, t) for t, m in zip(ts, mids)]
    return ts


@jax.custom_vjp
def _stored_solve(a, t, rhs):
    return _dot_raw(t, rhs, "NN")


def _stored_solve_fwd(a, t, rhs):
    sol = _dot_raw(t, rhs, "NN")
    return sol, (t, sol)


def _stored_solve_bwd(res, g):
    t, sol = res
    g_rhs = _dot_vjp(t, g, "TN")
    return -_dot_vjp(g_rhs, sol, "NT"), jnp.zeros_like(t), g_rhs


_stored_solve.defvjp(_stored_solve_fwd, _stored_solve_bwd)


def _lane_pick(row, idx, width):
    sel = lax.broadcasted_iota(jnp.int32, (1, width), 1) == idx
    return jnp.sum(jnp.where(sel, row, 0.0), axis=1, keepdims=True)


def f_prep(cqs, cks, cvs, araws, braws, alogs, dtbs, t_stored, dot):
    ri = lax.broadcasted_iota(jnp.int32, (CH, CH), 0)
    ci = lax.broadcasted_iota(jnp.int32, (CH, CH), 1)
    eye = (ri == ci).astype(F32)
    low = (ri >= ci).astype(F32)
    last = lax.broadcasted_iota(jnp.int32, (CH, 1), 0) == CH - 1
    nh = range(len(cqs))
    qs, ks, vbs, kbs, gcs, decays = [], [], [], [], [], []
    for h in nh:
        q = f_silu(cqs[h])
        qs.append(q * lax.rsqrt(jnp.sum(q * q, axis=-1, keepdims=True) + 1e-6) * (HD ** -0.5))
        k = f_silu(cks[h])
        k = k * lax.rsqrt(jnp.sum(k * k, axis=-1, keepdims=True) + 1e-6)
        ks.append(k)
        beta = jax.nn.sigmoid(braws[h])
        sp_in = araws[h] + dtbs[h]
        softplus = jnp.maximum(sp_in, 0.0) + jnp.log(1.0 + jnp.exp(-jnp.abs(sp_in)))
        g = -jnp.exp(alogs[h]) * softplus
        g_row = jnp.sum(eye * g, axis=0, keepdims=True)
        gc = jnp.sum(low * g_row, axis=1, keepdims=True)
        gc_row = jnp.sum(eye * gc, axis=0, keepdims=True)
        gcs.append(gc)
        decays.append(jnp.exp(jnp.where(ri >= ci, gc - gc_row, -1e30)))
        vbs.append(f_silu(cvs[h]) * beta)
        kbs.append(k * beta)
    kks = [dot(kbs[h], ks[h], "NT") for h in nh]
    a_list = [jnp.where(ri > ci, kks[h] * decays[h], 0.0) for h in nh]
    if t_stored is None:
        ts = _tril_inverse(a_list)
        solve = lambda h, rhs: dot(ts[h], rhs, "NN")
    else:
        ts = t_stored
        solve = lambda h, rhs: _stored_solve(a_list[h], t_stored[h], rhs)
    egcs = [jnp.exp(gc) for gc in gcs]
    us = [solve(h, vbs[h]) for h in nh]
    ws = [solve(h, kbs[h] * egcs[h]) for h in nh]
    qks = [dot(qs[h], ks[h], "NT") * decays[h] for h in nh]
    qgs = [qs[h] * egcs[h] for h in nh]
    gls = [jnp.sum(jnp.where(last, gc, 0.0), axis=0, keepdims=True) for gc in gcs]
    kgs = [ks[h] * jnp.exp(gls[h] - gcs[h]) for h in nh]
    egs = [jnp.exp(gl) * jnp.ones((1, HD), F32) for gl in gls]
    return us, ws, qks, qgs, kgs, egs, ts


def f_scan(ss, us, ws, qgs, kgs, qks, egs, dot):
    nh = range(len(ss))
    ws_s = [dot(ws[h], ss[h], "NN") for h in nh]
    qs_s = [dot(qgs[h], ss[h], "NN") for h in nh]
    vns = [us[h] - ws_s[h] for h in nh]
    os_ = [qs_s[h] + dot(qks[h], vns[h], "NN") for h in nh]
    s2s = [ss[h] * egs[h] + dot(kgs[h], vns[h], "TN") for h in nh]
    return os_, s2s


def row_call(name, fn, rows, pars, out_rows, out_accs, tm):
    T = rows[0][0].shape[0]
    n_r, n_p, n_o = len(rows), len(pars), len(out_rows)
    in_specs = [pl.BlockSpec((tm, w), functools.partial(lambda i, cb: (i, cb), cb=cb)) for (_, w, cb) in rows]
    in_specs += [pl.BlockSpec(p.shape, functools.partial(lambda i, nd: (0,) * nd, nd=p.ndim)) for p in pars]
    out_specs = [pl.BlockSpec((tm, w), lambda i: (i, 0)) for (w, _) in out_rows]
    out_specs += [pl.BlockSpec(s, lambda i: (0, 0)) for s in out_accs]
    out_shape = [jax.ShapeDtypeStruct((T, w), dt) for (w, dt) in out_rows]
    out_shape += [jax.ShapeDtypeStruct(s, F32) for s in out_accs]

    def body(*refs):
        rin, pin = refs[:n_r], refs[n_r:n_r + n_p]
        rout, aout = refs[n_r + n_p:n_r + n_p + n_o], refs[n_r + n_p + n_o:]
        if aout:
            @pl.when(pl.program_id(0) == 0)
            def _():
                for a in aout:
                    a[...] = jnp.zeros(a.shape, F32)
        pv = [p[...] for p in pin]

        def step(r, carry):
            sl = pl.ds(pl.multiple_of(r * SUB, SUB), SUB)
            outs, accs = fn(*[x[sl, :] for x in rin], *pv)
            for o, val in zip(rout, outs):
                o[sl, :] = val.astype(o.dtype)
            for a, val in zip(aout, accs):
                a[...] += val
            return carry

        lax.fori_loop(0, tm // SUB, step, 0)

    return pl.pallas_call(body, name=name, grid=(T // tm,), in_specs=in_specs, out_specs=out_specs,
                          out_shape=out_shape, compiler_params=_cp("arbitrary"))(*[r[0] for r in rows], *pars)


EW_TILE_ELEMS = 256 * 1024


def _ew_rows(R, Cc):
    if R * Cc <= EW_TILE_ELEMS or R % 8:
        return R
    tr = 8
    while tr * 2 * Cc <= EW_TILE_ELEMS and R % (tr * 2) == 0:
        tr *= 2
    return tr


def ew_call(name, fn, ins, n_out):
    shape = ins[0].shape
    lead = shape[:-2]
    R, Cc = shape[-2:]
    tr = _ew_rows(R, Cc)
    grid = lead + (R // tr,)
    nl = len(lead)
    spec = pl.BlockSpec((None,) * nl + (tr, Cc), lambda *idx: idx + (0,))

    def body(*refs):
        outs = fn(*[r[...] for r in refs[:len(ins)]])
        for o, val in zip(refs[len(ins):], outs):
            o[...] = val

    return pl.pallas_call(body, name=name, grid=grid, in_specs=[spec] * len(ins), out_specs=[spec] * n_out,
                          out_shape=[jax.ShapeDtypeStruct(shape, F32)] * n_out,
                          compiler_params=_cp(*(("arbitrary",) * len(grid))))(*ins)


def mm(name, a, b, mode, out_dtype, a_fn=None, epi=None, epi_ins=(), tm=512):
    M, K = a.shape
    N = b.shape[1] if mode == "NN" else b.shape[0]
    tm = min(tm, M)
    tn = min(N, 1024)
    in_specs = [pl.BlockSpec((tm, K), lambda j, i: (i, 0)),
                pl.BlockSpec((K, tn), lambda j, i: (0, j)) if mode == "NN" else pl.BlockSpec((tn, K), lambda j, i: (j, 0))]
    for (_, kind) in epi_ins:
        in_specs.append(pl.BlockSpec((tm, tn), lambda j, i: (i, j)) if kind == "tile" else pl.BlockSpec((1, tn), lambda j, i: (0, j)))

    def body(a_ref, b_ref, *rest):
        av = a_ref[...]
        if a_fn is not None:
            av = a_fn(av)
        acc = _dot_raw(av, b_ref[...], mode)
        if epi is not None:
            acc = epi(acc, *[r[...] for r in rest[:-1]])
        rest[-1][...] = acc.astype(out_dtype)

    return pl.pallas_call(body, name=name, grid=(N // tn, M // tm), in_specs=in_specs,
                          out_specs=pl.BlockSpec((tm, tn), lambda j, i: (i, j)),
                          out_shape=jax.ShapeDtypeStruct((M, N), out_dtype),
                          compiler_params=_cp("arbitrary", "arbitrary"))(a, b, *[e[0] for e in epi_ins])


def mm_tn(name, a, g, a_fn=None, a_cols=None, tt=512):
    T = a.shape[0]
    ka, acb = (a.shape[1], 0) if a_cols is None else a_cols
    N = g.shape[1]
    tt = min(tt, T)
    tka, tn = min(ka, 1024), min(N, 1024)
    nkb = ka // tka

    def body(a_ref, g_ref, o_ref):
        @pl.when(pl.program_id(2) == 0)
        def _():
            o_ref[...] = jnp.zeros(o_ref.shape, F32)
        av = a_ref[...]
        if a_fn is not None:
            av = a_fn(av)
        o_ref[...] += _dot_raw(av, g_ref[...], "TN")

    return pl.pallas_call(body, name=name, grid=(nkb, N // tn, T // tt),
                          in_specs=[pl.BlockSpec((tt, tka), lambda ia, j, t: (t, acb * nkb + ia)),
                                    pl.BlockSpec((tt, tn), lambda ia, j, t: (t, j))],
                          out_specs=pl.BlockSpec((tka, tn), lambda ia, j, t: (ia, j)),
                          out_shape=jax.ShapeDtypeStruct((ka, N), F32),
                          compiler_params=_cp("arbitrary", "arbitrary", "arbitrary"))(a, g)


def dwconv_fwd(name, x, xw, w_pad, bias, S, K, HB, pre, post, tm):
    T = x.shape[0]
    C = w_pad.shape[1]
    nb, per_seq = tm // HB, S // tm
    has_b, has_post = bias is not None, post is not None

    def body(*refs):
        x_ref, xp_ref, w_ref = refs[:3]
        pos = 3
        b_ref = refs[pos] if has_b else None
        pos += has_b
        ppars = refs[pos:pos + (len(post[1]) if has_post else 0)]
        pos += len(ppars)
        c_ref = refs[pos]
        s_ref = refs[pos + 1] if has_post else None
        ext = refs[-1]
        first = (pl.program_id(0) % per_seq) == 0
        ext[0:HB, :] = jnp.where(first, 0.0, pre(xp_ref[...]))
        for r in range(tm // SUB):
            ext[HB + r * SUB:HB + (r + 1) * SUB, :] = pre(x_ref[r * SUB:(r + 1) * SUB, :])
        pv = [p[...] for p in ppars]
        for r in range(tm // SUB):
            acc = jnp.zeros((SUB, C), F32)
            if has_b:
                acc = acc + b_ref[...]
            for k in range(K):
                off = HB + r * SUB - (K - 1) + k
                acc = acc + w_ref[k:k + 1, :] * ext[off:off + SUB, :]
            c_ref[r * SUB:(r + 1) * SUB, :] = acc
            if has_post:
                s_ref[r * SUB:(r + 1) * SUB, :] = post[0](acc, *pv).astype(BF16)

    ins = [x, x, w_pad] + ([bias] if has_b else []) + (list(post[1]) if has_post else [])
    in_specs = [pl.BlockSpec((tm, xw), lambda i: (i, 0)),
                pl.BlockSpec((HB, xw), lambda i: (jnp.maximum(i * nb - 1, 0), 0)),
                pl.BlockSpec(w_pad.shape, lambda i: (0, 0))]
    in_specs += [pl.BlockSpec(p.shape, lambda i: (0, 0)) for p in ins[3:]]
    out_specs = [pl.BlockSpec((tm, C), lambda i: (i, 0))] * (1 + has_post)
    out_shape = [jax.ShapeDtypeStruct((T, C), F32)] + ([jax.ShapeDtypeStruct((T, C), BF16)] if has_post else [])
    return pl.pallas_call(body, name=name, grid=(T // tm,), in_specs=in_specs, out_specs=out_specs, out_shape=out_shape,
                          scratch_shapes=[pltpu.VMEM((HB + tm, C), F32)], compiler_params=_cp("arbitrary"))(*ins)


def dwconv_bwd(name, g, x, xw, w_pad, S, K, HB, pre, pre_bwd, tm):
    T = g.shape[0]
    C = w_pad.shape[1]
    nb, per_seq = tm // HB, S // tm
    nblk = T // HB

    def body(g_ref, gn_ref, x_ref, xp_ref, w_ref, dx_ref, dw_ref, dbx_ref, extg, exta):
        i = pl.program_id(0)
        first = (i % per_seq) == 0
        last = (i % per_seq) == per_seq - 1

        @pl.when(i == 0)
        def _():
            dw_ref[...] = jnp.zeros(dw_ref.shape, F32)
            dbx_ref[...] = jnp.zeros(dbx_ref.shape, F32)

        extg[tm:tm + HB, :] = jnp.where(last, 0.0, gn_ref[...])
        exta[0:HB, :] = jnp.where(first, 0.0, pre(xp_ref[...]))
        for r in range(tm // SUB):
            extg[r * SUB:(r + 1) * SUB, :] = g_ref[r * SUB:(r + 1) * SUB, :]
            exta[HB + r * SUB:HB + (r + 1) * SUB, :] = pre(x_ref[r * SUB:(r + 1) * SUB, :])
        for r in range(tm // SUB):
            acc = jnp.zeros((SUB, C), F32)
            for k in range(K):
                off = r * SUB + (K - 1) - k
                acc = acc + w_ref[k:k + 1, :] * extg[off:off + SUB, :]
            dx = pre_bwd(x_ref[r * SUB:(r + 1) * SUB, :], acc)
            dx_ref[r * SUB:(r + 1) * SUB, :] = dx
            dbx_ref[...] += jnp.sum(dx, axis=0, keepdims=True)
        for k in range(K):
            p = jnp.zeros((SUB, C), F32)
            for r in range(tm // SUB):
                off = HB + r * SUB - (K - 1) + k
                p = p + extg[r * SUB:(r + 1) * SUB, :] * exta[off:off + SUB, :]
            dw_ref[k:k + 1, :] += jnp.sum(p, axis=0, keepdims=True)

    in_specs = [pl.BlockSpec((tm, C), lambda i: (i, 0)),
                pl.BlockSpec((HB, C), lambda i: (jnp.minimum((i + 1) * nb, nblk - 1), 0)),
                pl.BlockSpec((tm, xw), lambda i: (i, 0)),
                pl.BlockSpec((HB, xw), lambda i: (jnp.maximum(i * nb - 1, 0), 0)),
                pl.BlockSpec(w_pad.shape, lambda i: (0, 0))]
    out_specs = [pl.BlockSpec((tm, xw), lambda i: (i, 0)), pl.BlockSpec((HB, C), lambda i: (0, 0)),
                 pl.BlockSpec((1, xw), lambda i: (0, 0))]
    out_shape = [jax.ShapeDtypeStruct((T, xw), F32), jax.ShapeDtypeStruct((HB, C), F32), jax.ShapeDtypeStruct((1, xw), F32)]
    return pl.pallas_call(body, name=name, grid=(T // tm,), in_specs=in_specs, out_specs=out_specs, out_shape=out_shape,
                          scratch_shapes=[pltpu.VMEM((tm + HB, C), F32), pltpu.VMEM((HB + tm, C), F32)],
                          compiler_params=_cp("arbitrary"))(g, g, x, x, w_pad)


def _glu_bwd(u, da):
    u1, sg = u[:, :D], jax.nn.sigmoid(u[:, D:])
    return jnp.concatenate([da * sg, da * u1 * sg * (1.0 - sg)], axis=1)


def _head_cols(ref, h, base=0):
    return ref[:, base + h * HD:base + (h + 1) * HD]


def _prep_inputs(c_ref, ab, al, dt):
    hs = range(H)
    return ([_head_cols(c_ref, h) for h in hs], [_head_cols(c_ref, h, D) for h in hs],
            [_head_cols(c_ref, h, 2 * D) for h in hs], [_lane_pick(ab, h, HD) for h in hs],
            [_lane_pick(ab, H + h, HD) for h in hs], [_lane_pick(al, h, HD) for h in hs],
            [_lane_pick(dt, h, HD) for h in hs])


def gdn_prep_fwd(cpre, pab, alog, dtb):
    T = cpre.shape[0]
    nc = T // CH

    def body(c_ref, ab_ref, al_ref, dt_ref, u_ref, w_ref, qg_ref, kg_ref, qk_ref, t_ref, eg_ref):
        us, ws, qks, qgs, kgs, egs, ts = f_prep(*_prep_inputs(c_ref, ab_ref[...], al_ref[...], dt_ref[...]), None, _dot_raw)
        for h in range(H):
            cols = slice(h * HD, (h + 1) * HD)
            u_ref[:, cols] = us[h]
            w_ref[:, cols] = ws[h].astype(BF16)
            qg_ref[:, cols] = qgs[h].astype(BF16)
            kg_ref[:, cols] = kgs[h].astype(BF16)
            qk_ref[0, h] = qks[h].astype(BF16)
            t_ref[0, h] = ts[h].astype(BF16)
            eg_ref[0, h:h + 1, :] = egs[h]

    row = lambda w: pl.BlockSpec((CH, w), lambda n: (n, 0))
    par = pl.BlockSpec((1, HD), lambda n: (0, 0))
    mat = pl.BlockSpec((1, H, CH, CH), lambda n: (n, 0, 0, 0))
    return pl.pallas_call(
        body, name="gdn_prep_fwd", grid=(nc,), in_specs=[row(3 * D), row(HD), par, par],
        out_specs=[row(D), row(D), row(D), row(D), mat, mat, pl.BlockSpec((1, H, HD), lambda n: (n, 0, 0))],
        out_shape=[jax.ShapeDtypeStruct((T, D), F32)] + [jax.ShapeDtypeStruct((T, D), BF16)] * 3
        + [jax.ShapeDtypeStruct((nc, H, CH, CH), BF16)] * 2 + [jax.ShapeDtypeStruct((nc, H, HD), F32)],
        compiler_params=_cp("arbitrary"))(cpre, pab, alog, dtb)


def gdn_prep_bwd(cpre, pab, alog, dtb, tmat, du, dw, dqg, dkg, dqk, deg):
    T = cpre.shape[0]
    nc = T // CH

    def body(c_ref, ab_ref, al_ref, dt_ref, t_ref, du_ref, dw_ref, dqg_ref, dkg_ref, dqk_ref, deg_ref,
             dc_ref, dab_ref, dal_ref, ddt_ref):
        @pl.when(pl.program_id(0) == 0)
        def _():
            dal_ref[...] = jnp.zeros(dal_ref.shape, F32)
            ddt_ref[...] = jnp.zeros(ddt_ref.shape, F32)

        lane = lax.broadcasted_iota(jnp.int32, (1, HD), 1)
        dab = jnp.zeros((CH, HD), F32)
        dal = jnp.zeros((1, HD), F32)
        ddt = jnp.zeros((1, HD), F32)
        hs = range(H)
        t_st = [t_ref[0, h].astype(F32) for h in hs]

        def fwd(*args):
            return tuple(f_prep(*args, t_st, _dot_vjp)[:6])

        _, vjp = jax.vjp(fwd, *_prep_inputs(c_ref, ab_ref[...], al_ref[...], dt_ref[...]))
        dcqs, dcks, dcvs, dars, dbrs, dals, ddts = vjp((
            [_head_cols(du_ref, h) for h in hs], [_head_cols(dw_ref, h) for h in hs], [dqk_ref[0, h] for h in hs],
            [_head_cols(dqg_ref, h) for h in hs], [_head_cols(dkg_ref, h) for h in hs],
            [deg_ref[0, h:h + 1, :] for h in hs]))
        for h in hs:
            dc_ref[:, h * HD:(h + 1) * HD] = dcqs[h]
            dc_ref[:, D + h * HD:D + (h + 1) * HD] = dcks[h]
            dc_ref[:, 2 * D + h * HD:2 * D + (h + 1) * HD] = dcvs[h]
            dab = dab + jnp.where(lane == h, dars[h], 0.0) + jnp.where(lane == H + h, dbrs[h], 0.0)
            dal = dal + jnp.where(lane == h, dals[h], 0.0)
            ddt = ddt + jnp.where(lane == h, ddts[h], 0.0)
        dab_ref[...] = dab
        dal_ref[...] += dal
        ddt_ref[...] += ddt

    row = lambda w: pl.BlockSpec((CH, w), lambda n: (n, 0))
    par = pl.BlockSpec((1, HD), lambda n: (0, 0))
    mat = pl.BlockSpec((1, H, CH, CH), lambda n: (n, 0, 0, 0))
    vec = pl.BlockSpec((1, H, HD), lambda n: (n, 0, 0))
    return pl.pallas_call(
        body, name="gdn_prep_bwd", grid=(nc,),
        in_specs=[row(3 * D), row(HD), par, par, mat, row(D), row(D), row(D), row(D), mat, vec],
        out_specs=[row(3 * D), row(HD), par, par],
        out_shape=[jax.ShapeDtypeStruct((T, 3 * D), F32), jax.ShapeDtypeStruct((T, HD), F32),
                   jax.ShapeDtypeStruct((1, HD), F32), jax.ShapeDtypeStruct((1, HD), F32)],
        compiler_params=_cp("arbitrary"))(cpre, pab, alog, dtb, tmat, du, dw, dqg, dkg, dqk, deg)


def gdn_scan_fwd(u, w, qg, kg, qk, eg, S):
    T = u.shape[0]
    nc, per_seq = T // CH, S // CH

    def body(u_ref, w_ref, qg_ref, kg_ref, qk_ref, eg_ref, o_ref, sall_ref, s_ref):
        @pl.when(pl.program_id(0) % per_seq == 0)
        def _():
            s_ref[...] = jnp.zeros(s_ref.shape, F32)

        hs = range(H)
        ss = [s_ref[h] for h in hs]
        os_, s2s = f_scan(ss, [_head_cols(u_ref, h) for h in hs], [_head_cols(w_ref, h) for h in hs],
                          [_head_cols(qg_ref, h) for h in hs], [_head_cols(kg_ref, h) for h in hs],
                          [qk_ref[0, h] for h in hs], [eg_ref[0, h:h + 1, :] for h in hs], _dot_raw)
        for h in hs:
            sall_ref[0, h] = ss[h]
            o_ref[:, h * HD:(h + 1) * HD] = os_[h]
            s_ref[h] = s2s[h]

    row = pl.BlockSpec((CH, D), lambda n: (n, 0))
    return pl.pallas_call(
        body, name="gdn_scan_fwd", grid=(nc,),
        in_specs=[row, row, row, row, pl.BlockSpec((1, H, CH, CH), lambda n: (n, 0, 0, 0)),
                  pl.BlockSpec((1, H, HD), lambda n: (n, 0, 0))],
        out_specs=[row, pl.BlockSpec((1, H, HD, HD), lambda n: (n, 0, 0, 0))],
        out_shape=[jax.ShapeDtypeStruct((T, D), F32), jax.ShapeDtypeStruct((nc, H, HD, HD), F32)],
        scratch_shapes=[pltpu.VMEM((H, HD, HD), F32)], compiler_params=_cp("arbitrary"))(u, w, qg, kg, qk, eg)


def gdn_scan_bwd(do, u, w, qg, kg, qk, eg, sall, S):
    T = u.shape[0]
    nc, per_seq = T // CH, S // CH

    def body(do_ref, u_ref, w_ref, qg_ref, kg_ref, qk_ref, eg_ref, sall_ref,
             du_ref, dw_ref, dqg_ref, dkg_ref, dqk_ref, deg_ref, ds_ref):
        n = nc - 1 - pl.program_id(0)

        @pl.when(n % per_seq == per_seq - 1)
        def _():
            ds_ref[...] = jnp.zeros(ds_ref.shape, F32)

        hs = range(H)

        def fwd(*args):
            return f_scan(*args, _dot_vjp)

        _, vjp = jax.vjp(fwd, [sall_ref[0, h] for h in hs], [_head_cols(u_ref, h) for h in hs],
                         [_head_cols(w_ref, h).astype(F32) for h in hs], [_head_cols(qg_ref, h).astype(F32) for h in hs],
                         [_head_cols(kg_ref, h).astype(F32) for h in hs], [qk_ref[0, h].astype(F32) for h in hs],
                         [eg_ref[0, h:h + 1, :] for h in hs])
        dss, dus, dws, dqgs, dkgs, dqks, degs = vjp(([_head_cols(do_ref, h) for h in hs], [ds_ref[h] for h in hs]))
        for h in hs:
            cols = slice(h * HD, (h + 1) * HD)
            du_ref[:, cols] = dus[h]
            dw_ref[:, cols] = dws[h]
            dqg_ref[:, cols] = dqgs[h]
            dkg_ref[:, cols] = dkgs[h]
            dqk_ref[0, h] = dqks[h]
            deg_ref[0, h:h + 1, :] = degs[h]
            ds_ref[h] = dss[h]

    rev = lambda n: (nc - 1 - n, 0)
    row = pl.BlockSpec((CH, D), rev)
    mat = pl.BlockSpec((1, H, CH, CH), lambda n: (nc - 1 - n, 0, 0, 0))
    vec = pl.BlockSpec((1, H, HD), lambda n: (nc - 1 - n, 0, 0))
    return pl.pallas_call(
        body, name="gdn_scan_bwd", grid=(nc,),
        in_specs=[row, row, row, row, row, mat, vec, pl.BlockSpec((1, H, HD, HD), lambda n: (nc - 1 - n, 0, 0, 0))],
        out_specs=[row, row, row, row, mat, vec],
        out_shape=[jax.ShapeDtypeStruct((T, D), F32)] * 4
        + [jax.ShapeDtypeStruct((nc, H, CH, CH), F32), jax.ShapeDtypeStruct((nc, H, HD), F32)],
        scratch_shapes=[pltpu.VMEM((H, HD, HD), F32)], compiler_params=_cp("arbitrary"))(do, u, w, qg, kg, qk, eg, sall)


def xor_exchange(name, ins, inplace, out_shapes, plan, n_remote, n_local=0):
    n_in, n_ip, n_out = len(ins), len(inplace), len(out_shapes)

    def body(*refs):
        in_refs = refs[:n_in]
        ip_refs = refs[n_in + n_ip:n_in + 2 * n_ip]
        out_refs = refs[n_in + 2 * n_ip:n_in + 2 * n_ip + n_out]
        send_sems, recv_sems, loc_sems = refs[n_in + 2 * n_ip + n_out:]
        x, y, c = lax.axis_index("x"), lax.axis_index("y"), lax.axis_index("c")
        remote, local = plan(in_refs, ip_refs, out_refs, (x, y, c))
        assert len(remote) == n_remote and len(local) == n_local
        copies = []
        for k, ((dx, dy, dc), src, dst) in enumerate(remote):
            peer = (1 - x if dx else x, 1 - y if dy else y, 1 - c if dc else c)
            copies.append(pltpu.make_async_remote_copy(src_ref=src, dst_ref=dst, send_sem=send_sems.at[k],
                                                       recv_sem=recv_sems.at[k], device_id=peer, device_id_type=MESH))
        for cp in copies:
            cp.start()
        locs = [pltpu.make_async_copy(src, dst, loc_sems.at[k]) for k, (src, dst) in enumerate(local)]
        for cp in locs:
            cp.start()
        for cp in copies:
            cp.wait()
        for cp in locs:
            cp.wait()

    anyspec = pl.BlockSpec(memory_space=pl.ANY)
    res = pl.pallas_call(
        body, name=name, in_specs=[anyspec] * (n_in + n_ip), out_specs=[anyspec] * (n_ip + n_out),
        out_shape=[jax.ShapeDtypeStruct(a.shape, a.dtype) for a in inplace] + list(out_shapes),
        input_output_aliases={n_in + i: i for i in range(n_ip)},
        scratch_shapes=[pltpu.SemaphoreType.DMA((n_remote,)), pltpu.SemaphoreType.DMA((n_remote,)),
                        pltpu.SemaphoreType.DMA((max(n_local, 1),))],
        )(*ins, *inplace)
    return list(res[:n_ip]), list(res[n_ip:])


class WSpec:
    def __init__(self, name, full, sa, ha, group, layer=None, lead=False):
        self.name, self.full, self.sa, self.ha, self.group, self.layer, self.lead = name, full, sa, ha, group, layer, lead
        self.ws = 1 if lead else full[sa] // 4
        self.wh = full[ha] // 2

    def shard_shape(self):
        if self.lead:
            return tuple(n for a, n in enumerate(self.full) if a != self.sa)
        return tuple(self.ws if a == self.sa else n for a, n in enumerate(self.full))

    def half_full_shape(self):
        return tuple(self.wh if a == self.ha else n for a, n in enumerate(self.full))

    def shard_half_shape(self):
        s = list(self.half_full_shape())
        if self.lead:
            del s[self.sa]
        else:
            s[self.sa] = self.ws
        return tuple(s)

    def full_view(self, ref, q=None, h=None):
        idx = []
        for a in range(len(self.full)):
            if a == self.sa and q is not None:
                idx.append(q if self.lead else pl.ds(pl.multiple_of(q * self.ws, self.ws), self.ws))
            elif a == self.ha and h is not None:
                idx.append(pl.ds(pl.multiple_of(h * self.wh, self.wh), self.wh))
            else:
                idx.append(slice(None))
        return ref.at[tuple(idx)]

    def shard_view(self, ref, h):
        idx = [] if self.layer is None else [self.layer]
        for a in range(len(self.full)):
            if self.lead and a == self.sa:
                continue
            idx.append(pl.ds(pl.multiple_of(h * self.wh, self.wh), self.wh) if a == self.ha else slice(None))
        return ref.at[tuple(idx)]

    def rows_cols(self, shard, half):
        rows, cols = self.full[-2:]
        if shard and not self.lead:
            rows, cols = (rows // 4, cols) if self.sa == 0 else (rows, cols // 4)
        if half:
            rows, cols = (rows // 2, cols) if self.ha == len(self.full) - 2 else (rows, cols // 2)
        return rows, cols

    def spec(self, tr, cw, nr, shard=False, half=False, has_lead=False, stacked=False):
        two_d = len(self.full) == 2
        shard_on_cols = two_d and self.sa == 1
        half_on_cols = two_d and self.ha == 1
        layer = self.layer

        def index(*args):
            pref = args[-1]
            i = args[-2]
            r, cblk, pre = i, 0, ()
            if shard:
                if self.lead:
                    pre = (pref[0],)
                elif shard_on_cols:
                    cblk = pref[0]
                else:
                    r = pref[0] * nr + i
            elif has_lead:
                pre = (args[0],)
            if half:
                if half_on_cols:
                    cblk = pref[1]
                else:
                    r = pref[1] * nr + i
            if stacked:
                pre = (layer,) + pre
            return pre + (r, cblk)

        n_pre = int(stacked) + int(self.lead and (shard or has_lead))
        return pl.BlockSpec((None,) * n_pre + (tr, cw), index)


WSPECS = [
    WSpec("cv_w_pw1", (D, 2 * D), 1, 0, 0),
    WSpec("cv_w_pw2", (D, D), 0, 1, 1),
    WSpec("gdn_w_in", (4, D, (4 * D + 2 * H) // 4), 0, 1, 2, lead=True),
    WSpec("gdn_w_out", (D, D), 0, 1, 3),
    WSpec("mlp_w1_0", (D, DFF), 1, 0, 4, layer=0),
    WSpec("mlp_w1_1", (D, DFF), 1, 0, 4, layer=1),
    WSpec("mlp_w2_0", (DFF, D), 0, 1, 5, layer=0),
    WSpec("mlp_w2_1", (DFF, D), 0, 1, 5, layer=1),
]
FLIPS = [(1, 0, 0), (0, 1, 0), (1, 1, 0)]
SIB = (0, 0, 1)


def _chip(x, y):
    return 2 * x + y


def _prefetch_call(name, body, grid, in_specs, out_specs, out_shape, pref, args, aliases=None):
    return pl.pallas_call(
        body, name=name, out_shape=out_shape, input_output_aliases=aliases or {},
        grid_spec=pltpu.PrefetchScalarGridSpec(num_scalar_prefetch=1, grid=grid, in_specs=in_specs, out_specs=out_specs),
        compiler_params=_cp(*(("arbitrary",) * len(grid))))(pref, *args)


def place_shard(ws, shard, pref):
    rows, cols = ws.rows_cols(True, False)
    tr = _ew_rows(rows, cols)
    nr = rows // tr
    stacked = ws.layer is not None
    layer = ws.layer

    def body(_, s_ref, o_ref):
        o_ref[...] = s_ref[...].astype(BF16)

    in_spec = pl.BlockSpec(((None,) if stacked else ()) + (tr, cols),
                           (lambda i, p: (layer, i, 0)) if stacked else (lambda i, p: (i, 0)))
    return _prefetch_call("place_" + ws.name, body, (nr,), [in_spec], ws.spec(tr, cols, nr, shard=True),
                          jax.ShapeDtypeStruct(ws.full, BF16), pref, [shard])


def gather_weights(shards, wdw_shard, wcv_shard, pref):
    n = len(WSPECS)
    placed = [place_shard(ws, shards[ws.group], pref) for ws in WSPECS]

    def plan_a(in_refs, ip_refs, out_refs, pos):
        x, y, c = pos
        q = _chip(x, y)
        remote, local = [], []
        for i, ws in enumerate(WSPECS):
            mine = ws.full_view(ip_refs[i], q, c)
            for f in FLIPS:
                remote.append((f, mine, mine))
        for j, width in enumerate((D // 4, 3 * D // 4)):
            dst = out_refs[j].at[:, pl.ds(pl.multiple_of(q * width, 128), width)]
            local.append((in_refs[j], dst))
            for f in FLIPS:
                remote.append((f, in_refs[j], dst))
        return remote, local

    taps = [jax.ShapeDtypeStruct((KCV, D), F32), jax.ShapeDtypeStruct((KSC, 3 * D), F32)]
    nat, (wdw, wcv) = xor_exchange("gather_chips", [wdw_shard, wcv_shard], placed, taps, plan_a, 3 * (n + 2), 2)

    def plan_b(in_refs, ip_refs, out_refs, pos):
        x, y, c = pos
        remote = []
        for i, ws in enumerate(WSPECS):
            for (dx, dy, _) in FLIPS:
                got = ws.full_view(ip_refs[i], _chip(1 - x if dx else x, 1 - y if dy else y), c)
                remote.append((SIB, got, got))
        return remote, []

    nat, _ = xor_exchange("gather_cores", [], nat, [], plan_b, 3 * n)
    return nat, wdw, wcv


def half_add(ws, g, rsib, pref):
    rows, cols = ws.rows_cols(False, True)
    tr = _ew_rows(rows, cols)
    nr = rows // tr

    def body(_, a_ref, b_ref, o_ref):
        o_ref[...] = (a_ref[...] + b_ref[...]).astype(BF16)

    whole = ws.spec(tr, cols, nr, has_lead=ws.lead)
    return _prefetch_call("reduce_add_" + ws.name, body, (4, nr) if ws.lead else (nr,),
                          [ws.spec(tr, cols, nr, half=True, has_lead=ws.lead), whole], whole,
                          jax.ShapeDtypeStruct(ws.half_full_shape(), BF16), pref, [g, rsib])


def shard_sum(ws, s, parts, buf, pref):
    rows, cols = ws.rows_cols(True, True)
    tr = _ew_rows(rows, cols)
    nr = rows // tr
    stacked = ws.layer is not None

    def body(_, s_ref, p_ref, *rest):
        rest[-1][...] = ((s_ref[...].astype(F32) + p_ref[0].astype(F32)) + p_ref[1].astype(F32)) + p_ref[2].astype(F32)

    in_specs = [ws.spec(tr, cols, nr, shard=True, has_lead=ws.lead), pl.BlockSpec((3, tr, cols), lambda i, p: (0, i, 0))]
    args, aliases = [s, parts], {}
    if buf is not None:
        in_specs.append(pl.BlockSpec(memory_space=pl.ANY))
        args.append(buf)
        aliases = {3: 0}
    shape = ((2,) if stacked else ()) + ws.shard_shape()
    return _prefetch_call("reduce_sum_" + ws.name, body, (nr,), in_specs, ws.spec(tr, cols, nr, half=True, stacked=stacked),
                          jax.ShapeDtypeStruct(shape, F32), pref, args, aliases)


def reduce_grads(grads, pref):
    n = len(WSPECS)

    def plan1(in_refs, ip_refs, out_refs, pos):
        c = pos[2]
        return [(SIB, ws.full_view(in_refs[i], None, 1 - c), out_refs[i]) for i, ws in enumerate(WSPECS)], []

    halves = [jax.ShapeDtypeStruct(ws.half_full_shape(), F32) for ws in WSPECS]
    _, rsib = xor_exchange("reduce_cores", grads, [], halves, plan1, n)
    sums = [half_add(ws, grads[i], rsib[i], pref) for i, ws in enumerate(WSPECS)]

    def plan2(in_refs, ip_refs, out_refs, pos):
        x, y, c = pos
        remote = []
        for i, ws in enumerate(WSPECS):
            for s, (dx, dy, _) in enumerate(FLIPS):
                qq = _chip(1 - x if dx else x, 1 - y if dy else y)
                remote.append(((dx, dy, 0), ws.full_view(in_refs[i], qq), out_refs[i].at[s]))
        return remote, []

    parts = [jax.ShapeDtypeStruct((3,) + ws.shard_half_shape(), BF16) for ws in WSPECS]
    _, got = xor_exchange("reduce_chips", sums, [], parts, plan2, 3 * n)
    bufs = {}
    for i, ws in enumerate(WSPECS):
        bufs[ws.group] = shard_sum(ws, sums[i], got[i], bufs.get(ws.group), pref)

    def plan3(in_refs, ip_refs, out_refs, pos):
        c = pos[2]
        remote = []
        for ws in WSPECS:
            mine = ws.shard_view(ip_refs[ws.group], c)
            remote.append((SIB, mine, mine))
        return remote, []

    return xor_exchange("reduce_swap", [], [bufs[g] for g in sorted(bufs)], [], plan3, n)[0]


def gather_small(buf):
    flips = [(dx, dy, dc) for dx in (0, 1) for dy in (0, 1) for dc in (0, 1)][1:]

    def plan(in_refs, ip_refs, out_refs, pos):
        x, y, c = pos
        me = 4 * x + 2 * y + c
        dst = out_refs[0].at[me]
        return [(f, in_refs[0], dst) for f in flips], [(in_refs[0], dst)]

    return xor_exchange("gather_small", [buf], [], [jax.ShapeDtypeStruct((8,) + buf.shape, F32)], plan, 7, 1)[1][0]


def _pad_rows(a, rows):
    return jnp.pad(a, ((0, rows - a.shape[0]), (0, 0)))


def _row1(v):
    v = v.reshape((1, -1))
    return jnp.pad(v, ((0, 0), (0, D - v.shape[1])))


def _rms_fwd(name, h, g, tm):
    return row_call(name, lambda hh, gg: ((f_rms(hh, gg),), ()), [(h, D, 0)], [g], [(D, BF16)], [], tm)[0]


def _rms_bwd(name, h, dhn, dres, g, tm):
    def fn(hh, dd, rr, gg):
        _, vjp = jax.vjp(f_rms, hh, gg)
        dh, dg = vjp(dd)
        dh = dh + rr
        return (dh,), (dg, jnp.sum(dh, axis=0, keepdims=True))

    return row_call(name, fn, [(h, D, 0), (dhn, D, 0), (dres, D, 0)], [g], [(D, F32)], [(1, D), (1, D)], tm)


def _mlp_fwd(tag, h, g, w1, w2, tm):
    hn = _rms_fwd("rms_ffn" + tag, h, g, tm)
    z1 = mm("mlp_up" + tag, hn, w1, "NN", BF16)
    h2 = mm("mlp_down" + tag, z1, w2, "NN", F32, a_fn=f_relu2, epi=lambda acc, r: acc + r, epi_ins=[(h, "tile")])
    return h2, (hn, z1)


def _mlp_bwd(tag, dh, h, g, w1, w2, saved, tm):
    hn, z1 = saved
    dz1 = mm("mlp_down_dx" + tag, dh, w2, "NT", BF16,
             epi=lambda acc, z: acc * (2.0 * jnp.maximum(z.astype(F32), 0.0)), epi_ins=[(z1, "tile")])
    dw2 = mm_tn("mlp_down_dw" + tag, z1, dh, a_fn=f_relu2)
    dhn = mm("mlp_up_dx" + tag, dz1, w1, "NT", F32)
    dw1 = mm_tn("mlp_up_dw" + tag, hn, dz1)
    dh_in, dg, colsum = _rms_bwd("rms_ffn_bwd" + tag, h, dhn, dh, g, tm)
    return dh_in, dg, colsum, dw1, dw2


def kernel(x, norm_mix_g, norm_ffn_g, final_norm_g, cv_w_pw1, cv_b_pw1, cv_w_dw, cv_b_dw, cv_ln_g, cv_ln_b, cv_w_pw2, cv_b_pw2, gdn_w_in, gdn_conv_w, gdn_a_log, gdn_dt_bias, gdn_norm_g, gdn_w_out, mlp_w1, mlp_w2, loss_target, m_norm_mix_g, m_norm_ffn_g, m_final_norm_g, m_cv_w_pw1, m_cv_b_pw1, m_cv_w_dw, m_cv_b_dw, m_cv_ln_g, m_cv_ln_b, m_cv_w_pw2, m_cv_b_pw2, m_gdn_w_in, m_gdn_conv_w, m_gdn_a_log, m_gdn_dt_bias, m_gdn_norm_g, m_gdn_w_out, m_mlp_w1, m_mlp_w2, v_norm_mix_g, v_norm_ffn_g, v_final_norm_g, v_cv_w_pw1, v_cv_b_pw1, v_cv_w_dw, v_cv_b_dw, v_cv_ln_g, v_cv_ln_b, v_cv_w_pw2, v_cv_b_pw2, v_gdn_w_in, v_gdn_conv_w, v_gdn_a_log, v_gdn_dt_bias, v_gdn_norm_g, v_gdn_w_out, v_mlp_w1, v_mlp_w2):
    env = dict(locals())
    bl, S, _ = x.shape
    T = bl * S
    tm = min(256, S)
    xf = x.reshape((T, D))
    tgt = loss_target.reshape((T, D))

    chip = 2 * lax.axis_index("x") + lax.axis_index("y")
    pref = jnp.stack([chip, lax.axis_index("c")]).astype(jnp.int32)
    big = [cv_w_pw1[0], cv_w_pw2[0], gdn_w_in[0], gdn_w_out[0], mlp_w1, mlp_w2]
    nat, wdw, wcv = gather_weights(big, cv_w_dw[0], gdn_conv_w[0], pref)
    w_pw1, w_pw2, w_in_sm, w_out, w1_0, w1_1, w2_0, w2_1 = nat
    w_in = jnp.transpose(w_in_sm, (1, 0, 2)).reshape((D, 4 * D + 2 * H))
    w_qkv, w_z = w_in[:, :3 * D], w_in[:, 3 * D:4 * D]
    w_qkvz = w_in[:, :4 * D]
    w_ab = jnp.pad(w_in[:, 4 * D:], ((0, 0), (0, HD - 2 * H)))
    wdw_p, wcv_p = _pad_rows(wdw, HB_CV), _pad_rows(wcv, HB_SC)
    alog_p = jnp.pad(gdn_a_log, ((0, 0), (0, HD - H)))
    dtb_p = jnp.pad(gdn_dt_bias, ((0, 0), (0, HD - H)))
    g_mix0, g_mix1 = norm_mix_g[0:1], norm_mix_g[1:2]
    g_ffn0, g_ffn1 = norm_ffn_g[0:1], norm_ffn_g[1:2]
    g_fin = final_norm_g.reshape((1, D))

    hn0 = _rms_fwd("rms_mix0", xf, g_mix0, tm)
    u = mm("cv_pw1", hn0, w_pw1, "NN", F32, epi=lambda acc, b: acc + b, epi_ins=[(cv_b_pw1, "row")])
    dwc, s_act = dwconv_fwd("cv_dwconv", u, 2 * D, wdw_p, cv_b_dw, S, KCV, HB_CV, f_glu, (f_ln_silu, (cv_ln_g, cv_ln_b)), tm)
    h1 = mm("cv_pw2", s_act, w_pw2, "NN", F32, epi=lambda acc, b, r: acc + b + r,
            epi_ins=[(cv_b_pw2, "row"), (xf, "tile")])
    h2, mlp0_saved = _mlp_fwd("0", h1, g_ffn0, w1_0, w2_0, tm)

    hn2 = _rms_fwd("rms_mix1", h2, g_mix1, tm)
    pqkvz = mm("gdn_in", hn2, w_qkvz, "NN", F32)
    pab = mm("gdn_in_ab", hn2, w_ab, "NN", F32)
    cpre = dwconv_fwd("gdn_conv", pqkvz, 3 * D, wcv_p, None, S, KSC, HB_SC, lambda v: v, None, tm)[0]
    gu, gw, gqg, gkg, gqk, gt, geg = gdn_prep_fwd(cpre, pab, alog_p, dtb_p)
    o, sall = gdn_scan_fwd(gu, gw, gqg, gkg, gqk, geg, S)
    on = row_call("gdn_post", lambda oo, zz, ng: ((f_post(oo, zz, ng),), ()), [(o, D, 0), (pqkvz, D, 3)],
                  [gdn_norm_g], [(D, BF16)], [], tm)[0]
    h3 = mm("gdn_out", on, w_out, "NN", F32, epi=lambda acc, r: acc + r, epi_ins=[(h2, "tile")])
    h4, mlp1_saved = _mlp_fwd("1", h3, g_ffn1, w1_1, w2_1, tm)

    def head(hh, tt, gg):
        def loss_of(h_, g_):
            return 0.5 * jnp.sum(jnp.mean(jnp.square(f_rms(h_, g_) - tt), axis=-1))
        lv, (dh_, dg_) = jax.value_and_grad(loss_of, (0, 1))(hh, gg)
        return (dh_,), (dg_, jnp.full((1, D), lv, F32))

    dh4, dg_fin, loss_row = row_call("loss_head", head, [(h4, D, 0), (tgt, D, 0)], [g_fin], [(D, F32)], [(1, D), (1, D)], tm)

    dh3, dg_ffn1, _, dw1_1, dw2_1 = _mlp_bwd("1", dh4, h3, g_ffn1, w1_1, w2_1, mlp1_saved, tm)
    don = mm("gdn_out_dx", dh3, w_out, "NT", F32)
    dw_out = mm_tn("gdn_out_dw", on, dh3)

    def post_bwd(oo, zz, dd, ng):
        _, vjp = jax.vjp(f_post, oo, zz, ng)
        do_, dz_, dng_ = vjp(dd)
        return (do_, dz_), (dng_,)

    do, dz, dng = row_call("gdn_post_bwd", post_bwd, [(o, D, 0), (pqkvz, D, 3), (don, D, 0)], [gdn_norm_g],
                           [(D, F32), (D, F32)], [(1, HD)], tm)
    du, dw, dqg, dkg, dqk, deg = gdn_scan_bwd(do, gu, gw, gqg, gkg, gqk, geg, sall, S)
    dcpre, dpab, dalog, ddtb = gdn_prep_bwd(cpre, pab, alog_p, dtb_p, gt, du, dw, dqg, dkg, dqk, deg)
    dqkv, dwcv, _ = dwconv_bwd("gdn_conv_bwd", dcpre, pqkvz, 3 * D, wcv_p, S, KSC, HB_SC, lambda v: v, lambda xx, da: da, tm)
    dhn2 = mm("gdn_in_dx_ab", dpab, w_ab, "NT", F32)
    dhn2 = mm("gdn_in_dx_z", dz, w_z, "NT", F32, epi=lambda acc, r: acc + r, epi_ins=[(dhn2, "tile")])
    dhn2 = mm("gdn_in_dx_qkv", dqkv, w_qkv, "NT", F32, epi=lambda acc, r: acc + r, epi_ins=[(dhn2, "tile")])
    dw_in = jnp.concatenate([mm_tn("gdn_in_dw_qkv", hn2, dqkv), mm_tn("gdn_in_dw_z", hn2, dz),
                             mm_tn("gdn_in_dw_ab", hn2, dpab)[:, :2 * H]], axis=1)
    dh2, dg_mix1, _ = _rms_bwd("rms_mix1_bwd", h2, dhn2, dh3, g_mix1, tm)

    dh1, dg_ffn0, db_pw2, dw1_0, dw2_0 = _mlp_bwd("0", dh2, h1, g_ffn0, w1_0, w2_0, mlp0_saved, tm)
    ds = mm("cv_pw2_dx", dh1, w_pw2, "NT", F32)
    dw_pw2 = mm_tn("cv_pw2_dw", s_act, dh1)

    def ln_bwd(xx, dd, gg, bb):
        _, vjp = jax.vjp(f_ln_silu, xx, gg, bb)
        dx_, dg_, db_ = vjp(dd)
        return (dx_,), (dg_, db_, jnp.sum(dx_, axis=0, keepdims=True))

    ddw, dln_g, dln_b, db_dw = row_call("cv_ln_bwd", ln_bwd, [(dwc, D, 0), (ds, D, 0)], [cv_ln_g, cv_ln_b],
                                        [(D, F32)], [(1, D), (1, D), (1, D)], tm)
    du_cv, dwdw, db_pw1 = dwconv_bwd("cv_dwconv_bwd", ddw, u, 2 * D, wdw_p, S, KCV, HB_CV, f_glu, _glu_bwd, tm)
    dhn0 = mm("cv_pw1_dx", du_cv, w_pw1, "NT", F32)
    dw_pw1 = mm_tn("cv_pw1_dw", hn0, du_cv)
    grad_x, dg_mix0, _ = _rms_bwd("rms_mix0_bwd", xf, dhn0, dh1, g_mix0, tm)

    dw_in_sm = jnp.transpose(dw_in.reshape((D, 4, D + 4)), (1, 0, 2))
    g_pw1, g_pw2, g_in, g_out, g_w1, g_w2 = reduce_grads([dw_pw1, dw_pw2, dw_in_sm, dw_out, dw1_0, dw1_1, dw2_0, dw2_1], pref)

    small = jnp.concatenate([
        dg_mix0, dg_mix1, dg_ffn0, dg_ffn1, dg_fin, db_pw1.reshape((2, D)), db_dw, dln_g, dln_b, db_pw2,
        _row1(dalog[:, :H]), _row1(ddtb[:, :H]), _row1(dng), loss_row, jnp.zeros((1, D), F32),
        dwdw, dwcv[:KSC].reshape((3 * KSC, D)), jnp.zeros((NSMALL - 48 - 3 * KSC, D), F32)], axis=0)
    small_all = gather_small(small)

    def pack(a, b, c_, d, e, f, g_, h_, i_, j_, k_):
        return jnp.concatenate([a, b, c_.reshape((1, D)), d.reshape((2, D)), e, f, g_, h_, _row1(i_), _row1(j_), _row1(k_),
                                jnp.zeros((2, D), F32)], axis=0)

    order = lambda p: (p + "norm_mix_g", p + "norm_ffn_g", p + "final_norm_g", p + "cv_b_pw1", p + "cv_b_dw", p + "cv_ln_g",
                       p + "cv_ln_b", p + "cv_b_pw2", p + "gdn_a_log", p + "gdn_dt_bias", p + "gdn_norm_g")
    w16, m16, v16 = (pack(*[env[nm] for nm in order(p)]) for p in ("", "m_", "v_"))

    def small_step(ga, ww, mm_, vv):
        gsum = ga[0]
        for dev in range(1, 8):
            gsum = gsum + ga[dev]
        delta, m2, v2 = f_adamw(ww, gsum[:16], mm_, vv)
        return gsum, delta, m2, v2

    def small_body(ga_ref, w_ref, m_ref, v_ref, g_out, d_out, m_out, v_out):
        gsum, delta, m2, v2 = small_step(ga_ref[...], w_ref[...], m_ref[...], v_ref[...])
        g_out[...] = gsum
        d_out[...] = delta
        m_out[...] = m2
        v_out[...] = v2

    vm = pl.BlockSpec(memory_space=pltpu.VMEM)
    sg, sd, sm, sv = pl.pallas_call(
        small_body, name="adamw_small", in_specs=[vm] * 4, out_specs=[vm] * 4,
        out_shape=[jax.ShapeDtypeStruct((NSMALL, D), F32)] + [jax.ShapeDtypeStruct((16, D), F32)] * 3)(small_all, w16, m16, v16)

    def unpack(b):
        return (b[0:2], b[2:4], b[4], b[5:7].reshape((1, 2 * D)), b[7:8], b[8:9], b[9:10], b[10:11],
                b[11:12, :H], b[12:13, :H], b[13:14, :HD])

    loss = sg[14, 0]
    g_dw = lax.dynamic_slice(sg[16:16 + KCV], (0, chip * (D // 4)), (KCV, D // 4))
    g_cv = lax.dynamic_slice(sg[48:48 + 3 * KSC].reshape((KSC, 3 * D)), (0, chip * (3 * D // 4)), (KSC, 3 * D // 4))

    def adamw(name, w, g, m, v):
        lead = w.shape[:-2]
        if len(lead) == 1 and lead[0] == 1:
            d, m2, v2 = ew_call(name, f_adamw, [w[0], g.reshape(w.shape[1:]), m[0], v[0]], 3)
            return g.reshape(w.shape), d[None], m2[None], v2[None]
        return (g.reshape(w.shape),) + tuple(ew_call(name, f_adamw, [w, g.reshape(w.shape), m, v], 3))

    res = {
        "cv_w_pw1": adamw("adamw_pw1", cv_w_pw1, g_pw1, m_cv_w_pw1, v_cv_w_pw1),
        "cv_w_dw": adamw("adamw_dw", cv_w_dw, g_dw, m_cv_w_dw, v_cv_w_dw),
        "cv_w_pw2": adamw("adamw_pw2", cv_w_pw2, g_pw2, m_cv_w_pw2, v_cv_w_pw2),
        "gdn_w_in": adamw("adamw_win", gdn_w_in, g_in, m_gdn_w_in, v_gdn_w_in),
        "gdn_conv_w": adamw("adamw_cvw", gdn_conv_w, g_cv, m_gdn_conv_w, v_gdn_conv_w),
        "gdn_w_out": adamw("adamw_wout", gdn_w_out, g_out, m_gdn_w_out, v_gdn_w_out),
        "mlp_w1": adamw("adamw_w1", mlp_w1, g_w1, m_mlp_w1, v_mlp_w1),
        "mlp_w2": adamw("adamw_w2", mlp_w2, g_w2, m_mlp_w2, v_mlp_w2),
    }
    names = ("norm_mix_g", "norm_ffn_g", "final_norm_g", "cv_b_pw1", "cv_b_dw", "cv_ln_g", "cv_ln_b", "cv_b_pw2",
             "gdn_a_log", "gdn_dt_bias", "gdn_norm_g")
    for nm, gg, dd, mm_, vv in zip(names, unpack(sg), unpack(sd), unpack(sm), unpack(sv)):
        res[nm] = (gg, dd, mm_, vv)
    weights = ("norm_mix_g", "norm_ffn_g", "final_norm_g", "cv_w_pw1", "cv_b_pw1", "cv_w_dw", "cv_b_dw", "cv_ln_g",
               "cv_ln_b", "cv_w_pw2", "cv_b_pw2", "gdn_w_in", "gdn_conv_w", "gdn_a_log", "gdn_dt_bias", "gdn_norm_g",
               "gdn_w_out", "mlp_w1", "mlp_w2")
    outs = [loss, grad_x.reshape(x.shape)]
    for kind in range(4):
        outs += [res[nm][kind] for nm in weights]
    return tuple(outs)
```

```python
import functools

import jax
import jax.numpy as jnp
from jax import lax
from jax.experimental import pallas as pl
from jax.experimental.pallas import tpu as pltpu

F32, BF16 = jnp.float32, jnp.bfloat16
D = 1024
H = 8
HD = 128
CH = 64
DFF = 4 * D
KCV, HB_CV = 31, 32
KSC, HB_SC = 4, 8
EPS = 1e-6
LR, B1, B2, EPS_A, WD, STEP = 0.001, 0.9, 0.999, 1e-08, 0.01, 10
VMEM_LIMIT = 56 * 1024 * 1024
SUB = 32
NSMALL = 64
MESH = pl.DeviceIdType.MESH


def _cp(*sem):
    return pltpu.CompilerParams(dimension_semantics=sem, vmem_limit_bytes=VMEM_LIMIT)


def f_rms(h, g):
    return h * lax.rsqrt(jnp.mean(h * h, axis=-1, keepdims=True) + EPS) * g


def f_silu(x):
    return x * jax.nn.sigmoid(x)


def f_glu(u):
    return u[:, :D] * jax.nn.sigmoid(u[:, D:])


def f_ln_silu(x, g, b):
    mu = jnp.mean(x, axis=-1, keepdims=True)
    xc = x - mu
    y = xc * lax.rsqrt(jnp.mean(xc * xc, axis=-1, keepdims=True) + EPS)
    return f_silu(y * g + b)


def f_relu2(z):
    r = jnp.maximum(z.astype(F32), 0.0)
    return r * r


def f_post(o, z, ng):
    outs = []
    for h in range(H):
        oh = o[:, h * HD:(h + 1) * HD]
        y = oh * lax.rsqrt(jnp.mean(oh * oh, axis=-1, keepdims=True) + EPS) * ng
        outs.append(y * f_silu(z[:, h * HD:(h + 1) * HD]))
    return jnp.concatenate(outs, axis=1)


def f_adamw(w, g, m, v):
    m2 = B1 * m + (1.0 - B1) * g
    v2 = B2 * v + (1.0 - B2) * (g * g)
    m_hat = m2 / (1.0 - B1 ** STEP)
    v_hat = v2 / (1.0 - B2 ** STEP)
    delta = -LR * (m_hat / (jnp.sqrt(v_hat) + EPS_A) + WD * w)
    return delta, m2, v2


def _dot_raw(a, b, mode):
    dims = {"NN": ((1,), (0,)), "NT": ((1,), (1,)), "TN": ((0,), (0,))}[mode]
    return lax.dot_general(a.astype(BF16), b.astype(BF16), (dims, ((), ())), preferred_element_type=F32)


@functools.partial(jax.custom_vjp, nondiff_argnums=(2,))
def _dot_vjp(a, b, mode):
    return _dot_raw(a, b, mode)


def _dot_fwd(a, b, mode):
    return _dot_raw(a, b, mode), (a, b)


def _dot_bwd(mode, res, dc):
    a, b = res
    if mode == "NN":
        return _dot_vjp(dc, b, "NT"), _dot_vjp(a, dc, "TN")
    if mode == "NT":
        return _dot_vjp(dc, b, "NN"), _dot_vjp(dc, a, "TN")
    return _dot_vjp(b, dc, "NT"), _dot_vjp(a, dc, "NN")


_dot_vjp.defvjp(_dot_fwd, _dot_bwd)


def _dot_split(x, y):
    xh, yh = x.astype(BF16), y.astype(BF16)
    xl, yl = x - xh.astype(F32), y - yh.astype(F32)
    return _dot_raw(xh, yh, "NN") + (_dot_raw(xh, yl, "NN") + _dot_raw(xl, yh, "NN"))


def _tril_inverse(a_list):
    ri = lax.broadcasted_iota(jnp.int32, (CH, CH), 0)
    ci = lax.broadcasted_iota(jnp.int32, (CH, CH), 1)
    ts = [(ri == ci).astype(F32) for _ in a_list]
    for lvl in range(CH.bit_length() - 1):
        same_pair = jnp.right_shift(ri, lvl + 1) == jnp.right_shift(ci, lvl + 1)
        quarter = (jnp.bitwise_and(jnp.right_shift(ri, lvl), 1) == 1) & (jnp.bitwise_and(jnp.right_shift(ci, lvl), 1) == 0)
        mids = [_dot_split(t, jnp.where(same_pair & quarter, a, 0.0)) for t, a in zip(ts, a_list)]
        ts = [t - _dot_split(m, t) for t, m in zip(ts, mids)]
    return ts


@jax.custom_vjp
def _stored_solve(a, t, rhs):
    return _dot_raw(t, rhs, "NN")


def _stored_solve_fwd(a, t, rhs):
    sol = _dot_raw(t, rhs, "NN")
    return sol, (t, sol)


def _stored_solve_bwd(res, g):
    t, sol = res
    g_rhs = _dot_vjp(t, g, "TN")
    return -_dot_vjp(g_rhs, sol, "NT"), jnp.zeros_like(t), g_rhs


_stored_solve.defvjp(_stored_solve_fwd, _stored_solve_bwd)


def _lane_pick(row, idx, width):
    sel = lax.broadcasted_iota(jnp.int32, (1, width), 1) == idx
    return jnp.sum(jnp.where(sel, row, 0.0), axis=1, keepdims=True)


def f_prep(cqs, cks, cvs, araws, braws, alogs, dtbs, t_stored, dot):
    ri = lax.broadcasted_iota(jnp.int32, (CH, CH), 0)
    ci = lax.broadcasted_iota(jnp.int32, (CH, CH), 1)
    eye = (ri == ci).astype(F32)
    low = (ri >= ci).astype(F32)
    last = lax.broadcasted_iota(jnp.int32, (CH, 1), 0) == CH - 1
    nh = range(len(cqs))
    qs, ks, vbs, kbs, gcs, decays = [], [], [], [], [], []
    for h in nh:
        q = f_silu(cqs[h])
        qs.append(q * lax.rsqrt(jnp.sum(q * q, axis=-1, keepdims=True) + 1e-6) * (HD ** -0.5))
        k = f_silu(cks[h])
        k = k * lax.rsqrt(jnp.sum(k * k, axis=-1, keepdims=True) + 1e-6)
        ks.append(k)
        beta = jax.nn.sigmoid(braws[h])
        sp_in = araws[h] + dtbs[h]
        softplus = jnp.maximum(sp_in, 0.0) + jnp.log(1.0 + jnp.exp(-jnp.abs(sp_in)))
        g = -jnp.exp(alogs[h]) * softplus
        g_row = jnp.sum(eye * g, axis=0, keepdims=True)
        gc = jnp.sum(low * g_row, axis=1, keepdims=True)
        gc_row = jnp.sum(eye * gc, axis=0, keepdims=True)
        gcs.append(gc)
        decays.append(jnp.exp(jnp.where(ri >= ci, gc - gc_row, -1e30)))
        vbs.append(f_silu(cvs[h]) * beta)
        kbs.append(k * beta)
    kks = [dot(kbs[h], ks[h], "NT") for h in nh]
    a_list = [jnp.where(ri > ci, kks[h] * decays[h], 0.0) for h in nh]
    if t_stored is None:
        ts = _tril_inverse(a_list)
        solve = lambda h, rhs: dot(ts[h], rhs, "NN")
    else:
        ts = t_stored
        solve = lambda h, rhs: _stored_solve(a_list[h], t_stored[h], rhs)
    egcs = [jnp.exp(gc) for gc in gcs]
    us = [solve(h, vbs[h]) for h in nh]
    ws = [solve(h, kbs[h] * egcs[h]) for h in nh]
    qks = [dot(qs[h], ks[h], "NT") * decays[h] for h in nh]
    qgs = [qs[h] * egcs[h] for h in nh]
    gls = [jnp.sum(jnp.where(last, gc, 0.0), axis=0, keepdims=True) for gc in gcs]
    kgs = [ks[h] * jnp.exp(gls[h] - gcs[h]) for h in nh]
    egs = [jnp.exp(gl) * jnp.ones((1, HD), F32) for gl in gls]
    return us, ws, qks, qgs, kgs, egs, ts


def f_scan(ss, us, ws, qgs, kgs, qks, egs, dot):
    nh = range(len(ss))
    ws_s = [dot(ws[h], ss[h], "NN") for h in nh]
    qs_s = [dot(qgs[h], ss[h], "NN") for h in nh]
    vns = [us[h] - ws_s[h] for h in nh]
    os_ = [qs_s[h] + dot(qks[h], vns[h], "NN") for h in nh]
    s2s = [ss[h] * egs[h] + dot(kgs[h], vns[h], "TN") for h in nh]
    return os_, s2s


def row_call(name, fn, rows, pars, out_rows, out_accs, tm):
    T = rows[0][0].shape[0]
    n_r, n_p, n_o = len(rows), len(pars), len(out_rows)
    in_specs = [pl.BlockSpec((tm, w), functools.partial(lambda i, cb: (i, cb), cb=cb)) for (_, w, cb) in rows]
    in_specs += [pl.BlockSpec(p.shape, functools.partial(lambda i, nd: (0,) * nd, nd=p.ndim)) for p in pars]
    out_specs = [pl.BlockSpec((tm, w), lambda i: (i, 0)) for (w, _) in out_rows]
    out_specs += [pl.BlockSpec(s, lambda i: (0, 0)) for s in out_accs]
    out_shape = [jax.ShapeDtypeStruct((T, w), dt) for (w, dt) in out_rows]
    out_shape += [jax.ShapeDtypeStruct(s, F32) for s in out_accs]

    def body(*refs):
        rin, pin = refs[:n_r], refs[n_r:n_r + n_p]
        rout, aout = refs[n_r + n_p:n_r + n_p + n_o], refs[n_r + n_p + n_o:]
        if aout:
            @pl.when(pl.program_id(0) == 0)
            def _():
                for a in aout:
                    a[...] = jnp.zeros(a.shape, F32)
        pv = [p[...] for p in pin]

        def step(r, carry):
            sl = pl.ds(pl.multiple_of(r * SUB, SUB), SUB)
            outs, accs = fn(*[x[sl, :] for x in rin], *pv)
            for o, val in zip(rout, outs):
                o[sl, :] = val.astype(o.dtype)
            for a, val in zip(aout, accs):
                a[...] += val
            return carry

        lax.fori_loop(0, tm // SUB, step, 0)

    return pl.pallas_call(body, name=name, grid=(T // tm,), in_specs=in_specs, out_specs=out_specs,
                          out_shape=out_shape, compiler_params=_cp("arbitrary"))(*[r[0] for r in rows], *pars)


EW_TILE_ELEMS = 256 * 1024


def _ew_rows(R, Cc):
    if R * Cc <= EW_TILE_ELEMS or R % 8:
        return R
    tr = 8
    while tr * 2 * Cc <= EW_TILE_ELEMS and R % (tr * 2) == 0:
        tr *= 2
    return tr


def ew_call(name, fn, ins, n_out):
    shape = ins[0].shape
    lead = shape[:-2]
    R, Cc = shape[-2:]
    tr = _ew_rows(R, Cc)
    grid = lead + (R // tr,)
    nl = len(lead)
    spec = pl.BlockSpec((None,) * nl + (tr, Cc), lambda *idx: idx + (0,))

    def body(*refs):
        outs = fn(*[r[...] for r in refs[:len(ins)]])
        for o, val in zip(refs[len(ins):], outs):
            o[...] = val

    return pl.pallas_call(body, name=name, grid=grid, in_specs=[spec] * len(ins), out_specs=[spec] * n_out,
                          out_shape=[jax.ShapeDtypeStruct(shape, F32)] * n_out,
                          compiler_params=_cp(*(("arbitrary",) * len(grid))))(*ins)


def mm(name, a, b, mode, out_dtype, a_fn=None, epi=None, epi_ins=(), accs=(), tm=512):
    M, K = a.shape
    N = b.shape[1] if mode == "NN" else b.shape[0]
    tm = min(tm, M)
    tn = min(N, 1024)
    multi = isinstance(out_dtype, (list, tuple))
    dts = list(out_dtype) if multi else [out_dtype]
    n_e, n_o = len(epi_ins), len(dts)
    in_specs = [pl.BlockSpec((tm, K), lambda j, i: (i, 0)),
                pl.BlockSpec((K, tn), lambda j, i: (0, j)) if mode == "NN" else pl.BlockSpec((tn, K), lambda j, i: (j, 0))]
    row_kinds = []
    for (arr, kind) in epi_ins:
        if kind == "tile" or isinstance(kind, tuple):
            cb = kind[1] if isinstance(kind, tuple) else 0
            in_specs.append(pl.BlockSpec((tm, tn), functools.partial(lambda j, i, cb: (i, cb + j), cb=cb)))
            row_kinds.append(True)
        elif kind == "row":
            in_specs.append(pl.BlockSpec((1, tn), lambda j, i: (0, j)))
            row_kinds.append(False)
        else:
            in_specs.append(pl.BlockSpec(arr.shape, lambda j, i: (0, 0)))
            row_kinds.append(False)

    def body(a_ref, b_ref, *rest):
        e_refs, o_refs, acc_refs = rest[:n_e], rest[n_e:n_e + n_o], rest[n_e + n_o:n_e + n_o + len(accs)]
        av = a_ref[...]
        if a_fn is not None:
            av = a_fn(av)
        res = _dot_raw(av, b_ref[...], mode)
        if epi is None:
            o_refs[0][...] = res.astype(o_refs[0].dtype)
            return
        prod = rest[-1]
        prod[...] = res
        if acc_refs:
            @pl.when((pl.program_id(0) == 0) & (pl.program_id(1) == 0))
            def _():
                for r in acc_refs:
                    r[...] = jnp.zeros(r.shape, F32)
        small = [None if is_rows else r[...] for r, is_rows in zip(e_refs, row_kinds)]

        def step(k, carry):
            sl = pl.ds(pl.multiple_of(k * SUB, SUB), SUB)
            out = epi(prod[sl, :], *[r[sl, :] if is_rows else sm for r, is_rows, sm in zip(e_refs, row_kinds, small)])
            tiles, contribs = out if multi else ((out,), ())
            for r, t in zip(o_refs, tiles):
                r[sl, :] = t.astype(r.dtype)
            for r, t in zip(acc_refs, contribs):
                r[...] += t
            return carry

        lax.fori_loop(0, tm // SUB, step, 0)

    out_specs = [pl.BlockSpec((tm, tn), lambda j, i: (i, j))] * n_o + [pl.BlockSpec(s, lambda j, i: (0, 0)) for s in accs]
    out_shape = [jax.ShapeDtypeStruct((M, N), dt) for dt in dts] + [jax.ShapeDtypeStruct(s, F32) for s in accs]
    res = pl.pallas_call(body, name=name, grid=(N // tn, M // tm), in_specs=in_specs, out_specs=out_specs,
                         out_shape=out_shape, scratch_shapes=[pltpu.VMEM((tm, tn), F32)] if epi is not None else [],
                         compiler_params=_cp("arbitrary", "arbitrary"))(a, b, *[e[0] for e in epi_ins])
    return res if multi else res[0]


def mm_tn(name, a, g, a_fn=None, a_cols=None, tt=512):
    T = a.shape[0]
    ka, acb = (a.shape[1], 0) if a_cols is None else a_cols
    N = g.shape[1]
    tt = min(tt, T)
    tka, tn = min(ka, 1024), min(N, 1024)
    nkb = ka // tka

    def body(a_ref, g_ref, o_ref):
        @pl.when(pl.program_id(2) == 0)
        def _():
            o_ref[...] = jnp.zeros(o_ref.shape, F32)
        av = a_ref[...]
        if a_fn is not None:
            av = a_fn(av)
        o_ref[...] += _dot_raw(av, g_ref[...], "TN")

    return pl.pallas_call(body, name=name, grid=(nkb, N // tn, T // tt),
                          in_specs=[pl.BlockSpec((tt, tka), lambda ia, j, t: (t, acb * nkb + ia)),
                                    pl.BlockSpec((tt, tn), lambda ia, j, t: (t, j))],
                          out_specs=pl.BlockSpec((tka, tn), lambda ia, j, t: (ia, j)),
                          out_shape=jax.ShapeDtypeStruct((ka, N), F32),
                          compiler_params=_cp("arbitrary", "arbitrary", "arbitrary"))(a, g)


def dwconv_fwd(name, x, xw, w_pad, bias, S, K, HB, pre, post, tm):
    T = x.shape[0]
    C = w_pad.shape[1]
    nb, per_seq = tm // HB, S // tm
    has_b, has_post = bias is not None, post is not None

    def body(*refs):
        x_ref, xp_ref, w_ref = refs[:3]
        pos = 3
        b_ref = refs[pos] if has_b else None
        pos += has_b
        ppars = refs[pos:pos + (len(post[1]) if has_post else 0)]
        pos += len(ppars)
        c_ref = refs[pos]
        s_ref = refs[pos + 1] if has_post else None
        ext = refs[-1]
        first = (pl.program_id(0) % per_seq) == 0
        ext[0:HB, :] = jnp.where(first, 0.0, pre(xp_ref[...]))
        for r in range(tm // SUB):
            ext[HB + r * SUB:HB + (r + 1) * SUB, :] = pre(x_ref[r * SUB:(r + 1) * SUB, :])
        pv = [p[...] for p in ppars]
        for r in range(tm // SUB):
            acc = jnp.zeros((SUB, C), F32)
            if has_b:
                acc = acc + b_ref[...]
            for k in range(K):
                off = HB + r * SUB - (K - 1) + k
                acc = acc + w_ref[k:k + 1, :] * ext[off:off + SUB, :]
            c_ref[r * SUB:(r + 1) * SUB, :] = acc
            if has_post:
                s_ref[r * SUB:(r + 1) * SUB, :] = post[0](acc, *pv).astype(BF16)

    ins = [x, x, w_pad] + ([bias] if has_b else []) + (list(post[1]) if has_post else [])
    in_specs = [pl.BlockSpec((tm, xw), lambda i: (i, 0)),
                pl.BlockSpec((HB, xw), lambda i: (jnp.maximum(i * nb - 1, 0), 0)),
                pl.BlockSpec(w_pad.shape, lambda i: (0, 0))]
    in_specs += [pl.BlockSpec(p.shape, lambda i: (0, 0)) for p in ins[3:]]
    out_specs = [pl.BlockSpec((tm, C), lambda i: (i, 0))] * (1 + has_post)
    out_shape = [jax.ShapeDtypeStruct((T, C), F32)] + ([jax.ShapeDtypeStruct((T, C), BF16)] if has_post else [])
    return pl.pallas_call(body, name=name, grid=(T // tm,), in_specs=in_specs, out_specs=out_specs, out_shape=out_shape,
                          scratch_shapes=[pltpu.VMEM((HB + tm, C), F32)], compiler_params=_cp("arbitrary"))(*ins)


def dwconv_bwd(name, g, x, xw, w_pad, S, K, HB, pre, pre_bwd, tm):
    T = g.shape[0]
    C = w_pad.shape[1]
    nb, per_seq = tm // HB, S // tm
    nblk = T // HB

    def body(g_ref, gn_ref, x_ref, xp_ref, w_ref, dx_ref, dw_ref, dbx_ref, extg, exta):
        i = pl.program_id(0)
        first = (i % per_seq) == 0
        last = (i % per_seq) == per_seq - 1

        @pl.when(i == 0)
        def _():
            dw_ref[...] = jnp.zeros(dw_ref.shape, F32)
            dbx_ref[...] = jnp.zeros(dbx_ref.shape, F32)

        extg[tm:tm + HB, :] = jnp.where(last, 0.0, gn_ref[...])
        exta[0:HB, :] = jnp.where(first, 0.0, pre(xp_ref[...]))
        for r in range(tm // SUB):
            extg[r * SUB:(r + 1) * SUB, :] = g_ref[r * SUB:(r + 1) * SUB, :]
            exta[HB + r * SUB:HB + (r + 1) * SUB, :] = pre(x_ref[r * SUB:(r + 1) * SUB, :])
        for r in range(tm // SUB):
            acc = jnp.zeros((SUB, C), F32)
            for k in range(K):
                off = r * SUB + (K - 1) - k
                acc = acc + w_ref[k:k + 1, :] * extg[off:off + SUB, :]
            dx = pre_bwd(x_ref[r * SUB:(r + 1) * SUB, :], acc)
            dx_ref[r * SUB:(r + 1) * SUB, :] = dx
            dbx_ref[...] += jnp.sum(dx, axis=0, keepdims=True)
        for k in range(K):
            p = jnp.zeros((SUB, C), F32)
            for r in range(tm // SUB):
                off = HB + r * SUB - (K - 1) + k
                p = p + extg[r * SUB:(r + 1) * SUB, :] * exta[off:off + SUB, :]
            dw_ref[k:k + 1, :] += jnp.sum(p, axis=0, keepdims=True)

    in_specs = [pl.BlockSpec((tm, C), lambda i: (i, 0)),
                pl.BlockSpec((HB, C), lambda i: (jnp.minimum((i + 1) * nb, nblk - 1), 0)),
                pl.BlockSpec((tm, xw), lambda i: (i, 0)),
                pl.BlockSpec((HB, xw), lambda i: (jnp.maximum(i * nb - 1, 0), 0)),
                pl.BlockSpec(w_pad.shape, lambda i: (0, 0))]
    out_specs = [pl.BlockSpec((tm, xw), lambda i: (i, 0)), pl.BlockSpec((HB, C), lambda i: (0, 0)),
                 pl.BlockSpec((1, xw), lambda i: (0, 0))]
    out_shape = [jax.ShapeDtypeStruct((T, xw), F32), jax.ShapeDtypeStruct((HB, C), F32), jax.ShapeDtypeStruct((1, xw), F32)]
    return pl.pallas_call(body, name=name, grid=(T // tm,), in_specs=in_specs, out_specs=out_specs, out_shape=out_shape,
                          scratch_shapes=[pltpu.VMEM((tm + HB, C), F32), pltpu.VMEM((HB + tm, C), F32)],
                          compiler_params=_cp("arbitrary"))(g, g, x, x, w_pad)


def _glu_bwd(u, da):
    u1, sg = u[:, :D], jax.nn.sigmoid(u[:, D:])
    return jnp.concatenate([da * sg, da * u1 * sg * (1.0 - sg)], axis=1)


def _head_cols(ref, h, base=0):
    return ref[:, base + h * HD:base + (h + 1) * HD]


def _prep_inputs(c_ref, ab, al, dt):
    hs = range(H)
    return ([_head_cols(c_ref, h) for h in hs], [_head_cols(c_ref, h, D) for h in hs],
            [_head_cols(c_ref, h, 2 * D) for h in hs], [_lane_pick(ab, h, HD) for h in hs],
            [_lane_pick(ab, H + h, HD) for h in hs], [_lane_pick(al, h, HD) for h in hs],
            [_lane_pick(dt, h, HD) for h in hs])


def gdn_prep_fwd(cpre, pab, alog, dtb):
    T = cpre.shape[0]
    nc = T // CH

    def body(c_ref, ab_ref, al_ref, dt_ref, u_ref, w_ref, qg_ref, kg_ref, qk_ref, t_ref, eg_ref):
        us, ws, qks, qgs, kgs, egs, ts = f_prep(*_prep_inputs(c_ref, ab_ref[...], al_ref[...], dt_ref[...]), None, _dot_raw)
        for h in range(H):
            cols = slice(h * HD, (h + 1) * HD)
            u_ref[:, cols] = us[h]
            w_ref[:, cols] = ws[h].astype(BF16)
            qg_ref[:, cols] = qgs[h].astype(BF16)
            kg_ref[:, cols] = kgs[h].astype(BF16)
            qk_ref[0, h] = qks[h].astype(BF16)
            t_ref[0, h] = ts[h].astype(BF16)
            eg_ref[0, h:h + 1, :] = egs[h]

    row = lambda w: pl.BlockSpec((CH, w), lambda n: (n, 0))
    par = pl.BlockSpec((1, HD), lambda n: (0, 0))
    mat = pl.BlockSpec((1, H, CH, CH), lambda n: (n, 0, 0, 0))
    return pl.pallas_call(
        body, name="gdn_prep_fwd", grid=(nc,), in_specs=[row(3 * D), row(HD), par, par],
        out_specs=[row(D), row(D), row(D), row(D), mat, mat, pl.BlockSpec((1, H, HD), lambda n: (n, 0, 0))],
        out_shape=[jax.ShapeDtypeStruct((T, D), F32)] + [jax.ShapeDtypeStruct((T, D), BF16)] * 3
        + [jax.ShapeDtypeStruct((nc, H, CH, CH), BF16)] * 2 + [jax.ShapeDtypeStruct((nc, H, HD), F32)],
        compiler_params=_cp("arbitrary"))(cpre, pab, alog, dtb)


def gdn_prep_bwd(cpre, pab, alog, dtb, tmat, du, dw, dqg, dkg, dqk, deg):
    T = cpre.shape[0]
    nc = T // CH

    def body(c_ref, ab_ref, al_ref, dt_ref, t_ref, du_ref, dw_ref, dqg_ref, dkg_ref, dqk_ref, deg_ref,
             dc_ref, dab_ref, dal_ref, ddt_ref):
        @pl.when(pl.program_id(0) == 0)
        def _():
            dal_ref[...] = jnp.zeros(dal_ref.shape, F32)
            ddt_ref[...] = jnp.zeros(ddt_ref.shape, F32)

        lane = lax.broadcasted_iota(jnp.int32, (1, HD), 1)
        dab = jnp.zeros((CH, HD), F32)
        dal = jnp.zeros((1, HD), F32)
        ddt = jnp.zeros((1, HD), F32)
        hs = range(H)
        t_st = [t_ref[0, h].astype(F32) for h in hs]

        def fwd(*args):
            return tuple(f_prep(*args, t_st, _dot_vjp)[:6])

        _, vjp = jax.vjp(fwd, *_prep_inputs(c_ref, ab_ref[...], al_ref[...], dt_ref[...]))
        dcqs, dcks, dcvs, dars, dbrs, dals, ddts = vjp((
            [_head_cols(du_ref, h) for h in hs], [_head_cols(dw_ref, h) for h in hs], [dqk_ref[0, h] for h in hs],
            [_head_cols(dqg_ref, h) for h in hs], [_head_cols(dkg_ref, h) for h in hs],
            [deg_ref[0, h:h + 1, :] for h in hs]))
        for h in hs:
            dc_ref[:, h * HD:(h + 1) * HD] = dcqs[h]
            dc_ref[:, D + h * HD:D + (h + 1) * HD] = dcks[h]
            dc_ref[:, 2 * D + h * HD:2 * D + (h + 1) * HD] = dcvs[h]
            dab = dab + jnp.where(lane == h, dars[h], 0.0) + jnp.where(lane == H + h, dbrs[h], 0.0)
            dal = dal + jnp.where(lane == h, dals[h], 0.0)
            ddt = ddt + jnp.where(lane == h, ddts[h], 0.0)
        dab_ref[...] = dab
        dal_ref[...] += dal
        ddt_ref[...] += ddt

    row = lambda w: pl.BlockSpec((CH, w), lambda n: (n, 0))
    par = pl.BlockSpec((1, HD), lambda n: (0, 0))
    mat = pl.BlockSpec((1, H, CH, CH), lambda n: (n, 0, 0, 0))
    vec = pl.BlockSpec((1, H, HD), lambda n: (n, 0, 0))
    return pl.pallas_call(
        body, name="gdn_prep_bwd", grid=(nc,),
        in_specs=[row(3 * D), row(HD), par, par, mat, row(D), row(D), row(D), row(D), mat, vec],
        out_specs=[row(3 * D), row(HD), par, par],
        out_shape=[jax.ShapeDtypeStruct((T, 3 * D), F32), jax.ShapeDtypeStruct((T, HD), F32),
                   jax.ShapeDtypeStruct((1, HD), F32), jax.ShapeDtypeStruct((1, HD), F32)],
        compiler_params=_cp("arbitrary"))(cpre, pab, alog, dtb, tmat, du, dw, dqg, dkg, dqk, deg)


def gdn_scan_fwd(u, w, qg, kg, qk, eg, S):
    T = u.shape[0]
    nc, per_seq = T // CH, S // CH

    def body(u_ref, w_ref, qg_ref, kg_ref, qk_ref, eg_ref, o_ref, sall_ref, s_ref):
        @pl.when(pl.program_id(0) % per_seq == 0)
        def _():
            s_ref[...] = jnp.zeros(s_ref.shape, F32)

        hs = range(H)
        ss = [s_ref[h] for h in hs]
        os_, s2s = f_scan(ss, [_head_cols(u_ref, h) for h in hs], [_head_cols(w_ref, h) for h in hs],
                          [_head_cols(qg_ref, h) for h in hs], [_head_cols(kg_ref, h) for h in hs],
                          [qk_ref[0, h] for h in hs], [eg_ref[0, h:h + 1, :] for h in hs], _dot_raw)
        for h in hs:
            sall_ref[0, h] = ss[h]
            o_ref[:, h * HD:(h + 1) * HD] = os_[h]
            s_ref[h] = s2s[h]

    row = pl.BlockSpec((CH, D), lambda n: (n, 0))
    return pl.pallas_call(
        body, name="gdn_scan_fwd", grid=(nc,),
        in_specs=[row, row, row, row, pl.BlockSpec((1, H, CH, CH), lambda n: (n, 0, 0, 0)),
                  pl.BlockSpec((1, H, HD), lambda n: (n, 0, 0))],
        out_specs=[row, pl.BlockSpec((1, H, HD, HD), lambda n: (n, 0, 0, 0))],
        out_shape=[jax.ShapeDtypeStruct((T, D), F32), jax.ShapeDtypeStruct((nc, H, HD, HD), F32)],
        scratch_shapes=[pltpu.VMEM((H, HD, HD), F32)], compiler_params=_cp("arbitrary"))(u, w, qg, kg, qk, eg)


def gdn_scan_bwd(do, u, w, qg, kg, qk, eg, sall, S):
    T = u.shape[0]
    nc, per_seq = T // CH, S // CH

    def body(do_ref, u_ref, w_ref, qg_ref, kg_ref, qk_ref, eg_ref, sall_ref,
             du_ref, dw_ref, dqg_ref, dkg_ref, dqk_ref, deg_ref, ds_ref):
        n = nc - 1 - pl.program_id(0)

        @pl.when(n % per_seq == per_seq - 1)
        def _():
            ds_ref[...] = jnp.zeros(ds_ref.shape, F32)

        hs = range(H)

        def fwd(*args):
            return f_scan(*args, _dot_vjp)

        _, vjp = jax.vjp(fwd, [sall_ref[0, h] for h in hs], [_head_cols(u_ref, h) for h in hs],
                         [_head_cols(w_ref, h).astype(F32) for h in hs], [_head_cols(qg_ref, h).astype(F32) for h in hs],
                         [_head_cols(kg_ref, h).astype(F32) for h in hs], [qk_ref[0, h].astype(F32) for h in hs],
                         [eg_ref[0, h:h + 1, :] for h in hs])
        dss, dus, dws, dqgs, dkgs, dqks, degs = vjp(([_head_cols(do_ref, h) for h in hs], [ds_ref[h] for h in hs]))
        for h in hs:
            cols = slice(h * HD, (h + 1) * HD)
            du_ref[:, cols] = dus[h]
            dw_ref[:, cols] = dws[h]
            dqg_ref[:, cols] = dqgs[h]
            dkg_ref[:, cols] = dkgs[h]
            dqk_ref[0, h] = dqks[h]
            deg_ref[0, h:h + 1, :] = degs[h]
            ds_ref[h] = dss[h]

    rev = lambda n: (nc - 1 - n, 0)
    row = pl.BlockSpec((CH, D), rev)
    mat = pl.BlockSpec((1, H, CH, CH), lambda n: (nc - 1 - n, 0, 0, 0))
    vec = pl.BlockSpec((1, H, HD), lambda n: (nc - 1 - n, 0, 0))
    return pl.pallas_call(
        body, name="gdn_scan_bwd", grid=(nc,),
        in_specs=[row, row, row, row, row, mat, vec, pl.BlockSpec((1, H, HD, HD), lambda n: (nc - 1 - n, 0, 0, 0))],
        out_specs=[row, row, row, row, mat, vec],
        out_shape=[jax.ShapeDtypeStruct((T, D), F32)] * 4
        + [jax.ShapeDtypeStruct((nc, H, CH, CH), F32), jax.ShapeDtypeStruct((nc, H, HD), F32)],
        scratch_shapes=[pltpu.VMEM((H, HD, HD), F32)], compiler_params=_cp("arbitrary"))(do, u, w, qg, kg, qk, eg, sall)


def xor_exchange(name, ins, inplace, out_shapes, plan, n_remote, n_local=0):
    n_in, n_ip, n_out = len(ins), len(inplace), len(out_shapes)

    def body(*refs):
        in_refs = refs[:n_in]
        ip_refs = refs[n_in + n_ip:n_in + 2 * n_ip]
        out_refs = refs[n_in + 2 * n_ip:n_in + 2 * n_ip + n_out]
        send_sems, recv_sems, loc_sems = refs[n_in + 2 * n_ip + n_out:]
        x, y, c = lax.axis_index("x"), lax.axis_index("y"), lax.axis_index("c")
        remote, local = plan(in_refs, ip_refs, out_refs, (x, y, c))
        assert len(remote) == n_remote and len(local) == n_local
        copies = []
        for k, ((dx, dy, dc), src, dst) in enumerate(remote):
            peer = (1 - x if dx else x, 1 - y if dy else y, 1 - c if dc else c)
            copies.append(pltpu.make_async_remote_copy(src_ref=src, dst_ref=dst, send_sem=send_sems.at[k],
                                                       recv_sem=recv_sems.at[k], device_id=peer, device_id_type=MESH))
        for cp in copies:
            cp.start()
        locs = [pltpu.make_async_copy(src, dst, loc_sems.at[k]) for k, (src, dst) in enumerate(local)]
        for cp in locs:
            cp.start()
        for cp in copies:
            cp.wait()
        for cp in locs:
            cp.wait()

    anyspec = pl.BlockSpec(memory_space=pl.ANY)
    res = pl.pallas_call(
        body, name=name, in_specs=[anyspec] * (n_in + n_ip), out_specs=[anyspec] * (n_ip + n_out),
        out_shape=[jax.ShapeDtypeStruct(a.shape, a.dtype) for a in inplace] + list(out_shapes),
        input_output_aliases={n_in + i: i for i in range(n_ip)},
        scratch_shapes=[pltpu.SemaphoreType.DMA((n_remote,)), pltpu.SemaphoreType.DMA((n_remote,)),
                        pltpu.SemaphoreType.DMA((max(n_local, 1),))],
        )(*ins, *inplace)
    return list(res[:n_ip]), list(res[n_ip:])


class WSpec:
    def __init__(self, name, full, sa, ha, group, layer=None, lead=False):
        self.name, self.full, self.sa, self.ha, self.group, self.layer, self.lead = name, full, sa, ha, group, layer, lead
        self.ws = 1 if lead else full[sa] // 4
        self.wh = full[ha] // 2

    def shard_shape(self):
        if self.lead:
            return tuple(n for a, n in enumerate(self.full) if a != self.sa)
        return tuple(self.ws if a == self.sa else n for a, n in enumerate(self.full))

    def half_full_shape(self):
        return tuple(self.wh if a == self.ha else n for a, n in enumerate(self.full))

    def shard_half_shape(self):
        s = list(self.half_full_shape())
        if self.lead:
            del s[self.sa]
        else:
            s[self.sa] = self.ws
        return tuple(s)

    def full_view(self, ref, q=None, h=None):
        idx = []
        for a in range(len(self.full)):
            if a == self.sa and q is not None:
                idx.append(q if self.lead else pl.ds(pl.multiple_of(q * self.ws, self.ws), self.ws))
            elif a == self.ha and h is not None:
                idx.append(pl.ds(pl.multiple_of(h * self.wh, self.wh), self.wh))
            else:
                idx.append(slice(None))
        return ref.at[tuple(idx)]

    def shard_view(self, ref, h):
        idx = [] if self.layer is None else [self.layer]
        for a in range(len(self.full)):
            if self.lead and a == self.sa:
                continue
            idx.append(pl.ds(pl.multiple_of(h * self.wh, self.wh), self.wh) if a == self.ha else slice(None))
        return ref.at[tuple(idx)]

    def rows_cols(self, shard, half):
        rows, cols = self.full[-2:]
        if shard and not self.lead:
            rows, cols = (rows // 4, cols) if self.sa == 0 else (rows, cols // 4)
        if half:
            rows, cols = (rows // 2, cols) if self.ha == len(self.full) - 2 else (rows, cols // 2)
        return rows, cols

    def spec(self, tr, cw, nr, shard=False, half=False, has_lead=False, stacked=False):
        two_d = len(self.full) == 2
        shard_on_cols = two_d and self.sa == 1
        half_on_cols = two_d and self.ha == 1
        layer = self.layer

        def index(*args):
            pref = args[-1]
            i = args[-2]
            r, cblk, pre = i, 0, ()
            if shard:
                if self.lead:
                    pre = (pref[0],)
                elif shard_on_cols:
                    cblk = pref[0]
                else:
                    r = pref[0] * nr + i
            elif has_lead:
                pre = (args[0],)
            if half:
                if half_on_cols:
                    cblk = pref[1]
                else:
                    r = pref[1] * nr + i
            if stacked:
                pre = (layer,) + pre
            return pre + (r, cblk)

        n_pre = int(stacked) + int(self.lead and (shard or has_lead))
        return pl.BlockSpec((None,) * n_pre + (tr, cw), index)


WSPECS = [
    WSpec("cv_w_pw1", (D, 2 * D), 1, 0, 0),
    WSpec("cv_w_pw2", (D, D), 0, 1, 1),
    WSpec("gdn_w_in", (4, D, (4 * D + 2 * H) // 4), 0, 1, 2, lead=True),
    WSpec("gdn_w_out", (D, D), 0, 1, 3),
    WSpec("mlp_w1_0", (D, DFF), 1, 0, 4, layer=0),
    WSpec("mlp_w1_1", (D, DFF), 1, 0, 4, layer=1),
    WSpec("mlp_w2_0", (DFF, D), 0, 1, 5, layer=0),
    WSpec("mlp_w2_1", (DFF, D), 0, 1, 5, layer=1),
]
FLIPS = [(1, 0, 0), (0, 1, 0), (1, 1, 0)]
SIB = (0, 0, 1)


def _chip(x, y):
    return 2 * x + y


def _prefetch_call(name, body, grid, in_specs, out_specs, out_shape, pref, args, aliases=None):
    return pl.pallas_call(
        body, name=name, out_shape=out_shape, input_output_aliases=aliases or {},
        grid_spec=pltpu.PrefetchScalarGridSpec(num_scalar_prefetch=1, grid=grid, in_specs=in_specs, out_specs=out_specs),
        compiler_params=_cp(*(("arbitrary",) * len(grid))))(pref, *args)


def place_shard(ws, shard, pref):
    rows, cols = ws.rows_cols(True, False)
    tr = _ew_rows(rows, cols)
    nr = rows // tr
    stacked = ws.layer is not None
    layer = ws.layer

    def body(_, s_ref, o_ref):
        o_ref[...] = s_ref[...].astype(BF16)

    in_spec = pl.BlockSpec(((None,) if stacked else ()) + (tr, cols),
                           (lambda i, p: (layer, i, 0)) if stacked else (lambda i, p: (i, 0)))
    return _prefetch_call("place_" + ws.name, body, (nr,), [in_spec], ws.spec(tr, cols, nr, shard=True),
                          jax.ShapeDtypeStruct(ws.full, BF16), pref, [shard])


def gather_weights(shards, wdw_shard, wcv_shard, pref):
    n = len(WSPECS)
    placed = [place_shard(ws, shards[ws.group], pref) for ws in WSPECS]

    def plan_a(in_refs, ip_refs, out_refs, pos):
        x, y, c = pos
        q = _chip(x, y)
        remote, local = [], []
        for i, ws in enumerate(WSPECS):
            mine = ws.full_view(ip_refs[i], q, c)
            for f in FLIPS:
                remote.append((f, mine, mine))
        for j, width in enumerate((D // 4, 3 * D // 4)):
            dst = out_refs[j].at[:, pl.ds(pl.multiple_of(q * width, 128), width)]
            local.append((in_refs[j], dst))
            for f in FLIPS:
                remote.append((f, in_refs[j], dst))
        return remote, local

    taps = [jax.ShapeDtypeStruct((KCV, D), F32), jax.ShapeDtypeStruct((KSC, 3 * D), F32)]
    nat, (wdw, wcv) = xor_exchange("gather_chips", [wdw_shard, wcv_shard], placed, taps, plan_a, 3 * (n + 2), 2)

    def plan_b(in_refs, ip_refs, out_refs, pos):
        x, y, c = pos
        remote = []
        for i, ws in enumerate(WSPECS):
            for (dx, dy, _) in FLIPS:
                got = ws.full_view(ip_refs[i], _chip(1 - x if dx else x, 1 - y if dy else y), c)
                remote.append((SIB, got, got))
        return remote, []

    nat, _ = xor_exchange("gather_cores", [], nat, [], plan_b, 3 * n)
    return nat, wdw, wcv


def half_add(ws, g, rsib, pref):
    rows, cols = ws.rows_cols(False, True)
    tr = _ew_rows(rows, cols)
    nr = rows // tr

    def body(_, a_ref, b_ref, o_ref):
        o_ref[...] = (a_ref[...] + b_ref[...]).astype(BF16)

    whole = ws.spec(tr, cols, nr, has_lead=ws.lead)
    return _prefetch_call("reduce_add_" + ws.name, body, (4, nr) if ws.lead else (nr,),
                          [ws.spec(tr, cols, nr, half=True, has_lead=ws.lead), whole], whole,
                          jax.ShapeDtypeStruct(ws.half_full_shape(), BF16), pref, [g, rsib])


def shard_sum(ws, s, parts, buf, pref):
    rows, cols = ws.rows_cols(True, True)
    tr = _ew_rows(rows, cols)
    nr = rows // tr
    stacked = ws.layer is not None

    def body(_, s_ref, p_ref, *rest):
        rest[-1][...] = ((s_ref[...].astype(F32) + p_ref[0].astype(F32)) + p_ref[1].astype(F32)) + p_ref[2].astype(F32)

    in_specs = [ws.spec(tr, cols, nr, shard=True, has_lead=ws.lead), pl.BlockSpec((3, tr, cols), lambda i, p: (0, i, 0))]
    args, aliases = [s, parts], {}
    if buf is not None:
        in_specs.append(pl.BlockSpec(memory_space=pl.ANY))
        args.append(buf)
        aliases = {3: 0}
    shape = ((2,) if stacked else ()) + ws.shard_shape()
    return _prefetch_call("reduce_sum_" + ws.name, body, (nr,), in_specs, ws.spec(tr, cols, nr, half=True, stacked=stacked),
                          jax.ShapeDtypeStruct(shape, F32), pref, args, aliases)


def reduce_grads(grads, pref):
    n = len(WSPECS)

    def plan1(in_refs, ip_refs, out_refs, pos):
        c = pos[2]
        return [(SIB, ws.full_view(in_refs[i], None, 1 - c), out_refs[i]) for i, ws in enumerate(WSPECS)], []

    halves = [jax.ShapeDtypeStruct(ws.half_full_shape(), F32) for ws in WSPECS]
    _, rsib = xor_exchange("reduce_cores", grads, [], halves, plan1, n)
    sums = [half_add(ws, grads[i], rsib[i], pref) for i, ws in enumerate(WSPECS)]

    def plan2(in_refs, ip_refs, out_refs, pos):
        x, y, c = pos
        remote = []
        for i, ws in enumerate(WSPECS):
            for s, (dx, dy, _) in enumerate(FLIPS):
                qq = _chip(1 - x if dx else x, 1 - y if dy else y)
                remote.append(((dx, dy, 0), ws.full_view(in_refs[i], qq), out_refs[i].at[s]))
        return remote, []

    parts = [jax.ShapeDtypeStruct((3,) + ws.shard_half_shape(), BF16) for ws in WSPECS]
    _, got = xor_exchange("reduce_chips", sums, [], parts, plan2, 3 * n)
    bufs = {}
    for i, ws in enumerate(WSPECS):
        bufs[ws.group] = shard_sum(ws, sums[i], got[i], bufs.get(ws.group), pref)

    def plan3(in_refs, ip_refs, out_refs, pos):
        c = pos[2]
        remote = []
        for ws in WSPECS:
            mine = ws.shard_view(ip_refs[ws.group], c)
            remote.append((SIB, mine, mine))
        return remote, []

    return xor_exchange("reduce_swap", [], [bufs[g] for g in sorted(bufs)], [], plan3, n)[0]


def gather_small(buf):
    flips = [(dx, dy, dc) for dx in (0, 1) for dy in (0, 1) for dc in (0, 1)][1:]

    def plan(in_refs, ip_refs, out_refs, pos):
        x, y, c = pos
        me = 4 * x + 2 * y + c
        dst = out_refs[0].at[me]
        return [(f, in_refs[0], dst) for f in flips], [(in_refs[0], dst)]

    return xor_exchange("gather_small", [buf], [], [jax.ShapeDtypeStruct((8,) + buf.shape, F32)], plan, 7, 1)[1][0]


def _pad_rows(a, rows):
    return jnp.pad(a, ((0, rows - a.shape[0]), (0, 0)))


def _row1(v):
    v = v.reshape((1, -1))
    return jnp.pad(v, ((0, 0), (0, D - v.shape[1])))


def _rms_fwd(name, h, g, tm):
    return row_call(name, lambda hh, gg: ((f_rms(hh, gg),), ()), [(h, D, 0)], [g], [(D, BF16)], [], tm)[0]


def _res_rms(h, g):
    return (h, f_rms(h, g)), ()


def _rms_bwd_epi(dhn, h, dres, g):
    _, vjp = jax.vjp(f_rms, h, g)
    dh, dg = vjp(dhn)
    dh = dh + dres
    return (dh,), (dg, jnp.sum(dh, axis=0, keepdims=True))


def _mlp_bwd(tag, dh, h, g, w1, w2, hn, z1):
    dz1 = mm("mlp_down_dx" + tag, dh, w2, "NT", BF16,
             epi=lambda acc, z: acc * (2.0 * jnp.maximum(z.astype(F32), 0.0)), epi_ins=[(z1, "tile")])
    dw2 = mm_tn("mlp_down_dw" + tag, z1, dh, a_fn=f_relu2)
    dh_in, dg, colsum = mm("mlp_up_dx" + tag, dz1, w1, "NT", [F32], epi=_rms_bwd_epi,
                           epi_ins=[(h, "tile"), (dh, "tile"), (g, "row")], accs=[(1, D), (1, D)])
    dw1 = mm_tn("mlp_up_dw" + tag, hn, dz1)
    return dh_in, dg, colsum, dw1, dw2


def kernel(x, norm_mix_g, norm_ffn_g, final_norm_g, cv_w_pw1, cv_b_pw1, cv_w_dw, cv_b_dw, cv_ln_g, cv_ln_b, cv_w_pw2, cv_b_pw2, gdn_w_in, gdn_conv_w, gdn_a_log, gdn_dt_bias, gdn_norm_g, gdn_w_out, mlp_w1, mlp_w2, loss_target, m_norm_mix_g, m_norm_ffn_g, m_final_norm_g, m_cv_w_pw1, m_cv_b_pw1, m_cv_w_dw, m_cv_b_dw, m_cv_ln_g, m_cv_ln_b, m_cv_w_pw2, m_cv_b_pw2, m_gdn_w_in, m_gdn_conv_w, m_gdn_a_log, m_gdn_dt_bias, m_gdn_norm_g, m_gdn_w_out, m_mlp_w1, m_mlp_w2, v_norm_mix_g, v_norm_ffn_g, v_final_norm_g, v_cv_w_pw1, v_cv_b_pw1, v_cv_w_dw, v_cv_b_dw, v_cv_ln_g, v_cv_ln_b, v_cv_w_pw2, v_cv_b_pw2, v_gdn_w_in, v_gdn_conv_w, v_gdn_a_log, v_gdn_dt_bias, v_gdn_norm_g, v_gdn_w_out, v_mlp_w1, v_mlp_w2):
    env = dict(locals())
    bl, S, _ = x.shape
    T = bl * S
    tm = min(256, S)
    xf = x.reshape((T, D))
    tgt = loss_target.reshape((T, D))

    chip = 2 * lax.axis_index("x") + lax.axis_index("y")
    pref = jnp.stack([chip, lax.axis_index("c")]).astype(jnp.int32)
    big = [cv_w_pw1[0], cv_w_pw2[0], gdn_w_in[0], gdn_w_out[0], mlp_w1, mlp_w2]
    nat, wdw, wcv = gather_weights(big, cv_w_dw[0], gdn_conv_w[0], pref)
    w_pw1, w_pw2, w_in_sm, w_out, w1_0, w1_1, w2_0, w2_1 = nat
    w_in = jnp.transpose(w_in_sm, (1, 0, 2)).reshape((D, 4 * D + 2 * H))
    w_qkv, w_z = w_in[:, :3 * D], w_in[:, 3 * D:4 * D]
    w_qkvz = w_in[:, :4 * D]
    w_ab = jnp.pad(w_in[:, 4 * D:], ((0, 0), (0, HD - 2 * H)))
    wdw_p, wcv_p = _pad_rows(wdw, HB_CV), _pad_rows(wcv, HB_SC)
    alog_p = jnp.pad(gdn_a_log, ((0, 0), (0, HD - H)))
    dtb_p = jnp.pad(gdn_dt_bias, ((0, 0), (0, HD - H)))
    g_mix0, g_mix1 = norm_mix_g[0:1], norm_mix_g[1:2]
    g_ffn0, g_ffn1 = norm_ffn_g[0:1], norm_ffn_g[1:2]
    g_fin = final_norm_g.reshape((1, D))

    hn0 = _rms_fwd("rms_mix0", xf, g_mix0, tm)
    u = mm("cv_pw1", hn0, w_pw1, "NN", F32, epi=lambda acc, b: acc + b, epi_ins=[(cv_b_pw1, "row")])
    dwc, s_act = dwconv_fwd("cv_dwconv", u, 2 * D, wdw_p, cv_b_dw, S, KCV, HB_CV, f_glu, (f_ln_silu, (cv_ln_g, cv_ln_b)), tm)
    h1, hnf0 = mm("cv_pw2", s_act, w_pw2, "NN", [F32, BF16], epi=lambda acc, b, r, g: _res_rms(acc + b + r, g),
                  epi_ins=[(cv_b_pw2, "row"), (xf, "tile"), (g_ffn0, "row")])
    z1_0 = mm("mlp_up0", hnf0, w1_0, "NN", BF16)
    h2, hn2 = mm("mlp_down0", z1_0, w2_0, "NN", [F32, BF16], a_fn=f_relu2, epi=lambda acc, r, g: _res_rms(acc + r, g),
                 epi_ins=[(h1, "tile"), (g_mix1, "row")])

    pqkvz = mm("gdn_in", hn2, w_qkvz, "NN", F32)
    pab = mm("gdn_in_ab", hn2, w_ab, "NN", F32)
    cpre = dwconv_fwd("gdn_conv", pqkvz, 3 * D, wcv_p, None, S, KSC, HB_SC, lambda v: v, None, tm)[0]
    gu, gw, gqg, gkg, gqk, gt, geg = gdn_prep_fwd(cpre, pab, alog_p, dtb_p)
    o, sall = gdn_scan_fwd(gu, gw, gqg, gkg, gqk, geg, S)
    on = row_call("gdn_post", lambda oo, zz, ng: ((f_post(oo, zz, ng),), ()), [(o, D, 0), (pqkvz, D, 3)],
                  [gdn_norm_g], [(D, BF16)], [], tm)[0]
    h3, hnf1 = mm("gdn_out", on, w_out, "NN", [F32, BF16], epi=lambda acc, r, g: _res_rms(acc + r, g),
                  epi_ins=[(h2, "tile"), (g_ffn1, "row")])
    z1_1 = mm("mlp_up1", hnf1, w1_1, "NN", BF16)

    def head(acc, res, tt, gg):
        def loss_of(h_, g_):
            return 0.5 * jnp.sum(jnp.mean(jnp.square(f_rms(h_, g_) - tt), axis=-1))
        lv, (dh_, dg_) = jax.value_and_grad(loss_of, (0, 1))(acc + res, gg)
        return (dh_,), (dg_, jnp.full((1, D), lv, F32))

    dh4, dg_fin, loss_row = mm("mlp_down1", z1_1, w2_1, "NN", [F32], a_fn=f_relu2, epi=head,
                               epi_ins=[(h3, "tile"), (tgt, "tile"), (g_fin, "row")], accs=[(1, D), (1, D)])

    dh3, dg_ffn1, _, dw1_1, dw2_1 = _mlp_bwd("1", dh4, h3, g_ffn1, w1_1, w2_1, hnf1, z1_1)
    dw_out = mm_tn("gdn_out_dw", on, dh3)

    def post_bwd(don, oo, zz, ng):
        _, vjp = jax.vjp(f_post, oo, zz, ng)
        do_, dz_, dng_ = vjp(don)
        return (do_, dz_), (dng_,)

    do, dz, dng = mm("gdn_out_dx", dh3, w_out, "NT", [F32, F32], epi=post_bwd,
                     epi_ins=[(o, "tile"), (pqkvz, ("cols", 3)), (gdn_norm_g, "whole")], accs=[(1, HD)])
    du, dw, dqg, dkg, dqk, deg = gdn_scan_bwd(do, gu, gw, gqg, gkg, gqk, geg, sall, S)
    dcpre, dpab, dalog, ddtb = gdn_prep_bwd(cpre, pab, alog_p, dtb_p, gt, du, dw, dqg, dkg, dqk, deg)
    dqkv, dwcv, _ = dwconv_bwd("gdn_conv_bwd", dcpre, pqkvz, 3 * D, wcv_p, S, KSC, HB_SC, lambda v: v, lambda xx, da: da, tm)
    dhn2 = mm("gdn_in_dx_ab", dpab, w_ab, "NT", F32)
    dhn2 = mm("gdn_in_dx_z", dz, w_z, "NT", F32, epi=lambda acc, r: acc + r, epi_ins=[(dhn2, "tile")])
    dh2, dg_mix1, _ = mm("gdn_in_dx_qkv", dqkv, w_qkv, "NT", [F32],
                         epi=lambda acc, prev, hh, rr, gg: _rms_bwd_epi(acc + prev, hh, rr, gg),
                         epi_ins=[(dhn2, "tile"), (h2, "tile"), (dh3, "tile"), (g_mix1, "row")], accs=[(1, D), (1, D)])
    dw_in = jnp.concatenate([mm_tn("gdn_in_dw_qkv", hn2, dqkv), mm_tn("gdn_in_dw_z", hn2, dz),
                             mm_tn("gdn_in_dw_ab", hn2, dpab)[:, :2 * H]], axis=1)

    dh1, dg_ffn0, db_pw2, dw1_0, dw2_0 = _mlp_bwd("0", dh2, h1, g_ffn0, w1_0, w2_0, hnf0, z1_0)
    dw_pw2 = mm_tn("cv_pw2_dw", s_act, dh1)

    def ln_bwd(ds, xx, gg, bb):
        _, vjp = jax.vjp(f_ln_silu, xx, gg, bb)
        dx_, dg_, db_ = vjp(ds)
        return (dx_,), (dg_, db_, jnp.sum(dx_, axis=0, keepdims=True))

    ddw, dln_g, dln_b, db_dw = mm("cv_pw2_dx", dh1, w_pw2, "NT", [F32], epi=ln_bwd,
                                  epi_ins=[(dwc, "tile"), (cv_ln_g, "row"), (cv_ln_b, "row")], accs=[(1, D)] * 3)
    du_cv, dwdw, db_pw1 = dwconv_bwd("cv_dwconv_bwd", ddw, u, 2 * D, wdw_p, S, KCV, HB_CV, f_glu, _glu_bwd, tm)
    dw_pw1 = mm_tn("cv_pw1_dw", hn0, du_cv)
    grad_x, dg_mix0, _ = mm("cv_pw1_dx", du_cv, w_pw1, "NT", [F32], epi=_rms_bwd_epi,
                            epi_ins=[(xf, "tile"), (dh1, "tile"), (g_mix0, "row")], accs=[(1, D), (1, D)])

    dw_in_sm = jnp.transpose(dw_in.reshape((D, 4, D + 4)), (1, 0, 2))
    g_pw1, g_pw2, g_in, g_out, g_w1, g_w2 = reduce_grads([dw_pw1, dw_pw2, dw_in_sm, dw_out, dw1_0, dw1_1, dw2_0, dw2_1], pref)

    small = jnp.concatenate([
        dg_mix0, dg_mix1, dg_ffn0, dg_ffn1, dg_fin, db_pw1.reshape((2, D)), db_dw, dln_g, dln_b, db_pw2,
        _row1(dalog[:, :H]), _row1(ddtb[:, :H]), _row1(dng), loss_row, jnp.zeros((1, D), F32),
        dwdw, dwcv[:KSC].reshape((3 * KSC, D)), jnp.zeros((NSMALL - 48 - 3 * KSC, D), F32)], axis=0)
    small_all = gather_small(small)

    def pack(a, b, c_, d, e, f, g_, h_, i_, j_, k_):
        return jnp.concatenate([a, b, c_.reshape((1, D)), d.reshape((2, D)), e, f, g_, h_, _row1(i_), _row1(j_), _row1(k_),
                                jnp.zeros((2, D), F32)], axis=0)

    order = lambda p: (p + "norm_mix_g", p + "norm_ffn_g", p + "final_norm_g", p + "cv_b_pw1", p + "cv_b_dw", p + "cv_ln_g",
                       p + "cv_ln_b", p + "cv_b_pw2", p + "gdn_a_log", p + "gdn_dt_bias", p + "gdn_norm_g")
    w16, m16, v16 = (pack(*[env[nm] for nm in order(p)]) for p in ("", "m_", "v_"))

    def small_step(ga, ww, mm_, vv):
        gsum = ga[0]
        for dev in range(1, 8):
            gsum = gsum + ga[dev]
        delta, m2, v2 = f_adamw(ww, gsum[:16], mm_, vv)
        return gsum, delta, m2, v2

    def small_body(ga_ref, w_ref, m_ref, v_ref, g_out, d_out, m_out, v_out):
        gsum, delta, m2, v2 = small_step(ga_ref[...], w_ref[...], m_ref[...], v_ref[...])
        g_out[...] = gsum
        d_out[...] = delta
        m_out[...] = m2
        v_out[...] = v2

    vm = pl.BlockSpec(memory_space=pltpu.VMEM)
    sg, sd, sm, sv = pl.pallas_call(
        small_body, name="adamw_small", in_specs=[vm] * 4, out_specs=[vm] * 4,
        out_shape=[jax.ShapeDtypeStruct((NSMALL, D), F32)] + [jax.ShapeDtypeStruct((16, D), F32)] * 3)(small_all, w16, m16, v16)

    def unpack(b):
        return (b[0:2], b[2:4], b[4], b[5:7].reshape((1, 2 * D)), b[7:8], b[8:9], b[9:10], b[10:11],
                b[11:12, :H], b[12:13, :H], b[13:14, :HD])

    loss = sg[14, 0]
    g_dw = lax.dynamic_slice(sg[16:16 + KCV], (0, chip * (D // 4)), (KCV, D // 4))
    g_cv = lax.dynamic_slice(sg[48:48 + 3 * KSC].reshape((KSC, 3 * D)), (0, chip * (3 * D // 4)), (KSC, 3 * D // 4))

    def adamw(name, w, g, m, v):
        lead = w.shape[:-2]
        if len(lead) == 1 and lead[0] == 1:
            d, m2, v2 = ew_call(name, f_adamw, [w[0], g.reshape(w.shape[1:]), m[0], v[0]], 3)
            return g.reshape(w.shape), d[None], m2[None], v2[None]
        return (g.reshape(w.shape),) + tuple(ew_call(name, f_adamw, [w, g.reshape(w.shape), m, v], 3))

    res = {
        "cv_w_pw1": adamw("adamw_pw1", cv_w_pw1, g_pw1, m_cv_w_pw1, v_cv_w_pw1),
        "cv_w_dw": adamw("adamw_dw", cv_w_dw, g_dw, m_cv_w_dw, v_cv_w_dw),
        "cv_w_pw2": adamw("adamw_pw2", cv_w_pw2, g_pw2, m_cv_w_pw2, v_cv_w_pw2),
        "gdn_w_in": adamw("adamw_win", gdn_w_in, g_in, m_gdn_w_in, v_gdn_w_in),
        "gdn_conv_w": adamw("adamw_cvw", gdn_conv_w, g_cv, m_gdn_conv_w, v_gdn_conv_w),
        "gdn_w_out": adamw("adamw_wout", gdn_w_out, g_out, m_gdn_w_out, v_gdn_w_out),
        "mlp_w1": adamw("adamw_w1", mlp_w1, g_w1, m_mlp_w1, v_mlp_w1),
        "mlp_w2": adamw("adamw_w2", mlp_w2, g_w2, m_mlp_w2, v_mlp_w2),
    }
    names = ("norm_mix_g", "norm_ffn_g", "final_norm_g", "cv_b_pw1", "cv_b_dw", "cv_ln_g", "cv_ln_b", "cv_b_pw2",
             "gdn_a_log", "gdn_dt_bias", "gdn_norm_g")
    for nm, gg, dd, mm_, vv in zip(names, unpack(sg), unpack(sd), unpack(sm), unpack(sv)):
        res[nm] = (gg, dd, mm_, vv)
    weights = ("norm_mix_g", "norm_ffn_g", "final_norm_g", "cv_w_pw1", "cv_b_pw1", "cv_w_dw", "cv_b_dw", "cv_ln_g",
               "cv_ln_b", "cv_w_pw2", "cv_b_pw2", "gdn_w_in", "gdn_conv_w", "gdn_a_log", "gdn_dt_bias", "gdn_norm_g",
               "gdn_w_out", "mlp_w1", "mlp_w2")
    outs = [loss, grad_x.reshape(x.shape)]
    for kind in range(4):
        outs += [res[nm][kind] for nm in weights]
    return tuple(outs)
```

```python
import functools

import jax
import jax.numpy as jnp
from jax import lax
from jax.experimental import pallas as pl
from jax.experimental.pallas import tpu as pltpu

F32, BF16 = jnp.float32, jnp.bfloat16
D = 1024
H = 8
HD = 128
CH = 64
DFF = 4 * D
KCV, HB_CV = 31, 32
KSC, HB_SC = 4, 8
EPS = 1e-6
LR, B1, B2, EPS_A, WD, STEP = 0.001, 0.9, 0.999, 1e-08, 0.01, 10
VMEM_LIMIT = 56 * 1024 * 1024
SUB = 32
NSMALL = 64
MESH = pl.DeviceIdType.MESH


def _cp(*sem):
    return pltpu.CompilerParams(dimension_semantics=sem, vmem_limit_bytes=VMEM_LIMIT)


def f_rms(h, g):
    return h * lax.rsqrt(jnp.mean(h * h, axis=-1, keepdims=True) + EPS) * g


def f_silu(x):
    return x * jax.nn.sigmoid(x)


def f_glu(u):
    return u[:, :D] * jax.nn.sigmoid(u[:, D:])


def f_ln_silu(x, g, b):
    mu = jnp.mean(x, axis=-1, keepdims=True)
    xc = x - mu
    y = xc * lax.rsqrt(jnp.mean(xc * xc, axis=-1, keepdims=True) + EPS)
    return f_silu(y * g + b)


def f_relu2(z):
    r = jnp.maximum(z.astype(F32), 0.0)
    return r * r


def f_post(o, z, ng):
    outs = []
    for h in range(H):
        oh = o[:, h * HD:(h + 1) * HD]
        y = oh * lax.rsqrt(jnp.mean(oh * oh, axis=-1, keepdims=True) + EPS) * ng
        outs.append(y * f_silu(z[:, h * HD:(h + 1) * HD]))
    return jnp.concatenate(outs, axis=1)


def f_adamw(w, g, m, v):
    m2 = B1 * m + (1.0 - B1) * g
    v2 = B2 * v + (1.0 - B2) * (g * g)
    m_hat = m2 / (1.0 - B1 ** STEP)
    v_hat = v2 / (1.0 - B2 ** STEP)
    delta = -LR * (m_hat / (jnp.sqrt(v_hat) + EPS_A) + WD * w)
    return delta, m2, v2


def _dot_raw(a, b, mode):
    dims = {"NN": ((1,), (0,)), "NT": ((1,), (1,)), "TN": ((0,), (0,))}[mode]
    return lax.dot_general(a.astype(BF16), b.astype(BF16), (dims, ((), ())), preferred_element_type=F32)


@functools.partial(jax.custom_vjp, nondiff_argnums=(2,))
def _dot_vjp(a, b, mode):
    return _dot_raw(a, b, mode)


def _dot_fwd(a, b, mode):
    return _dot_raw(a, b, mode), (a, b)


def _dot_bwd(mode, res, dc):
    a, b = res
    if mode == "NN":
        return _dot_vjp(dc, b, "NT"), _dot_vjp(a, dc, "TN")
    if mode == "NT":
        return _dot_vjp(dc, b, "NN"), _dot_vjp(dc, a, "TN")
    return _dot_vjp(b, dc, "NT"), _dot_vjp(a, dc, "NN")


_dot_vjp.defvjp(_dot_fwd, _dot_bwd)


def _split(x):
    xh = x.astype(BF16)
    return xh, (x - xh.astype(F32)).astype(BF16)


def _dot_split(xs, ys):
    (xh, xl), (yh, yl) = xs, ys
    return _dot_raw(xh, yh, "NN") + (_dot_raw(xh, yl, "NN") + _dot_raw(xl, yh, "NN"))


def _tril_inverse(a_list):
    ri = lax.broadcasted_iota(jnp.int32, (CH, CH), 0)
    ci = lax.broadcasted_iota(jnp.int32, (CH, CH), 1)
    eye = (ri == ci).astype(F32)
    ts = None
    for lvl in range(CH.bit_length() - 1):
        same_pair = jnp.right_shift(ri, lvl + 1) == jnp.right_shift(ci, lvl + 1)
        quarter = (jnp.bitwise_and(jnp.right_shift(ri, lvl), 1) == 1) & (jnp.bitwise_and(jnp.right_shift(ci, lvl), 1) == 0)
        offs = [jnp.where(same_pair & quarter, a, 0.0) for a in a_list]
        if ts is None:
            ts = [eye - off for off in offs]
            continue
        tsp = [_split(t) for t in ts]
        mids = [_dot_split(tp, _split(off)) for tp, off in zip(tsp, offs)]
        ts = [t - _dot_split(_split(m), tp) for t, m, tp in zip(ts, mids, tsp)]
    return ts


@jax.custom_vjp
def _stored_solve(a, t, rhs):
    return _dot_raw(t, rhs, "NN")


def _stored_solve_fwd(a, t, rhs):
    sol = _dot_raw(t, rhs, "NN")
    return sol, (t, sol)


def _stored_solve_bwd(res, g):
    t, sol = res
    g_rhs = _dot_vjp(t, g, "TN")
    return -_dot_vjp(g_rhs, sol, "NT"), jnp.zeros_like(t), g_rhs


_stored_solve.defvjp(_stored_solve_fwd, _stored_solve_bwd)


def _lane_pick(row, idx, width):
    sel = lax.broadcasted_iota(jnp.int32, (1, width), 1) == idx
    return jnp.sum(jnp.where(sel, row, 0.0), axis=1, keepdims=True)


def f_prep(cqs, cks, cvs, araws, braws, alogs, dtbs, t_stored, dot):
    ri = lax.broadcasted_iota(jnp.int32, (CH, CH), 0)
    ci = lax.broadcasted_iota(jnp.int32, (CH, CH), 1)
    eye = (ri == ci).astype(F32)
    low = (ri >= ci).astype(F32)
    last = lax.broadcasted_iota(jnp.int32, (CH, 1), 0) == CH - 1
    nh = range(len(cqs))
    qs, ks, vbs, kbs, gcs, decays = [], [], [], [], [], []
    for h in nh:
        q = f_silu(cqs[h])
        qs.append(q * lax.rsqrt(jnp.sum(q * q, axis=-1, keepdims=True) + 1e-6) * (HD ** -0.5))
        k = f_silu(cks[h])
        k = k * lax.rsqrt(jnp.sum(k * k, axis=-1, keepdims=True) + 1e-6)
        ks.append(k)
        beta = jax.nn.sigmoid(braws[h])
        sp_in = araws[h] + dtbs[h]
        softplus = jnp.maximum(sp_in, 0.0) + jnp.log(1.0 + jnp.exp(-jnp.abs(sp_in)))
        g = -jnp.exp(alogs[h]) * softplus
        g_row = jnp.sum(eye * g, axis=0, keepdims=True)
        gc = jnp.sum(low * g_row, axis=1, keepdims=True)
        gc_row = jnp.sum(eye * gc, axis=0, keepdims=True)
        gcs.append(gc)
        decays.append(jnp.exp(jnp.where(ri >= ci, gc - gc_row, -1e30)))
        vbs.append(f_silu(cvs[h]) * beta)
        kbs.append(k * beta)
    kks = [dot(kbs[h], ks[h], "NT") for h in nh]
    a_list = [jnp.where(ri > ci, kks[h] * decays[h], 0.0) for h in nh]
    if t_stored is None:
        ts = _tril_inverse(a_list)
        solve = lambda h, rhs: dot(ts[h], rhs, "NN")
    else:
        ts = t_stored
        solve = lambda h, rhs: _stored_solve(a_list[h], t_stored[h], rhs)
    egcs = [jnp.exp(gc) for gc in gcs]
    us = [solve(h, vbs[h]) for h in nh]
    ws = [solve(h, kbs[h] * egcs[h]) for h in nh]
    qks = [dot(qs[h], ks[h], "NT") * decays[h] for h in nh]
    qgs = [qs[h] * egcs[h] for h in nh]
    gls = [jnp.sum(jnp.where(last, gc, 0.0), axis=0, keepdims=True) for gc in gcs]
    kgs = [ks[h] * jnp.exp(gls[h] - gcs[h]) for h in nh]
    egs = [jnp.exp(gl) * jnp.ones((1, HD), F32) for gl in gls]
    return us, ws, qks, qgs, kgs, egs, ts


def f_scan(ss, us, ws, qgs, kgs, qks, egs, dot):
    nh = range(len(ss))
    ws_s = [dot(ws[h], ss[h], "NN") for h in nh]
    qs_s = [dot(qgs[h], ss[h], "NN") for h in nh]
    vns = [us[h] - ws_s[h] for h in nh]
    os_ = [qs_s[h] + dot(qks[h], vns[h], "NN") for h in nh]
    s2s = [ss[h] * egs[h] + dot(kgs[h], vns[h], "TN") for h in nh]
    return os_, s2s


def row_call(name, fn, rows, pars, out_rows, out_accs, tm):
    T = rows[0][0].shape[0]
    n_r, n_p, n_o = len(rows), len(pars), len(out_rows)
    in_specs = [pl.BlockSpec((tm, w), functools.partial(lambda i, cb: (i, cb), cb=cb)) for (_, w, cb) in rows]
    in_specs += [pl.BlockSpec(p.shape, functools.partial(lambda i, nd: (0,) * nd, nd=p.ndim)) for p in pars]
    out_specs = [pl.BlockSpec((tm, w), lambda i: (i, 0)) for (w, _) in out_rows]
    out_specs += [pl.BlockSpec(s, lambda i: (0, 0)) for s in out_accs]
    out_shape = [jax.ShapeDtypeStruct((T, w), dt) for (w, dt) in out_rows]
    out_shape += [jax.ShapeDtypeStruct(s, F32) for s in out_accs]

    def body(*refs):
        rin, pin = refs[:n_r], refs[n_r:n_r + n_p]
        rout, aout = refs[n_r + n_p:n_r + n_p + n_o], refs[n_r + n_p + n_o:]
        if aout:
            @pl.when(pl.program_id(0) == 0)
            def _():
                for a in aout:
                    a[...] = jnp.zeros(a.shape, F32)
        pv = [p[...] for p in pin]

        def step(r, carry):
            sl = pl.ds(pl.multiple_of(r * SUB, SUB), SUB)
            outs, accs = fn(*[x[sl, :] for x in rin], *pv)
            for o, val in zip(rout, outs):
                o[sl, :] = val.astype(o.dtype)
            for a, val in zip(aout, accs):
                a[...] += val
            return carry

        lax.fori_loop(0, tm // SUB, step, 0)

    return pl.pallas_call(body, name=name, grid=(T // tm,), in_specs=in_specs, out_specs=out_specs,
                          out_shape=out_shape, compiler_params=_cp("arbitrary"))(*[r[0] for r in rows], *pars)


EW_TILE_ELEMS = 256 * 1024


def _ew_rows(R, Cc):
    if R * Cc <= EW_TILE_ELEMS or R % 8:
        return R
    tr = 8
    while tr * 2 * Cc <= EW_TILE_ELEMS and R % (tr * 2) == 0:
        tr *= 2
    return tr


def ew_call(name, fn, ins, n_out):
    shape = ins[0].shape
    lead = shape[:-2]
    R, Cc = shape[-2:]
    tr = _ew_rows(R, Cc)
    grid = lead + (R // tr,)
    nl = len(lead)
    spec = pl.BlockSpec((None,) * nl + (tr, Cc), lambda *idx: idx + (0,))

    def body(*refs):
        outs = fn(*[r[...] for r in refs[:len(ins)]])
        for o, val in zip(refs[len(ins):], outs):
            o[...] = val

    return pl.pallas_call(body, name=name, grid=grid, in_specs=[spec] * len(ins), out_specs=[spec] * n_out,
                          out_shape=[jax.ShapeDtypeStruct(shape, F32)] * n_out,
                          compiler_params=_cp(*(("arbitrary",) * len(grid))))(*ins)


MM_RESIDENT_BYTES = 8 * 1024 * 1024


def mm(name, a, b, mode, out_dtype, a_fn=None, epi=None, epi_ins=(), accs=(), tm=512):
    sub_epi = epi is not None and isinstance(out_dtype, (list, tuple))
    M, K = a.shape
    N = b.shape[1] if mode == "NN" else b.shape[0]
    tn = N if K * N * 2 <= MM_RESIDENT_BYTES else min(N, 1024)
    tm = min(tm if tn <= 1024 else tm // 2, M)
    multi = isinstance(out_dtype, (list, tuple))
    dts = list(out_dtype) if multi else [out_dtype]
    n_e, n_o = len(epi_ins), len(dts)
    in_specs = [pl.BlockSpec((tm, K), lambda j, i: (i, 0)),
                pl.BlockSpec((K, tn), lambda j, i: (0, j)) if mode == "NN" else pl.BlockSpec((tn, K), lambda j, i: (j, 0))]
    row_kinds = []
    for (arr, kind) in epi_ins:
        if kind == "tile" or isinstance(kind, tuple):
            cb = kind[1] if isinstance(kind, tuple) else 0
            in_specs.append(pl.BlockSpec((tm, tn), functools.partial(lambda j, i, cb: (i, cb + j), cb=cb)))
            row_kinds.append(True)
        elif kind == "row":
            in_specs.append(pl.BlockSpec((1, tn), lambda j, i: (0, j)))
            row_kinds.append(False)
        else:
            in_specs.append(pl.BlockSpec(arr.shape, lambda j, i: (0, 0)))
            row_kinds.append(False)

    def body(a_ref, b_ref, *rest):
        e_refs, o_refs, acc_refs = rest[:n_e], rest[n_e:n_e + n_o], rest[n_e + n_o:n_e + n_o + len(accs)]
        av = a_ref[...]
        if a_fn is not None:
            av = a_fn(av)
        res = _dot_raw(av, b_ref[...], mode)
        if epi is None or not sub_epi:
            if epi is not None:
                res = epi(res, *[r[...] for r in e_refs])
            o_refs[0][...] = res.astype(o_refs[0].dtype)
            return
        prod = rest[-1]
        prod[...] = res
        if acc_refs:
            @pl.when((pl.program_id(0) == 0) & (pl.program_id(1) == 0))
            def _():
                for r in acc_refs:
                    r[...] = jnp.zeros(r.shape, F32)
        small = [None if is_rows else r[...] for r, is_rows in zip(e_refs, row_kinds)]

        def step(k, carry):
            sl = pl.ds(pl.multiple_of(k * SUB, SUB), SUB)
            out = epi(prod[sl, :], *[r[sl, :] if is_rows else sm for r, is_rows, sm in zip(e_refs, row_kinds, small)])
            tiles, contribs = out if multi else ((out,), ())
            for r, t in zip(o_refs, tiles):
                r[sl, :] = t.astype(r.dtype)
            for r, t in zip(acc_refs, contribs):
                r[...] += t
            return carry

        lax.fori_loop(0, tm // SUB, step, 0)

    out_specs = [pl.BlockSpec((tm, tn), lambda j, i: (i, j))] * n_o + [pl.BlockSpec(s, lambda j, i: (0, 0)) for s in accs]
    out_shape = [jax.ShapeDtypeStruct((M, N), dt) for dt in dts] + [jax.ShapeDtypeStruct(s, F32) for s in accs]
    res = pl.pallas_call(body, name=name, grid=(N // tn, M // tm), in_specs=in_specs, out_specs=out_specs,
                         out_shape=out_shape, scratch_shapes=[pltpu.VMEM((tm, tn), F32)] if sub_epi else [],
                         compiler_params=_cp("arbitrary", "arbitrary"))(a, b, *[e[0] for e in epi_ins])
    return res if multi else res[0]


def mm_tn(name, a, g, a_fn=None, a_cols=None, tt=512):
    T = a.shape[0]
    ka, acb = (a.shape[1], 0) if a_cols is None else a_cols
    N = g.shape[1]
    tt = min(tt, T)
    tka, tn = min(ka, 1024), min(N, 1024)
    nkb = ka // tka

    def body(a_ref, g_ref, o_ref):
        @pl.when(pl.program_id(2) == 0)
        def _():
            o_ref[...] = jnp.zeros(o_ref.shape, F32)
        av = a_ref[...]
        if a_fn is not None:
            av = a_fn(av)
        o_ref[...] += _dot_raw(av, g_ref[...], "TN")

    return pl.pallas_call(body, name=name, grid=(nkb, N // tn, T // tt),
                          in_specs=[pl.BlockSpec((tt, tka), lambda ia, j, t: (t, acb * nkb + ia)),
                                    pl.BlockSpec((tt, tn), lambda ia, j, t: (t, j))],
                          out_specs=pl.BlockSpec((tka, tn), lambda ia, j, t: (ia, j)),
                          out_shape=jax.ShapeDtypeStruct((ka, N), F32),
                          compiler_params=_cp("arbitrary", "arbitrary", "arbitrary"))(a, g)


def dwconv_fwd(name, x, xw, w_pad, bias, S, K, HB, pre, post, tm):
    T = x.shape[0]
    C = w_pad.shape[1]
    nb, per_seq = tm // HB, S // tm
    has_b, has_post = bias is not None, post is not None

    def body(*refs):
        x_ref, xp_ref, w_ref = refs[:3]
        pos = 3
        b_ref = refs[pos] if has_b else None
        pos += has_b
        ppars = refs[pos:pos + (len(post[1]) if has_post else 0)]
        pos += len(ppars)
        c_ref = refs[pos]
        s_ref = refs[pos + 1] if has_post else None
        ext = refs[-1]
        first = (pl.program_id(0) % per_seq) == 0
        ext[0:HB, :] = jnp.where(first, 0.0, pre(xp_ref[...]))
        for r in range(tm // SUB):
            ext[HB + r * SUB:HB + (r + 1) * SUB, :] = pre(x_ref[r * SUB:(r + 1) * SUB, :])
        pv = [p[...] for p in ppars]
        assert not has_post or C == D
        for r in range(tm // SUB):
            for c0 in range(0, C, D):
                cols = slice(c0, c0 + D)
                acc = jnp.zeros((SUB, D), F32)
                if has_b:
                    acc = acc + b_ref[:, cols]
                for k in range(K):
                    off = HB + r * SUB - (K - 1) + k
                    acc = acc + w_ref[k:k + 1, cols] * ext[off:off + SUB, cols]
                c_ref[r * SUB:(r + 1) * SUB, cols] = acc
                if has_post:
                    s_ref[r * SUB:(r + 1) * SUB, :] = post[0](acc, *pv).astype(BF16)

    ins = [x, x, w_pad] + ([bias] if has_b else []) + (list(post[1]) if has_post else [])
    in_specs = [pl.BlockSpec((tm, xw), lambda i: (i, 0)),
                pl.BlockSpec((HB, xw), lambda i: (jnp.maximum(i * nb - 1, 0), 0)),
                pl.BlockSpec(w_pad.shape, lambda i: (0, 0))]
    in_specs += [pl.BlockSpec(p.shape, lambda i: (0, 0)) for p in ins[3:]]
    out_specs = [pl.BlockSpec((tm, C), lambda i: (i, 0))] * (1 + has_post)
    out_shape = [jax.ShapeDtypeStruct((T, C), F32)] + ([jax.ShapeDtypeStruct((T, C), BF16)] if has_post else [])
    return pl.pallas_call(body, name=name, grid=(T // tm,), in_specs=in_specs, out_specs=out_specs, out_shape=out_shape,
                          scratch_shapes=[pltpu.VMEM((HB + tm, C), F32)], compiler_params=_cp("arbitrary"))(*ins)


def dwconv_bwd(name, g, x, xw, w_pad, S, K, HB, pre, pre_bwd, tm):
    T = g.shape[0]
    C = w_pad.shape[1]
    nb, per_seq = tm // HB, S // tm
    nblk = T // HB

    def body(g_ref, gn_ref, x_ref, xp_ref, w_ref, dx_ref, dw_ref, dbx_ref, extg, exta):
        i = pl.program_id(0)
        first = (i % per_seq) == 0
        last = (i % per_seq) == per_seq - 1

        @pl.when(i == 0)
        def _():
            dw_ref[...] = jnp.zeros(dw_ref.shape, F32)
            dbx_ref[...] = jnp.zeros(dbx_ref.shape, F32)

        extg[tm:tm + HB, :] = jnp.where(last, 0.0, gn_ref[...])
        exta[0:HB, :] = jnp.where(first, 0.0, pre(xp_ref[...]))
        for r in range(tm // SUB):
            extg[r * SUB:(r + 1) * SUB, :] = g_ref[r * SUB:(r + 1) * SUB, :]
            exta[HB + r * SUB:HB + (r + 1) * SUB, :] = pre(x_ref[r * SUB:(r + 1) * SUB, :])
        assert pre_bwd is None or C == D
        for r in range(tm // SUB):
            rows = slice(r * SUB, (r + 1) * SUB)
            for c0 in range(0, C, D):
                cols = slice(c0, c0 + D)
                acc = jnp.zeros((SUB, D), F32)
                for k in range(K):
                    off = r * SUB + (K - 1) - k
                    acc = acc + w_ref[k:k + 1, cols] * extg[off:off + SUB, cols]
                if pre_bwd is None:
                    dx_ref[rows, cols] = acc
                    dbx_ref[:, cols] += jnp.sum(acc, axis=0, keepdims=True)
                else:
                    dx = pre_bwd(x_ref[rows, :], acc)
                    dx_ref[rows, :] = dx
                    dbx_ref[...] += jnp.sum(dx, axis=0, keepdims=True)
        for k in range(K):
            for c0 in range(0, C, D):
                cols = slice(c0, c0 + D)
                p = jnp.zeros((SUB, D), F32)
                for r in range(tm // SUB):
                    off = HB + r * SUB - (K - 1) + k
                    p = p + extg[r * SUB:(r + 1) * SUB, cols] * exta[off:off + SUB, cols]
                dw_ref[k:k + 1, cols] += jnp.sum(p, axis=0, keepdims=True)

    in_specs = [pl.BlockSpec((tm, C), lambda i: (i, 0)),
                pl.BlockSpec((HB, C), lambda i: (jnp.minimum((i + 1) * nb, nblk - 1), 0)),
                pl.BlockSpec((tm, xw), lambda i: (i, 0)),
                pl.BlockSpec((HB, xw), lambda i: (jnp.maximum(i * nb - 1, 0), 0)),
                pl.BlockSpec(w_pad.shape, lambda i: (0, 0))]
    out_specs = [pl.BlockSpec((tm, xw), lambda i: (i, 0)), pl.BlockSpec((HB, C), lambda i: (0, 0)),
                 pl.BlockSpec((1, xw), lambda i: (0, 0))]
    out_shape = [jax.ShapeDtypeStruct((T, xw), F32), jax.ShapeDtypeStruct((HB, C), F32), jax.ShapeDtypeStruct((1, xw), F32)]
    return pl.pallas_call(body, name=name, grid=(T // tm,), in_specs=in_specs, out_specs=out_specs, out_shape=out_shape,
                          scratch_shapes=[pltpu.VMEM((tm + HB, C), F32), pltpu.VMEM((HB + tm, C), F32)],
                          compiler_params=_cp("arbitrary"))(g, g, x, x, w_pad)


def _glu_bwd(u, da):
    u1, sg = u[:, :D], jax.nn.sigmoid(u[:, D:])
    return jnp.concatenate([da * sg, da * u1 * sg * (1.0 - sg)], axis=1)


def _head_cols(ref, h, base=0):
    return ref[:, base + h * HD:base + (h + 1) * HD]


def _prep_inputs(c_ref, ab, al, dt):
    hs = range(H)
    return ([_head_cols(c_ref, h) for h in hs], [_head_cols(c_ref, h, D) for h in hs],
            [_head_cols(c_ref, h, 2 * D) for h in hs], [_lane_pick(ab, h, HD) for h in hs],
            [_lane_pick(ab, H + h, HD) for h in hs], [_lane_pick(al, h, HD) for h in hs],
            [_lane_pick(dt, h, HD) for h in hs])


def gdn_prep_fwd(cpre, pab, alog, dtb):
    T = cpre.shape[0]
    nc = T // CH

    def body(c_ref, ab_ref, al_ref, dt_ref, u_ref, w_ref, qg_ref, kg_ref, qk_ref, t_ref, eg_ref):
        us, ws, qks, qgs, kgs, egs, ts = f_prep(*_prep_inputs(c_ref, ab_ref[...], al_ref[...], dt_ref[...]), None, _dot_raw)
        for h in range(H):
            cols = slice(h * HD, (h + 1) * HD)
            u_ref[:, cols] = us[h]
            w_ref[:, cols] = ws[h].astype(BF16)
            qg_ref[:, cols] = qgs[h].astype(BF16)
            kg_ref[:, cols] = kgs[h].astype(BF16)
            qk_ref[0, h] = qks[h].astype(BF16)
            t_ref[0, h] = ts[h].astype(BF16)
            eg_ref[0, h:h + 1, :] = egs[h]

    row = lambda w: pl.BlockSpec((CH, w), lambda n: (n, 0))
    par = pl.BlockSpec((1, HD), lambda n: (0, 0))
    mat = pl.BlockSpec((1, H, CH, CH), lambda n: (n, 0, 0, 0))
    return pl.pallas_call(
        body, name="gdn_prep_fwd", grid=(nc,), in_specs=[row(3 * D), row(HD), par, par],
        out_specs=[row(D), row(D), row(D), row(D), mat, mat, pl.BlockSpec((1, H, HD), lambda n: (n, 0, 0))],
        out_shape=[jax.ShapeDtypeStruct((T, D), F32)] + [jax.ShapeDtypeStruct((T, D), BF16)] * 3
        + [jax.ShapeDtypeStruct((nc, H, CH, CH), BF16)] * 2 + [jax.ShapeDtypeStruct((nc, H, HD), F32)],
        compiler_params=_cp("arbitrary"))(cpre, pab, alog, dtb)


def gdn_prep_bwd(cpre, pab, alog, dtb, tmat, du, dw, dqg, dkg, dqk, deg):
    T = cpre.shape[0]
    nc = T // CH

    def body(c_ref, ab_ref, al_ref, dt_ref, t_ref, du_ref, dw_ref, dqg_ref, dkg_ref, dqk_ref, deg_ref,
             dc_ref, dab_ref, dal_ref, ddt_ref):
        @pl.when(pl.program_id(0) == 0)
        def _():
            dal_ref[...] = jnp.zeros(dal_ref.shape, F32)
            ddt_ref[...] = jnp.zeros(ddt_ref.shape, F32)

        lane = lax.broadcasted_iota(jnp.int32, (1, HD), 1)
        dab = jnp.zeros((CH, HD), F32)
        dal = jnp.zeros((1, HD), F32)
        ddt = jnp.zeros((1, HD), F32)
        hs = range(H)
        t_st = [t_ref[0, h].astype(F32) for h in hs]

        def fwd(*args):
            return tuple(f_prep(*args, t_st, _dot_vjp)[:6])

        _, vjp = jax.vjp(fwd, *_prep_inputs(c_ref, ab_ref[...], al_ref[...], dt_ref[...]))
        dcqs, dcks, dcvs, dars, dbrs, dals, ddts = vjp((
            [_head_cols(du_ref, h) for h in hs], [_head_cols(dw_ref, h) for h in hs], [dqk_ref[0, h] for h in hs],
            [_head_cols(dqg_ref, h) for h in hs], [_head_cols(dkg_ref, h) for h in hs],
            [deg_ref[0, h:h + 1, :] for h in hs]))
        for h in hs:
            dc_ref[:, h * HD:(h + 1) * HD] = dcqs[h]
            dc_ref[:, D + h * HD:D + (h + 1) * HD] = dcks[h]
            dc_ref[:, 2 * D + h * HD:2 * D + (h + 1) * HD] = dcvs[h]
            dab = dab + jnp.where(lane == h, dars[h], 0.0) + jnp.where(lane == H + h, dbrs[h], 0.0)
            dal = dal + jnp.where(lane == h, dals[h], 0.0)
            ddt = ddt + jnp.where(lane == h, ddts[h], 0.0)
        dab_ref[...] = dab
        dal_ref[...] += dal
        ddt_ref[...] += ddt

    row = lambda w: pl.BlockSpec((CH, w), lambda n: (n, 0))
    par = pl.BlockSpec((1, HD), lambda n: (0, 0))
    mat = pl.BlockSpec((1, H, CH, CH), lambda n: (n, 0, 0, 0))
    vec = pl.BlockSpec((1, H, HD), lambda n: (n, 0, 0))
    return pl.pallas_call(
        body, name="gdn_prep_bwd", grid=(nc,),
        in_specs=[row(3 * D), row(HD), par, par, mat, row(D), row(D), row(D), row(D), mat, vec],
        out_specs=[row(3 * D), row(HD), par, par],
        out_shape=[jax.ShapeDtypeStruct((T, 3 * D), F32), jax.ShapeDtypeStruct((T, HD), F32),
                   jax.ShapeDtypeStruct((1, HD), F32), jax.ShapeDtypeStruct((1, HD), F32)],
        compiler_params=_cp("arbitrary"))(cpre, pab, alog, dtb, tmat, du, dw, dqg, dkg, dqk, deg)


def gdn_scan_fwd(u, w, qg, kg, qk, eg, S):
    T = u.shape[0]
    nc, per_seq = T // CH, S // CH

    def body(u_ref, w_ref, qg_ref, kg_ref, qk_ref, eg_ref, o_ref, sall_ref, s_ref):
        @pl.when(pl.program_id(0) % per_seq == 0)
        def _():
            s_ref[...] = jnp.zeros(s_ref.shape, F32)

        hs = range(H)
        ss = [s_ref[h] for h in hs]
        os_, s2s = f_scan(ss, [_head_cols(u_ref, h) for h in hs], [_head_cols(w_ref, h) for h in hs],
                          [_head_cols(qg_ref, h) for h in hs], [_head_cols(kg_ref, h) for h in hs],
                          [qk_ref[0, h] for h in hs], [eg_ref[0, h:h + 1, :] for h in hs], _dot_raw)
        for h in hs:
            sall_ref[0, h] = ss[h]
            o_ref[:, h * HD:(h + 1) * HD] = os_[h]
            s_ref[h] = s2s[h]

    row = pl.BlockSpec((CH, D), lambda n: (n, 0))
    return pl.pallas_call(
        body, name="gdn_scan_fwd", grid=(nc,),
        in_specs=[row, row, row, row, pl.BlockSpec((1, H, CH, CH), lambda n: (n, 0, 0, 0)),
                  pl.BlockSpec((1, H, HD), lambda n: (n, 0, 0))],
        out_specs=[row, pl.BlockSpec((1, H, HD, HD), lambda n: (n, 0, 0, 0))],
        out_shape=[jax.ShapeDtypeStruct((T, D), F32), jax.ShapeDtypeStruct((nc, H, HD, HD), F32)],
        scratch_shapes=[pltpu.VMEM((H, HD, HD), F32)], compiler_params=_cp("arbitrary"))(u, w, qg, kg, qk, eg)


def gdn_scan_bwd(do, u, w, qg, kg, qk, eg, sall, S):
    T = u.shape[0]
    nc, per_seq = T // CH, S // CH

    def body(do_ref, u_ref, w_ref, qg_ref, kg_ref, qk_ref, eg_ref, sall_ref,
             du_ref, dw_ref, dqg_ref, dkg_ref, dqk_ref, deg_ref, ds_ref):
        n = nc - 1 - pl.program_id(0)

        @pl.when(n % per_seq == per_seq - 1)
        def _():
            ds_ref[...] = jnp.zeros(ds_ref.shape, F32)

        hs = range(H)

        def fwd(*args):
            return f_scan(*args, _dot_vjp)

        _, vjp = jax.vjp(fwd, [sall_ref[0, h] for h in hs], [_head_cols(u_ref, h) for h in hs],
                         [_head_cols(w_ref, h).astype(F32) for h in hs], [_head_cols(qg_ref, h).astype(F32) for h in hs],
                         [_head_cols(kg_ref, h).astype(F32) for h in hs], [qk_ref[0, h].astype(F32) for h in hs],
                         [eg_ref[0, h:h + 1, :] for h in hs])
        dss, dus, dws, dqgs, dkgs, dqks, degs = vjp(([_head_cols(do_ref, h) for h in hs], [ds_ref[h] for h in hs]))
        for h in hs:
            cols = slice(h * HD, (h + 1) * HD)
            du_ref[:, cols] = dus[h]
            dw_ref[:, cols] = dws[h]
            dqg_ref[:, cols] = dqgs[h]
            dkg_ref[:, cols] = dkgs[h]
            dqk_ref[0, h] = dqks[h]
            deg_ref[0, h:h + 1, :] = degs[h]
            ds_ref[h] = dss[h]

    rev = lambda n: (nc - 1 - n, 0)
    row = pl.BlockSpec((CH, D), rev)
    mat = pl.BlockSpec((1, H, CH, CH), lambda n: (nc - 1 - n, 0, 0, 0))
    vec = pl.BlockSpec((1, H, HD), lambda n: (nc - 1 - n, 0, 0))
    return pl.pallas_call(
        body, name="gdn_scan_bwd", grid=(nc,),
        in_specs=[row, row, row, row, row, mat, vec, pl.BlockSpec((1, H, HD, HD), lambda n: (nc - 1 - n, 0, 0, 0))],
        out_specs=[row, row, row, row, mat, vec],
        out_shape=[jax.ShapeDtypeStruct((T, D), F32)] * 4
        + [jax.ShapeDtypeStruct((nc, H, CH, CH), F32), jax.ShapeDtypeStruct((nc, H, HD), F32)],
        scratch_shapes=[pltpu.VMEM((H, HD, HD), F32)], compiler_params=_cp("arbitrary"))(do, u, w, qg, kg, qk, eg, sall)


def xor_exchange(name, ins, inplace, out_shapes, plan, n_remote, n_local=0):
    n_in, n_ip, n_out = len(ins), len(inplace), len(out_shapes)

    def body(*refs):
        in_refs = refs[:n_in]
        ip_refs = refs[n_in + n_ip:n_in + 2 * n_ip]
        out_refs = refs[n_in + 2 * n_ip:n_in + 2 * n_ip + n_out]
        send_sems, recv_sems, loc_sems = refs[n_in + 2 * n_ip + n_out:]
        x, y, c = lax.axis_index("x"), lax.axis_index("y"), lax.axis_index("c")
        remote, local = plan(in_refs, ip_refs, out_refs, (x, y, c))
        assert len(remote) == n_remote and len(local) == n_local
        copies = []
        for k, ((dx, dy, dc), src, dst) in enumerate(remote):
            peer = (1 - x if dx else x, 1 - y if dy else y, 1 - c if dc else c)
            copies.append(pltpu.make_async_remote_copy(src_ref=src, dst_ref=dst, send_sem=send_sems.at[k],
                                                       recv_sem=recv_sems.at[k], device_id=peer, device_id_type=MESH))
        for cp in copies:
            cp.start()
        locs = [pltpu.make_async_copy(src, dst, loc_sems.at[k]) for k, (src, dst) in enumerate(local)]
        for cp in locs:
            cp.start()
        for cp in copies:
            cp.wait()
        for cp in locs:
            cp.wait()

    anyspec = pl.BlockSpec(memory_space=pl.ANY)
    res = pl.pallas_call(
        body, name=name, in_specs=[anyspec] * (n_in + n_ip), out_specs=[anyspec] * (n_ip + n_out),
        out_shape=[jax.ShapeDtypeStruct(a.shape, a.dtype) for a in inplace] + list(out_shapes),
        input_output_aliases={n_in + i: i for i in range(n_ip)},
        scratch_shapes=[pltpu.SemaphoreType.DMA((n_remote,)), pltpu.SemaphoreType.DMA((n_remote,)),
                        pltpu.SemaphoreType.DMA((max(n_local, 1),))],
        )(*ins, *inplace)
    return list(res[:n_ip]), list(res[n_ip:])


class WSpec:
    def __init__(self, name, full, sa, ha, group, layer=None, lead=False):
        self.name, self.full, self.sa, self.ha, self.group, self.layer, self.lead = name, full, sa, ha, group, layer, lead
        self.ws = 1 if lead else full[sa] // 4
        self.wh = full[ha] // 2

    def shard_shape(self):
        if self.lead:
            return tuple(n for a, n in enumerate(self.full) if a != self.sa)
        return tuple(self.ws if a == self.sa else n for a, n in enumerate(self.full))

    def half_full_shape(self):
        return tuple(self.wh if a == self.ha else n for a, n in enumerate(self.full))

    def shard_half_shape(self):
        s = list(self.half_full_shape())
        if self.lead:
            del s[self.sa]
        else:
            s[self.sa] = self.ws
        return tuple(s)

    def full_view(self, ref, q=None, h=None):
        idx = []
        for a in range(len(self.full)):
            if a == self.sa and q is not None:
                idx.append(q if self.lead else pl.ds(pl.multiple_of(q * self.ws, self.ws), self.ws))
            elif a == self.ha and h is not None:
                idx.append(pl.ds(pl.multiple_of(h * self.wh, self.wh), self.wh))
            else:
                idx.append(slice(None))
        return ref.at[tuple(idx)]

    def shard_view(self, ref, h):
        idx = [] if self.layer is None else [self.layer]
        for a in range(len(self.full)):
            if self.lead and a == self.sa:
                continue
            idx.append(pl.ds(pl.multiple_of(h * self.wh, self.wh), self.wh) if a == self.ha else slice(None))
        return ref.at[tuple(idx)]

    def rows_cols(self, shard, half):
        rows, cols = self.full[-2:]
        if shard and not self.lead:
            rows, cols = (rows // 4, cols) if self.sa == 0 else (rows, cols // 4)
        if half:
            rows, cols = (rows // 2, cols) if self.ha == len(self.full) - 2 else (rows, cols // 2)
        return rows, cols

    def spec(self, tr, cw, nr, shard=False, half=False, has_lead=False, stacked=False):
        two_d = len(self.full) == 2
        shard_on_cols = two_d and self.sa == 1
        half_on_cols = two_d and self.ha == 1
        layer = self.layer

        def index(*args):
            pref = args[-1]
            i = args[-2]
            r, cblk, pre = i, 0, ()
            if shard:
                if self.lead:
                    pre = (pref[0],)
                elif shard_on_cols:
                    cblk = pref[0]
                else:
                    r = pref[0] * nr + i
            elif has_lead:
                pre = (args[0],)
            if half:
                if half_on_cols:
                    cblk = pref[1]
                else:
                    r = pref[1] * nr + i
            if stacked:
                pre = (layer,) + pre
            return pre + (r, cblk)

        n_pre = int(stacked) + int(self.lead and (shard or has_lead))
        return pl.BlockSpec((None,) * n_pre + (tr, cw), index)


WSPECS = [
    WSpec("cv_w_pw1", (D, 2 * D), 1, 0, 0),
    WSpec("cv_w_pw2", (D, D), 0, 1, 1),
    WSpec("gdn_w_in", (4, D, (4 * D + 2 * H) // 4), 0, 1, 2, lead=True),
    WSpec("gdn_w_out", (D, D), 0, 1, 3),
    WSpec("mlp_w1_0", (D, DFF), 1, 0, 4, layer=0),
    WSpec("mlp_w1_1", (D, DFF), 1, 0, 4, layer=1),
    WSpec("mlp_w2_0", (DFF, D), 0, 1, 5, layer=0),
    WSpec("mlp_w2_1", (DFF, D), 0, 1, 5, layer=1),
]
FLIPS = [(1, 0, 0), (0, 1, 0), (1, 1, 0)]
SIB = (0, 0, 1)


def _chip(x, y):
    return 2 * x + y


def _prefetch_call(name, body, grid, in_specs, out_specs, out_shape, pref, args, aliases=None):
    return pl.pallas_call(
        body, name=name, out_shape=out_shape, input_output_aliases=aliases or {},
        grid_spec=pltpu.PrefetchScalarGridSpec(num_scalar_prefetch=1, grid=grid, in_specs=in_specs, out_specs=out_specs),
        compiler_params=_cp(*(("arbitrary",) * len(grid))))(pref, *args)


def place_shard(ws, shard, pref):
    rows, cols = ws.rows_cols(True, False)
    tr = _ew_rows(rows, cols)
    nr = rows // tr
    stacked = ws.layer is not None
    layer = ws.layer

    def body(_, s_ref, o_ref):
        o_ref[...] = s_ref[...].astype(BF16)

    in_spec = pl.BlockSpec(((None,) if stacked else ()) + (tr, cols),
                           (lambda i, p: (layer, i, 0)) if stacked else (lambda i, p: (i, 0)))
    return _prefetch_call("place_" + ws.name, body, (nr,), [in_spec], ws.spec(tr, cols, nr, shard=True),
                          jax.ShapeDtypeStruct(ws.full, BF16), pref, [shard])


def gather_weights(shards, wdw_shard, wcv_shard, pref):
    n = len(WSPECS)
    placed = [place_shard(ws, shards[ws.group], pref) for ws in WSPECS]

    def plan_a(in_refs, ip_refs, out_refs, pos):
        x, y, c = pos
        q = _chip(x, y)
        remote, local = [], []
        for i, ws in enumerate(WSPECS):
            mine = ws.full_view(ip_refs[i], q, c)
            for f in FLIPS:
                remote.append((f, mine, mine))
        for j, width in enumerate((D // 4, 3 * D // 4)):
            dst = out_refs[j].at[:, pl.ds(pl.multiple_of(q * width, 128), width)]
            local.append((in_refs[j], dst))
            for f in FLIPS:
                remote.append((f, in_refs[j], dst))
        return remote, local

    taps = [jax.ShapeDtypeStruct((KCV, D), F32), jax.ShapeDtypeStruct((KSC, 3 * D), F32)]
    nat, (wdw, wcv) = xor_exchange("gather_chips", [wdw_shard, wcv_shard], placed, taps, plan_a, 3 * (n + 2), 2)

    def plan_b(in_refs, ip_refs, out_refs, pos):
        x, y, c = pos
        remote = []
        for i, ws in enumerate(WSPECS):
            for (dx, dy, _) in FLIPS:
                got = ws.full_view(ip_refs[i], _chip(1 - x if dx else x, 1 - y if dy else y), c)
                remote.append((SIB, got, got))
        return remote, []

    nat, _ = xor_exchange("gather_cores", [], nat, [], plan_b, 3 * n)
    return nat, wdw, wcv


def half_add(ws, g, rsib, pref):
    rows, cols = ws.rows_cols(False, True)
    tr = _ew_rows(rows, cols)
    nr = rows // tr

    def body(_, a_ref, b_ref, o_ref):
        o_ref[...] = (a_ref[...] + b_ref[...]).astype(BF16)

    whole = ws.spec(tr, cols, nr, has_lead=ws.lead)
    return _prefetch_call("reduce_add_" + ws.name, body, (4, nr) if ws.lead else (nr,),
                          [ws.spec(tr, cols, nr, half=True, has_lead=ws.lead), whole], whole,
                          jax.ShapeDtypeStruct(ws.half_full_shape(), BF16), pref, [g, rsib])


def shard_sum(ws, s, parts, buf, pref):
    rows, cols = ws.rows_cols(True, True)
    tr = _ew_rows(rows, cols)
    nr = rows // tr
    stacked = ws.layer is not None

    def body(_, s_ref, p_ref, *rest):
        rest[-1][...] = ((s_ref[...].astype(F32) + p_ref[0].astype(F32)) + p_ref[1].astype(F32)) + p_ref[2].astype(F32)

    in_specs = [ws.spec(tr, cols, nr, shard=True, has_lead=ws.lead), pl.BlockSpec((3, tr, cols), lambda i, p: (0, i, 0))]
    args, aliases = [s, parts], {}
    if buf is not None:
        in_specs.append(pl.BlockSpec(memory_space=pl.ANY))
        args.append(buf)
        aliases = {3: 0}
    shape = ((2,) if stacked else ()) + ws.shard_shape()
    return _prefetch_call("reduce_sum_" + ws.name, body, (nr,), in_specs, ws.spec(tr, cols, nr, half=True, stacked=stacked),
                          jax.ShapeDtypeStruct(shape, F32), pref, args, aliases)


def reduce_grads(grads, pref):
    n = len(WSPECS)

    def plan1(in_refs, ip_refs, out_refs, pos):
        c = pos[2]
        return [(SIB, ws.full_view(in_refs[i], None, 1 - c), out_refs[i]) for i, ws in enumerate(WSPECS)], []

    halves = [jax.ShapeDtypeStruct(ws.half_full_shape(), F32) for ws in WSPECS]
    _, rsib = xor_exchange("reduce_cores", grads, [], halves, plan1, n)
    sums = [half_add(ws, grads[i], rsib[i], pref) for i, ws in enumerate(WSPECS)]

    def plan2(in_refs, ip_refs, out_refs, pos):
        x, y, c = pos
        remote = []
        for i, ws in enumerate(WSPECS):
            for s, (dx, dy, _) in enumerate(FLIPS):
                qq = _chip(1 - x if dx else x, 1 - y if dy else y)
                remote.append(((dx, dy, 0), ws.full_view(in_refs[i], qq), out_refs[i].at[s]))
        return remote, []

    parts = [jax.ShapeDtypeStruct((3,) + ws.shard_half_shape(), BF16) for ws in WSPECS]
    _, got = xor_exchange("reduce_chips", sums, [], parts, plan2, 3 * n)
    bufs = {}
    for i, ws in enumerate(WSPECS):
        bufs[ws.group] = shard_sum(ws, sums[i], got[i], bufs.get(ws.group), pref)

    def plan3(in_refs, ip_refs, out_refs, pos):
        c = pos[2]
        remote = []
        for ws in WSPECS:
            mine = ws.shard_view(ip_refs[ws.group], c)
            remote.append((SIB, mine, mine))
        return remote, []

    return xor_exchange("reduce_swap", [], [bufs[g] for g in sorted(bufs)], [], plan3, n)[0]


def gather_small(buf):
    flips = [(dx, dy, dc) for dx in (0, 1) for dy in (0, 1) for dc in (0, 1)][1:]

    def plan(in_refs, ip_refs, out_refs, pos):
        x, y, c = pos
        me = 4 * x + 2 * y + c
        dst = out_refs[0].at[me]
        return [(f, in_refs[0], dst) for f in flips], [(in_refs[0], dst)]

    return xor_exchange("gather_small", [buf], [], [jax.ShapeDtypeStruct((8,) + buf.shape, F32)], plan, 7, 1)[1][0]


def _pad_rows(a, rows):
    return jnp.pad(a, ((0, rows - a.shape[0]), (0, 0)))


def _row1(v):
    v = v.reshape((1, -1))
    return jnp.pad(v, ((0, 0), (0, D - v.shape[1])))


def _rms_fwd(name, h, g, tm):
    return row_call(name, lambda hh, gg: ((f_rms(hh, gg),), ()), [(h, D, 0)], [g], [(D, BF16)], [], tm)[0]


def _res_rms(h, g):
    return (h, f_rms(h, g)), ()


def _rms_bwd_epi(dhn, h, dres, g):
    _, vjp = jax.vjp(f_rms, h, g)
    dh, dg = vjp(dhn)
    dh = dh + dres
    return (dh,), (dg, jnp.sum(dh, axis=0, keepdims=True))


def _mlp_bwd(tag, dh, h, g, w1, w2, hn, z1):
    dz1 = mm("mlp_down_dx" + tag, dh, w2, "NT", BF16,
             epi=lambda acc, z: acc * (2.0 * jnp.maximum(z.astype(F32), 0.0)), epi_ins=[(z1, "tile")])
    dw2 = mm_tn("mlp_down_dw" + tag, z1, dh, a_fn=f_relu2)
    dh_in, dg, colsum = mm("mlp_up_dx" + tag, dz1, w1, "NT", [F32], epi=_rms_bwd_epi,
                           epi_ins=[(h, "tile"), (dh, "tile"), (g, "row")], accs=[(1, D), (1, D)])
    dw1 = mm_tn("mlp_up_dw" + tag, hn, dz1)
    return dh_in, dg, colsum, dw1, dw2


def kernel(x, norm_mix_g, norm_ffn_g, final_norm_g, cv_w_pw1, cv_b_pw1, cv_w_dw, cv_b_dw, cv_ln_g, cv_ln_b, cv_w_pw2, cv_b_pw2, gdn_w_in, gdn_conv_w, gdn_a_log, gdn_dt_bias, gdn_norm_g, gdn_w_out, mlp_w1, mlp_w2, loss_target, m_norm_mix_g, m_norm_ffn_g, m_final_norm_g, m_cv_w_pw1, m_cv_b_pw1, m_cv_w_dw, m_cv_b_dw, m_cv_ln_g, m_cv_ln_b, m_cv_w_pw2, m_cv_b_pw2, m_gdn_w_in, m_gdn_conv_w, m_gdn_a_log, m_gdn_dt_bias, m_gdn_norm_g, m_gdn_w_out, m_mlp_w1, m_mlp_w2, v_norm_mix_g, v_norm_ffn_g, v_final_norm_g, v_cv_w_pw1, v_cv_b_pw1, v_cv_w_dw, v_cv_b_dw, v_cv_ln_g, v_cv_ln_b, v_cv_w_pw2, v_cv_b_pw2, v_gdn_w_in, v_gdn_conv_w, v_gdn_a_log, v_gdn_dt_bias, v_gdn_norm_g, v_gdn_w_out, v_mlp_w1, v_mlp_w2):
    env = dict(locals())
    bl, S, _ = x.shape
    T = bl * S
    tm = min(256, S)
    xf = x.reshape((T, D))
    tgt = loss_target.reshape((T, D))

    chip = 2 * lax.axis_index("x") + lax.axis_index("y")
    pref = jnp.stack([chip, lax.axis_index("c")]).astype(jnp.int32)
    big = [cv_w_pw1[0], cv_w_pw2[0], gdn_w_in[0], gdn_w_out[0], mlp_w1, mlp_w2]
    nat, wdw, wcv = gather_weights(big, cv_w_dw[0], gdn_conv_w[0], pref)
    w_pw1, w_pw2, w_in_sm, w_out, w1_0, w1_1, w2_0, w2_1 = nat
    w_in = jnp.transpose(w_in_sm, (1, 0, 2)).reshape((D, 4 * D + 2 * H))
    w_qkv, w_z = w_in[:, :3 * D], w_in[:, 3 * D:4 * D]
    w_qkvz = w_in[:, :4 * D]
    w_ab = jnp.pad(w_in[:, 4 * D:], ((0, 0), (0, HD - 2 * H)))
    wdw_p, wcv_p = _pad_rows(wdw, HB_CV), _pad_rows(wcv, HB_SC)
    alog_p = jnp.pad(gdn_a_log, ((0, 0), (0, HD - H)))
    dtb_p = jnp.pad(gdn_dt_bias, ((0, 0), (0, HD - H)))
    g_mix0, g_mix1 = norm_mix_g[0:1], norm_mix_g[1:2]
    g_ffn0, g_ffn1 = norm_ffn_g[0:1], norm_ffn_g[1:2]
    g_fin = final_norm_g.reshape((1, D))

    hn0 = _rms_fwd("rms_mix0", xf, g_mix0, tm)
    u = mm("cv_pw1", hn0, w_pw1, "NN", F32, epi=lambda acc, b: acc + b, epi_ins=[(cv_b_pw1, "row")])
    dwc, s_act = dwconv_fwd("cv_dwconv", u, 2 * D, wdw_p, cv_b_dw, S, KCV, HB_CV, f_glu, (f_ln_silu, (cv_ln_g, cv_ln_b)), tm)
    h1, hnf0 = mm("cv_pw2", s_act, w_pw2, "NN", [F32, BF16], epi=lambda acc, b, r, g: _res_rms(acc + b + r, g),
                  epi_ins=[(cv_b_pw2, "row"), (xf, "tile"), (g_ffn0, "row")])
    z1_0 = mm("mlp_up0", hnf0, w1_0, "NN", BF16)
    h2, hn2 = mm("mlp_down0", z1_0, w2_0, "NN", [F32, BF16], a_fn=f_relu2, epi=lambda acc, r, g: _res_rms(acc + r, g),
                 epi_ins=[(h1, "tile"), (g_mix1, "row")])

    pqkvz = mm("gdn_in", hn2, w_qkvz, "NN", F32)
    pab = mm("gdn_in_ab", hn2, w_ab, "NN", F32)
    cpre = dwconv_fwd("gdn_conv", pqkvz, 3 * D, wcv_p, None, S, KSC, HB_SC, lambda v: v, None, tm)[0]
    gu, gw, gqg, gkg, gqk, gt, geg = gdn_prep_fwd(cpre, pab, alog_p, dtb_p)
    o, sall = gdn_scan_fwd(gu, gw, gqg, gkg, gqk, geg, S)
    on = row_call("gdn_post", lambda oo, zz, ng: ((f_post(oo, zz, ng),), ()), [(o, D, 0), (pqkvz, D, 3)],
                  [gdn_norm_g], [(D, BF16)], [], tm)[0]
    h3, hnf1 = mm("gdn_out", on, w_out, "NN", [F32, BF16], epi=lambda acc, r, g: _res_rms(acc + r, g),
                  epi_ins=[(h2, "tile"), (g_ffn1, "row")])
    z1_1 = mm("mlp_up1", hnf1, w1_1, "NN", BF16)

    def head(acc, res, tt, gg):
        def loss_of(h_, g_):
            return 0.5 * jnp.sum(jnp.mean(jnp.square(f_rms(h_, g_) - tt), axis=-1))
        lv, (dh_, dg_) = jax.value_and_grad(loss_of, (0, 1))(acc + res, gg)
        return (dh_,), (dg_, jnp.full((1, D), lv, F32))

    dh4, dg_fin, loss_row = mm("mlp_down1", z1_1, w2_1, "NN", [F32], a_fn=f_relu2, epi=head,
                               epi_ins=[(h3, "tile"), (tgt, "tile"), (g_fin, "row")], accs=[(1, D), (1, D)])

    dh3, dg_ffn1, _, dw1_1, dw2_1 = _mlp_bwd("1", dh4, h3, g_ffn1, w1_1, w2_1, hnf1, z1_1)
    dw_out = mm_tn("gdn_out_dw", on, dh3)

    def post_bwd(don, oo, zz, ng):
        _, vjp = jax.vjp(f_post, oo, zz, ng)
        do_, dz_, dng_ = vjp(don)
        return (do_, dz_), (dng_,)

    do, dz, dng = mm("gdn_out_dx", dh3, w_out, "NT", [F32, F32], epi=post_bwd,
                     epi_ins=[(o, "tile"), (pqkvz, ("cols", 3)), (gdn_norm_g, "whole")], accs=[(1, HD)])
    du, dw, dqg, dkg, dqk, deg = gdn_scan_bwd(do, gu, gw, gqg, gkg, gqk, geg, sall, S)
    dcpre, dpab, dalog, ddtb = gdn_prep_bwd(cpre, pab, alog_p, dtb_p, gt, du, dw, dqg, dkg, dqk, deg)
    dqkv, dwcv, _ = dwconv_bwd("gdn_conv_bwd", dcpre, pqkvz, 3 * D, wcv_p, S, KSC, HB_SC, lambda v: v, None, tm)
    dhn2 = mm("gdn_in_dx_ab", dpab, w_ab, "NT", F32)
    dhn2 = mm("gdn_in_dx_z", dz, w_z, "NT", F32, epi=lambda acc, r: acc + r, epi_ins=[(dhn2, "tile")])
    dh2, dg_mix1, _ = mm("gdn_in_dx_qkv", dqkv, w_qkv, "NT", [F32],
                         epi=lambda acc, prev, hh, rr, gg: _rms_bwd_epi(acc + prev, hh, rr, gg),
                         epi_ins=[(dhn2, "tile"), (h2, "tile"), (dh3, "tile"), (g_mix1, "row")], accs=[(1, D), (1, D)])
    dw_in = jnp.concatenate([mm_tn("gdn_in_dw_qkv", hn2, dqkv), mm_tn("gdn_in_dw_z", hn2, dz),
                             mm_tn("gdn_in_dw_ab", hn2, dpab)[:, :2 * H]], axis=1)

    dh1, dg_ffn0, db_pw2, dw1_0, dw2_0 = _mlp_bwd("0", dh2, h1, g_ffn0, w1_0, w2_0, hnf0, z1_0)
    dw_pw2 = mm_tn("cv_pw2_dw", s_act, dh1)

    def ln_bwd(ds, xx, gg, bb):
        _, vjp = jax.vjp(f_ln_silu, xx, gg, bb)
        dx_, dg_, db_ = vjp(ds)
        return (dx_,), (dg_, db_, jnp.sum(dx_, axis=0, keepdims=True))

    ddw, dln_g, dln_b, db_dw = mm("cv_pw2_dx", dh1, w_pw2, "NT", [F32], epi=ln_bwd,
                                  epi_ins=[(dwc, "tile"), (cv_ln_g, "row"), (cv_ln_b, "row")], accs=[(1, D)] * 3)
    du_cv, dwdw, db_pw1 = dwconv_bwd("cv_dwconv_bwd", ddw, u, 2 * D, wdw_p, S, KCV, HB_CV, f_glu, _glu_bwd, tm)
    dw_pw1 = mm_tn("cv_pw1_dw", hn0, du_cv)
    grad_x, dg_mix0, _ = mm("cv_pw1_dx", du_cv, w_pw1, "NT", [F32], epi=_rms_bwd_epi,
                            epi_ins=[(xf, "tile"), (dh1, "tile"), (g_mix0, "row")], accs=[(1, D), (1, D)])

    dw_in_sm = jnp.transpose(dw_in.reshape((D, 4, D + 4)), (1, 0, 2))
    g_pw1, g_pw2, g_in, g_out, g_w1, g_w2 = reduce_grads([dw_pw1, dw_pw2, dw_in_sm, dw_out, dw1_0, dw1_1, dw2_0, dw2_1], pref)

    small = jnp.concatenate([
        dg_mix0, dg_mix1, dg_ffn0, dg_ffn1, dg_fin, db_pw1.reshape((2, D)), db_dw, dln_g, dln_b, db_pw2,
        _row1(dalog[:, :H]), _row1(ddtb[:, :H]), _row1(dng), loss_row, jnp.zeros((1, D), F32),
        dwdw, dwcv[:KSC].reshape((3 * KSC, D)), jnp.zeros((NSMALL - 48 - 3 * KSC, D), F32)], axis=0)
    small_all = gather_small(small)

    def pack(a, b, c_, d, e, f, g_, h_, i_, j_, k_):
        return jnp.concatenate([a, b, c_.reshape((1, D)), d.reshape((2, D)), e, f, g_, h_, _row1(i_), _row1(j_), _row1(k_),
                                jnp.zeros((2, D), F32)], axis=0)

    order = lambda p: (p + "norm_mix_g", p + "norm_ffn_g", p + "final_norm_g", p + "cv_b_pw1", p + "cv_b_dw", p + "cv_ln_g",
                       p + "cv_ln_b", p + "cv_b_pw2", p + "gdn_a_log", p + "gdn_dt_bias", p + "gdn_norm_g")
    w16, m16, v16 = (pack(*[env[nm] for nm in order(p)]) for p in ("", "m_", "v_"))

    def small_step(ga, ww, mm_, vv):
        gsum = ga[0]
        for dev in range(1, 8):
            gsum = gsum + ga[dev]
        delta, m2, v2 = f_adamw(ww, gsum[:16], mm_, vv)
        return gsum, delta, m2, v2

    def small_body(ga_ref, w_ref, m_ref, v_ref, g_out, d_out, m_out, v_out):
        gsum, delta, m2, v2 = small_step(ga_ref[...], w_ref[...], m_ref[...], v_ref[...])
        g_out[...] = gsum
        d_out[...] = delta
        m_out[...] = m2
        v_out[...] = v2

    vm = pl.BlockSpec(memory_space=pltpu.VMEM)
    sg, sd, sm, sv = pl.pallas_call(
        small_body, name="adamw_small", in_specs=[vm] * 4, out_specs=[vm] * 4,
        out_shape=[jax.ShapeDtypeStruct((NSMALL, D), F32)] + [jax.ShapeDtypeStruct((16, D), F32)] * 3)(small_all, w16, m16, v16)

    def unpack(b):
        return (b[0:2], b[2:4], b[4], b[5:7].reshape((1, 2 * D)), b[7:8], b[8:9], b[9:10], b[10:11],
                b[11:12, :H], b[12:13, :H], b[13:14, :HD])

    loss = sg[14, 0]
    g_dw = lax.dynamic_slice(sg[16:16 + KCV], (0, chip * (D // 4)), (KCV, D // 4))
    g_cv = lax.dynamic_slice(sg[48:48 + 3 * KSC].reshape((KSC, 3 * D)), (0, chip * (3 * D // 4)), (KSC, 3 * D // 4))

    def adamw(name, w, g, m, v):
        lead = w.shape[:-2]
        if len(lead) == 1 and lead[0] == 1:
            d, m2, v2 = ew_call(name, f_adamw, [w[0], g.reshape(w.shape[1:]), m[0], v[0]], 3)
            return g.reshape(w.shape), d[None], m2[None], v2[None]
        return (g.reshape(w.shape),) + tuple(ew_call(name, f_adamw, [w, g.reshape(w.shape), m, v], 3))

    res = {
        "cv_w_pw1": adamw("adamw_pw1", cv_w_pw1, g_pw1, m_cv_w_pw1, v_cv_w_pw1),
        "cv_w_dw": adamw("adamw_dw", cv_w_dw, g_dw, m_cv_w_dw, v_cv_w_dw),
        "cv_w_pw2": adamw("adamw_pw2", cv_w_pw2, g_pw2, m_cv_w_pw2, v_cv_w_pw2),
        "gdn_w_in": adamw("adamw_win", gdn_w_in, g_in, m_gdn_w_in, v_gdn_w_in),
        "gdn_conv_w": adamw("adamw_cvw", gdn_conv_w, g_cv, m_gdn_conv_w, v_gdn_conv_w),
        "gdn_w_out": adamw("adamw_wout", gdn_w_out, g_out, m_gdn_w_out, v_gdn_w_out),
        "mlp_w1": adamw("adamw_w1", mlp_w1, g_w1, m_mlp_w1, v_mlp_w1),
        "mlp_w2": adamw("adamw_w2", mlp_w2, g_w2, m_mlp_w2, v_mlp_w2),
    }
    names = ("norm_mix_g", "norm_ffn_g", "final_norm_g", "cv_b_pw1", "cv_b_dw", "cv_ln_g", "cv_ln_b", "cv_b_pw2",
             "gdn_a_log", "gdn_dt_bias", "gdn_norm_g")
    for nm, gg, dd, mm_, vv in zip(names, unpack(sg), unpack(sd), unpack(sm), unpack(sv)):
        res[nm] = (gg, dd, mm_, vv)
    weights = ("norm_mix_g", "norm_ffn_g", "final_norm_g", "cv_w_pw1", "cv_b_pw1", "cv_w_dw", "cv_b_dw", "cv_ln_g",
               "cv_ln_b", "cv_w_pw2", "cv_b_pw2", "gdn_w_in", "gdn_conv_w", "gdn_a_log", "gdn_dt_bias", "gdn_norm_g",
               "gdn_w_out", "mlp_w1", "mlp_w2")
    outs = [loss, grad_x.reshape(x.shape)]
    for kind in range(4):
        outs += [res[nm][kind] for nm in weights]
    return tuple(outs)
```

```python
import functools

import jax
import jax.numpy as jnp
from jax import lax
from jax.experimental import pallas as pl
from jax.experimental.pallas import tpu as pltpu

F32, BF16 = jnp.float32, jnp.bfloat16
D = 1024
H = 8
HD = 128
CH = 64
DFF = 4 * D
KCV, HB_CV = 31, 32
KSC, HB_SC = 4, 8
EPS = 1e-6
LR, B1, B2, EPS_A, WD, STEP = 0.001, 0.9, 0.999, 1e-08, 0.01, 10
VMEM_LIMIT = 56 * 1024 * 1024
SUB = 32
NSMALL = 64
MESH = pl.DeviceIdType.MESH


def _cp(*sem):
    return pltpu.CompilerParams(dimension_semantics=sem, vmem_limit_bytes=VMEM_LIMIT)


def f_rms(h, g):
    return h * lax.rsqrt(jnp.mean(h * h, axis=-1, keepdims=True) + EPS) * g


def f_silu(x):
    return x * jax.nn.sigmoid(x)


def f_glu(u):
    return u[:, :D] * jax.nn.sigmoid(u[:, D:])


def f_ln_silu(x, g, b):
    mu = jnp.mean(x, axis=-1, keepdims=True)
    xc = x - mu
    y = xc * lax.rsqrt(jnp.mean(xc * xc, axis=-1, keepdims=True) + EPS)
    return f_silu(y * g + b)


def f_relu2(z):
    r = jnp.maximum(z.astype(F32), 0.0)
    return r * r


def f_post(o, z, ng):
    outs = []
    for h in range(H):
        oh = o[:, h * HD:(h + 1) * HD]
        y = oh * lax.rsqrt(jnp.mean(oh * oh, axis=-1, keepdims=True) + EPS) * ng
        outs.append(y * f_silu(z[:, h * HD:(h + 1) * HD]))
    return jnp.concatenate(outs, axis=1)


def f_adamw(w, g, m, v):
    m2 = B1 * m + (1.0 - B1) * g
    v2 = B2 * v + (1.0 - B2) * (g * g)
    m_hat = m2 / (1.0 - B1 ** STEP)
    v_hat = v2 / (1.0 - B2 ** STEP)
    delta = -LR * (m_hat / (jnp.sqrt(v_hat) + EPS_A) + WD * w)
    return delta, m2, v2


def _dot_raw(a, b, mode):
    dims = {"NN": ((1,), (0,)), "NT": ((1,), (1,)), "TN": ((0,), (0,))}[mode]
    return lax.dot_general(a.astype(BF16), b.astype(BF16), (dims, ((), ())), preferred_element_type=F32)


@functools.partial(jax.custom_vjp, nondiff_argnums=(2,))
def _dot_vjp(a, b, mode):
    return _dot_raw(a, b, mode)


def _dot_fwd(a, b, mode):
    return _dot_raw(a, b, mode), (a, b)


def _dot_bwd(mode, res, dc):
    a, b = res
    if mode == "NN":
        return _dot_vjp(dc, b, "NT"), _dot_vjp(a, dc, "TN")
    if mode == "NT":
        return _dot_vjp(dc, b, "NN"), _dot_vjp(dc, a, "TN")
    return _dot_vjp(b, dc, "NT"), _dot_vjp(a, dc, "NN")


_dot_vjp.defvjp(_dot_fwd, _dot_bwd)


def _split(x):
    xh = x.astype(BF16)
    return xh, (x - xh.astype(F32)).astype(BF16)


def _dot_split(xs, ys):
    (xh, xl), (yh, yl) = xs, ys
    return _dot_raw(xh, yh, "NN") + (_dot_raw(xh, yl, "NN") + _dot_raw(xl, yh, "NN"))


def _tril_inverse(a_list):
    ri = lax.broadcasted_iota(jnp.int32, (CH, CH), 0)
    ci = lax.broadcasted_iota(jnp.int32, (CH, CH), 1)
    eye = (ri == ci).astype(F32)
    ts = None
    for lvl in range(CH.bit_length() - 1):
        same_pair = jnp.right_shift(ri, lvl + 1) == jnp.right_shift(ci, lvl + 1)
        quarter = (jnp.bitwise_and(jnp.right_shift(ri, lvl), 1) == 1) & (jnp.bitwise_and(jnp.right_shift(ci, lvl), 1) == 0)
        offs = [jnp.where(same_pair & quarter, a, 0.0) for a in a_list]
        if ts is None:
            ts = [eye - off for off in offs]
            continue
        tsp = [_split(t) for t in ts]
        mids = [_dot_split(tp, _split(off)) for tp, off in zip(tsp, offs)]
        ts = [t - _dot_split(_split(m), tp) for t, m, tp in zip(ts, mids, tsp)]
    return ts


@jax.custom_vjp
def _stored_solve(a, t, rhs):
    return _dot_raw(t, rhs, "NN")


def _stored_solve_fwd(a, t, rhs):
    sol = _dot_raw(t, rhs, "NN")
    return sol, (t, sol)


def _stored_solve_bwd(res, g):
    t, sol = res
    g_rhs = _dot_vjp(t, g, "TN")
    return -_dot_vjp(g_rhs, sol, "NT"), jnp.zeros_like(t), g_rhs


_stored_solve.defvjp(_stored_solve_fwd, _stored_solve_bwd)


def _lane_pick(row, idx, width):
    sel = lax.broadcasted_iota(jnp.int32, (1, width), 1) == idx
    return jnp.sum(jnp.where(sel, row, 0.0), axis=1, keepdims=True)


def f_prep(cqs, cks, cvs, araws, braws, alogs, dtbs, t_stored, dot):
    ri = lax.broadcasted_iota(jnp.int32, (CH, CH), 0)
    ci = lax.broadcasted_iota(jnp.int32, (CH, CH), 1)
    eye = (ri == ci).astype(F32)
    low = (ri >= ci).astype(F32)
    last = lax.broadcasted_iota(jnp.int32, (CH, 1), 0) == CH - 1
    nh = range(len(cqs))
    qs, ks, vbs, kbs, gcs, decays = [], [], [], [], [], []
    for h in nh:
        q = f_silu(cqs[h])
        qs.append(q * lax.rsqrt(jnp.sum(q * q, axis=-1, keepdims=True) + 1e-6) * (HD ** -0.5))
        k = f_silu(cks[h])
        k = k * lax.rsqrt(jnp.sum(k * k, axis=-1, keepdims=True) + 1e-6)
        ks.append(k)
        beta = jax.nn.sigmoid(braws[h])
        sp_in = araws[h] + dtbs[h]
        softplus = jnp.maximum(sp_in, 0.0) + jnp.log(1.0 + jnp.exp(-jnp.abs(sp_in)))
        g = -jnp.exp(alogs[h]) * softplus
        g_row = jnp.sum(eye * g, axis=0, keepdims=True)
        gc = jnp.sum(low * g_row, axis=1, keepdims=True)
        gc_row = jnp.sum(eye * gc, axis=0, keepdims=True)
        gcs.append(gc)
        decays.append(jnp.exp(jnp.where(ri >= ci, gc - gc_row, -1e30)))
        vbs.append(f_silu(cvs[h]) * beta)
        kbs.append(k * beta)
    kks = [dot(kbs[h], ks[h], "NT") for h in nh]
    a_list = [jnp.where(ri > ci, kks[h] * decays[h], 0.0) for h in nh]
    if t_stored is None:
        ts = _tril_inverse(a_list)
        solve = lambda h, rhs: dot(ts[h], rhs, "NN")
    else:
        ts = t_stored
        solve = lambda h, rhs: _stored_solve(a_list[h], t_stored[h], rhs)
    egcs = [jnp.exp(gc) for gc in gcs]
    us = [solve(h, vbs[h]) for h in nh]
    ws = [solve(h, kbs[h] * egcs[h]) for h in nh]
    qks = [dot(qs[h], ks[h], "NT") * decays[h] for h in nh]
    qgs = [qs[h] * egcs[h] for h in nh]
    gls = [jnp.sum(jnp.where(last, gc, 0.0), axis=0, keepdims=True) for gc in gcs]
    kgs = [ks[h] * jnp.exp(gls[h] - gcs[h]) for h in nh]
    egs = [jnp.exp(gl) * jnp.ones((1, HD), F32) for gl in gls]
    return us, ws, qks, qgs, kgs, egs, ts


def f_scan(ss, us, ws, qgs, kgs, qks, egs, dot):
    nh = range(len(ss))
    ws_s = [dot(ws[h], ss[h], "NN") for h in nh]
    qs_s = [dot(qgs[h], ss[h], "NN") for h in nh]
    vns = [us[h] - ws_s[h] for h in nh]
    os_ = [qs_s[h] + dot(qks[h], vns[h], "NN") for h in nh]
    s2s = [ss[h] * egs[h] + dot(kgs[h], vns[h], "TN") for h in nh]
    return os_, s2s


def row_call(name, fn, rows, pars, out_rows, out_accs, tm):
    T = rows[0][0].shape[0]
    n_r, n_p, n_o = len(rows), len(pars), len(out_rows)
    in_specs = [pl.BlockSpec((tm, w), functools.partial(lambda i, cb: (i, cb), cb=cb)) for (_, w, cb) in rows]
    in_specs += [pl.BlockSpec(p.shape, functools.partial(lambda i, nd: (0,) * nd, nd=p.ndim)) for p in pars]
    out_specs = [pl.BlockSpec((tm, w), lambda i: (i, 0)) for (w, _) in out_rows]
    out_specs += [pl.BlockSpec(s, lambda i: (0, 0)) for s in out_accs]
    out_shape = [jax.ShapeDtypeStruct((T, w), dt) for (w, dt) in out_rows]
    out_shape += [jax.ShapeDtypeStruct(s, F32) for s in out_accs]

    def body(*refs):
        rin, pin = refs[:n_r], refs[n_r:n_r + n_p]
        rout, aout = refs[n_r + n_p:n_r + n_p + n_o], refs[n_r + n_p + n_o:]
        if aout:
            @pl.when(pl.program_id(0) == 0)
            def _():
                for a in aout:
                    a[...] = jnp.zeros(a.shape, F32)
        pv = [p[...] for p in pin]

        def step(r, carry):
            sl = pl.ds(pl.multiple_of(r * SUB, SUB), SUB)
            outs, accs = fn(*[x[sl, :] for x in rin], *pv)
            for o, val in zip(rout, outs):
                o[sl, :] = val.astype(o.dtype)
            for a, val in zip(aout, accs):
                a[...] += val
            return carry

        lax.fori_loop(0, tm // SUB, step, 0)

    return pl.pallas_call(body, name=name, grid=(T // tm,), in_specs=in_specs, out_specs=out_specs,
                          out_shape=out_shape, compiler_params=_cp("arbitrary"))(*[r[0] for r in rows], *pars)


EW_TILE_ELEMS = 256 * 1024


def _ew_rows(R, Cc):
    if R * Cc <= EW_TILE_ELEMS or R % 8:
        return R
    tr = 8
    while tr * 2 * Cc <= EW_TILE_ELEMS and R % (tr * 2) == 0:
        tr *= 2
    return tr


def ew_call(name, fn, ins, n_out):
    shape = ins[0].shape
    lead = shape[:-2]
    R, Cc = shape[-2:]
    tr = _ew_rows(R, Cc)
    grid = lead + (R // tr,)
    nl = len(lead)
    spec = pl.BlockSpec((None,) * nl + (tr, Cc), lambda *idx: idx + (0,))

    def body(*refs):
        outs = fn(*[r[...] for r in refs[:len(ins)]])
        for o, val in zip(refs[len(ins):], outs):
            o[...] = val

    return pl.pallas_call(body, name=name, grid=grid, in_specs=[spec] * len(ins), out_specs=[spec] * n_out,
                          out_shape=[jax.ShapeDtypeStruct(shape, F32)] * n_out,
                          compiler_params=_cp(*(("arbitrary",) * len(grid))))(*ins)


MM_RESIDENT_BYTES = 8 * 1024 * 1024


def mm(name, a, b, mode, out_dtype, a_fn=None, epi=None, epi_ins=(), accs=(), tm=512):
    sub_epi = epi is not None and isinstance(out_dtype, (list, tuple))
    M, K = a.shape
    N = b.shape[1] if mode == "NN" else b.shape[0]
    tn = N if K * N * 2 <= MM_RESIDENT_BYTES else min(N, 1024)
    tm = min(tm if tn <= 1024 else tm // 2, M)
    multi = isinstance(out_dtype, (list, tuple))
    dts = list(out_dtype) if multi else [out_dtype]
    n_e, n_o = len(epi_ins), len(dts)
    in_specs = [pl.BlockSpec((tm, K), lambda j, i: (i, 0)),
                pl.BlockSpec((K, tn), lambda j, i: (0, j)) if mode == "NN" else pl.BlockSpec((tn, K), lambda j, i: (j, 0))]
    row_kinds = []
    for (arr, kind) in epi_ins:
        if kind == "tile" or isinstance(kind, tuple):
            cb = kind[1] if isinstance(kind, tuple) else 0
            in_specs.append(pl.BlockSpec((tm, tn), functools.partial(lambda j, i, cb: (i, cb + j), cb=cb)))
            row_kinds.append(True)
        elif kind == "row":
            in_specs.append(pl.BlockSpec((1, tn), lambda j, i: (0, j)))
            row_kinds.append(False)
        else:
            in_specs.append(pl.BlockSpec(arr.shape, lambda j, i: (0, 0)))
            row_kinds.append(False)

    def body(a_ref, b_ref, *rest):
        e_refs, o_refs, acc_refs = rest[:n_e], rest[n_e:n_e + n_o], rest[n_e + n_o:n_e + n_o + len(accs)]
        av = a_ref[...]
        if a_fn is not None:
            av = a_fn(av)
        res = _dot_raw(av, b_ref[...], mode)
        if epi is None or not sub_epi:
            if epi is not None:
                res = epi(res, *[r[...] for r in e_refs])
            o_refs[0][...] = res.astype(o_refs[0].dtype)
            return
        prod = rest[-1]
        prod[...] = res
        if acc_refs:
            @pl.when((pl.program_id(0) == 0) & (pl.program_id(1) == 0))
            def _():
                for r in acc_refs:
                    r[...] = jnp.zeros(r.shape, F32)
        small = [None if is_rows else r[...] for r, is_rows in zip(e_refs, row_kinds)]

        def step(k, carry):
            sl = pl.ds(pl.multiple_of(k * SUB, SUB), SUB)
            out = epi(prod[sl, :], *[r[sl, :] if is_rows else sm for r, is_rows, sm in zip(e_refs, row_kinds, small)])
            tiles, contribs = out if multi else ((out,), ())
            for r, t in zip(o_refs, tiles):
                r[sl, :] = t.astype(r.dtype)
            for r, t in zip(acc_refs, contribs):
                r[...] += t
            return carry

        lax.fori_loop(0, tm // SUB, step, 0)

    out_specs = [pl.BlockSpec((tm, tn), lambda j, i: (i, j))] * n_o + [pl.BlockSpec(s, lambda j, i: (0, 0)) for s in accs]
    out_shape = [jax.ShapeDtypeStruct((M, N), dt) for dt in dts] + [jax.ShapeDtypeStruct(s, F32) for s in accs]
    res = pl.pallas_call(body, name=name, grid=(N // tn, M // tm), in_specs=in_specs, out_specs=out_specs,
                         out_shape=out_shape, scratch_shapes=[pltpu.VMEM((tm, tn), F32)] if sub_epi else [],
                         compiler_params=_cp("arbitrary", "arbitrary"))(a, b, *[e[0] for e in epi_ins])
    return res if multi else res[0]


def mm_tn(name, a, g, a_fn=None, a_cols=None, tt=512):
    T = a.shape[0]
    ka, acb = (a.shape[1], 0) if a_cols is None else a_cols
    N = g.shape[1]
    tt = min(tt, T)
    tka, tn = min(ka, 1024), min(N, 1024)
    nkb = ka // tka

    def body(a_ref, g_ref, o_ref):
        @pl.when(pl.program_id(2) == 0)
        def _():
            o_ref[...] = jnp.zeros(o_ref.shape, F32)
        av = a_ref[...]
        if a_fn is not None:
            av = a_fn(av)
        o_ref[...] += _dot_raw(av, g_ref[...], "TN")

    return pl.pallas_call(body, name=name, grid=(nkb, N // tn, T // tt),
                          in_specs=[pl.BlockSpec((tt, tka), lambda ia, j, t: (t, acb * nkb + ia)),
                                    pl.BlockSpec((tt, tn), lambda ia, j, t: (t, j))],
                          out_specs=pl.BlockSpec((tka, tn), lambda ia, j, t: (ia, j)),
                          out_shape=jax.ShapeDtypeStruct((ka, N), F32),
                          compiler_params=_cp("arbitrary", "arbitrary", "arbitrary"))(a, g)


SUBLANES = 8


class _RowShifts:
    def __init__(self, src, shifted, nrows, reuse):
        self.src, self.shifted, self.reuse = src, shifted, reuse
        if reuse:
            for ph in range(1, SUBLANES):
                for r0 in range(0, nrows - SUBLANES, SUB):
                    n = min(SUB, nrows - SUBLANES - r0)
                    shifted[ph - 1, r0:r0 + n, :] = src[r0 + ph:r0 + ph + n, :]

    def window(self, off, cols):
        ph = off % SUBLANES
        if not self.reuse or ph == 0:
            return self.src[off:off + SUB, cols]
        return self.shifted[ph - 1, off - ph:off - ph + SUB, cols]


def _shift_scratch(nrows, C, reuse):
    return [pltpu.VMEM((SUBLANES - 1, nrows - SUBLANES, C), F32)] if reuse else []


def dwconv_fwd(name, x, xw, w_pad, bias, S, K, HB, pre, post, tm):
    T = x.shape[0]
    C = w_pad.shape[1]
    nb, per_seq = tm // HB, S // tm
    has_b, has_post = bias is not None, post is not None
    reuse = K > SUBLANES

    def body(*refs):
        x_ref, xp_ref, w_ref = refs[:3]
        pos = 3
        b_ref = refs[pos] if has_b else None
        pos += has_b
        ppars = refs[pos:pos + (len(post[1]) if has_post else 0)]
        pos += len(ppars)
        c_ref = refs[pos]
        s_ref = refs[pos + 1] if has_post else None
        ext = refs[pos + 1 + has_post]
        first = (pl.program_id(0) % per_seq) == 0
        ext[0:HB, :] = jnp.where(first, 0.0, pre(xp_ref[...]))
        for r in range(tm // SUB):
            ext[HB + r * SUB:HB + (r + 1) * SUB, :] = pre(x_ref[r * SUB:(r + 1) * SUB, :])
        rows_of = _RowShifts(ext, refs[-1] if reuse else None, HB + tm, reuse)
        pv = [p[...] for p in ppars]
        assert not has_post or C == D
        for r in range(tm // SUB):
            for c0 in range(0, C, D):
                cols = slice(c0, c0 + D)
                acc = jnp.zeros((SUB, D), F32)
                if has_b:
                    acc = acc + b_ref[:, cols]
                for k in range(K):
                    acc = acc + w_ref[k:k + 1, cols] * rows_of.window(HB + r * SUB - (K - 1) + k, cols)
                c_ref[r * SUB:(r + 1) * SUB, cols] = acc
                if has_post:
                    s_ref[r * SUB:(r + 1) * SUB, :] = post[0](acc, *pv).astype(BF16)

    ins = [x, x, w_pad] + ([bias] if has_b else []) + (list(post[1]) if has_post else [])
    in_specs = [pl.BlockSpec((tm, xw), lambda i: (i, 0)),
                pl.BlockSpec((HB, xw), lambda i: (jnp.maximum(i * nb - 1, 0), 0)),
                pl.BlockSpec(w_pad.shape, lambda i: (0, 0))]
    in_specs += [pl.BlockSpec(p.shape, lambda i: (0, 0)) for p in ins[3:]]
    out_specs = [pl.BlockSpec((tm, C), lambda i: (i, 0))] * (1 + has_post)
    out_shape = [jax.ShapeDtypeStruct((T, C), F32)] + ([jax.ShapeDtypeStruct((T, C), BF16)] if has_post else [])
    return pl.pallas_call(body, name=name, grid=(T // tm,), in_specs=in_specs, out_specs=out_specs, out_shape=out_shape,
                          scratch_shapes=[pltpu.VMEM((HB + tm, C), F32)] + _shift_scratch(HB + tm, C, reuse),
                          compiler_params=_cp("arbitrary"))(*ins)


def dwconv_bwd(name, g, x, xw, w_pad, S, K, HB, pre, pre_bwd, tm):
    T = g.shape[0]
    C = w_pad.shape[1]
    nb, per_seq = tm // HB, S // tm
    nblk = T // HB

    reuse = K > SUBLANES

    def body(g_ref, gn_ref, x_ref, xp_ref, w_ref, dx_ref, dw_ref, dbx_ref, extg, exta, *shift_refs):
        i = pl.program_id(0)
        first = (i % per_seq) == 0
        last = (i % per_seq) == per_seq - 1

        @pl.when(i == 0)
        def _():
            dw_ref[...] = jnp.zeros(dw_ref.shape, F32)
            dbx_ref[...] = jnp.zeros(dbx_ref.shape, F32)

        extg[tm:tm + HB, :] = jnp.where(last, 0.0, gn_ref[...])
        exta[0:HB, :] = jnp.where(first, 0.0, pre(xp_ref[...]))
        for r in range(tm // SUB):
            extg[r * SUB:(r + 1) * SUB, :] = g_ref[r * SUB:(r + 1) * SUB, :]
            exta[HB + r * SUB:HB + (r + 1) * SUB, :] = pre(x_ref[r * SUB:(r + 1) * SUB, :])
        assert pre_bwd is None or C == D
        g_rows = _RowShifts(extg, shift_refs[0] if reuse else None, tm + HB, reuse)
        a_rows = _RowShifts(exta, shift_refs[1] if reuse else None, HB + tm, reuse)
        for r in range(tm // SUB):
            rows = slice(r * SUB, (r + 1) * SUB)
            for c0 in range(0, C, D):
                cols = slice(c0, c0 + D)
                acc = jnp.zeros((SUB, D), F32)
                for k in range(K):
                    acc = acc + w_ref[k:k + 1, cols] * g_rows.window(r * SUB + (K - 1) - k, cols)
                if pre_bwd is None:
                    dx_ref[rows, cols] = acc
                    dbx_ref[:, cols] += jnp.sum(acc, axis=0, keepdims=True)
                else:
                    dx = pre_bwd(x_ref[rows, :], acc)
                    dx_ref[rows, :] = dx
                    dbx_ref[...] += jnp.sum(dx, axis=0, keepdims=True)
        for k in range(K):
            for c0 in range(0, C, D):
                cols = slice(c0, c0 + D)
                p = jnp.zeros((SUB, D), F32)
                for r in range(tm // SUB):
                    p = p + extg[r * SUB:(r + 1) * SUB, cols] * a_rows.window(HB + r * SUB - (K - 1) + k, cols)
                dw_ref[k:k + 1, cols] += jnp.sum(p, axis=0, keepdims=True)

    in_specs = [pl.BlockSpec((tm, C), lambda i: (i, 0)),
                pl.BlockSpec((HB, C), lambda i: (jnp.minimum((i + 1) * nb, nblk - 1), 0)),
                pl.BlockSpec((tm, xw), lambda i: (i, 0)),
                pl.BlockSpec((HB, xw), lambda i: (jnp.maximum(i * nb - 1, 0), 0)),
                pl.BlockSpec(w_pad.shape, lambda i: (0, 0))]
    out_specs = [pl.BlockSpec((tm, xw), lambda i: (i, 0)), pl.BlockSpec((HB, C), lambda i: (0, 0)),
                 pl.BlockSpec((1, xw), lambda i: (0, 0))]
    out_shape = [jax.ShapeDtypeStruct((T, xw), F32), jax.ShapeDtypeStruct((HB, C), F32), jax.ShapeDtypeStruct((1, xw), F32)]
    return pl.pallas_call(body, name=name, grid=(T // tm,), in_specs=in_specs, out_specs=out_specs, out_shape=out_shape,
                          scratch_shapes=[pltpu.VMEM((tm + HB, C), F32), pltpu.VMEM((HB + tm, C), F32)]
                          + _shift_scratch(tm + HB, C, reuse) * 2,
                          compiler_params=_cp("arbitrary"))(g, g, x, x, w_pad)


def _glu_bwd(u, da):
    u1, sg = u[:, :D], jax.nn.sigmoid(u[:, D:])
    return jnp.concatenate([da * sg, da * u1 * sg * (1.0 - sg)], axis=1)


def _head_cols(ref, h, base=0):
    return ref[:, base + h * HD:base + (h + 1) * HD]


def _prep_inputs(c_ref, ab, al, dt):
    hs = range(H)
    return ([_head_cols(c_ref, h) for h in hs], [_head_cols(c_ref, h, D) for h in hs],
            [_head_cols(c_ref, h, 2 * D) for h in hs], [_lane_pick(ab, h, HD) for h in hs],
            [_lane_pick(ab, H + h, HD) for h in hs], [_lane_pick(al, h, HD) for h in hs],
            [_lane_pick(dt, h, HD) for h in hs])


def gdn_prep_fwd(cpre, pab, alog, dtb):
    T = cpre.shape[0]
    nc = T // CH

    def body(c_ref, ab_ref, al_ref, dt_ref, u_ref, w_ref, qg_ref, kg_ref, qk_ref, t_ref, eg_ref):
        us, ws, qks, qgs, kgs, egs, ts = f_prep(*_prep_inputs(c_ref, ab_ref[...], al_ref[...], dt_ref[...]), None, _dot_raw)
        for h in range(H):
            cols = slice(h * HD, (h + 1) * HD)
            u_ref[:, cols] = us[h]
            w_ref[:, cols] = ws[h].astype(BF16)
            qg_ref[:, cols] = qgs[h].astype(BF16)
            kg_ref[:, cols] = kgs[h].astype(BF16)
            qk_ref[0, h] = qks[h].astype(BF16)
            t_ref[0, h] = ts[h].astype(BF16)
            eg_ref[0, h:h + 1, :] = egs[h]

    row = lambda w: pl.BlockSpec((CH, w), lambda n: (n, 0))
    par = pl.BlockSpec((1, HD), lambda n: (0, 0))
    mat = pl.BlockSpec((1, H, CH, CH), lambda n: (n, 0, 0, 0))
    return pl.pallas_call(
        body, name="gdn_prep_fwd", grid=(nc,), in_specs=[row(3 * D), row(HD), par, par],
        out_specs=[row(D), row(D), row(D), row(D), mat, mat, pl.BlockSpec((1, H, HD), lambda n: (n, 0, 0))],
        out_shape=[jax.ShapeDtypeStruct((T, D), F32)] + [jax.ShapeDtypeStruct((T, D), BF16)] * 3
        + [jax.ShapeDtypeStruct((nc, H, CH, CH), BF16)] * 2 + [jax.ShapeDtypeStruct((nc, H, HD), F32)],
        compiler_params=_cp("arbitrary"))(cpre, pab, alog, dtb)


def gdn_prep_bwd(cpre, pab, alog, dtb, tmat, du, dw, dqg, dkg, dqk, deg):
    T = cpre.shape[0]
    nc = T // CH

    def body(c_ref, ab_ref, al_ref, dt_ref, t_ref, du_ref, dw_ref, dqg_ref, dkg_ref, dqk_ref, deg_ref,
             dc_ref, dab_ref, dal_ref, ddt_ref):
        @pl.when(pl.program_id(0) == 0)
        def _():
            dal_ref[...] = jnp.zeros(dal_ref.shape, F32)
            ddt_ref[...] = jnp.zeros(ddt_ref.shape, F32)

        lane = lax.broadcasted_iota(jnp.int32, (1, HD), 1)
        dab = jnp.zeros((CH, HD), F32)
        dal = jnp.zeros((1, HD), F32)
        ddt = jnp.zeros((1, HD), F32)
        hs = range(H)
        t_st = [t_ref[0, h].astype(F32) for h in hs]

        def fwd(*args):
            return tuple(f_prep(*args, t_st, _dot_vjp)[:6])

        _, vjp = jax.vjp(fwd, *_prep_inputs(c_ref, ab_ref[...], al_ref[...], dt_ref[...]))
        dcqs, dcks, dcvs, dars, dbrs, dals, ddts = vjp((
            [_head_cols(du_ref, h) for h in hs], [_head_cols(dw_ref, h) for h in hs], [dqk_ref[0, h] for h in hs],
            [_head_cols(dqg_ref, h) for h in hs], [_head_cols(dkg_ref, h) for h in hs],
            [deg_ref[0, h:h + 1, :] for h in hs]))
        for h in hs:
            dc_ref[:, h * HD:(h + 1) * HD] = dcqs[h]
            dc_ref[:, D + h * HD:D + (h + 1) * HD] = dcks[h]
            dc_ref[:, 2 * D + h * HD:2 * D + (h + 1) * HD] = dcvs[h]
            dab = dab + jnp.where(lane == h, dars[h], 0.0) + jnp.where(lane == H + h, dbrs[h], 0.0)
            dal = dal + jnp.where(lane == h, dals[h], 0.0)
            ddt = ddt + jnp.where(lane == h, ddts[h], 0.0)
        dab_ref[...] = dab
        dal_ref[...] += dal
        ddt_ref[...] += ddt

    row = lambda w: pl.BlockSpec((CH, w), lambda n: (n, 0))
    par = pl.BlockSpec((1, HD), lambda n: (0, 0))
    mat = pl.BlockSpec((1, H, CH, CH), lambda n: (n, 0, 0, 0))
    vec = pl.BlockSpec((1, H, HD), lambda n: (n, 0, 0))
    return pl.pallas_call(
        body, name="gdn_prep_bwd", grid=(nc,),
        in_specs=[row(3 * D), row(HD), par, par, mat, row(D), row(D), row(D), row(D), mat, vec],
        out_specs=[row(3 * D), row(HD), par, par],
        out_shape=[jax.ShapeDtypeStruct((T, 3 * D), F32), jax.ShapeDtypeStruct((T, HD), F32),
                   jax.ShapeDtypeStruct((1, HD), F32), jax.ShapeDtypeStruct((1, HD), F32)],
        compiler_params=_cp("arbitrary"))(cpre, pab, alog, dtb, tmat, du, dw, dqg, dkg, dqk, deg)


def gdn_scan_fwd(u, w, qg, kg, qk, eg, S):
    T = u.shape[0]
    nc, per_seq = T // CH, S // CH

    def body(u_ref, w_ref, qg_ref, kg_ref, qk_ref, eg_ref, o_ref, sall_ref, s_ref):
        @pl.when(pl.program_id(0) % per_seq == 0)
        def _():
            s_ref[...] = jnp.zeros(s_ref.shape, F32)

        hs = range(H)
        ss = [s_ref[h] for h in hs]
        os_, s2s = f_scan(ss, [_head_cols(u_ref, h) for h in hs], [_head_cols(w_ref, h) for h in hs],
                          [_head_cols(qg_ref, h) for h in hs], [_head_cols(kg_ref, h) for h in hs],
                          [qk_ref[0, h] for h in hs], [eg_ref[0, h:h + 1, :] for h in hs], _dot_raw)
        for h in hs:
            sall_ref[0, h] = ss[h]
            o_ref[:, h * HD:(h + 1) * HD] = os_[h]
            s_ref[h] = s2s[h]

    row = pl.BlockSpec((CH, D), lambda n: (n, 0))
    return pl.pallas_call(
        body, name="gdn_scan_fwd", grid=(nc,),
        in_specs=[row, row, row, row, pl.BlockSpec((1, H, CH, CH), lambda n: (n, 0, 0, 0)),
                  pl.BlockSpec((1, H, HD), lambda n: (n, 0, 0))],
        out_specs=[row, pl.BlockSpec((1, H, HD, HD), lambda n: (n, 0, 0, 0))],
        out_shape=[jax.ShapeDtypeStruct((T, D), F32), jax.ShapeDtypeStruct((nc, H, HD, HD), F32)],
        scratch_shapes=[pltpu.VMEM((H, HD, HD), F32)], compiler_params=_cp("arbitrary"))(u, w, qg, kg, qk, eg)


def gdn_scan_bwd(do, u, w, qg, kg, qk, eg, sall, S):
    T = u.shape[0]
    nc, per_seq = T // CH, S // CH

    def body(do_ref, u_ref, w_ref, qg_ref, kg_ref, qk_ref, eg_ref, sall_ref,
             du_ref, dw_ref, dqg_ref, dkg_ref, dqk_ref, deg_ref, ds_ref):
        n = nc - 1 - pl.program_id(0)

        @pl.when(n % per_seq == per_seq - 1)
        def _():
            ds_ref[...] = jnp.zeros(ds_ref.shape, F32)

        hs = range(H)

        def fwd(*args):
            return f_scan(*args, _dot_vjp)

        _, vjp = jax.vjp(fwd, [sall_ref[0, h] for h in hs], [_head_cols(u_ref, h) for h in hs],
                         [_head_cols(w_ref, h).astype(F32) for h in hs], [_head_cols(qg_ref, h).astype(F32) for h in hs],
                         [_head_cols(kg_ref, h).astype(F32) for h in hs], [qk_ref[0, h].astype(F32) for h in hs],
                         [eg_ref[0, h:h + 1, :] for h in hs])
        dss, dus, dws, dqgs, dkgs, dqks, degs = vjp(([_head_cols(do_ref, h) for h in hs], [ds_ref[h] for h in hs]))
        for h in hs:
            cols = slice(h * HD, (h + 1) * HD)
            du_ref[:, cols] = dus[h]
            dw_ref[:, cols] = dws[h]
            dqg_ref[:, cols] = dqgs[h]
            dkg_ref[:, cols] = dkgs[h]
            dqk_ref[0, h] = dqks[h]
            deg_ref[0, h:h + 1, :] = degs[h]
            ds_ref[h] = dss[h]

    rev = lambda n: (nc - 1 - n, 0)
    row = pl.BlockSpec((CH, D), rev)
    mat = pl.BlockSpec((1, H, CH, CH), lambda n: (nc - 1 - n, 0, 0, 0))
    vec = pl.BlockSpec((1, H, HD), lambda n: (nc - 1 - n, 0, 0))
    return pl.pallas_call(
        body, name="gdn_scan_bwd", grid=(nc,),
        in_specs=[row, row, row, row, row, mat, vec, pl.BlockSpec((1, H, HD, HD), lambda n: (nc - 1 - n, 0, 0, 0))],
        out_specs=[row, row, row, row, mat, vec],
        out_shape=[jax.ShapeDtypeStruct((T, D), F32)] * 4
        + [jax.ShapeDtypeStruct((nc, H, CH, CH), F32), jax.ShapeDtypeStruct((nc, H, HD), F32)],
        scratch_shapes=[pltpu.VMEM((H, HD, HD), F32)], compiler_params=_cp("arbitrary"))(do, u, w, qg, kg, qk, eg, sall)


def xor_exchange(name, ins, inplace, out_shapes, plan, n_remote, n_local=0):
    n_in, n_ip, n_out = len(ins), len(inplace), len(out_shapes)

    def body(*refs):
        in_refs = refs[:n_in]
        ip_refs = refs[n_in + n_ip:n_in + 2 * n_ip]
        out_refs = refs[n_in + 2 * n_ip:n_in + 2 * n_ip + n_out]
        send_sems, recv_sems, loc_sems = refs[n_in + 2 * n_ip + n_out:]
        x, y, c = lax.axis_index("x"), lax.axis_index("y"), lax.axis_index("c")
        remote, local = plan(in_refs, ip_refs, out_refs, (x, y, c))
        assert len(remote) == n_remote and len(local) == n_local
        copies = []
        for k, ((dx, dy, dc), src, dst) in enumerate(remote):
            peer = (1 - x if dx else x, 1 - y if dy else y, 1 - c if dc else c)
            copies.append(pltpu.make_async_remote_copy(src_ref=src, dst_ref=dst, send_sem=send_sems.at[k],
                                                       recv_sem=recv_sems.at[k], device_id=peer, device_id_type=MESH))
        for cp in copies:
            cp.start()
        locs = [pltpu.make_async_copy(src, dst, loc_sems.at[k]) for k, (src, dst) in enumerate(local)]
        for cp in locs:
            cp.start()
        for cp in copies:
            cp.wait()
        for cp in locs:
            cp.wait()

    anyspec = pl.BlockSpec(memory_space=pl.ANY)
    res = pl.pallas_call(
        body, name=name, in_specs=[anyspec] * (n_in + n_ip), out_specs=[anyspec] * (n_ip + n_out),
        out_shape=[jax.ShapeDtypeStruct(a.shape, a.dtype) for a in inplace] + list(out_shapes),
        input_output_aliases={n_in + i: i for i in range(n_ip)},
        scratch_shapes=[pltpu.SemaphoreType.DMA((n_remote,)), pltpu.SemaphoreType.DMA((n_remote,)),
                        pltpu.SemaphoreType.DMA((max(n_local, 1),))],
        )(*ins, *inplace)
    return list(res[:n_ip]), list(res[n_ip:])


class WSpec:
    def __init__(self, name, full, sa, ha, group, layer=None, lead=False):
        self.name, self.full, self.sa, self.ha, self.group, self.layer, self.lead = name, full, sa, ha, group, layer, lead
        self.ws = 1 if lead else full[sa] // 4
        self.wh = full[ha] // 2

    def shard_shape(self):
        if self.lead:
            return tuple(n for a, n in enumerate(self.full) if a != self.sa)
        return tuple(self.ws if a == self.sa else n for a, n in enumerate(self.full))

    def half_full_shape(self):
        return tuple(self.wh if a == self.ha else n for a, n in enumerate(self.full))

    def shard_half_shape(self):
        s = list(self.half_full_shape())
        if self.lead:
            del s[self.sa]
        else:
            s[self.sa] = self.ws
        return tuple(s)

    def full_view(self, ref, q=None, h=None):
        idx = []
        for a in range(len(self.full)):
            if a == self.sa and q is not None:
                idx.append(q if self.lead else pl.ds(pl.multiple_of(q * self.ws, self.ws), self.ws))
            elif a == self.ha and h is not None:
                idx.append(pl.ds(pl.multiple_of(h * self.wh, self.wh), self.wh))
            else:
                idx.append(slice(None))
        return ref.at[tuple(idx)]

    def shard_view(self, ref, h):
        idx = [] if self.layer is None else [self.layer]
        for a in range(len(self.full)):
            if self.lead and a == self.sa:
                continue
            idx.append(pl.ds(pl.multiple_of(h * self.wh, self.wh), self.wh) if a == self.ha else slice(None))
        return ref.at[tuple(idx)]

    def rows_cols(self, shard, half):
        rows, cols = self.full[-2:]
        if shard and not self.lead:
            rows, cols = (rows // 4, cols) if self.sa == 0 else (rows, cols // 4)
        if half:
            rows, cols = (rows // 2, cols) if self.ha == len(self.full) - 2 else (rows, cols // 2)
        return rows, cols

    def spec(self, tr, cw, nr, shard=False, half=False, has_lead=False, stacked=False):
        two_d = len(self.full) == 2
        shard_on_cols = two_d and self.sa == 1
        half_on_cols = two_d and self.ha == 1
        layer = self.layer

        def index(*args):
            pref = args[-1]
            i = args[-2]
            r, cblk, pre = i, 0, ()
            if shard:
                if self.lead:
                    pre = (pref[0],)
                elif shard_on_cols:
                    cblk = pref[0]
                else:
                    r = pref[0] * nr + i
            elif has_lead:
                pre = (args[0],)
            if half:
                if half_on_cols:
                    cblk = pref[1]
                else:
                    r = pref[1] * nr + i
            if stacked:
                pre = (layer,) + pre
            return pre + (r, cblk)

        n_pre = int(stacked) + int(self.lead and (shard or has_lead))
        return pl.BlockSpec((None,) * n_pre + (tr, cw), index)


WSPECS = [
    WSpec("cv_w_pw1", (D, 2 * D), 1, 0, 0),
    WSpec("cv_w_pw2", (D, D), 0, 1, 1),
    WSpec("gdn_w_in", (4, D, (4 * D + 2 * H) // 4), 0, 1, 2, lead=True),
    WSpec("gdn_w_out", (D, D), 0, 1, 3),
    WSpec("mlp_w1_0", (D, DFF), 1, 0, 4, layer=0),
    WSpec("mlp_w1_1", (D, DFF), 1, 0, 4, layer=1),
    WSpec("mlp_w2_0", (DFF, D), 0, 1, 5, layer=0),
    WSpec("mlp_w2_1", (DFF, D), 0, 1, 5, layer=1),
]
FLIPS = [(1, 0, 0), (0, 1, 0), (1, 1, 0)]
SIB = (0, 0, 1)


def _chip(x, y):
    return 2 * x + y


def _prefetch_call(name, body, grid, in_specs, out_specs, out_shape, pref, args, aliases=None):
    return pl.pallas_call(
        body, name=name, out_shape=out_shape, input_output_aliases=aliases or {},
        grid_spec=pltpu.PrefetchScalarGridSpec(num_scalar_prefetch=1, grid=grid, in_specs=in_specs, out_specs=out_specs),
        compiler_params=_cp(*(("arbitrary",) * len(grid))))(pref, *args)


def place_shard(ws, shard, pref):
    rows, cols = ws.rows_cols(True, False)
    tr = _ew_rows(rows, cols)
    nr = rows // tr
    stacked = ws.layer is not None
    layer = ws.layer

    def body(_, s_ref, o_ref):
        o_ref[...] = s_ref[...].astype(BF16)

    in_spec = pl.BlockSpec(((None,) if stacked else ()) + (tr, cols),
                           (lambda i, p: (layer, i, 0)) if stacked else (lambda i, p: (i, 0)))
    return _prefetch_call("place_" + ws.name, body, (nr,), [in_spec], ws.spec(tr, cols, nr, shard=True),
                          jax.ShapeDtypeStruct(ws.full, BF16), pref, [shard])


def gather_weights(shards, wdw_shard, wcv_shard, pref):
    n = len(WSPECS)
    placed = [place_shard(ws, shards[ws.group], pref) for ws in WSPECS]

    def plan_a(in_refs, ip_refs, out_refs, pos):
        x, y, c = pos
        q = _chip(x, y)
        remote, local = [], []
        for i, ws in enumerate(WSPECS):
            mine = ws.full_view(ip_refs[i], q, c)
            for f in FLIPS:
                remote.append((f, mine, mine))
        for j, width in enumerate((D // 4, 3 * D // 4)):
            dst = out_refs[j].at[:, pl.ds(pl.multiple_of(q * width, 128), width)]
            local.append((in_refs[j], dst))
            for f in FLIPS:
                remote.append((f, in_refs[j], dst))
        return remote, local

    taps = [jax.ShapeDtypeStruct((KCV, D), F32), jax.ShapeDtypeStruct((KSC, 3 * D), F32)]
    nat, (wdw, wcv) = xor_exchange("gather_chips", [wdw_shard, wcv_shard], placed, taps, plan_a, 3 * (n + 2), 2)

    def plan_b(in_refs, ip_refs, out_refs, pos):
        x, y, c = pos
        remote = []
        for i, ws in enumerate(WSPECS):
            for (dx, dy, _) in FLIPS:
                got = ws.full_view(ip_refs[i], _chip(1 - x if dx else x, 1 - y if dy else y), c)
                remote.append((SIB, got, got))
        return remote, []

    nat, _ = xor_exchange("gather_cores", [], nat, [], plan_b, 3 * n)
    return nat, wdw, wcv


def half_add(ws, g, rsib, pref):
    rows, cols = ws.rows_cols(False, True)
    tr = _ew_rows(rows, cols)
    nr = rows // tr

    def body(_, a_ref, b_ref, o_ref):
        o_ref[...] = (a_ref[...] + b_ref[...]).astype(BF16)

    whole = ws.spec(tr, cols, nr, has_lead=ws.lead)
    return _prefetch_call("reduce_add_" + ws.name, body, (4, nr) if ws.lead else (nr,),
                          [ws.spec(tr, cols, nr, half=True, has_lead=ws.lead), whole], whole,
                          jax.ShapeDtypeStruct(ws.half_full_shape(), BF16), pref, [g, rsib])


def shard_sum(ws, s, parts, buf, pref):
    rows, cols = ws.rows_cols(True, True)
    tr = _ew_rows(rows, cols)
    nr = rows // tr
    stacked = ws.layer is not None

    def body(_, s_ref, p_ref, *rest):
        rest[-1][...] = ((s_ref[...].astype(F32) + p_ref[0].astype(F32)) + p_ref[1].astype(F32)) + p_ref[2].astype(F32)

    in_specs = [ws.spec(tr, cols, nr, shard=True, has_lead=ws.lead), pl.BlockSpec((3, tr, cols), lambda i, p: (0, i, 0))]
    args, aliases = [s, parts], {}
    if buf is not None:
        in_specs.append(pl.BlockSpec(memory_space=pl.ANY))
        args.append(buf)
        aliases = {3: 0}
    shape = ((2,) if stacked else ()) + ws.shard_shape()
    return _prefetch_call("reduce_sum_" + ws.name, body, (nr,), in_specs, ws.spec(tr, cols, nr, half=True, stacked=stacked),
                          jax.ShapeDtypeStruct(shape, F32), pref, args, aliases)


def reduce_grads(grads, pref):
    n = len(WSPECS)

    def plan1(in_refs, ip_refs, out_refs, pos):
        c = pos[2]
        return [(SIB, ws.full_view(in_refs[i], None, 1 - c), out_refs[i]) for i, ws in enumerate(WSPECS)], []

    halves = [jax.ShapeDtypeStruct(ws.half_full_shape(), F32) for ws in WSPECS]
    _, rsib = xor_exchange("reduce_cores", grads, [], halves, plan1, n)
    sums = [half_add(ws, grads[i], rsib[i], pref) for i, ws in enumerate(WSPECS)]

    def plan2(in_refs, ip_refs, out_refs, pos):
        x, y, c = pos
        remote = []
        for i, ws in enumerate(WSPECS):
            for s, (dx, dy, _) in enumerate(FLIPS):
                qq = _chip(1 - x if dx else x, 1 - y if dy else y)
                remote.append(((dx, dy, 0), ws.full_view(in_refs[i], qq), out_refs[i].at[s]))
        return remote, []

    parts = [jax.ShapeDtypeStruct((3,) + ws.shard_half_shape(), BF16) for ws in WSPECS]
    _, got = xor_exchange("reduce_chips", sums, [], parts, plan2, 3 * n)
    bufs = {}
    for i, ws in enumerate(WSPECS):
        bufs[ws.group] = shard_sum(ws, sums[i], got[i], bufs.get(ws.group), pref)

    def plan3(in_refs, ip_refs, out_refs, pos):
        c = pos[2]
        remote = []
        for ws in WSPECS:
            mine = ws.shard_view(ip_refs[ws.group], c)
            remote.append((SIB, mine, mine))
        return remote, []

    return xor_exchange("reduce_swap", [], [bufs[g] for g in sorted(bufs)], [], plan3, n)[0]


def gather_small(buf):
    flips = [(dx, dy, dc) for dx in (0, 1) for dy in (0, 1) for dc in (0, 1)][1:]

    def plan(in_refs, ip_refs, out_refs, pos):
        x, y, c = pos
        me = 4 * x + 2 * y + c
        dst = out_refs[0].at[me]
        return [(f, in_refs[0], dst) for f in flips], [(in_refs[0], dst)]

    return xor_exchange("gather_small", [buf], [], [jax.ShapeDtypeStruct((8,) + buf.shape, F32)], plan, 7, 1)[1][0]


def _pad_rows(a, rows):
    return jnp.pad(a, ((0, rows - a.shape[0]), (0, 0)))


def _row1(v):
    v = v.reshape((1, -1))
    return jnp.pad(v, ((0, 0), (0, D - v.shape[1])))


def _rms_fwd(name, h, g, tm):
    return row_call(name, lambda hh, gg: ((f_rms(hh, gg),), ()), [(h, D, 0)], [g], [(D, BF16)], [], tm)[0]


def _res_rms(h, g):
    return (h, f_rms(h, g)), ()


def _rms_bwd_epi(dhn, h, dres, g):
    _, vjp = jax.vjp(f_rms, h, g)
    dh, dg = vjp(dhn)
    dh = dh + dres
    return (dh,), (dg, jnp.sum(dh, axis=0, keepdims=True))


def _mlp_bwd(tag, dh, h, g, w1, w2, hn, z1):
    dz1 = mm("mlp_down_dx" + tag, dh, w2, "NT", BF16,
             epi=lambda acc, z: acc * (2.0 * jnp.maximum(z.astype(F32), 0.0)), epi_ins=[(z1, "tile")])
    dw2 = mm_tn("mlp_down_dw" + tag, z1, dh, a_fn=f_relu2)
    dh_in, dg, colsum = mm("mlp_up_dx" + tag, dz1, w1, "NT", [F32], epi=_rms_bwd_epi,
                           epi_ins=[(h, "tile"), (dh, "tile"), (g, "row")], accs=[(1, D), (1, D)])
    dw1 = mm_tn("mlp_up_dw" + tag, hn, dz1)
    return dh_in, dg, colsum, dw1, dw2


def kernel(x, norm_mix_g, norm_ffn_g, final_norm_g, cv_w_pw1, cv_b_pw1, cv_w_dw, cv_b_dw, cv_ln_g, cv_ln_b, cv_w_pw2, cv_b_pw2, gdn_w_in, gdn_conv_w, gdn_a_log, gdn_dt_bias, gdn_norm_g, gdn_w_out, mlp_w1, mlp_w2, loss_target, m_norm_mix_g, m_norm_ffn_g, m_final_norm_g, m_cv_w_pw1, m_cv_b_pw1, m_cv_w_dw, m_cv_b_dw, m_cv_ln_g, m_cv_ln_b, m_cv_w_pw2, m_cv_b_pw2, m_gdn_w_in, m_gdn_conv_w, m_gdn_a_log, m_gdn_dt_bias, m_gdn_norm_g, m_gdn_w_out, m_mlp_w1, m_mlp_w2, v_norm_mix_g, v_norm_ffn_g, v_final_norm_g, v_cv_w_pw1, v_cv_b_pw1, v_cv_w_dw, v_cv_b_dw, v_cv_ln_g, v_cv_ln_b, v_cv_w_pw2, v_cv_b_pw2, v_gdn_w_in, v_gdn_conv_w, v_gdn_a_log, v_gdn_dt_bias, v_gdn_norm_g, v_gdn_w_out, v_mlp_w1, v_mlp_w2):
    env = dict(locals())
    bl, S, _ = x.shape
    T = bl * S
    tm = min(256, S)
    xf = x.reshape((T, D))
    tgt = loss_target.reshape((T, D))

    chip = 2 * lax.axis_index("x") + lax.axis_index("y")
    pref = jnp.stack([chip, lax.axis_index("c")]).astype(jnp.int32)
    big = [cv_w_pw1[0], cv_w_pw2[0], gdn_w_in[0], gdn_w_out[0], mlp_w1, mlp_w2]
    nat, wdw, wcv = gather_weights(big, cv_w_dw[0], gdn_conv_w[0], pref)
    w_pw1, w_pw2, w_in_sm, w_out, w1_0, w1_1, w2_0, w2_1 = nat
    w_in = jnp.transpose(w_in_sm, (1, 0, 2)).reshape((D, 4 * D + 2 * H))
    w_qkv, w_z = w_in[:, :3 * D], w_in[:, 3 * D:4 * D]
    w_qkvz = w_in[:, :4 * D]
    w_ab = jnp.pad(w_in[:, 4 * D:], ((0, 0), (0, HD - 2 * H)))
    wdw_p, wcv_p = _pad_rows(wdw, HB_CV), _pad_rows(wcv, HB_SC)
    alog_p = jnp.pad(gdn_a_log, ((0, 0), (0, HD - H)))
    dtb_p = jnp.pad(gdn_dt_bias, ((0, 0), (0, HD - H)))
    g_mix0, g_mix1 = norm_mix_g[0:1], norm_mix_g[1:2]
    g_ffn0, g_ffn1 = norm_ffn_g[0:1], norm_ffn_g[1:2]
    g_fin = final_norm_g.reshape((1, D))

    hn0 = _rms_fwd("rms_mix0", xf, g_mix0, tm)
    u = mm("cv_pw1", hn0, w_pw1, "NN", F32, epi=lambda acc, b: acc + b, epi_ins=[(cv_b_pw1, "row")])
    dwc, s_act = dwconv_fwd("cv_dwconv", u, 2 * D, wdw_p, cv_b_dw, S, KCV, HB_CV, f_glu, (f_ln_silu, (cv_ln_g, cv_ln_b)), tm)
    h1, hnf0 = mm("cv_pw2", s_act, w_pw2, "NN", [F32, BF16], epi=lambda acc, b, r, g: _res_rms(acc + b + r, g),
                  epi_ins=[(cv_b_pw2, "row"), (xf, "tile"), (g_ffn0, "row")])
    z1_0 = mm("mlp_up0", hnf0, w1_0, "NN", BF16)
    h2, hn2 = mm("mlp_down0", z1_0, w2_0, "NN", [F32, BF16], a_fn=f_relu2, epi=lambda acc, r, g: _res_rms(acc + r, g),
                 epi_ins=[(h1, "tile"), (g_mix1, "row")])

    pqkvz = mm("gdn_in", hn2, w_qkvz, "NN", F32)
    pab = mm("gdn_in_ab", hn2, w_ab, "NN", F32)
    cpre = dwconv_fwd("gdn_conv", pqkvz, 3 * D, wcv_p, None, S, KSC, HB_SC, lambda v: v, None, tm)[0]
    gu, gw, gqg, gkg, gqk, gt, geg = gdn_prep_fwd(cpre, pab, alog_p, dtb_p)
    o, sall = gdn_scan_fwd(gu, gw, gqg, gkg, gqk, geg, S)
    on = row_call("gdn_post", lambda oo, zz, ng: ((f_post(oo, zz, ng),), ()), [(o, D, 0), (pqkvz, D, 3)],
                  [gdn_norm_g], [(D, BF16)], [], tm)[0]
    h3, hnf1 = mm("gdn_out", on, w_out, "NN", [F32, BF16], epi=lambda acc, r, g: _res_rms(acc + r, g),
                  epi_ins=[(h2, "tile"), (g_ffn1, "row")])
    z1_1 = mm("mlp_up1", hnf1, w1_1, "NN", BF16)

    def head(acc, res, tt, gg):
        def loss_of(h_, g_):
            return 0.5 * jnp.sum(jnp.mean(jnp.square(f_rms(h_, g_) - tt), axis=-1))
        lv, (dh_, dg_) = jax.value_and_grad(loss_of, (0, 1))(acc + res, gg)
        return (dh_,), (dg_, jnp.full((1, D), lv, F32))

    dh4, dg_fin, loss_row = mm("mlp_down1", z1_1, w2_1, "NN", [F32], a_fn=f_relu2, epi=head,
                               epi_ins=[(h3, "tile"), (tgt, "tile"), (g_fin, "row")], accs=[(1, D), (1, D)])

    dh3, dg_ffn1, _, dw1_1, dw2_1 = _mlp_bwd("1", dh4, h3, g_ffn1, w1_1, w2_1, hnf1, z1_1)
    dw_out = mm_tn("gdn_out_dw", on, dh3)

    def post_bwd(don, oo, zz, ng):
        _, vjp = jax.vjp(f_post, oo, zz, ng)
        do_, dz_, dng_ = vjp(don)
        return (do_, dz_), (dng_,)

    do, dz, dng = mm("gdn_out_dx", dh3, w_out, "NT", [F32, F32], epi=post_bwd,
                     epi_ins=[(o, "tile"), (pqkvz, ("cols", 3)), (gdn_norm_g, "whole")], accs=[(1, HD)])
    du, dw, dqg, dkg, dqk, deg = gdn_scan_bwd(do, gu, gw, gqg, gkg, gqk, geg, sall, S)
    dcpre, dpab, dalog, ddtb = gdn_prep_bwd(cpre, pab, alog_p, dtb_p, gt, du, dw, dqg, dkg, dqk, deg)
    dqkv, dwcv, _ = dwconv_bwd("gdn_conv_bwd", dcpre, pqkvz, 3 * D, wcv_p, S, KSC, HB_SC, lambda v: v, None, tm)
    dhn2 = mm("gdn_in_dx_ab", dpab, w_ab, "NT", F32)
    dhn2 = mm("gdn_in_dx_z", dz, w_z, "NT", F32, epi=lambda acc, r: acc + r, epi_ins=[(dhn2, "tile")])
    dh2, dg_mix1, _ = mm("gdn_in_dx_qkv", dqkv, w_qkv, "NT", [F32],
                         epi=lambda acc, prev, hh, rr, gg: _rms_bwd_epi(acc + prev, hh, rr, gg),
                         epi_ins=[(dhn2, "tile"), (h2, "tile"), (dh3, "tile"), (g_mix1, "row")], accs=[(1, D), (1, D)])
    dw_in = jnp.concatenate([mm_tn("gdn_in_dw_qkv", hn2, dqkv), mm_tn("gdn_in_dw_z", hn2, dz),
                             mm_tn("gdn_in_dw_ab", hn2, dpab)[:, :2 * H]], axis=1)

    dh1, dg_ffn0, db_pw2, dw1_0, dw2_0 = _mlp_bwd("0", dh2, h1, g_ffn0, w1_0, w2_0, hnf0, z1_0)
    dw_pw2 = mm_tn("cv_pw2_dw", s_act, dh1)

    def ln_bwd(ds, xx, gg, bb):
        _, vjp = jax.vjp(f_ln_silu, xx, gg, bb)
        dx_, dg_, db_ = vjp(ds)
        return (dx_,), (dg_, db_, jnp.sum(dx_, axis=0, keepdims=True))

    ddw, dln_g, dln_b, db_dw = mm("cv_pw2_dx", dh1, w_pw2, "NT", [F32], epi=ln_bwd,
                                  epi_ins=[(dwc, "tile"), (cv_ln_g, "row"), (cv_ln_b, "row")], accs=[(1, D)] * 3)
    du_cv, dwdw, db_pw1 = dwconv_bwd("cv_dwconv_bwd", ddw, u, 2 * D, wdw_p, S, KCV, HB_CV, f_glu, _glu_bwd, tm)
    dw_pw1 = mm_tn("cv_pw1_dw", hn0, du_cv)
    grad_x, dg_mix0, _ = mm("cv_pw1_dx", du_cv, w_pw1, "NT", [F32], epi=_rms_bwd_epi,
                            epi_ins=[(xf, "tile"), (dh1, "tile"), (g_mix0, "row")], accs=[(1, D), (1, D)])

    dw_in_sm = jnp.transpose(dw_in.reshape((D, 4, D + 4)), (1, 0, 2))
    g_pw1, g_pw2, g_in, g_out, g_w1, g_w2 = reduce_grads([dw_pw1, dw_pw2, dw_in_sm, dw_out, dw1_0, dw1_1, dw2_0, dw2_1], pref)

    small = jnp.concatenate([
        dg_mix0, dg_mix1, dg_ffn0, dg_ffn1, dg_fin, db_pw1.reshape((2, D)), db_dw, dln_g, dln_b, db_pw2,
        _row1(dalog[:, :H]), _row1(ddtb[:, :H]), _row1(dng), loss_row, jnp.zeros((1, D), F32),
        dwdw, dwcv[:KSC].reshape((3 * KSC, D)), jnp.zeros((NSMALL - 48 - 3 * KSC, D), F32)], axis=0)
    small_all = gather_small(small)

    def pack(a, b, c_, d, e, f, g_, h_, i_, j_, k_):
        return jnp.concatenate([a, b, c_.reshape((1, D)), d.reshape((2, D)), e, f, g_, h_, _row1(i_), _row1(j_), _row1(k_),
                                jnp.zeros((2, D), F32)], axis=0)

    order = lambda p: (p + "norm_mix_g", p + "norm_ffn_g", p + "final_norm_g", p + "cv_b_pw1", p + "cv_b_dw", p + "cv_ln_g",
                       p + "cv_ln_b", p + "cv_b_pw2", p + "gdn_a_log", p + "gdn_dt_bias", p + "gdn_norm_g")
    w16, m16, v16 = (pack(*[env[nm] for nm in order(p)]) for p in ("", "m_", "v_"))

    def small_step(ga, ww, mm_, vv):
        gsum = ga[0]
        for dev in range(1, 8):
            gsum = gsum + ga[dev]
        delta, m2, v2 = f_adamw(ww, gsum[:16], mm_, vv)
        return gsum, delta, m2, v2

    def small_body(ga_ref, w_ref, m_ref, v_ref, g_out, d_out, m_out, v_out):
        gsum, delta, m2, v2 = small_step(ga_ref[...], w_ref[...], m_ref[...], v_ref[...])
        g_out[...] = gsum
        d_out[...] = delta
        m_out[...] = m2
        v_out[...] = v2

    vm = pl.BlockSpec(memory_space=pltpu.VMEM)
    sg, sd, sm, sv = pl.pallas_call(
        small_body, name="adamw_small", in_specs=[vm] * 4, out_specs=[vm] * 4,
        out_shape=[jax.ShapeDtypeStruct((NSMALL, D), F32)] + [jax.ShapeDtypeStruct((16, D), F32)] * 3)(small_all, w16, m16, v16)

    def unpack(b):
        return (b[0:2], b[2:4], b[4], b[5:7].reshape((1, 2 * D)), b[7:8], b[8:9], b[9:10], b[10:11],
                b[11:12, :H], b[12:13, :H], b[13:14, :HD])

    loss = sg[14, 0]
    g_dw = lax.dynamic_slice(sg[16:16 + KCV], (0, chip * (D // 4)), (KCV, D // 4))
    g_cv = lax.dynamic_slice(sg[48:48 + 3 * KSC].reshape((KSC, 3 * D)), (0, chip * (3 * D // 4)), (KSC, 3 * D // 4))

    def adamw(name, w, g, m, v):
        lead = w.shape[:-2]
        if len(lead) == 1 and lead[0] == 1:
            d, m2, v2 = ew_call(name, f_adamw, [w[0], g.reshape(w.shape[1:]), m[0], v[0]], 3)
            return g.reshape(w.shape), d[None], m2[None], v2[None]
        return (g.reshape(w.shape),) + tuple(ew_call(name, f_adamw, [w, g.reshape(w.shape), m, v], 3))

    res = {
        "cv_w_pw1": adamw("adamw_pw1", cv_w_pw1, g_pw1, m_cv_w_pw1, v_cv_w_pw1),
        "cv_w_dw": adamw("adamw_dw", cv_w_dw, g_dw, m_cv_w_dw, v_cv_w_dw),
        "cv_w_pw2": adamw("adamw_pw2", cv_w_pw2, g_pw2, m_cv_w_pw2, v_cv_w_pw2),
        "gdn_w_in": adamw("adamw_win", gdn_w_in, g_in, m_gdn_w_in, v_gdn_w_in),
        "gdn_conv_w": adamw("adamw_cvw", gdn_conv_w, g_cv, m_gdn_conv_w, v_gdn_conv_w),
        "gdn_w_out": adamw("adamw_wout", gdn_w_out, g_out, m_gdn_w_out, v_gdn_w_out),
        "mlp_w1": adamw("adamw_w1", mlp_w1, g_w1, m_mlp_w1, v_mlp_w1),
        "mlp_w2": adamw("adamw_w2", mlp_w2, g_w2, m_mlp_w2, v_mlp_w2),
    }
    names = ("norm_mix_g", "norm_ffn_g", "final_norm_g", "cv_b_pw1", "cv_b_dw", "cv_ln_g", "cv_ln_b", "cv_b_pw2",
             "gdn_a_log", "gdn_dt_bias", "gdn_norm_g")
    for nm, gg, dd, mm_, vv in zip(names, unpack(sg), unpack(sd), unpack(sm), unpack(sv)):
        res[nm] = (gg, dd, mm_, vv)
    weights = ("norm_mix_g", "norm_ffn_g", "final_norm_g", "cv_w_pw1", "cv_b_pw1", "cv_w_dw", "cv_b_dw", "cv_ln_g",
               "cv_ln_b", "cv_w_pw2", "cv_b_pw2", "gdn_w_in", "gdn_conv_w", "gdn_a_log", "gdn_dt_bias", "gdn_norm_g",
               "gdn_w_out", "mlp_w1", "mlp_w2")
    outs = [loss, grad_x.reshape(x.shape)]
    for kind in range(4):
        outs += [res[nm][kind] for nm in weights]
    return tuple(outs)
```

```python
import functools

import jax
import jax.numpy as jnp
from jax import lax
from jax.experimental import pallas as pl
from jax.experimental.pallas import tpu as pltpu

F32, BF16 = jnp.float32, jnp.bfloat16
D = 1024
H = 8
HD = 128
CH = 64
DFF = 4 * D
KCV, HB_CV = 31, 32
KSC, HB_SC = 4, 8
EPS = 1e-6
LR, B1, B2, EPS_A, WD, STEP = 0.001, 0.9, 0.999, 1e-08, 0.01, 10
VMEM_LIMIT = 56 * 1024 * 1024
SUB = 32
NSMALL = 64
MESH = pl.DeviceIdType.MESH


def _cp(*sem):
    return pltpu.CompilerParams(dimension_semantics=sem, vmem_limit_bytes=VMEM_LIMIT)


def f_rms(h, g):
    return h * lax.rsqrt(jnp.mean(h * h, axis=-1, keepdims=True) + EPS) * g


def f_silu(x):
    return x * jax.nn.sigmoid(x)


def f_glu(u):
    return u[:, :D] * jax.nn.sigmoid(u[:, D:])


def f_ln_silu(x, g, b):
    mu = jnp.mean(x, axis=-1, keepdims=True)
    xc = x - mu
    y = xc * lax.rsqrt(jnp.mean(xc * xc, axis=-1, keepdims=True) + EPS)
    return f_silu(y * g + b)


def f_relu2(z):
    r = jnp.maximum(z.astype(F32), 0.0)
    return r * r


def f_post(o, z, ng):
    outs = []
    for h in range(H):
        oh = o[:, h * HD:(h + 1) * HD]
        y = oh * lax.rsqrt(jnp.mean(oh * oh, axis=-1, keepdims=True) + EPS) * ng
        outs.append(y * f_silu(z[:, h * HD:(h + 1) * HD]))
    return jnp.concatenate(outs, axis=1)


def f_adamw(w, g, m, v):
    m2 = B1 * m + (1.0 - B1) * g
    v2 = B2 * v + (1.0 - B2) * (g * g)
    m_hat = m2 / (1.0 - B1 ** STEP)
    v_hat = v2 / (1.0 - B2 ** STEP)
    delta = -LR * (m_hat / (jnp.sqrt(v_hat) + EPS_A) + WD * w)
    return delta, m2, v2


def _dot_raw(a, b, mode):
    dims = {"NN": ((1,), (0,)), "NT": ((1,), (1,)), "TN": ((0,), (0,))}[mode]
    return lax.dot_general(a.astype(BF16), b.astype(BF16), (dims, ((), ())), preferred_element_type=F32)


@functools.partial(jax.custom_vjp, nondiff_argnums=(2,))
def _dot_vjp(a, b, mode):
    return _dot_raw(a, b, mode)


def _dot_fwd(a, b, mode):
    return _dot_raw(a, b, mode), (a, b)


def _dot_bwd(mode, res, dc):
    a, b = res
    if mode == "NN":
        return _dot_vjp(dc, b, "NT"), _dot_vjp(a, dc, "TN")
    if mode == "NT":
        return _dot_vjp(dc, b, "NN"), _dot_vjp(dc, a, "TN")
    return _dot_vjp(b, dc, "NT"), _dot_vjp(a, dc, "NN")


_dot_vjp.defvjp(_dot_fwd, _dot_bwd)


def _split(x):
    xh = x.astype(BF16)
    return xh, (x - xh.astype(F32)).astype(BF16)


def _dot_split(xs, ys):
    (xh, xl), (yh, yl) = xs, ys
    return _dot_raw(xh, yh, "NN") + (_dot_raw(xh, yl, "NN") + _dot_raw(xl, yh, "NN"))


def _tril_inverse(a_list):
    ri = lax.broadcasted_iota(jnp.int32, (CH, CH), 0)
    ci = lax.broadcasted_iota(jnp.int32, (CH, CH), 1)
    eye = (ri == ci).astype(F32)
    ts = None
    for lvl in range(CH.bit_length() - 1):
        same_pair = jnp.right_shift(ri, lvl + 1) == jnp.right_shift(ci, lvl + 1)
        quarter = (jnp.bitwise_and(jnp.right_shift(ri, lvl), 1) == 1) & (jnp.bitwise_and(jnp.right_shift(ci, lvl), 1) == 0)
        offs = [jnp.where(same_pair & quarter, a, 0.0) for a in a_list]
        if ts is None:
            ts = [eye - off for off in offs]
            continue
        tsp = [_split(t) for t in ts]
        mids = [_dot_split(tp, _split(off)) for tp, off in zip(tsp, offs)]
        ts = [t - _dot_split(_split(m), tp) for t, m, tp in zip(ts, mids, tsp)]
    return ts


@jax.custom_vjp
def _stored_solve(a, t, rhs):
    return _dot_raw(t, rhs, "NN")


def _stored_solve_fwd(a, t, rhs):
    sol = _dot_raw(t, rhs, "NN")
    return sol, (t, sol)


def _stored_solve_bwd(res, g):
    t, sol = res
    g_rhs = _dot_vjp(t, g, "TN")
    return -_dot_vjp(g_rhs, sol, "NT"), jnp.zeros_like(t), g_rhs


_stored_solve.defvjp(_stored_solve_fwd, _stored_solve_bwd)


def _lane_pick(row, idx, width):
    sel = lax.broadcasted_iota(jnp.int32, (1, width), 1) == idx
    return jnp.sum(jnp.where(sel, row, 0.0), axis=1, keepdims=True)


def f_prep(cqs, cks, cvs, araws, braws, alogs, dtbs, t_stored, dot):
    ri = lax.broadcasted_iota(jnp.int32, (CH, CH), 0)
    ci = lax.broadcasted_iota(jnp.int32, (CH, CH), 1)
    eye = (ri == ci).astype(F32)
    low = (ri >= ci).astype(F32)
    last = lax.broadcasted_iota(jnp.int32, (CH, 1), 0) == CH - 1
    nh = range(len(cqs))
    qs, ks, vbs, kbs, gcs, decays = [], [], [], [], [], []
    for h in nh:
        q = f_silu(cqs[h])
        qs.append(q * lax.rsqrt(jnp.sum(q * q, axis=-1, keepdims=True) + 1e-6) * (HD ** -0.5))
        k = f_silu(cks[h])
        k = k * lax.rsqrt(jnp.sum(k * k, axis=-1, keepdims=True) + 1e-6)
        ks.append(k)
        beta = jax.nn.sigmoid(braws[h])
        sp_in = araws[h] + dtbs[h]
        softplus = jnp.maximum(sp_in, 0.0) + jnp.log(1.0 + jnp.exp(-jnp.abs(sp_in)))
        g = -jnp.exp(alogs[h]) * softplus
        g_row = jnp.sum(eye * g, axis=0, keepdims=True)
        gc = jnp.sum(low * g_row, axis=1, keepdims=True)
        gc_row = jnp.sum(eye * gc, axis=0, keepdims=True)
        gcs.append(gc)
        decays.append(jnp.exp(jnp.where(ri >= ci, gc - gc_row, -1e30)))
        vbs.append(f_silu(cvs[h]) * beta)
        kbs.append(k * beta)
    kks = [dot(kbs[h], ks[h], "NT") for h in nh]
    a_list = [jnp.where(ri > ci, kks[h] * decays[h], 0.0) for h in nh]
    if t_stored is None:
        ts = _tril_inverse(a_list)
        solve = lambda h, rhs: dot(ts[h], rhs, "NN")
    else:
        ts = t_stored
        solve = lambda h, rhs: _stored_solve(a_list[h], t_stored[h], rhs)
    egcs = [jnp.exp(gc) for gc in gcs]
    us = [solve(h, vbs[h]) for h in nh]
    ws = [solve(h, kbs[h] * egcs[h]) for h in nh]
    qks = [dot(qs[h], ks[h], "NT") * decays[h] for h in nh]
    qgs = [qs[h] * egcs[h] for h in nh]
    gls = [jnp.sum(jnp.where(last, gc, 0.0), axis=0, keepdims=True) for gc in gcs]
    kgs = [ks[h] * jnp.exp(gls[h] - gcs[h]) for h in nh]
    egs = [jnp.exp(gl) * jnp.ones((1, HD), F32) for gl in gls]
    return us, ws, qks, qgs, kgs, egs, ts


def f_scan(ss, us, ws, qgs, kgs, qks, egs, dot):
    nh = range(len(ss))
    ws_s = [dot(ws[h], ss[h], "NN") for h in nh]
    qs_s = [dot(qgs[h], ss[h], "NN") for h in nh]
    vns = [us[h] - ws_s[h] for h in nh]
    os_ = [qs_s[h] + dot(qks[h], vns[h], "NN") for h in nh]
    s2s = [ss[h] * egs[h] + dot(kgs[h], vns[h], "TN") for h in nh]
    return os_, s2s


def row_call(name, fn, rows, pars, out_rows, out_accs, tm):
    T = rows[0][0].shape[0]
    n_r, n_p, n_o = len(rows), len(pars), len(out_rows)
    in_specs = [pl.BlockSpec((tm, w), functools.partial(lambda i, cb: (i, cb), cb=cb)) for (_, w, cb) in rows]
    in_specs += [pl.BlockSpec(p.shape, functools.partial(lambda i, nd: (0,) * nd, nd=p.ndim)) for p in pars]
    out_specs = [pl.BlockSpec((tm, w), lambda i: (i, 0)) for (w, _) in out_rows]
    out_specs += [pl.BlockSpec(s, lambda i: (0, 0)) for s in out_accs]
    out_shape = [jax.ShapeDtypeStruct((T, w), dt) for (w, dt) in out_rows]
    out_shape += [jax.ShapeDtypeStruct(s, F32) for s in out_accs]

    def body(*refs):
        rin, pin = refs[:n_r], refs[n_r:n_r + n_p]
        rout, aout = refs[n_r + n_p:n_r + n_p + n_o], refs[n_r + n_p + n_o:]
        if aout:
            @pl.when(pl.program_id(0) == 0)
            def _():
                for a in aout:
                    a[...] = jnp.zeros(a.shape, F32)
        pv = [p[...] for p in pin]

        def step(r, carry):
            sl = pl.ds(pl.multiple_of(r * SUB, SUB), SUB)
            outs, accs = fn(*[x[sl, :] for x in rin], *pv)
            for o, val in zip(rout, outs):
                o[sl, :] = val.astype(o.dtype)
            for a, val in zip(aout, accs):
                a[...] += val
            return carry

        lax.fori_loop(0, tm // SUB, step, 0)

    return pl.pallas_call(body, name=name, grid=(T // tm,), in_specs=in_specs, out_specs=out_specs,
                          out_shape=out_shape, compiler_params=_cp("arbitrary"))(*[r[0] for r in rows], *pars)


EW_TILE_ELEMS = 256 * 1024


def _ew_rows(R, Cc):
    if R * Cc <= EW_TILE_ELEMS or R % 8:
        return R
    tr = 8
    while tr * 2 * Cc <= EW_TILE_ELEMS and R % (tr * 2) == 0:
        tr *= 2
    return tr


def ew_call(name, fn, ins, n_out):
    shape = ins[0].shape
    lead = shape[:-2]
    R, Cc = shape[-2:]
    tr = _ew_rows(R, Cc)
    grid = lead + (R // tr,)
    nl = len(lead)
    spec = pl.BlockSpec((None,) * nl + (tr, Cc), lambda *idx: idx + (0,))

    def body(*refs):
        outs = fn(*[r[...] for r in refs[:len(ins)]])
        for o, val in zip(refs[len(ins):], outs):
            o[...] = val

    return pl.pallas_call(body, name=name, grid=grid, in_specs=[spec] * len(ins), out_specs=[spec] * n_out,
                          out_shape=[jax.ShapeDtypeStruct(shape, F32)] * n_out,
                          compiler_params=_cp(*(("arbitrary",) * len(grid))))(*ins)


MM_RESIDENT_BYTES = 8 * 1024 * 1024


def mm(name, a, b, mode, out_dtype, a_fn=None, epi=None, epi_ins=(), accs=(), tm=512):
    sub_epi = epi is not None and isinstance(out_dtype, (list, tuple))
    M, K = a.shape
    N = b.shape[1] if mode == "NN" else b.shape[0]
    tn = N if K * N * 2 <= MM_RESIDENT_BYTES else min(N, 1024)
    tm = min(tm if tn <= 1024 else tm // 2, M)
    multi = isinstance(out_dtype, (list, tuple))
    dts = list(out_dtype) if multi else [out_dtype]
    n_e, n_o = len(epi_ins), len(dts)
    in_specs = [pl.BlockSpec((tm, K), lambda j, i: (i, 0)),
                pl.BlockSpec((K, tn), lambda j, i: (0, j)) if mode == "NN" else pl.BlockSpec((tn, K), lambda j, i: (j, 0))]
    row_kinds = []
    for (arr, kind) in epi_ins:
        if kind == "tile" or isinstance(kind, tuple):
            cb = kind[1] if isinstance(kind, tuple) else 0
            in_specs.append(pl.BlockSpec((tm, tn), functools.partial(lambda j, i, cb: (i, cb + j), cb=cb)))
            row_kinds.append(True)
        elif kind == "row":
            in_specs.append(pl.BlockSpec((1, tn), lambda j, i: (0, j)))
            row_kinds.append(False)
        else:
            in_specs.append(pl.BlockSpec(arr.shape, lambda j, i: (0, 0)))
            row_kinds.append(False)

    def body(a_ref, b_ref, *rest):
        e_refs, o_refs, acc_refs = rest[:n_e], rest[n_e:n_e + n_o], rest[n_e + n_o:n_e + n_o + len(accs)]
        av = a_ref[...]
        if a_fn is not None:
            av = a_fn(av)
        res = _dot_raw(av, b_ref[...], mode)
        if epi is None or not sub_epi:
            if epi is not None:
                res = epi(res, *[r[...] for r in e_refs])
            o_refs[0][...] = res.astype(o_refs[0].dtype)
            return
        prod = rest[-1]
        prod[...] = res
        if acc_refs:
            @pl.when((pl.program_id(0) == 0) & (pl.program_id(1) == 0))
            def _():
                for r in acc_refs:
                    r[...] = jnp.zeros(r.shape, F32)
        small = [None if is_rows else r[...] for r, is_rows in zip(e_refs, row_kinds)]

        def step(k, carry):
            sl = pl.ds(pl.multiple_of(k * SUB, SUB), SUB)
            out = epi(prod[sl, :], *[r[sl, :] if is_rows else sm for r, is_rows, sm in zip(e_refs, row_kinds, small)])
            tiles, contribs = out if multi else ((out,), ())
            for r, t in zip(o_refs, tiles):
                r[sl, :] = t.astype(r.dtype)
            for r, t in zip(acc_refs, contribs):
                r[...] += t
            return carry

        lax.fori_loop(0, tm // SUB, step, 0)

    out_specs = [pl.BlockSpec((tm, tn), lambda j, i: (i, j))] * n_o + [pl.BlockSpec(s, lambda j, i: (0, 0)) for s in accs]
    out_shape = [jax.ShapeDtypeStruct((M, N), dt) for dt in dts] + [jax.ShapeDtypeStruct(s, F32) for s in accs]
    res = pl.pallas_call(body, name=name, grid=(N // tn, M // tm), in_specs=in_specs, out_specs=out_specs,
                         out_shape=out_shape, scratch_shapes=[pltpu.VMEM((tm, tn), F32)] if sub_epi else [],
                         compiler_params=_cp("arbitrary", "arbitrary"))(a, b, *[e[0] for e in epi_ins])
    return res if multi else res[0]


def mm_tn(name, a, g, a_fn=None, a_cols=None, tt=512):
    T = a.shape[0]
    ka, acb = (a.shape[1], 0) if a_cols is None else a_cols
    N = g.shape[1]
    tt = min(tt, T)
    tka, tn = min(ka, 1024), min(N, 1024)
    nkb = ka // tka

    def body(a_ref, g_ref, o_ref):
        @pl.when(pl.program_id(2) == 0)
        def _():
            o_ref[...] = jnp.zeros(o_ref.shape, F32)
        av = a_ref[...]
        if a_fn is not None:
            av = a_fn(av)
        o_ref[...] += _dot_raw(av, g_ref[...], "TN")

    return pl.pallas_call(body, name=name, grid=(nkb, N // tn, T // tt),
                          in_specs=[pl.BlockSpec((tt, tka), lambda ia, j, t: (t, acb * nkb + ia)),
                                    pl.BlockSpec((tt, tn), lambda ia, j, t: (t, j))],
                          out_specs=pl.BlockSpec((tka, tn), lambda ia, j, t: (ia, j)),
                          out_shape=jax.ShapeDtypeStruct((ka, N), F32),
                          compiler_params=_cp("arbitrary", "arbitrary", "arbitrary"))(a, g)


SUBLANES = 8


class _RowShifts:
    def __init__(self, src, shifted, nrows, reuse):
        self.src, self.shifted, self.reuse = src, shifted, reuse
        if reuse:
            for ph in range(1, SUBLANES):
                for r0 in range(0, nrows - SUBLANES, SUB):
                    n = min(SUB, nrows - SUBLANES - r0)
                    shifted[ph - 1, r0:r0 + n, :] = src[r0 + ph:r0 + ph + n, :]

    def window(self, off, cols):
        ph = off % SUBLANES
        if not self.reuse or ph == 0:
            return self.src[off:off + SUB, cols]
        return self.shifted[ph - 1, off - ph:off - ph + SUB, cols]


def _shift_scratch(nrows, C, reuse):
    return [pltpu.VMEM((SUBLANES - 1, nrows - SUBLANES, C), F32)] if reuse else []


def dwconv_fwd(name, x, xw, w_pad, bias, S, K, HB, pre, post, tm):
    T = x.shape[0]
    C = w_pad.shape[1]
    nb, per_seq = tm // HB, S // tm
    has_b, has_post = bias is not None, post is not None
    reuse = K > SUBLANES

    def body(*refs):
        x_ref, xp_ref, w_ref = refs[:3]
        pos = 3
        b_ref = refs[pos] if has_b else None
        pos += has_b
        ppars = refs[pos:pos + (len(post[1]) if has_post else 0)]
        pos += len(ppars)
        c_ref = refs[pos]
        s_ref = refs[pos + 1] if has_post else None
        ext = refs[pos + 1 + has_post]
        first = (pl.program_id(0) % per_seq) == 0
        ext[0:HB, :] = jnp.where(first, 0.0, pre(xp_ref[...]))
        for r in range(tm // SUB):
            ext[HB + r * SUB:HB + (r + 1) * SUB, :] = pre(x_ref[r * SUB:(r + 1) * SUB, :])
        rows_of = _RowShifts(ext, refs[-1] if reuse else None, HB + tm, reuse)
        pv = [p[...] for p in ppars]
        assert not has_post or C == D
        for r in range(tm // SUB):
            for c0 in range(0, C, D):
                cols = slice(c0, c0 + D)
                acc = jnp.zeros((SUB, D), F32)
                if has_b:
                    acc = acc + b_ref[:, cols]
                for k in range(K):
                    acc = acc + w_ref[k:k + 1, cols] * rows_of.window(HB + r * SUB - (K - 1) + k, cols)
                c_ref[r * SUB:(r + 1) * SUB, cols] = acc
                if has_post:
                    s_ref[r * SUB:(r + 1) * SUB, :] = post[0](acc, *pv).astype(BF16)

    ins = [x, x, w_pad] + ([bias] if has_b else []) + (list(post[1]) if has_post else [])
    in_specs = [pl.BlockSpec((tm, xw), lambda i: (i, 0)),
                pl.BlockSpec((HB, xw), lambda i: (jnp.maximum(i * nb - 1, 0), 0)),
                pl.BlockSpec(w_pad.shape, lambda i: (0, 0))]
    in_specs += [pl.BlockSpec(p.shape, lambda i: (0, 0)) for p in ins[3:]]
    out_specs = [pl.BlockSpec((tm, C), lambda i: (i, 0))] * (1 + has_post)
    out_shape = [jax.ShapeDtypeStruct((T, C), F32)] + ([jax.ShapeDtypeStruct((T, C), BF16)] if has_post else [])
    return pl.pallas_call(body, name=name, grid=(T // tm,), in_specs=in_specs, out_specs=out_specs, out_shape=out_shape,
                          scratch_shapes=[pltpu.VMEM((HB + tm, C), F32)] + _shift_scratch(HB + tm, C, reuse),
                          compiler_params=_cp("arbitrary"))(*ins)


def dwconv_bwd(name, g, x, xw, w_pad, S, K, HB, pre, pre_bwd, tm):
    T = g.shape[0]
    C = w_pad.shape[1]
    nb, per_seq = tm // HB, S // tm
    nblk = T // HB

    reuse = K > SUBLANES

    def body(g_ref, gn_ref, x_ref, xp_ref, w_ref, dx_ref, dw_ref, dbx_ref, extg, exta, *shift_refs):
        i = pl.program_id(0)
        first = (i % per_seq) == 0
        last = (i % per_seq) == per_seq - 1

        @pl.when(i == 0)
        def _():
            dw_ref[...] = jnp.zeros(dw_ref.shape, F32)
            dbx_ref[...] = jnp.zeros(dbx_ref.shape, F32)

        extg[tm:tm + HB, :] = jnp.where(last, 0.0, gn_ref[...])
        exta[0:HB, :] = jnp.where(first, 0.0, pre(xp_ref[...]))
        for r in range(tm // SUB):
            extg[r * SUB:(r + 1) * SUB, :] = g_ref[r * SUB:(r + 1) * SUB, :]
            exta[HB + r * SUB:HB + (r + 1) * SUB, :] = pre(x_ref[r * SUB:(r + 1) * SUB, :])
        assert pre_bwd is None or C == D
        g_rows = _RowShifts(extg, shift_refs[0] if reuse else None, tm + HB, reuse)
        a_rows = _RowShifts(exta, shift_refs[1] if reuse else None, HB + tm, reuse)
        for r in range(tm // SUB):
            rows = slice(r * SUB, (r + 1) * SUB)
            for c0 in range(0, C, D):
                cols = slice(c0, c0 + D)
                acc = jnp.zeros((SUB, D), F32)
                for k in range(K):
                    acc = acc + w_ref[k:k + 1, cols] * g_rows.window(r * SUB + (K - 1) - k, cols)
                if pre_bwd is None:
                    dx_ref[rows, cols] = acc
                    dbx_ref[:, cols] += jnp.sum(acc, axis=0, keepdims=True)
                else:
                    dx = pre_bwd(x_ref[rows, :], acc)
                    dx_ref[rows, :] = dx
                    dbx_ref[...] += jnp.sum(dx, axis=0, keepdims=True)
        for k in range(K):
            for c0 in range(0, C, D):
                cols = slice(c0, c0 + D)
                p = jnp.zeros((SUB, D), F32)
                for r in range(tm // SUB):
                    p = p + extg[r * SUB:(r + 1) * SUB, cols] * a_rows.window(HB + r * SUB - (K - 1) + k, cols)
                dw_ref[k:k + 1, cols] += jnp.sum(p, axis=0, keepdims=True)

    in_specs = [pl.BlockSpec((tm, C), lambda i: (i, 0)),
                pl.BlockSpec((HB, C), lambda i: (jnp.minimum((i + 1) * nb, nblk - 1), 0)),
                pl.BlockSpec((tm, xw), lambda i: (i, 0)),
                pl.BlockSpec((HB, xw), lambda i: (jnp.maximum(i * nb - 1, 0), 0)),
                pl.BlockSpec(w_pad.shape, lambda i: (0, 0))]
    out_specs = [pl.BlockSpec((tm, xw), lambda i: (i, 0)), pl.BlockSpec((HB, C), lambda i: (0, 0)),
                 pl.BlockSpec((1, xw), lambda i: (0, 0))]
    out_shape = [jax.ShapeDtypeStruct((T, xw), F32), jax.ShapeDtypeStruct((HB, C), F32), jax.ShapeDtypeStruct((1, xw), F32)]
    return pl.pallas_call(body, name=name, grid=(T // tm,), in_specs=in_specs, out_specs=out_specs, out_shape=out_shape,
                          scratch_shapes=[pltpu.VMEM((tm + HB, C), F32), pltpu.VMEM((HB + tm, C), F32)]
                          + _shift_scratch(tm + HB, C, reuse) * 2,
                          compiler_params=_cp("arbitrary"))(g, g, x, x, w_pad)


def _glu_bwd(u, da):
    u1, sg = u[:, :D], jax.nn.sigmoid(u[:, D:])
    return jnp.concatenate([da * sg, da * u1 * sg * (1.0 - sg)], axis=1)


def _head_cols(ref, h, base=0):
    return ref[:, base + h * HD:base + (h + 1) * HD]


def _prep_inputs(c_ref, ab, al, dt):
    hs = range(H)
    return ([_head_cols(c_ref, h) for h in hs], [_head_cols(c_ref, h, D) for h in hs],
            [_head_cols(c_ref, h, 2 * D) for h in hs], [_lane_pick(ab, h, HD) for h in hs],
            [_lane_pick(ab, H + h, HD) for h in hs], [_lane_pick(al, h, HD) for h in hs],
            [_lane_pick(dt, h, HD) for h in hs])


def gdn_prep_fwd(cpre, pab, alog, dtb):
    T = cpre.shape[0]
    nc = T // CH

    def body(c_ref, ab_ref, al_ref, dt_ref, u_ref, w_ref, qg_ref, kg_ref, qk_ref, t_ref, eg_ref):
        us, ws, qks, qgs, kgs, egs, ts = f_prep(*_prep_inputs(c_ref, ab_ref[...], al_ref[...], dt_ref[...]), None, _dot_raw)
        for h in range(H):
            cols = slice(h * HD, (h + 1) * HD)
            u_ref[:, cols] = us[h]
            w_ref[:, cols] = ws[h].astype(BF16)
            qg_ref[:, cols] = qgs[h].astype(BF16)
            kg_ref[:, cols] = kgs[h].astype(BF16)
            qk_ref[0, h] = qks[h].astype(BF16)
            t_ref[0, h] = ts[h].astype(BF16)
            eg_ref[0, h:h + 1, :] = egs[h]

    row = lambda w: pl.BlockSpec((CH, w), lambda n: (n, 0))
    par = pl.BlockSpec((1, HD), lambda n: (0, 0))
    mat = pl.BlockSpec((1, H, CH, CH), lambda n: (n, 0, 0, 0))
    return pl.pallas_call(
        body, name="gdn_prep_fwd", grid=(nc,), in_specs=[row(3 * D), row(HD), par, par],
        out_specs=[row(D), row(D), row(D), row(D), mat, mat, pl.BlockSpec((1, H, HD), lambda n: (n, 0, 0))],
        out_shape=[jax.ShapeDtypeStruct((T, D), F32)] + [jax.ShapeDtypeStruct((T, D), BF16)] * 3
        + [jax.ShapeDtypeStruct((nc, H, CH, CH), BF16)] * 2 + [jax.ShapeDtypeStruct((nc, H, HD), F32)],
        compiler_params=_cp("arbitrary"))(cpre, pab, alog, dtb)


def gdn_prep_bwd(cpre, pab, alog, dtb, tmat, du, dw, dqg, dkg, dqk, deg):
    T = cpre.shape[0]
    nc = T // CH

    def body(c_ref, ab_ref, al_ref, dt_ref, t_ref, du_ref, dw_ref, dqg_ref, dkg_ref, dqk_ref, deg_ref,
             dc_ref, dab_ref, dal_ref, ddt_ref):
        @pl.when(pl.program_id(0) == 0)
        def _():
            dal_ref[...] = jnp.zeros(dal_ref.shape, F32)
            ddt_ref[...] = jnp.zeros(ddt_ref.shape, F32)

        lane = lax.broadcasted_iota(jnp.int32, (1, HD), 1)
        dab = jnp.zeros((CH, HD), F32)
        dal = jnp.zeros((1, HD), F32)
        ddt = jnp.zeros((1, HD), F32)
        hs = range(H)
        t_st = [t_ref[0, h].astype(F32) for h in hs]

        def fwd(*args):
            return tuple(f_prep(*args, t_st, _dot_vjp)[:6])

        _, vjp = jax.vjp(fwd, *_prep_inputs(c_ref, ab_ref[...], al_ref[...], dt_ref[...]))
        dcqs, dcks, dcvs, dars, dbrs, dals, ddts = vjp((
            [_head_cols(du_ref, h) for h in hs], [_head_cols(dw_ref, h) for h in hs], [dqk_ref[0, h] for h in hs],
            [_head_cols(dqg_ref, h) for h in hs], [_head_cols(dkg_ref, h) for h in hs],
            [deg_ref[0, h:h + 1, :] for h in hs]))
        for h in hs:
            dc_ref[:, h * HD:(h + 1) * HD] = dcqs[h]
            dc_ref[:, D + h * HD:D + (h + 1) * HD] = dcks[h]
            dc_ref[:, 2 * D + h * HD:2 * D + (h + 1) * HD] = dcvs[h]
            dab = dab + jnp.where(lane == h, dars[h], 0.0) + jnp.where(lane == H + h, dbrs[h], 0.0)
            dal = dal + jnp.where(lane == h, dals[h], 0.0)
            ddt = ddt + jnp.where(lane == h, ddts[h], 0.0)
        dab_ref[...] = dab
        dal_ref[...] += dal
        ddt_ref[...] += ddt

    row = lambda w: pl.BlockSpec((CH, w), lambda n: (n, 0))
    par = pl.BlockSpec((1, HD), lambda n: (0, 0))
    mat = pl.BlockSpec((1, H, CH, CH), lambda n: (n, 0, 0, 0))
    vec = pl.BlockSpec((1, H, HD), lambda n: (n, 0, 0))
    return pl.pallas_call(
        body, name="gdn_prep_bwd", grid=(nc,),
        in_specs=[row(3 * D), row(HD), par, par, mat, row(D), row(D), row(D), row(D), mat, vec],
        out_specs=[row(3 * D), row(HD), par, par],
        out_shape=[jax.ShapeDtypeStruct((T, 3 * D), F32), jax.ShapeDtypeStruct((T, HD), F32),
                   jax.ShapeDtypeStruct((1, HD), F32), jax.ShapeDtypeStruct((1, HD), F32)],
        compiler_params=_cp("arbitrary"))(cpre, pab, alog, dtb, tmat, du, dw, dqg, dkg, dqk, deg)


def gdn_scan_fwd(u, w, qg, kg, qk, eg, S):
    T = u.shape[0]
    nc, per_seq = T // CH, S // CH

    def body(u_ref, w_ref, qg_ref, kg_ref, qk_ref, eg_ref, o_ref, sall_ref, s_ref):
        @pl.when(pl.program_id(0) % per_seq == 0)
        def _():
            s_ref[...] = jnp.zeros(s_ref.shape, F32)

        hs = range(H)
        ss = [s_ref[h] for h in hs]
        os_, s2s = f_scan(ss, [_head_cols(u_ref, h) for h in hs], [_head_cols(w_ref, h) for h in hs],
                          [_head_cols(qg_ref, h) for h in hs], [_head_cols(kg_ref, h) for h in hs],
                          [qk_ref[0, h] for h in hs], [eg_ref[0, h:h + 1, :] for h in hs], _dot_raw)
        for h in hs:
            sall_ref[0, h] = ss[h]
            o_ref[:, h * HD:(h + 1) * HD] = os_[h]
            s_ref[h] = s2s[h]

    row = pl.BlockSpec((CH, D), lambda n: (n, 0))
    return pl.pallas_call(
        body, name="gdn_scan_fwd", grid=(nc,),
        in_specs=[row, row, row, row, pl.BlockSpec((1, H, CH, CH), lambda n: (n, 0, 0, 0)),
                  pl.BlockSpec((1, H, HD), lambda n: (n, 0, 0))],
        out_specs=[row, pl.BlockSpec((1, H, HD, HD), lambda n: (n, 0, 0, 0))],
        out_shape=[jax.ShapeDtypeStruct((T, D), F32), jax.ShapeDtypeStruct((nc, H, HD, HD), F32)],
        scratch_shapes=[pltpu.VMEM((H, HD, HD), F32)], compiler_params=_cp("arbitrary"))(u, w, qg, kg, qk, eg)


def gdn_scan_bwd(do, u, w, qg, kg, qk, eg, sall, S):
    T = u.shape[0]
    nc, per_seq = T // CH, S // CH

    def body(do_ref, u_ref, w_ref, qg_ref, kg_ref, qk_ref, eg_ref, sall_ref,
             du_ref, dw_ref, dqg_ref, dkg_ref, dqk_ref, deg_ref, ds_ref):
        n = nc - 1 - pl.program_id(0)

        @pl.when(n % per_seq == per_seq - 1)
        def _():
            ds_ref[...] = jnp.zeros(ds_ref.shape, F32)

        hs = range(H)

        def fwd(*args):
            return f_scan(*args, _dot_vjp)

        _, vjp = jax.vjp(fwd, [sall_ref[0, h] for h in hs], [_head_cols(u_ref, h) for h in hs],
                         [_head_cols(w_ref, h).astype(F32) for h in hs], [_head_cols(qg_ref, h).astype(F32) for h in hs],
                         [_head_cols(kg_ref, h).astype(F32) for h in hs], [qk_ref[0, h].astype(F32) for h in hs],
                         [eg_ref[0, h:h + 1, :] for h in hs])
        dss, dus, dws, dqgs, dkgs, dqks, degs = vjp(([_head_cols(do_ref, h) for h in hs], [ds_ref[h] for h in hs]))
        for h in hs:
            cols = slice(h * HD, (h + 1) * HD)
            du_ref[:, cols] = dus[h]
            dw_ref[:, cols] = dws[h]
            dqg_ref[:, cols] = dqgs[h]
            dkg_ref[:, cols] = dkgs[h]
            dqk_ref[0, h] = dqks[h]
            deg_ref[0, h:h + 1, :] = degs[h]
            ds_ref[h] = dss[h]

    rev = lambda n: (nc - 1 - n, 0)
    row = pl.BlockSpec((CH, D), rev)
    mat = pl.BlockSpec((1, H, CH, CH), lambda n: (nc - 1 - n, 0, 0, 0))
    vec = pl.BlockSpec((1, H, HD), lambda n: (nc - 1 - n, 0, 0))
    return pl.pallas_call(
        body, name="gdn_scan_bwd", grid=(nc,),
        in_specs=[row, row, row, row, row, mat, vec, pl.BlockSpec((1, H, HD, HD), lambda n: (nc - 1 - n, 0, 0, 0))],
        out_specs=[row, row, row, row, mat, vec],
        out_shape=[jax.ShapeDtypeStruct((T, D), F32)] * 4
        + [jax.ShapeDtypeStruct((nc, H, CH, CH), F32), jax.ShapeDtypeStruct((nc, H, HD), F32)],
        scratch_shapes=[pltpu.VMEM((H, HD, HD), F32)], compiler_params=_cp("arbitrary"))(do, u, w, qg, kg, qk, eg, sall)


def xor_exchange(name, ins, inplace, out_shapes, plan, n_remote, n_local=0):
    n_in, n_ip, n_out = len(ins), len(inplace), len(out_shapes)

    def body(*refs):
        in_refs = refs[:n_in]
        ip_refs = refs[n_in + n_ip:n_in + 2 * n_ip]
        out_refs = refs[n_in + 2 * n_ip:n_in + 2 * n_ip + n_out]
        send_sems, recv_sems, loc_sems = refs[n_in + 2 * n_ip + n_out:]
        x, y, c = lax.axis_index("x"), lax.axis_index("y"), lax.axis_index("c")
        remote, local = plan(in_refs, ip_refs, out_refs, (x, y, c))
        assert len(remote) == n_remote and len(local) == n_local
        copies = []
        for k, ((dx, dy, dc), src, dst) in enumerate(remote):
            peer = (1 - x if dx else x, 1 - y if dy else y, 1 - c if dc else c)
            copies.append(pltpu.make_async_remote_copy(src_ref=src, dst_ref=dst, send_sem=send_sems.at[k],
                                                       recv_sem=recv_sems.at[k], device_id=peer, device_id_type=MESH))
        for cp in copies:
            cp.start()
        locs = [pltpu.make_async_copy(src, dst, loc_sems.at[k]) for k, (src, dst) in enumerate(local)]
        for cp in locs:
            cp.start()
        for cp in copies:
            cp.wait()
        for cp in locs:
            cp.wait()

    anyspec = pl.BlockSpec(memory_space=pl.ANY)
    res = pl.pallas_call(
        body, name=name, in_specs=[anyspec] * (n_in + n_ip), out_specs=[anyspec] * (n_ip + n_out),
        out_shape=[jax.ShapeDtypeStruct(a.shape, a.dtype) for a in inplace] + list(out_shapes),
        input_output_aliases={n_in + i: i for i in range(n_ip)},
        scratch_shapes=[pltpu.SemaphoreType.DMA((n_remote,)), pltpu.SemaphoreType.DMA((n_remote,)),
                        pltpu.SemaphoreType.DMA((max(n_local, 1),))],
        )(*ins, *inplace)
    return list(res[:n_ip]), list(res[n_ip:])


HBM_SPEC = pl.BlockSpec(memory_space=pltpu.HBM)
SEM_SPEC = pl.BlockSpec(memory_space=pltpu.SEMAPHORE)


def _flip_copies(plan, refs, send_sems, recv_sems):
    x, y, c = lax.axis_index("x"), lax.axis_index("y"), lax.axis_index("c")
    copies = []
    for k, ((dx, dy, dc), src, dst) in enumerate(plan(refs, (x, y, c))):
        peer = (1 - x if dx else x, 1 - y if dy else y, 1 - c if dc else c)
        copies.append(pltpu.make_async_remote_copy(src_ref=src, dst_ref=dst, send_sem=send_sems.at[k],
                                                   recv_sem=recv_sems.at[k], device_id=peer, device_id_type=MESH))
    return copies


def xor_start(name, arrays, plan, n_remote, after):
    n = len(arrays)

    def body(*refs):
        for cp in _flip_copies(plan, refs[:n], refs[n + 1], refs[n + 2]):
            cp.start()
        refs[-1][...] = jnp.zeros(refs[-1].shape, F32)

    res = pl.pallas_call(
        body, name=name, in_specs=[HBM_SPEC] * n + [pl.BlockSpec(memory_space=pl.ANY)],
        out_shape=(pltpu.SemaphoreType.DMA((n_remote,)), pltpu.SemaphoreType.DMA((n_remote,)),
                   *[pltpu.HBM(a.shape, a.dtype) for a in arrays], jax.ShapeDtypeStruct((8, 128), F32)),
        out_specs=(SEM_SPEC, SEM_SPEC, *([HBM_SPEC] * n), pl.BlockSpec(memory_space=pltpu.VMEM)),
        input_output_aliases={i: 2 + i for i in range(n)},
        compiler_params=pltpu.CompilerParams(has_side_effects=pltpu.SideEffectType.DATAFLOW_SIDE_EFFECTING),
    )(*[pltpu.with_memory_space_constraint(a, pltpu.HBM) for a in arrays], after)
    return res[0], res[1], list(res[2:2 + n]), res[2 + n]


def xor_wait(name, send_sems, recv_sems, arrays, plan, after):
    n = len(arrays)

    def body(*refs):
        for cp in _flip_copies(plan, refs[:n], refs[n], refs[n + 1]):
            cp.wait_send()
            cp.wait_recv()

    return list(pl.pallas_call(
        body, name=name, in_specs=[HBM_SPEC] * n + [SEM_SPEC, SEM_SPEC, pl.BlockSpec(memory_space=pl.ANY)],
        out_shape=[pltpu.HBM(a.shape, a.dtype) for a in arrays], out_specs=[HBM_SPEC] * n,
        input_output_aliases={i: i for i in range(n)},
        compiler_params=pltpu.CompilerParams(has_side_effects=pltpu.SideEffectType.DATAFLOW_SIDE_EFFECTING),
    )(*arrays, send_sems, recv_sems, after))


class WSpec:
    def __init__(self, name, full, sa, ha, group, layer=None, lead=False):
        self.name, self.full, self.sa, self.ha, self.group, self.layer, self.lead = name, full, sa, ha, group, layer, lead
        self.ws = 1 if lead else full[sa] // 4
        self.wh = full[ha] // 2

    def shard_shape(self):
        if self.lead:
            return tuple(n for a, n in enumerate(self.full) if a != self.sa)
        return tuple(self.ws if a == self.sa else n for a, n in enumerate(self.full))

    def half_full_shape(self):
        return tuple(self.wh if a == self.ha else n for a, n in enumerate(self.full))

    def shard_half_shape(self):
        s = list(self.half_full_shape())
        if self.lead:
            del s[self.sa]
        else:
            s[self.sa] = self.ws
        return tuple(s)

    def full_view(self, ref, q=None, h=None):
        idx = []
        for a in range(len(self.full)):
            if a == self.sa and q is not None:
                idx.append(q if self.lead else pl.ds(pl.multiple_of(q * self.ws, self.ws), self.ws))
            elif a == self.ha and h is not None:
                idx.append(pl.ds(pl.multiple_of(h * self.wh, self.wh), self.wh))
            else:
                idx.append(slice(None))
        return ref.at[tuple(idx)]

    def shard_view(self, ref, h):
        idx = [] if self.layer is None else [self.layer]
        for a in range(len(self.full)):
            if self.lead and a == self.sa:
                continue
            idx.append(pl.ds(pl.multiple_of(h * self.wh, self.wh), self.wh) if a == self.ha else slice(None))
        return ref.at[tuple(idx)]

    def rows_cols(self, shard, half):
        rows, cols = self.full[-2:]
        if shard and not self.lead:
            rows, cols = (rows // 4, cols) if self.sa == 0 else (rows, cols // 4)
        if half:
            rows, cols = (rows // 2, cols) if self.ha == len(self.full) - 2 else (rows, cols // 2)
        return rows, cols

    def spec(self, tr, cw, nr, shard=False, half=False, has_lead=False, stacked=False):
        two_d = len(self.full) == 2
        shard_on_cols = two_d and self.sa == 1
        half_on_cols = two_d and self.ha == 1
        layer = self.layer

        def index(*args):
            pref = args[-1]
            i = args[-2]
            r, cblk, pre = i, 0, ()
            if shard:
                if self.lead:
                    pre = (pref[0],)
                elif shard_on_cols:
                    cblk = pref[0]
                else:
                    r = pref[0] * nr + i
            elif has_lead:
                pre = (args[0],)
            if half:
                if half_on_cols:
                    cblk = pref[1]
                else:
                    r = pref[1] * nr + i
            if stacked:
                pre = (layer,) + pre
            return pre + (r, cblk)

        n_pre = int(stacked) + int(self.lead and (shard or has_lead))
        return pl.BlockSpec((None,) * n_pre + (tr, cw), index)


WSPECS = [
    WSpec("cv_w_pw1", (D, 2 * D), 1, 0, 0),
    WSpec("cv_w_pw2", (D, D), 0, 1, 1),
    WSpec("gdn_w_in", (4, D, (4 * D + 2 * H) // 4), 0, 1, 2, lead=True),
    WSpec("gdn_w_out", (D, D), 0, 1, 3),
    WSpec("mlp_w1_0", (D, DFF), 1, 0, 4, layer=0),
    WSpec("mlp_w1_1", (D, DFF), 1, 0, 4, layer=1),
    WSpec("mlp_w2_0", (DFF, D), 0, 1, 5, layer=0),
    WSpec("mlp_w2_1", (DFF, D), 0, 1, 5, layer=1),
]
FLIPS = [(1, 0, 0), (0, 1, 0), (1, 1, 0)]
SIB = (0, 0, 1)


def _chip(x, y):
    return 2 * x + y


def _prefetch_call(name, body, grid, in_specs, out_specs, out_shape, pref, args, aliases=None):
    return pl.pallas_call(
        body, name=name, out_shape=out_shape, input_output_aliases=aliases or {},
        grid_spec=pltpu.PrefetchScalarGridSpec(num_scalar_prefetch=1, grid=grid, in_specs=in_specs, out_specs=out_specs),
        compiler_params=_cp(*(("arbitrary",) * len(grid))))(pref, *args)


def place_shard(ws, shard, pref):
    rows, cols = ws.rows_cols(True, False)
    tr = _ew_rows(rows, cols)
    nr = rows // tr
    stacked = ws.layer is not None
    layer = ws.layer

    def body(_, s_ref, o_ref):
        o_ref[...] = s_ref[...].astype(BF16)

    in_spec = pl.BlockSpec(((None,) if stacked else ()) + (tr, cols),
                           (lambda i, p: (layer, i, 0)) if stacked else (lambda i, p: (i, 0)))
    return _prefetch_call("place_" + ws.name, body, (nr,), [in_spec], ws.spec(tr, cols, nr, shard=True),
                          jax.ShapeDtypeStruct(ws.full, BF16), pref, [shard])


LAYER0 = [0, 1, 4, 6]
LAYER1 = [2, 3, 5, 7]


def _plan_gather_chips(sel):
    def plan(refs, pos):
        x, y, c = pos
        remote = []
        for j, i in enumerate(sel):
            mine = WSPECS[i].full_view(refs[j], _chip(x, y), c)
            remote += [(f, mine, mine) for f in FLIPS]
        return remote
    return plan


def gather_cores(tag, sel, nat):
    def plan(in_refs, ip_refs, out_refs, pos):
        x, y, c = pos
        remote = []
        for j, i in enumerate(sel):
            for (dx, dy, _) in FLIPS:
                got = WSPECS[i].full_view(ip_refs[j], _chip(1 - x if dx else x, 1 - y if dy else y), c)
                remote.append((SIB, got, got))
        return remote, []

    return xor_exchange("gather_cores" + tag, [], nat, [], plan, 3 * len(sel))[0]


def gather_layer0(placed, wdw_shard, wcv_shard):
    plan_w = _plan_gather_chips(LAYER0)

    def plan(in_refs, ip_refs, out_refs, pos):
        x, y, c = pos
        remote, local = plan_w(ip_refs, pos), []
        for j, width in enumerate((D // 4, 3 * D // 4)):
            dst = out_refs[j].at[:, pl.ds(pl.multiple_of(_chip(x, y) * width, 128), width)]
            local.append((in_refs[j], dst))
            remote += [(f, in_refs[j], dst) for f in FLIPS]
        return remote, local

    taps = [jax.ShapeDtypeStruct((KCV, D), F32), jax.ShapeDtypeStruct((KSC, 3 * D), F32)]
    nat, (wdw, wcv) = xor_exchange("gather_chips0", [wdw_shard, wcv_shard], [placed[i] for i in LAYER0], taps, plan,
                                   3 * (len(LAYER0) + 2), 2)
    return gather_cores("0", LAYER0, nat), wdw, wcv


def half_add(ws, g, rsib, pref):
    rows, cols = ws.rows_cols(False, True)
    tr = _ew_rows(rows, cols)
    nr = rows // tr

    def body(_, a_ref, b_ref, o_ref):
        o_ref[...] = (a_ref[...] + b_ref[...]).astype(BF16)

    whole = ws.spec(tr, cols, nr, has_lead=ws.lead)
    return _prefetch_call("reduce_add_" + ws.name, body, (4, nr) if ws.lead else (nr,),
                          [ws.spec(tr, cols, nr, half=True, has_lead=ws.lead), whole], whole,
                          jax.ShapeDtypeStruct(ws.half_full_shape(), BF16), pref, [g, rsib])


def shard_sum(ws, s, parts, buf, pref):
    rows, cols = ws.rows_cols(True, True)
    tr = _ew_rows(rows, cols)
    nr = rows // tr
    stacked = ws.layer is not None

    def body(_, s_ref, p_ref, *rest):
        rest[-1][...] = ((s_ref[...].astype(F32) + p_ref[0].astype(F32)) + p_ref[1].astype(F32)) + p_ref[2].astype(F32)

    in_specs = [ws.spec(tr, cols, nr, shard=True, has_lead=ws.lead), pl.BlockSpec((3, tr, cols), lambda i, p: (0, i, 0))]
    args, aliases = [s, parts], {}
    if buf is not None:
        in_specs.append(pl.BlockSpec(memory_space=pl.ANY))
        args.append(buf)
        aliases = {3: 0}
    shape = ((2,) if stacked else ()) + ws.shard_shape()
    return _prefetch_call("reduce_sum_" + ws.name, body, (nr,), in_specs, ws.spec(tr, cols, nr, half=True, stacked=stacked),
                          jax.ShapeDtypeStruct(shape, F32), pref, args, aliases)


def chip_sums(tag, sel, grads, pref):
    def plan(in_refs, ip_refs, out_refs, pos):
        c = pos[2]
        return [(SIB, WSPECS[i].full_view(in_refs[j], None, 1 - c), out_refs[j]) for j, i in enumerate(sel)], []

    halves = [jax.ShapeDtypeStruct(WSPECS[i].half_full_shape(), F32) for i in sel]
    _, rsib = xor_exchange("reduce_cores" + tag, grads, [], halves, plan, len(sel))
    return [half_add(WSPECS[i], grads[j], rsib[j], pref) for j, i in enumerate(sel)]


def _plan_reduce_chips(sel):
    n = len(sel)

    def plan(refs, pos):
        x, y, c = pos
        remote = []
        for j, i in enumerate(sel):
            for s, (dx, dy, _) in enumerate(FLIPS):
                qq = _chip(1 - x if dx else x, 1 - y if dy else y)
                remote.append(((dx, dy, 0), WSPECS[i].full_view(refs[j], qq), refs[n + j].at[s]))
        return remote
    return plan


def _parts_shapes(sel):
    return [jax.ShapeDtypeStruct((3,) + WSPECS[i].shard_half_shape(), BF16) for i in sel]


def finish_reduce(sums, parts, pref):
    n = len(WSPECS)
    bufs = {}
    for i, ws in enumerate(WSPECS):
        bufs[ws.group] = shard_sum(ws, sums[i], parts[i], bufs.get(ws.group), pref)

    def plan3(in_refs, ip_refs, out_refs, pos):
        c = pos[2]
        remote = []
        for ws in WSPECS:
            mine = ws.shard_view(ip_refs[ws.group], c)
            remote.append((SIB, mine, mine))
        return remote, []

    return xor_exchange("reduce_swap", [], [bufs[g] for g in sorted(bufs)], [], plan3, n)[0]


def gather_small(buf):
    flips = [(dx, dy, dc) for dx in (0, 1) for dy in (0, 1) for dc in (0, 1)][1:]

    def plan(in_refs, ip_refs, out_refs, pos):
        x, y, c = pos
        me = 4 * x + 2 * y + c
        dst = out_refs[0].at[me]
        return [(f, in_refs[0], dst) for f in flips], [(in_refs[0], dst)]

    return xor_exchange("gather_small", [buf], [], [jax.ShapeDtypeStruct((8,) + buf.shape, F32)], plan, 7, 1)[1][0]


def _pad_rows(a, rows):
    return jnp.pad(a, ((0, rows - a.shape[0]), (0, 0)))


def _row1(v):
    v = v.reshape((1, -1))
    return jnp.pad(v, ((0, 0), (0, D - v.shape[1])))


def _rms_fwd(name, h, g, tm):
    return row_call(name, lambda hh, gg: ((f_rms(hh, gg),), ()), [(h, D, 0)], [g], [(D, BF16)], [], tm)[0]


def _res_rms(h, g):
    return (h, f_rms(h, g)), ()


def _rms_bwd_epi(dhn, h, dres, g):
    _, vjp = jax.vjp(f_rms, h, g)
    dh, dg = vjp(dhn)
    dh = dh + dres
    return (dh,), (dg, jnp.sum(dh, axis=0, keepdims=True))


def _mlp_bwd(tag, dh, h, g, w1, w2, hn, z1, token=None):
    dz1 = mm("mlp_down_dx" + tag, dh, w2, "NT", BF16,
             epi=lambda acc, z, *_: acc * (2.0 * jnp.maximum(z.astype(F32), 0.0)),
             epi_ins=[(z1, "tile")] + ([] if token is None else [(token, "whole")]))
    dw2 = mm_tn("mlp_down_dw" + tag, z1, dh, a_fn=f_relu2)
    dh_in, dg, colsum = mm("mlp_up_dx" + tag, dz1, w1, "NT", [F32], epi=_rms_bwd_epi,
                           epi_ins=[(h, "tile"), (dh, "tile"), (g, "row")], accs=[(1, D), (1, D)])
    dw1 = mm_tn("mlp_up_dw" + tag, hn, dz1)
    return dh_in, dg, colsum, dw1, dw2


def kernel(x, norm_mix_g, norm_ffn_g, final_norm_g, cv_w_pw1, cv_b_pw1, cv_w_dw, cv_b_dw, cv_ln_g, cv_ln_b, cv_w_pw2, cv_b_pw2, gdn_w_in, gdn_conv_w, gdn_a_log, gdn_dt_bias, gdn_norm_g, gdn_w_out, mlp_w1, mlp_w2, loss_target, m_norm_mix_g, m_norm_ffn_g, m_final_norm_g, m_cv_w_pw1, m_cv_b_pw1, m_cv_w_dw, m_cv_b_dw, m_cv_ln_g, m_cv_ln_b, m_cv_w_pw2, m_cv_b_pw2, m_gdn_w_in, m_gdn_conv_w, m_gdn_a_log, m_gdn_dt_bias, m_gdn_norm_g, m_gdn_w_out, m_mlp_w1, m_mlp_w2, v_norm_mix_g, v_norm_ffn_g, v_final_norm_g, v_cv_w_pw1, v_cv_b_pw1, v_cv_w_dw, v_cv_b_dw, v_cv_ln_g, v_cv_ln_b, v_cv_w_pw2, v_cv_b_pw2, v_gdn_w_in, v_gdn_conv_w, v_gdn_a_log, v_gdn_dt_bias, v_gdn_norm_g, v_gdn_w_out, v_mlp_w1, v_mlp_w2):
    env = dict(locals())
    bl, S, _ = x.shape
    T = bl * S
    tm = min(256, S)
    xf = x.reshape((T, D))
    tgt = loss_target.reshape((T, D))

    chip = 2 * lax.axis_index("x") + lax.axis_index("y")
    pref = jnp.stack([chip, lax.axis_index("c")]).astype(jnp.int32)
    big = [cv_w_pw1[0], cv_w_pw2[0], gdn_w_in[0], gdn_w_out[0], mlp_w1, mlp_w2]
    placed = [place_shard(ws, big[ws.group], pref) for ws in WSPECS]
    (w_pw1, w_pw2, w1_0, w2_0), wdw, wcv = gather_layer0(placed, cv_w_dw[0], gdn_conv_w[0])
    plan_g1 = _plan_gather_chips(LAYER1)
    g1_send, g1_recv, g1_arrays, g1_token = xor_start("gather_chips1_start", [placed[i] for i in LAYER1], plan_g1,
                                                      3 * len(LAYER1), wcv)
    wdw_p, wcv_p = _pad_rows(wdw, HB_CV), _pad_rows(wcv, HB_SC)
    alog_p = jnp.pad(gdn_a_log, ((0, 0), (0, HD - H)))
    dtb_p = jnp.pad(gdn_dt_bias, ((0, 0), (0, HD - H)))
    g_mix0, g_mix1 = norm_mix_g[0:1] + g1_token[0, 0], norm_mix_g[1:2]
    g_ffn0, g_ffn1 = norm_ffn_g[0:1], norm_ffn_g[1:2]
    g_fin = final_norm_g.reshape((1, D))

    hn0 = _rms_fwd("rms_mix0", xf, g_mix0, tm)
    u = mm("cv_pw1", hn0, w_pw1, "NN", F32, epi=lambda acc, b: acc + b, epi_ins=[(cv_b_pw1, "row")])
    dwc, s_act = dwconv_fwd("cv_dwconv", u, 2 * D, wdw_p, cv_b_dw, S, KCV, HB_CV, f_glu, (f_ln_silu, (cv_ln_g, cv_ln_b)), tm)
    h1, hnf0 = mm("cv_pw2", s_act, w_pw2, "NN", [F32, BF16], epi=lambda acc, b, r, g: _res_rms(acc + b + r, g),
                  epi_ins=[(cv_b_pw2, "row"), (xf, "tile"), (g_ffn0, "row")])
    z1_0 = mm("mlp_up0", hnf0, w1_0, "NN", BF16)
    h2, hn2 = mm("mlp_down0", z1_0, w2_0, "NN", [F32, BF16], a_fn=f_relu2, epi=lambda acc, r, g: _res_rms(acc + r, g),
                 epi_ins=[(h1, "tile"), (g_mix1, "row")])

    nat1 = xor_wait("gather_chips1_wait", g1_send, g1_recv, g1_arrays, plan_g1, h2)
    w_in_sm, w_out, w1_1, w2_1 = gather_cores("1", LAYER1, nat1)
    w_in = jnp.transpose(w_in_sm, (1, 0, 2)).reshape((D, 4 * D + 2 * H))
    w_qkv, w_z = w_in[:, :3 * D], w_in[:, 3 * D:4 * D]
    w_qkvz = w_in[:, :4 * D]
    w_ab = jnp.pad(w_in[:, 4 * D:], ((0, 0), (0, HD - 2 * H)))
    pqkvz = mm("gdn_in", hn2, w_qkvz, "NN", F32)
    pab = mm("gdn_in_ab", hn2, w_ab, "NN", F32)
    cpre = dwconv_fwd("gdn_conv", pqkvz, 3 * D, wcv_p, None, S, KSC, HB_SC, lambda v: v, None, tm)[0]
    gu, gw, gqg, gkg, gqk, gt, geg = gdn_prep_fwd(cpre, pab, alog_p, dtb_p)
    o, sall = gdn_scan_fwd(gu, gw, gqg, gkg, gqk, geg, S)
    on = row_call("gdn_post", lambda oo, zz, ng: ((f_post(oo, zz, ng),), ()), [(o, D, 0), (pqkvz, D, 3)],
                  [gdn_norm_g], [(D, BF16)], [], tm)[0]
    h3, hnf1 = mm("gdn_out", on, w_out, "NN", [F32, BF16], epi=lambda acc, r, g: _res_rms(acc + r, g),
                  epi_ins=[(h2, "tile"), (g_ffn1, "row")])
    z1_1 = mm("mlp_up1", hnf1, w1_1, "NN", BF16)

    def head(acc, res, tt, gg):
        def loss_of(h_, g_):
            return 0.5 * jnp.sum(jnp.mean(jnp.square(f_rms(h_, g_) - tt), axis=-1))
        lv, (dh_, dg_) = jax.value_and_grad(loss_of, (0, 1))(acc + res, gg)
        return (dh_,), (dg_, jnp.full((1, D), lv, F32))

    dh4, dg_fin, loss_row = mm("mlp_down1", z1_1, w2_1, "NN", [F32], a_fn=f_relu2, epi=head,
                               epi_ins=[(h3, "tile"), (tgt, "tile"), (g_fin, "row")], accs=[(1, D), (1, D)])

    dh3, dg_ffn1, _, dw1_1, dw2_1 = _mlp_bwd("1", dh4, h3, g_ffn1, w1_1, w2_1, hnf1, z1_1)
    dw_out = mm_tn("gdn_out_dw", on, dh3)

    def post_bwd(don, oo, zz, ng):
        _, vjp = jax.vjp(f_post, oo, zz, ng)
        do_, dz_, dng_ = vjp(don)
        return (do_, dz_), (dng_,)

    do, dz, dng = mm("gdn_out_dx", dh3, w_out, "NT", [F32, F32], epi=post_bwd,
                     epi_ins=[(o, "tile"), (pqkvz, ("cols", 3)), (gdn_norm_g, "whole")], accs=[(1, HD)])
    du, dw, dqg, dkg, dqk, deg = gdn_scan_bwd(do, gu, gw, gqg, gkg, gqk, geg, sall, S)
    dcpre, dpab, dalog, ddtb = gdn_prep_bwd(cpre, pab, alog_p, dtb_p, gt, du, dw, dqg, dkg, dqk, deg)
    dqkv, dwcv, _ = dwconv_bwd("gdn_conv_bwd", dcpre, pqkvz, 3 * D, wcv_p, S, KSC, HB_SC, lambda v: v, None, tm)
    dhn2 = mm("gdn_in_dx_ab", dpab, w_ab, "NT", F32)
    dhn2 = mm("gdn_in_dx_z", dz, w_z, "NT", F32, epi=lambda acc, r: acc + r, epi_ins=[(dhn2, "tile")])
    dh2, dg_mix1, _ = mm("gdn_in_dx_qkv", dqkv, w_qkv, "NT", [F32],
                         epi=lambda acc, prev, hh, rr, gg: _rms_bwd_epi(acc + prev, hh, rr, gg),
                         epi_ins=[(dhn2, "tile"), (h2, "tile"), (dh3, "tile"), (g_mix1, "row")], accs=[(1, D), (1, D)])
    dw_in = jnp.concatenate([mm_tn("gdn_in_dw_qkv", hn2, dqkv), mm_tn("gdn_in_dw_z", hn2, dz),
                             mm_tn("gdn_in_dw_ab", hn2, dpab)[:, :2 * H]], axis=1)

    dw_in_sm = jnp.transpose(dw_in.reshape((D, 4, D + 4)), (1, 0, 2))
    sums1 = chip_sums("1", LAYER1, [dw_in_sm, dw_out, dw1_1, dw2_1], pref)
    plan_r1 = _plan_reduce_chips(LAYER1)
    r1_send, r1_recv, r1_arrays, r1_token = xor_start(
        "reduce_chips1_start", sums1 + [lax.empty(s.shape, s.dtype) for s in _parts_shapes(LAYER1)], plan_r1,
        3 * len(LAYER1), pref)

    dh1, dg_ffn0, db_pw2, dw1_0, dw2_0 = _mlp_bwd("0", dh2, h1, g_ffn0, w1_0, w2_0, hnf0, z1_0, r1_token)
    dw_pw2 = mm_tn("cv_pw2_dw", s_act, dh1)

    def ln_bwd(ds, xx, gg, bb):
        _, vjp = jax.vjp(f_ln_silu, xx, gg, bb)
        dx_, dg_, db_ = vjp(ds)
        return (dx_,), (dg_, db_, jnp.sum(dx_, axis=0, keepdims=True))

    ddw, dln_g, dln_b, db_dw = mm("cv_pw2_dx", dh1, w_pw2, "NT", [F32], epi=ln_bwd,
                                  epi_ins=[(dwc, "tile"), (cv_ln_g, "row"), (cv_ln_b, "row")], accs=[(1, D)] * 3)
    du_cv, dwdw, db_pw1 = dwconv_bwd("cv_dwconv_bwd", ddw, u, 2 * D, wdw_p, S, KCV, HB_CV, f_glu, _glu_bwd, tm)
    dw_pw1 = mm_tn("cv_pw1_dw", hn0, du_cv)
    grad_x, dg_mix0, _ = mm("cv_pw1_dx", du_cv, w_pw1, "NT", [F32], epi=_rms_bwd_epi,
                            epi_ins=[(xf, "tile"), (dh1, "tile"), (g_mix0, "row")], accs=[(1, D), (1, D)])

    sums0 = chip_sums("0", LAYER0, [dw_pw1, dw_pw2, dw1_0, dw2_0], pref)
    plan_r0 = _plan_reduce_chips(LAYER0)
    _, parts0 = xor_exchange("reduce_chips0", sums0, [], _parts_shapes(LAYER0),
                             lambda ins_, ip_, outs_, pos: (plan_r0(list(ins_) + list(outs_), pos), []), 3 * len(LAYER0))
    r1_arrays = xor_wait("reduce_chips1_wait", r1_send, r1_recv, r1_arrays, plan_r1, grad_x)
    sums, parts = [None] * len(WSPECS), [None] * len(WSPECS)
    for j, i in enumerate(LAYER0):
        sums[i], parts[i] = sums0[j], parts0[j]
    for j, i in enumerate(LAYER1):
        sums[i], parts[i] = r1_arrays[j], r1_arrays[len(LAYER1) + j]
    g_pw1, g_pw2, g_in, g_out, g_w1, g_w2 = finish_reduce(sums, parts, pref)

    small = jnp.concatenate([
        dg_mix0, dg_mix1, dg_ffn0, dg_ffn1, dg_fin, db_pw1.reshape((2, D)), db_dw, dln_g, dln_b, db_pw2,
        _row1(dalog[:, :H]), _row1(ddtb[:, :H]), _row1(dng), loss_row, jnp.zeros((1, D), F32),
        dwdw, dwcv[:KSC].reshape((3 * KSC, D)), jnp.zeros((NSMALL - 48 - 3 * KSC, D), F32)], axis=0)
    small_all = gather_small(small)

    def pack(a, b, c_, d, e, f, g_, h_, i_, j_, k_):
        return jnp.concatenate([a, b, c_.reshape((1, D)), d.reshape((2, D)), e, f, g_, h_, _row1(i_), _row1(j_), _row1(k_),
                                jnp.zeros((2, D), F32)], axis=0)

    order = lambda p: (p + "norm_mix_g", p + "norm_ffn_g", p + "final_norm_g", p + "cv_b_pw1", p + "cv_b_dw", p + "cv_ln_g",
                       p + "cv_ln_b", p + "cv_b_pw2", p + "gdn_a_log", p + "gdn_dt_bias", p + "gdn_norm_g")
    w16, m16, v16 = (pack(*[env[nm] for nm in order(p)]) for p in ("", "m_", "v_"))

    def small_step(ga, ww, mm_, vv):
        gsum = ga[0]
        for dev in range(1, 8):
            gsum = gsum + ga[dev]
        delta, m2, v2 = f_adamw(ww, gsum[:16], mm_, vv)
        return gsum, delta, m2, v2

    def small_body(ga_ref, w_ref, m_ref, v_ref, g_out, d_out, m_out, v_out):
        gsum, delta, m2, v2 = small_step(ga_ref[...], w_ref[...], m_ref[...], v_ref[...])
        g_out[...] = gsum
        d_out[...] = delta
        m_out[...] = m2
        v_out[...] = v2

    vm = pl.BlockSpec(memory_space=pltpu.VMEM)
    sg, sd, sm, sv = pl.pallas_call(
        small_body, name="adamw_small", in_specs=[vm] * 4, out_specs=[vm] * 4,
        out_shape=[jax.ShapeDtypeStruct((NSMALL, D), F32)] + [jax.ShapeDtypeStruct((16, D), F32)] * 3)(small_all, w16, m16, v16)

    def unpack(b):
        return (b[0:2], b[2:4], b[4], b[5:7].reshape((1, 2 * D)), b[7:8], b[8:9], b[9:10], b[10:11],
                b[11:12, :H], b[12:13, :H], b[13:14, :HD])

    loss = sg[14, 0]
    g_dw = lax.dynamic_slice(sg[16:16 + KCV], (0, chip * (D // 4)), (KCV, D // 4))
    g_cv = lax.dynamic_slice(sg[48:48 + 3 * KSC].reshape((KSC, 3 * D)), (0, chip * (3 * D // 4)), (KSC, 3 * D // 4))

    def adamw(name, w, g, m, v):
        lead = w.shape[:-2]
        if len(lead) == 1 and lead[0] == 1:
            d, m2, v2 = ew_call(name, f_adamw, [w[0], g.reshape(w.shape[1:]), m[0], v[0]], 3)
            return g.reshape(w.shape), d[None], m2[None], v2[None]
        return (g.reshape(w.shape),) + tuple(ew_call(name, f_adamw, [w, g.reshape(w.shape), m, v], 3))

    res = {
        "cv_w_pw1": adamw("adamw_pw1", cv_w_pw1, g_pw1, m_cv_w_pw1, v_cv_w_pw1),
        "cv_w_dw": adamw("adamw_dw", cv_w_dw, g_dw, m_cv_w_dw, v_cv_w_dw),
        "cv_w_pw2": adamw("adamw_pw2", cv_w_pw2, g_pw2, m_cv_w_pw2, v_cv_w_pw2),
        "gdn_w_in": adamw("adamw_win", gdn_w_in, g_in, m_gdn_w_in, v_gdn_w_in),
        "gdn_conv_w": adamw("adamw_cvw", gdn_conv_w, g_cv, m_gdn_conv_w, v_gdn_conv_w),
        "gdn_w_out": adamw("adamw_wout", gdn_w_out, g_out, m_gdn_w_out, v_gdn_w_out),
        "mlp_w1": adamw("adamw_w1", mlp_w1, g_w1, m_mlp_w1, v_mlp_w1),
        "mlp_w2": adamw("adamw_w2", mlp_w2, g_w2, m_mlp_w2, v_mlp_w2),
    }
    names = ("norm_mix_g", "norm_ffn_g", "final_norm_g", "cv_b_pw1", "cv_b_dw", "cv_ln_g", "cv_ln_b", "cv_b_pw2",
             "gdn_a_log", "gdn_dt_bias", "gdn_norm_g")
    for nm, gg, dd, mm_, vv in zip(names, unpack(sg), unpack(sd), unpack(sm), unpack(sv)):
        res[nm] = (gg, dd, mm_, vv)
    weights = ("norm_mix_g", "norm_ffn_g", "final_norm_g", "cv_w_pw1", "cv_b_pw1", "cv_w_dw", "cv_b_dw", "cv_ln_g",
               "cv_ln_b", "cv_w_pw2", "cv_b_pw2", "gdn_w_in", "gdn_conv_w", "gdn_a_log", "gdn_dt_bias", "gdn_norm_g",
               "gdn_w_out", "mlp_w1", "mlp_w2")
    outs = [loss, grad_x.reshape(x.shape)]
    for kind in range(4):
        outs += [res[nm][kind] for nm in weights]
    return tuple(outs)
```

```python
import functools

import jax
import jax.numpy as jnp
from jax import lax
from jax.experimental import pallas as pl
from jax.experimental.pallas import tpu as pltpu

F32, BF16 = jnp.float32, jnp.bfloat16
D = 1024
H = 8
HD = 128
CH = 64
DFF = 4 * D
KCV, HB_CV = 31, 32
KSC, HB_SC = 4, 8
EPS = 1e-6
LR, B1, B2, EPS_A, WD, STEP = 0.001, 0.9, 0.999, 1e-08, 0.01, 10
VMEM_LIMIT = 56 * 1024 * 1024
SUB = 32
NSMALL = 64
MESH = pl.DeviceIdType.MESH


def _cp(*sem):
    return pltpu.CompilerParams(dimension_semantics=sem, vmem_limit_bytes=VMEM_LIMIT)


def f_rms(h, g):
    return h * lax.rsqrt(jnp.mean(h * h, axis=-1, keepdims=True) + EPS) * g


def f_silu(x):
    return x * jax.nn.sigmoid(x)


def f_glu(u):
    return u[:, :D] * jax.nn.sigmoid(u[:, D:])


def f_ln_silu(x, g, b):
    mu = jnp.mean(x, axis=-1, keepdims=True)
    xc = x - mu
    y = xc * lax.rsqrt(jnp.mean(xc * xc, axis=-1, keepdims=True) + EPS)
    return f_silu(y * g + b)


def f_relu2(z):
    r = jnp.maximum(z.astype(F32), 0.0)
    return r * r


def f_post(o, z, ng):
    outs = []
    for h in range(H):
        oh = o[:, h * HD:(h + 1) * HD]
        y = oh * lax.rsqrt(jnp.mean(oh * oh, axis=-1, keepdims=True) + EPS) * ng
        outs.append(y * f_silu(z[:, h * HD:(h + 1) * HD]))
    return jnp.concatenate(outs, axis=1)


def f_adamw(w, g, m, v):
    m2 = B1 * m + (1.0 - B1) * g
    v2 = B2 * v + (1.0 - B2) * (g * g)
    m_hat = m2 / (1.0 - B1 ** STEP)
    v_hat = v2 / (1.0 - B2 ** STEP)
    delta = -LR * (m_hat / (jnp.sqrt(v_hat) + EPS_A) + WD * w)
    return delta, m2, v2


def _dot_raw(a, b, mode):
    dims = {"NN": ((1,), (0,)), "NT": ((1,), (1,)), "TN": ((0,), (0,))}[mode]
    return lax.dot_general(a.astype(BF16), b.astype(BF16), (dims, ((), ())), preferred_element_type=F32)


@functools.partial(jax.custom_vjp, nondiff_argnums=(2,))
def _dot_vjp(a, b, mode):
    return _dot_raw(a, b, mode)


def _dot_fwd(a, b, mode):
    return _dot_raw(a, b, mode), (a, b)


def _dot_bwd(mode, res, dc):
    a, b = res
    if mode == "NN":
        return _dot_vjp(dc, b, "NT"), _dot_vjp(a, dc, "TN")
    if mode == "NT":
        return _dot_vjp(dc, b, "NN"), _dot_vjp(dc, a, "TN")
    return _dot_vjp(b, dc, "NT"), _dot_vjp(a, dc, "NN")


_dot_vjp.defvjp(_dot_fwd, _dot_bwd)


def _split(x):
    xh = x.astype(BF16)
    return xh, (x - xh.astype(F32)).astype(BF16)


def _dot_split(xs, ys):
    (xh, xl), (yh, yl) = xs, ys
    return _dot_raw(xh, yh, "NN") + (_dot_raw(xh, yl, "NN") + _dot_raw(xl, yh, "NN"))


def _tril_inverse(a_list):
    ri = lax.broadcasted_iota(jnp.int32, (CH, CH), 0)
    ci = lax.broadcasted_iota(jnp.int32, (CH, CH), 1)
    eye = (ri == ci).astype(F32)
    ts = None
    for lvl in range(CH.bit_length() - 1):
        same_pair = jnp.right_shift(ri, lvl + 1) == jnp.right_shift(ci, lvl + 1)
        quarter = (jnp.bitwise_and(jnp.right_shift(ri, lvl), 1) == 1) & (jnp.bitwise_and(jnp.right_shift(ci, lvl), 1) == 0)
        offs = [jnp.where(same_pair & quarter, a, 0.0) for a in a_list]
        if ts is None:
            ts = [eye - off for off in offs]
            continue
        tsp = [_split(t) for t in ts]
        mids = [_dot_split(tp, _split(off)) for tp, off in zip(tsp, offs)]
        ts = [t - _dot_split(_split(m), tp) for t, m, tp in zip(ts, mids, tsp)]
    return ts


@jax.custom_vjp
def _stored_solve(a, t, rhs):
    return _dot_raw(t, rhs, "NN")


def _stored_solve_fwd(a, t, rhs):
    sol = _dot_raw(t, rhs, "NN")
    return sol, (t, sol)


def _stored_solve_bwd(res, g):
    t, sol = res
    g_rhs = _dot_vjp(t, g, "TN")
    return -_dot_vjp(g_rhs, sol, "NT"), jnp.zeros_like(t), g_rhs


_stored_solve.defvjp(_stored_solve_fwd, _stored_solve_bwd)


def _lane_pick(row, idx, width):
    sel = lax.broadcasted_iota(jnp.int32, (1, width), 1) == idx
    return jnp.sum(jnp.where(sel, row, 0.0), axis=1, keepdims=True)


def f_prep(cqs, cks, cvs, araws, braws, alogs, dtbs, t_stored, dot):
    ri = lax.broadcasted_iota(jnp.int32, (CH, CH), 0)
    ci = lax.broadcasted_iota(jnp.int32, (CH, CH), 1)
    eye = (ri == ci).astype(F32)
    low = (ri >= ci).astype(F32)
    last = lax.broadcasted_iota(jnp.int32, (CH, 1), 0) == CH - 1
    nh = range(len(cqs))
    qs, ks, vbs, kbs, gcs, decays = [], [], [], [], [], []
    for h in nh:
        q = f_silu(cqs[h])
        qs.append(q * lax.rsqrt(jnp.sum(q * q, axis=-1, keepdims=True) + 1e-6) * (HD ** -0.5))
        k = f_silu(cks[h])
        k = k * lax.rsqrt(jnp.sum(k * k, axis=-1, keepdims=True) + 1e-6)
        ks.append(k)
        beta = jax.nn.sigmoid(braws[h])
        sp_in = araws[h] + dtbs[h]
        softplus = jnp.maximum(sp_in, 0.0) + jnp.log(1.0 + jnp.exp(-jnp.abs(sp_in)))
        g = -jnp.exp(alogs[h]) * softplus
        g_row = jnp.sum(eye * g, axis=0, keepdims=True)
        gc = jnp.sum(low * g_row, axis=1, keepdims=True)
        gc_row = jnp.sum(eye * gc, axis=0, keepdims=True)
        gcs.append(gc)
        decays.append(jnp.exp(jnp.where(ri >= ci, gc - gc_row, -1e30)))
        vbs.append(f_silu(cvs[h]) * beta)
        kbs.append(k * beta)
    kks = [dot(kbs[h], ks[h], "NT") for h in nh]
    a_list = [jnp.where(ri > ci, kks[h] * decays[h], 0.0) for h in nh]
    if t_stored is None:
        ts = _tril_inverse(a_list)
        solve = lambda h, rhs: dot(ts[h], rhs, "NN")
    else:
        ts = t_stored
        solve = lambda h, rhs: _stored_solve(a_list[h], t_stored[h], rhs)
    egcs = [jnp.exp(gc) for gc in gcs]
    us = [solve(h, vbs[h]) for h in nh]
    ws = [solve(h, kbs[h] * egcs[h]) for h in nh]
    qks = [dot(qs[h], ks[h], "NT") * decays[h] for h in nh]
    qgs = [qs[h] * egcs[h] for h in nh]
    gls = [jnp.sum(jnp.where(last, gc, 0.0), axis=0, keepdims=True) for gc in gcs]
    kgs = [ks[h] * jnp.exp(gls[h] - gcs[h]) for h in nh]
    egs = [jnp.exp(gl) * jnp.ones((1, HD), F32) for gl in gls]
    return us, ws, qks, qgs, kgs, egs, ts


def f_scan(ss, us, ws, qgs, kgs, qks, egs, dot):
    nh = range(len(ss))
    ws_s = [dot(ws[h], ss[h], "NN") for h in nh]
    qs_s = [dot(qgs[h], ss[h], "NN") for h in nh]
    vns = [us[h] - ws_s[h] for h in nh]
    os_ = [qs_s[h] + dot(qks[h], vns[h], "NN") for h in nh]
    s2s = [ss[h] * egs[h] + dot(kgs[h], vns[h], "TN") for h in nh]
    return os_, s2s


def row_call(name, fn, rows, pars, out_rows, out_accs, tm):
    T = rows[0][0].shape[0]
    n_r, n_p, n_o = len(rows), len(pars), len(out_rows)
    in_specs = [pl.BlockSpec((tm, w), functools.partial(lambda i, cb: (i, cb), cb=cb)) for (_, w, cb) in rows]
    in_specs += [pl.BlockSpec(p.shape, functools.partial(lambda i, nd: (0,) * nd, nd=p.ndim)) for p in pars]
    out_specs = [pl.BlockSpec((tm, w), lambda i: (i, 0)) for (w, _) in out_rows]
    out_specs += [pl.BlockSpec(s, lambda i: (0, 0)) for s in out_accs]
    out_shape = [jax.ShapeDtypeStruct((T, w), dt) for (w, dt) in out_rows]
    out_shape += [jax.ShapeDtypeStruct(s, F32) for s in out_accs]

    def body(*refs):
        rin, pin = refs[:n_r], refs[n_r:n_r + n_p]
        rout, aout = refs[n_r + n_p:n_r + n_p + n_o], refs[n_r + n_p + n_o:]
        if aout:
            @pl.when(pl.program_id(0) == 0)
            def _():
                for a in aout:
                    a[...] = jnp.zeros(a.shape, F32)
        pv = [p[...] for p in pin]

        def step(r, carry):
            sl = pl.ds(pl.multiple_of(r * SUB, SUB), SUB)
            outs, accs = fn(*[x[sl, :] for x in rin], *pv)
            for o, val in zip(rout, outs):
                o[sl, :] = val.astype(o.dtype)
            for a, val in zip(aout, accs):
                a[...] += val
            return carry

        lax.fori_loop(0, tm // SUB, step, 0)

    return pl.pallas_call(body, name=name, grid=(T // tm,), in_specs=in_specs, out_specs=out_specs,
                          out_shape=out_shape, compiler_params=_cp("arbitrary"))(*[r[0] for r in rows], *pars)


EW_TILE_ELEMS = 256 * 1024


def _ew_rows(R, Cc):
    if R * Cc <= EW_TILE_ELEMS or R % 8:
        return R
    tr = 8
    while tr * 2 * Cc <= EW_TILE_ELEMS and R % (tr * 2) == 0:
        tr *= 2
    return tr


def ew_call(name, fn, ins, n_out):
    shape = ins[0].shape
    lead = shape[:-2]
    R, Cc = shape[-2:]
    tr = _ew_rows(R, Cc)
    grid = lead + (R // tr,)
    nl = len(lead)
    spec = pl.BlockSpec((None,) * nl + (tr, Cc), lambda *idx: idx + (0,))

    def body(*refs):
        outs = fn(*[r[...] for r in refs[:len(ins)]])
        for o, val in zip(refs[len(ins):], outs):
            o[...] = val

    return pl.pallas_call(body, name=name, grid=grid, in_specs=[spec] * len(ins), out_specs=[spec] * n_out,
                          out_shape=[jax.ShapeDtypeStruct(shape, F32)] * n_out,
                          compiler_params=_cp(*(("arbitrary",) * len(grid))))(*ins)


MM_RESIDENT_BYTES = 8 * 1024 * 1024


def mm(name, a, b, mode, out_dtype, a_fn=None, epi=None, epi_ins=(), accs=(), tm=512):
    sub_epi = epi is not None and isinstance(out_dtype, (list, tuple))
    M, K = a.shape
    N = b.shape[1] if mode == "NN" else b.shape[0]
    tn = N if K * N * 2 <= MM_RESIDENT_BYTES else min(N, 1024)
    tm = min(tm if tn <= 1024 else tm // 2, M)
    multi = isinstance(out_dtype, (list, tuple))
    dts = list(out_dtype) if multi else [out_dtype]
    n_e, n_o = len(epi_ins), len(dts)
    in_specs = [pl.BlockSpec((tm, K), lambda j, i: (i, 0)),
                pl.BlockSpec((K, tn), lambda j, i: (0, j)) if mode == "NN" else pl.BlockSpec((tn, K), lambda j, i: (j, 0))]
    row_kinds = []
    for (arr, kind) in epi_ins:
        if kind == "tile" or isinstance(kind, tuple):
            cb = kind[1] if isinstance(kind, tuple) else 0
            in_specs.append(pl.BlockSpec((tm, tn), functools.partial(lambda j, i, cb: (i, cb + j), cb=cb)))
            row_kinds.append(True)
        elif kind == "row":
            in_specs.append(pl.BlockSpec((1, tn), lambda j, i: (0, j)))
            row_kinds.append(False)
        else:
            in_specs.append(pl.BlockSpec(arr.shape, lambda j, i: (0, 0)))
            row_kinds.append(False)

    def body(a_ref, b_ref, *rest):
        e_refs, o_refs, acc_refs = rest[:n_e], rest[n_e:n_e + n_o], rest[n_e + n_o:n_e + n_o + len(accs)]
        av = a_ref[...]
        if a_fn is not None:
            av = a_fn(av)
        res = _dot_raw(av, b_ref[...], mode)
        if epi is None or not sub_epi:
            if epi is not None:
                res = epi(res, *[r[...] for r in e_refs])
            o_refs[0][...] = res.astype(o_refs[0].dtype)
            return
        prod = rest[-1]
        prod[...] = res
        if acc_refs:
            @pl.when((pl.program_id(0) == 0) & (pl.program_id(1) == 0))
            def _():
                for r in acc_refs:
                    r[...] = jnp.zeros(r.shape, F32)
        small = [None if is_rows else r[...] for r, is_rows in zip(e_refs, row_kinds)]

        def step(k, carry):
            sl = pl.ds(pl.multiple_of(k * SUB, SUB), SUB)
            out = epi(prod[sl, :], *[r[sl, :] if is_rows else sm for r, is_rows, sm in zip(e_refs, row_kinds, small)])
            tiles, contribs = out if multi else ((out,), ())
            for r, t in zip(o_refs, tiles):
                r[sl, :] = t.astype(r.dtype)
            for r, t in zip(acc_refs, contribs):
                r[...] += t
            return carry

        lax.fori_loop(0, tm // SUB, step, 0)

    out_specs = [pl.BlockSpec((tm, tn), lambda j, i: (i, j))] * n_o + [pl.BlockSpec(s, lambda j, i: (0, 0)) for s in accs]
    out_shape = [jax.ShapeDtypeStruct((M, N), dt) for dt in dts] + [jax.ShapeDtypeStruct(s, F32) for s in accs]
    res = pl.pallas_call(body, name=name, grid=(N // tn, M // tm), in_specs=in_specs, out_specs=out_specs,
                         out_shape=out_shape, scratch_shapes=[pltpu.VMEM((tm, tn), F32)] if sub_epi else [],
                         compiler_params=_cp("arbitrary", "arbitrary"))(a, b, *[e[0] for e in epi_ins])
    return res if multi else res[0]


def mm_tn(name, a, g, a_fn=None, a_cols=None, tt=512):
    T = a.shape[0]
    ka, acb = (a.shape[1], 0) if a_cols is None else a_cols
    N = g.shape[1]
    tt = min(tt, T)
    tka, tn = min(ka, 1024), min(N, 1024)
    nkb = ka // tka

    def body(a_ref, g_ref, o_ref):
        @pl.when(pl.program_id(2) == 0)
        def _():
            o_ref[...] = jnp.zeros(o_ref.shape, F32)
        av = a_ref[...]
        if a_fn is not None:
            av = a_fn(av)
        o_ref[...] += _dot_raw(av, g_ref[...], "TN")

    return pl.pallas_call(body, name=name, grid=(nkb, N // tn, T // tt),
                          in_specs=[pl.BlockSpec((tt, tka), lambda ia, j, t: (t, acb * nkb + ia)),
                                    pl.BlockSpec((tt, tn), lambda ia, j, t: (t, j))],
                          out_specs=pl.BlockSpec((tka, tn), lambda ia, j, t: (ia, j)),
                          out_shape=jax.ShapeDtypeStruct((ka, N), F32),
                          compiler_params=_cp("arbitrary", "arbitrary", "arbitrary"))(a, g)


SUBLANES = 8


class _RowShifts:
    def __init__(self, src, shifted, nrows, reuse):
        self.src, self.shifted, self.reuse = src, shifted, reuse
        if reuse:
            for ph in range(1, SUBLANES):
                for r0 in range(0, nrows - SUBLANES, SUB):
                    n = min(SUB, nrows - SUBLANES - r0)
                    shifted[ph - 1, r0:r0 + n, :] = src[r0 + ph:r0 + ph + n, :]

    def window(self, off, cols):
        ph = off % SUBLANES
        if not self.reuse or ph == 0:
            return self.src[off:off + SUB, cols]
        return self.shifted[ph - 1, off - ph:off - ph + SUB, cols]


def _shift_scratch(nrows, C, reuse):
    return [pltpu.VMEM((SUBLANES - 1, nrows - SUBLANES, C), F32)] if reuse else []


def dwconv_fwd(name, x, xw, w_pad, bias, S, K, HB, pre, post, tm):
    T = x.shape[0]
    C = w_pad.shape[1]
    nb, per_seq = tm // HB, S // tm
    has_b, has_post = bias is not None, post is not None
    reuse = K > SUBLANES

    def body(*refs):
        x_ref, xp_ref, w_ref = refs[:3]
        pos = 3
        b_ref = refs[pos] if has_b else None
        pos += has_b
        ppars = refs[pos:pos + (len(post[1]) if has_post else 0)]
        pos += len(ppars)
        c_ref = refs[pos]
        s_ref = refs[pos + 1] if has_post else None
        ext = refs[pos + 1 + has_post]
        first = (pl.program_id(0) % per_seq) == 0
        ext[0:HB, :] = jnp.where(first, 0.0, pre(xp_ref[...]))
        for r in range(tm // SUB):
            ext[HB + r * SUB:HB + (r + 1) * SUB, :] = pre(x_ref[r * SUB:(r + 1) * SUB, :])
        rows_of = _RowShifts(ext, refs[-1] if reuse else None, HB + tm, reuse)
        pv = [p[...] for p in ppars]
        assert not has_post or C == D
        for r in range(tm // SUB):
            for c0 in range(0, C, D):
                cols = slice(c0, c0 + D)
                acc = jnp.zeros((SUB, D), F32)
                if has_b:
                    acc = acc + b_ref[:, cols]
                for k in range(K):
                    acc = acc + w_ref[k:k + 1, cols] * rows_of.window(HB + r * SUB - (K - 1) + k, cols)
                c_ref[r * SUB:(r + 1) * SUB, cols] = acc
                if has_post:
                    s_ref[r * SUB:(r + 1) * SUB, :] = post[0](acc, *pv).astype(BF16)

    ins = [x, x, w_pad] + ([bias] if has_b else []) + (list(post[1]) if has_post else [])
    in_specs = [pl.BlockSpec((tm, xw), lambda i: (i, 0)),
                pl.BlockSpec((HB, xw), lambda i: (jnp.maximum(i * nb - 1, 0), 0)),
                pl.BlockSpec(w_pad.shape, lambda i: (0, 0))]
    in_specs += [pl.BlockSpec(p.shape, lambda i: (0, 0)) for p in ins[3:]]
    out_specs = [pl.BlockSpec((tm, C), lambda i: (i, 0))] * (1 + has_post)
    out_shape = [jax.ShapeDtypeStruct((T, C), F32)] + ([jax.ShapeDtypeStruct((T, C), BF16)] if has_post else [])
    return pl.pallas_call(body, name=name, grid=(T // tm,), in_specs=in_specs, out_specs=out_specs, out_shape=out_shape,
                          scratch_shapes=[pltpu.VMEM((HB + tm, C), F32)] + _shift_scratch(HB + tm, C, reuse),
                          compiler_params=_cp("arbitrary"))(*ins)


def dwconv_bwd(name, g, x, xw, w_pad, S, K, HB, pre, pre_bwd, tm):
    T = g.shape[0]
    C = w_pad.shape[1]
    nb, per_seq = tm // HB, S // tm
    nblk = T // HB

    reuse = K > SUBLANES

    def body(g_ref, gn_ref, x_ref, xp_ref, w_ref, dx_ref, dw_ref, dbx_ref, extg, exta, *shift_refs):
        i = pl.program_id(0)
        first = (i % per_seq) == 0
        last = (i % per_seq) == per_seq - 1

        @pl.when(i == 0)
        def _():
            dw_ref[...] = jnp.zeros(dw_ref.shape, F32)
            dbx_ref[...] = jnp.zeros(dbx_ref.shape, F32)

        extg[tm:tm + HB, :] = jnp.where(last, 0.0, gn_ref[...])
        exta[0:HB, :] = jnp.where(first, 0.0, pre(xp_ref[...]))
        for r in range(tm // SUB):
            extg[r * SUB:(r + 1) * SUB, :] = g_ref[r * SUB:(r + 1) * SUB, :]
            exta[HB + r * SUB:HB + (r + 1) * SUB, :] = pre(x_ref[r * SUB:(r + 1) * SUB, :])
        assert pre_bwd is None or C == D
        g_rows = _RowShifts(extg, shift_refs[0] if reuse else None, tm + HB, reuse)
        a_rows = _RowShifts(exta, shift_refs[1] if reuse else None, HB + tm, reuse)
        for r in range(tm // SUB):
            rows = slice(r * SUB, (r + 1) * SUB)
            for c0 in range(0, C, D):
                cols = slice(c0, c0 + D)
                acc = jnp.zeros((SUB, D), F32)
                for k in range(K):
                    acc = acc + w_ref[k:k + 1, cols] * g_rows.window(r * SUB + (K - 1) - k, cols)
                if pre_bwd is None:
                    dx_ref[rows, cols] = acc
                    dbx_ref[:, cols] += jnp.sum(acc, axis=0, keepdims=True)
                else:
                    dx = pre_bwd(x_ref[rows, :], acc)
                    dx_ref[rows, :] = dx
                    dbx_ref[...] += jnp.sum(dx, axis=0, keepdims=True)
        for k in range(K):
            for c0 in range(0, C, D):
                cols = slice(c0, c0 + D)
                p = jnp.zeros((SUB, D), F32)
                for r in range(tm // SUB):
                    p = p + extg[r * SUB:(r + 1) * SUB, cols] * a_rows.window(HB + r * SUB - (K - 1) + k, cols)
                dw_ref[k:k + 1, cols] += jnp.sum(p, axis=0, keepdims=True)

    in_specs = [pl.BlockSpec((tm, C), lambda i: (i, 0)),
                pl.BlockSpec((HB, C), lambda i: (jnp.minimum((i + 1) * nb, nblk - 1), 0)),
                pl.BlockSpec((tm, xw), lambda i: (i, 0)),
                pl.BlockSpec((HB, xw), lambda i: (jnp.maximum(i * nb - 1, 0), 0)),
                pl.BlockSpec(w_pad.shape, lambda i: (0, 0))]
    out_specs = [pl.BlockSpec((tm, xw), lambda i: (i, 0)), pl.BlockSpec((HB, C), lambda i: (0, 0)),
                 pl.BlockSpec((1, xw), lambda i: (0, 0))]
    out_shape = [jax.ShapeDtypeStruct((T, xw), F32), jax.ShapeDtypeStruct((HB, C), F32), jax.ShapeDtypeStruct((1, xw), F32)]
    return pl.pallas_call(body, name=name, grid=(T // tm,), in_specs=in_specs, out_specs=out_specs, out_shape=out_shape,
                          scratch_shapes=[pltpu.VMEM((tm + HB, C), F32), pltpu.VMEM((HB + tm, C), F32)]
                          + _shift_scratch(tm + HB, C, reuse) * 2,
                          compiler_params=_cp("arbitrary"))(g, g, x, x, w_pad)


def _glu_bwd(u, da):
    u1, sg = u[:, :D], jax.nn.sigmoid(u[:, D:])
    return jnp.concatenate([da * sg, da * u1 * sg * (1.0 - sg)], axis=1)


def _head_cols(ref, h, base=0):
    return ref[:, base + h * HD:base + (h + 1) * HD]


def _prep_inputs(c_ref, ab, al, dt):
    hs = range(H)
    return ([_head_cols(c_ref, h) for h in hs], [_head_cols(c_ref, h, D) for h in hs],
            [_head_cols(c_ref, h, 2 * D) for h in hs], [_lane_pick(ab, h, HD) for h in hs],
            [_lane_pick(ab, H + h, HD) for h in hs], [_lane_pick(al, h, HD) for h in hs],
            [_lane_pick(dt, h, HD) for h in hs])


def gdn_prep_fwd(cpre, pab, alog, dtb):
    T = cpre.shape[0]
    nc = T // CH

    def body(c_ref, ab_ref, al_ref, dt_ref, u_ref, w_ref, qg_ref, kg_ref, qk_ref, t_ref, eg_ref):
        us, ws, qks, qgs, kgs, egs, ts = f_prep(*_prep_inputs(c_ref, ab_ref[...], al_ref[...], dt_ref[...]), None, _dot_raw)
        for h in range(H):
            cols = slice(h * HD, (h + 1) * HD)
            u_ref[:, cols] = us[h]
            w_ref[:, cols] = ws[h].astype(BF16)
            qg_ref[:, cols] = qgs[h].astype(BF16)
            kg_ref[:, cols] = kgs[h].astype(BF16)
            qk_ref[0, h] = qks[h].astype(BF16)
            t_ref[0, h] = ts[h].astype(BF16)
            eg_ref[0, h:h + 1, :] = egs[h]

    row = lambda w: pl.BlockSpec((CH, w), lambda n: (n, 0))
    par = pl.BlockSpec((1, HD), lambda n: (0, 0))
    mat = pl.BlockSpec((1, H, CH, CH), lambda n: (n, 0, 0, 0))
    return pl.pallas_call(
        body, name="gdn_prep_fwd", grid=(nc,), in_specs=[row(3 * D), row(HD), par, par],
        out_specs=[row(D), row(D), row(D), row(D), mat, mat, pl.BlockSpec((1, H, HD), lambda n: (n, 0, 0))],
        out_shape=[jax.ShapeDtypeStruct((T, D), F32)] + [jax.ShapeDtypeStruct((T, D), BF16)] * 3
        + [jax.ShapeDtypeStruct((nc, H, CH, CH), BF16)] * 2 + [jax.ShapeDtypeStruct((nc, H, HD), F32)],
        compiler_params=_cp("arbitrary"))(cpre, pab, alog, dtb)


def gdn_prep_bwd(cpre, pab, alog, dtb, tmat, du, dw, dqg, dkg, dqk, deg):
    T = cpre.shape[0]
    nc = T // CH

    def body(c_ref, ab_ref, al_ref, dt_ref, t_ref, du_ref, dw_ref, dqg_ref, dkg_ref, dqk_ref, deg_ref,
             dc_ref, dab_ref, dal_ref, ddt_ref):
        @pl.when(pl.program_id(0) == 0)
        def _():
            dal_ref[...] = jnp.zeros(dal_ref.shape, F32)
            ddt_ref[...] = jnp.zeros(ddt_ref.shape, F32)

        lane = lax.broadcasted_iota(jnp.int32, (1, HD), 1)
        dab = jnp.zeros((CH, HD), F32)
        dal = jnp.zeros((1, HD), F32)
        ddt = jnp.zeros((1, HD), F32)
        hs = range(H)
        t_st = [t_ref[0, h].astype(F32) for h in hs]

        def fwd(*args):
            return tuple(f_prep(*args, t_st, _dot_vjp)[:6])

        _, vjp = jax.vjp(fwd, *_prep_inputs(c_ref, ab_ref[...], al_ref[...], dt_ref[...]))
        dcqs, dcks, dcvs, dars, dbrs, dals, ddts = vjp((
            [_head_cols(du_ref, h) for h in hs], [_head_cols(dw_ref, h) for h in hs], [dqk_ref[0, h] for h in hs],
            [_head_cols(dqg_ref, h) for h in hs], [_head_cols(dkg_ref, h) for h in hs],
            [deg_ref[0, h:h + 1, :] for h in hs]))
        for h in hs:
            dc_ref[:, h * HD:(h + 1) * HD] = dcqs[h]
            dc_ref[:, D + h * HD:D + (h + 1) * HD] = dcks[h]
            dc_ref[:, 2 * D + h * HD:2 * D + (h + 1) * HD] = dcvs[h]
            dab = dab + jnp.where(lane == h, dars[h], 0.0) + jnp.where(lane == H + h, dbrs[h], 0.0)
            dal = dal + jnp.where(lane == h, dals[h], 0.0)
            ddt = ddt + jnp.where(lane == h, ddts[h], 0.0)
        dab_ref[...] = dab
        dal_ref[...] += dal
        ddt_ref[...] += ddt

    row = lambda w: pl.BlockSpec((CH, w), lambda n: (n, 0))
    par = pl.BlockSpec((1, HD), lambda n: (0, 0))
    mat = pl.BlockSpec((1, H, CH, CH), lambda n: (n, 0, 0, 0))
    vec = pl.BlockSpec((1, H, HD), lambda n: (n, 0, 0))
    return pl.pallas_call(
        body, name="gdn_prep_bwd", grid=(nc,),
        in_specs=[row(3 * D), row(HD), par, par, mat, row(D), row(D), row(D), row(D), mat, vec],
        out_specs=[row(3 * D), row(HD), par, par],
        out_shape=[jax.ShapeDtypeStruct((T, 3 * D), F32), jax.ShapeDtypeStruct((T, HD), F32),
                   jax.ShapeDtypeStruct((1, HD), F32), jax.ShapeDtypeStruct((1, HD), F32)],
        compiler_params=_cp("arbitrary"))(cpre, pab, alog, dtb, tmat, du, dw, dqg, dkg, dqk, deg)


def gdn_scan_fwd(u, w, qg, kg, qk, eg, S):
    T = u.shape[0]
    nc, per_seq = T // CH, S // CH

    def body(u_ref, w_ref, qg_ref, kg_ref, qk_ref, eg_ref, o_ref, sall_ref, s_ref):
        @pl.when(pl.program_id(0) % per_seq == 0)
        def _():
            s_ref[...] = jnp.zeros(s_ref.shape, F32)

        hs = range(H)
        ss = [s_ref[h] for h in hs]
        os_, s2s = f_scan(ss, [_head_cols(u_ref, h) for h in hs], [_head_cols(w_ref, h) for h in hs],
                          [_head_cols(qg_ref, h) for h in hs], [_head_cols(kg_ref, h) for h in hs],
                          [qk_ref[0, h] for h in hs], [eg_ref[0, h:h + 1, :] for h in hs], _dot_raw)
        for h in hs:
            sall_ref[0, h] = ss[h]
            o_ref[:, h * HD:(h + 1) * HD] = os_[h]
            s_ref[h] = s2s[h]

    row = pl.BlockSpec((CH, D), lambda n: (n, 0))
    return pl.pallas_call(
        body, name="gdn_scan_fwd", grid=(nc,),
        in_specs=[row, row, row, row, pl.BlockSpec((1, H, CH, CH), lambda n: (n, 0, 0, 0)),
                  pl.BlockSpec((1, H, HD), lambda n: (n, 0, 0))],
        out_specs=[row, pl.BlockSpec((1, H, HD, HD), lambda n: (n, 0, 0, 0))],
        out_shape=[jax.ShapeDtypeStruct((T, D), F32), jax.ShapeDtypeStruct((nc, H, HD, HD), F32)],
        scratch_shapes=[pltpu.VMEM((H, HD, HD), F32)], compiler_params=_cp("arbitrary"))(u, w, qg, kg, qk, eg)


def gdn_scan_bwd(do, u, w, qg, kg, qk, eg, sall, S):
    T = u.shape[0]
    nc, per_seq = T // CH, S // CH

    def body(do_ref, u_ref, w_ref, qg_ref, kg_ref, qk_ref, eg_ref, sall_ref,
             du_ref, dw_ref, dqg_ref, dkg_ref, dqk_ref, deg_ref, ds_ref):
        n = nc - 1 - pl.program_id(0)

        @pl.when(n % per_seq == per_seq - 1)
        def _():
            ds_ref[...] = jnp.zeros(ds_ref.shape, F32)

        hs = range(H)

        def fwd(*args):
            return f_scan(*args, _dot_vjp)

        _, vjp = jax.vjp(fwd, [sall_ref[0, h] for h in hs], [_head_cols(u_ref, h) for h in hs],
                         [_head_cols(w_ref, h).astype(F32) for h in hs], [_head_cols(qg_ref, h).astype(F32) for h in hs],
                         [_head_cols(kg_ref, h).astype(F32) for h in hs], [qk_ref[0, h].astype(F32) for h in hs],
                         [eg_ref[0, h:h + 1, :] for h in hs])
        dss, dus, dws, dqgs, dkgs, dqks, degs = vjp(([_head_cols(do_ref, h) for h in hs], [ds_ref[h] for h in hs]))
        for h in hs:
            cols = slice(h * HD, (h + 1) * HD)
            du_ref[:, cols] = dus[h]
            dw_ref[:, cols] = dws[h]
            dqg_ref[:, cols] = dqgs[h]
            dkg_ref[:, cols] = dkgs[h]
            dqk_ref[0, h] = dqks[h]
            deg_ref[0, h:h + 1, :] = degs[h]
            ds_ref[h] = dss[h]

    rev = lambda n: (nc - 1 - n, 0)
    row = pl.BlockSpec((CH, D), rev)
    mat = pl.BlockSpec((1, H, CH, CH), lambda n: (nc - 1 - n, 0, 0, 0))
    vec = pl.BlockSpec((1, H, HD), lambda n: (nc - 1 - n, 0, 0))
    return pl.pallas_call(
        body, name="gdn_scan_bwd", grid=(nc,),
        in_specs=[row, row, row, row, row, mat, vec, pl.BlockSpec((1, H, HD, HD), lambda n: (nc - 1 - n, 0, 0, 0))],
        out_specs=[row, row, row, row, mat, vec],
        out_shape=[jax.ShapeDtypeStruct((T, D), F32)] * 4
        + [jax.ShapeDtypeStruct((nc, H, CH, CH), F32), jax.ShapeDtypeStruct((nc, H, HD), F32)],
        scratch_shapes=[pltpu.VMEM((H, HD, HD), F32)], compiler_params=_cp("arbitrary"))(do, u, w, qg, kg, qk, eg, sall)


def xor_exchange(name, ins, inplace, out_shapes, plan, n_remote, n_local=0):
    n_in, n_ip, n_out = len(ins), len(inplace), len(out_shapes)

    def body(*refs):
        in_refs = refs[:n_in]
        ip_refs = refs[n_in + n_ip:n_in + 2 * n_ip]
        out_refs = refs[n_in + 2 * n_ip:n_in + 2 * n_ip + n_out]
        send_sems, recv_sems, loc_sems = refs[n_in + 2 * n_ip + n_out:]
        x, y, c = lax.axis_index("x"), lax.axis_index("y"), lax.axis_index("c")
        remote, local = plan(in_refs, ip_refs, out_refs, (x, y, c))
        assert len(remote) == n_remote and len(local) == n_local
        copies = []
        for k, ((dx, dy, dc), src, dst) in enumerate(remote):
            peer = (1 - x if dx else x, 1 - y if dy else y, 1 - c if dc else c)
            copies.append(pltpu.make_async_remote_copy(src_ref=src, dst_ref=dst, send_sem=send_sems.at[k],
                                                       recv_sem=recv_sems.at[k], device_id=peer, device_id_type=MESH))
        for cp in copies:
            cp.start()
        locs = [pltpu.make_async_copy(src, dst, loc_sems.at[k]) for k, (src, dst) in enumerate(local)]
        for cp in locs:
            cp.start()
        for cp in copies:
            cp.wait()
        for cp in locs:
            cp.wait()

    anyspec = pl.BlockSpec(memory_space=pl.ANY)
    res = pl.pallas_call(
        body, name=name, in_specs=[anyspec] * (n_in + n_ip), out_specs=[anyspec] * (n_ip + n_out),
        out_shape=[jax.ShapeDtypeStruct(a.shape, a.dtype) for a in inplace] + list(out_shapes),
        input_output_aliases={n_in + i: i for i in range(n_ip)},
        scratch_shapes=[pltpu.SemaphoreType.DMA((n_remote,)), pltpu.SemaphoreType.DMA((n_remote,)),
                        pltpu.SemaphoreType.DMA((max(n_local, 1),))],
        )(*ins, *inplace)
    return list(res[:n_ip]), list(res[n_ip:])


HBM_SPEC = pl.BlockSpec(memory_space=pltpu.HBM)
SEM_SPEC = pl.BlockSpec(memory_space=pltpu.SEMAPHORE)


def _flip_copies(plan, refs, send_sems, recv_sems):
    x, y, c = lax.axis_index("x"), lax.axis_index("y"), lax.axis_index("c")
    copies = []
    for k, ((dx, dy, dc), src, dst) in enumerate(plan(refs, (x, y, c))):
        peer = (1 - x if dx else x, 1 - y if dy else y, 1 - c if dc else c)
        copies.append(pltpu.make_async_remote_copy(src_ref=src, dst_ref=dst, send_sem=send_sems.at[k],
                                                   recv_sem=recv_sems.at[k], device_id=peer, device_id_type=MESH))
    return copies


def xor_start(name, arrays, plan, n_remote, after):
    n = len(arrays)

    def body(*refs):
        for cp in _flip_copies(plan, refs[:n], refs[n + 1], refs[n + 2]):
            cp.start()
        refs[-1][...] = jnp.zeros(refs[-1].shape, F32)

    res = pl.pallas_call(
        body, name=name, in_specs=[HBM_SPEC] * n + [pl.BlockSpec(memory_space=pl.ANY)],
        out_shape=(pltpu.SemaphoreType.DMA((n_remote,)), pltpu.SemaphoreType.DMA((n_remote,)),
                   *[pltpu.HBM(a.shape, a.dtype) for a in arrays], jax.ShapeDtypeStruct((8, 128), F32)),
        out_specs=(SEM_SPEC, SEM_SPEC, *([HBM_SPEC] * n), pl.BlockSpec(memory_space=pltpu.VMEM)),
        input_output_aliases={i: 2 + i for i in range(n)},
        compiler_params=pltpu.CompilerParams(has_side_effects=pltpu.SideEffectType.DATAFLOW_SIDE_EFFECTING),
    )(*[pltpu.with_memory_space_constraint(a, pltpu.HBM) for a in arrays], after)
    return res[0], res[1], list(res[2:2 + n]), res[2 + n]


def xor_wait(name, send_sems, recv_sems, arrays, plan, after):
    n = len(arrays)

    def body(*refs):
        for cp in _flip_copies(plan, refs[:n], refs[n], refs[n + 1]):
            cp.wait_send()
            cp.wait_recv()

    return list(pl.pallas_call(
        body, name=name, in_specs=[HBM_SPEC] * n + [SEM_SPEC, SEM_SPEC, pl.BlockSpec(memory_space=pl.ANY)],
        out_shape=[pltpu.HBM(a.shape, a.dtype) for a in arrays], out_specs=[HBM_SPEC] * n,
        input_output_aliases={i: i for i in range(n)},
        compiler_params=pltpu.CompilerParams(has_side_effects=pltpu.SideEffectType.DATAFLOW_SIDE_EFFECTING),
    )(*arrays, send_sems, recv_sems, after))


class WSpec:
    def __init__(self, name, full, sa, ha, group, layer=None, lead=False):
        self.name, self.full, self.sa, self.ha, self.group, self.layer, self.lead = name, full, sa, ha, group, layer, lead
        self.ws = 1 if lead else full[sa] // 4
        self.wh = full[ha] // 2

    def shard_shape(self):
        if self.lead:
            return tuple(n for a, n in enumerate(self.full) if a != self.sa)
        return tuple(self.ws if a == self.sa else n for a, n in enumerate(self.full))

    def half_full_shape(self):
        return tuple(self.wh if a == self.ha else n for a, n in enumerate(self.full))

    def shard_half_shape(self):
        s = list(self.half_full_shape())
        if self.lead:
            del s[self.sa]
        else:
            s[self.sa] = self.ws
        return tuple(s)

    def full_view(self, ref, q=None, h=None):
        idx = []
        for a in range(len(self.full)):
            if a == self.sa and q is not None:
                idx.append(q if self.lead else pl.ds(pl.multiple_of(q * self.ws, self.ws), self.ws))
            elif a == self.ha and h is not None:
                idx.append(pl.ds(pl.multiple_of(h * self.wh, self.wh), self.wh))
            else:
                idx.append(slice(None))
        return ref.at[tuple(idx)]

    def shard_view(self, ref, h):
        idx = [] if self.layer is None else [self.layer]
        for a in range(len(self.full)):
            if self.lead and a == self.sa:
                continue
            idx.append(pl.ds(pl.multiple_of(h * self.wh, self.wh), self.wh) if a == self.ha else slice(None))
        return ref.at[tuple(idx)]

    def rows_cols(self, shard, half):
        rows, cols = self.full[-2:]
        if shard and not self.lead:
            rows, cols = (rows // 4, cols) if self.sa == 0 else (rows, cols // 4)
        if half:
            rows, cols = (rows // 2, cols) if self.ha == len(self.full) - 2 else (rows, cols // 2)
        return rows, cols

    def spec(self, tr, cw, nr, shard=False, half=False, has_lead=False, stacked=False):
        two_d = len(self.full) == 2
        shard_on_cols = two_d and self.sa == 1
        half_on_cols = two_d and self.ha == 1
        layer = self.layer

        def index(*args):
            pref = args[-1]
            i = args[-2]
            r, cblk, pre = i, 0, ()
            if shard:
                if self.lead:
                    pre = (pref[0],)
                elif shard_on_cols:
                    cblk = pref[0]
                else:
                    r = pref[0] * nr + i
            elif has_lead:
                pre = (args[0],)
            if half:
                if half_on_cols:
                    cblk = pref[1]
                else:
                    r = pref[1] * nr + i
            if stacked:
                pre = (layer,) + pre
            return pre + (r, cblk)

        n_pre = int(stacked) + int(self.lead and (shard or has_lead))
        return pl.BlockSpec((None,) * n_pre + (tr, cw), index)


WSPECS = [
    WSpec("cv_w_pw1", (D, 2 * D), 1, 0, 0),
    WSpec("cv_w_pw2", (D, D), 0, 1, 1),
    WSpec("gdn_w_in", (4, D, (4 * D + 2 * H) // 4), 0, 1, 2, lead=True),
    WSpec("gdn_w_out", (D, D), 0, 1, 3),
    WSpec("mlp_w1_0", (D, DFF), 1, 0, 4, layer=0),
    WSpec("mlp_w1_1", (D, DFF), 1, 0, 4, layer=1),
    WSpec("mlp_w2_0", (DFF, D), 0, 1, 5, layer=0),
    WSpec("mlp_w2_1", (DFF, D), 0, 1, 5, layer=1),
]
FLIPS = [(1, 0, 0), (0, 1, 0), (1, 1, 0)]
SIB = (0, 0, 1)


def _chip(x, y):
    return 2 * x + y


def _prefetch_call(name, body, grid, in_specs, out_specs, out_shape, pref, args, aliases=None):
    return pl.pallas_call(
        body, name=name, out_shape=out_shape, input_output_aliases=aliases or {},
        grid_spec=pltpu.PrefetchScalarGridSpec(num_scalar_prefetch=1, grid=grid, in_specs=in_specs, out_specs=out_specs),
        compiler_params=_cp(*(("arbitrary",) * len(grid))))(pref, *args)


def place_shard(ws, shard, pref):
    rows, cols = ws.rows_cols(True, False)
    tr = _ew_rows(rows, cols)
    nr = rows // tr
    stacked = ws.layer is not None
    layer = ws.layer

    def body(_, s_ref, o_ref):
        o_ref[...] = s_ref[...].astype(BF16)

    in_spec = pl.BlockSpec(((None,) if stacked else ()) + (tr, cols),
                           (lambda i, p: (layer, i, 0)) if stacked else (lambda i, p: (i, 0)))
    return _prefetch_call("place_" + ws.name, body, (nr,), [in_spec], ws.spec(tr, cols, nr, shard=True),
                          jax.ShapeDtypeStruct(ws.full, BF16), pref, [shard])


FIRST = [0]
LAYER0 = [1, 4, 6]
LAYER1 = [2, 3, 5, 7]


def _plan_gather_chips(sel):
    def plan(refs, pos):
        x, y, c = pos
        remote = []
        for j, i in enumerate(sel):
            mine = WSPECS[i].full_view(refs[j], _chip(x, y), c)
            remote += [(f, mine, mine) for f in FLIPS]
        return remote
    return plan


def gather_cores(tag, sel, nat):
    def plan(in_refs, ip_refs, out_refs, pos):
        x, y, c = pos
        remote = []
        for j, i in enumerate(sel):
            for (dx, dy, _) in FLIPS:
                got = WSPECS[i].full_view(ip_refs[j], _chip(1 - x if dx else x, 1 - y if dy else y), c)
                remote.append((SIB, got, got))
        return remote, []

    return xor_exchange("gather_cores" + tag, [], nat, [], plan, 3 * len(sel))[0]


def gather_start(tag, sel, placed, after):
    plan = _plan_gather_chips(sel)
    send, recv, arrays, token = xor_start("gather_chips%s_start" % tag, [placed[i] for i in sel], plan, 3 * len(sel), after)
    return (tag, sel, plan, send, recv, arrays), token


def gather_wait(state, after):
    tag, sel, plan, send, recv, arrays = state
    return gather_cores(tag, sel, xor_wait("gather_chips%s_wait" % tag, send, recv, arrays, plan, after))


def reduce_start(tag, sel, grads, pref):
    plan = _plan_reduce_chips(sel)
    sums = chip_sums(tag, sel, grads, pref)
    send, recv, arrays, token = xor_start("reduce_chips%s_start" % tag,
                                          sums + [lax.empty(s.shape, s.dtype) for s in _parts_shapes(sel)], plan,
                                          3 * len(sel), pref)
    return (tag, sel, plan, send, recv, arrays), token


def reduce_wait(state, after):
    tag, sel, plan, send, recv, arrays = state
    arrays = xor_wait("reduce_chips%s_wait" % tag, send, recv, arrays, plan, after)
    return sel, arrays[:len(sel)], arrays[len(sel):]


def gather_first(sel, placed, wdw_shard, wcv_shard):
    plan_w = _plan_gather_chips(sel)

    def plan(in_refs, ip_refs, out_refs, pos):
        x, y, c = pos
        remote, local = plan_w(ip_refs, pos), []
        for j, width in enumerate((D // 4, 3 * D // 4)):
            dst = out_refs[j].at[:, pl.ds(pl.multiple_of(_chip(x, y) * width, 128), width)]
            local.append((in_refs[j], dst))
            remote += [(f, in_refs[j], dst) for f in FLIPS]
        return remote, local

    taps = [jax.ShapeDtypeStruct((KCV, D), F32), jax.ShapeDtypeStruct((KSC, 3 * D), F32)]
    nat, (wdw, wcv) = xor_exchange("gather_chips0", [wdw_shard, wcv_shard], [placed[i] for i in sel], taps, plan,
                                   3 * (len(sel) + 2), 2)
    return gather_cores("0", sel, nat), wdw, wcv


def half_add(ws, g, rsib, pref):
    rows, cols = ws.rows_cols(False, True)
    tr = _ew_rows(rows, cols)
    nr = rows // tr

    def body(_, a_ref, b_ref, o_ref):
        o_ref[...] = (a_ref[...] + b_ref[...]).astype(BF16)

    whole = ws.spec(tr, cols, nr, has_lead=ws.lead)
    return _prefetch_call("reduce_add_" + ws.name, body, (4, nr) if ws.lead else (nr,),
                          [ws.spec(tr, cols, nr, half=True, has_lead=ws.lead), whole], whole,
                          jax.ShapeDtypeStruct(ws.half_full_shape(), BF16), pref, [g, rsib])


def shard_sum(ws, s, parts, buf, pref):
    rows, cols = ws.rows_cols(True, True)
    tr = _ew_rows(rows, cols)
    nr = rows // tr
    stacked = ws.layer is not None

    def body(_, s_ref, p_ref, *rest):
        rest[-1][...] = ((s_ref[...].astype(F32) + p_ref[0].astype(F32)) + p_ref[1].astype(F32)) + p_ref[2].astype(F32)

    in_specs = [ws.spec(tr, cols, nr, shard=True, has_lead=ws.lead), pl.BlockSpec((3, tr, cols), lambda i, p: (0, i, 0))]
    args, aliases = [s, parts], {}
    if buf is not None:
        in_specs.append(pl.BlockSpec(memory_space=pl.ANY))
        args.append(buf)
        aliases = {3: 0}
    shape = ((2,) if stacked else ()) + ws.shard_shape()
    return _prefetch_call("reduce_sum_" + ws.name, body, (nr,), in_specs, ws.spec(tr, cols, nr, half=True, stacked=stacked),
                          jax.ShapeDtypeStruct(shape, F32), pref, args, aliases)


def chip_sums(tag, sel, grads, pref):
    def plan(in_refs, ip_refs, out_refs, pos):
        c = pos[2]
        return [(SIB, WSPECS[i].full_view(in_refs[j], None, 1 - c), out_refs[j]) for j, i in enumerate(sel)], []

    halves = [jax.ShapeDtypeStruct(WSPECS[i].half_full_shape(), F32) for i in sel]
    _, rsib = xor_exchange("reduce_cores" + tag, grads, [], halves, plan, len(sel))
    return [half_add(WSPECS[i], grads[j], rsib[j], pref) for j, i in enumerate(sel)]


def _plan_reduce_chips(sel):
    n = len(sel)

    def plan(refs, pos):
        x, y, c = pos
        remote = []
        for j, i in enumerate(sel):
            for s, (dx, dy, _) in enumerate(FLIPS):
                qq = _chip(1 - x if dx else x, 1 - y if dy else y)
                remote.append(((dx, dy, 0), WSPECS[i].full_view(refs[j], qq), refs[n + j].at[s]))
        return remote
    return plan


def _parts_shapes(sel):
    return [jax.ShapeDtypeStruct((3,) + WSPECS[i].shard_half_shape(), BF16) for i in sel]


def finish_reduce(sums, parts, pref):
    n = len(WSPECS)
    bufs = {}
    for i, ws in enumerate(WSPECS):
        bufs[ws.group] = shard_sum(ws, sums[i], parts[i], bufs.get(ws.group), pref)

    def plan3(in_refs, ip_refs, out_refs, pos):
        c = pos[2]
        remote = []
        for ws in WSPECS:
            mine = ws.shard_view(ip_refs[ws.group], c)
            remote.append((SIB, mine, mine))
        return remote, []

    return xor_exchange("reduce_swap", [], [bufs[g] for g in sorted(bufs)], [], plan3, n)[0]


def gather_small(buf):
    flips = [(dx, dy, dc) for dx in (0, 1) for dy in (0, 1) for dc in (0, 1)][1:]

    def plan(in_refs, ip_refs, out_refs, pos):
        x, y, c = pos
        me = 4 * x + 2 * y + c
        dst = out_refs[0].at[me]
        return [(f, in_refs[0], dst) for f in flips], [(in_refs[0], dst)]

    return xor_exchange("gather_small", [buf], [], [jax.ShapeDtypeStruct((8,) + buf.shape, F32)], plan, 7, 1)[1][0]


def _pad_rows(a, rows):
    return jnp.pad(a, ((0, rows - a.shape[0]), (0, 0)))


def _row1(v):
    v = v.reshape((1, -1))
    return jnp.pad(v, ((0, 0), (0, D - v.shape[1])))


def _rms_fwd(name, h, g, tm):
    return row_call(name, lambda hh, gg: ((f_rms(hh, gg),), ()), [(h, D, 0)], [g], [(D, BF16)], [], tm)[0]


def _res_rms(h, g):
    return (h, f_rms(h, g)), ()


def _rms_bwd_epi(dhn, h, dres, g):
    _, vjp = jax.vjp(f_rms, h, g)
    dh, dg = vjp(dhn)
    dh = dh + dres
    return (dh,), (dg, jnp.sum(dh, axis=0, keepdims=True))


def _mlp_bwd(tag, dh, h, g, w1, w2, hn, z1, token=None):
    dz1 = mm("mlp_down_dx" + tag, dh, w2, "NT", BF16,
             epi=lambda acc, z, *_: acc * (2.0 * jnp.maximum(z.astype(F32), 0.0)),
             epi_ins=[(z1, "tile")] + ([] if token is None else [(token, "whole")]))
    dw2 = mm_tn("mlp_down_dw" + tag, z1, dh, a_fn=f_relu2)
    dh_in, dg, colsum = mm("mlp_up_dx" + tag, dz1, w1, "NT", [F32], epi=_rms_bwd_epi,
                           epi_ins=[(h, "tile"), (dh, "tile"), (g, "row")], accs=[(1, D), (1, D)])
    dw1 = mm_tn("mlp_up_dw" + tag, hn, dz1)
    return dh_in, dg, colsum, dw1, dw2


def kernel(x, norm_mix_g, norm_ffn_g, final_norm_g, cv_w_pw1, cv_b_pw1, cv_w_dw, cv_b_dw, cv_ln_g, cv_ln_b, cv_w_pw2, cv_b_pw2, gdn_w_in, gdn_conv_w, gdn_a_log, gdn_dt_bias, gdn_norm_g, gdn_w_out, mlp_w1, mlp_w2, loss_target, m_norm_mix_g, m_norm_ffn_g, m_final_norm_g, m_cv_w_pw1, m_cv_b_pw1, m_cv_w_dw, m_cv_b_dw, m_cv_ln_g, m_cv_ln_b, m_cv_w_pw2, m_cv_b_pw2, m_gdn_w_in, m_gdn_conv_w, m_gdn_a_log, m_gdn_dt_bias, m_gdn_norm_g, m_gdn_w_out, m_mlp_w1, m_mlp_w2, v_norm_mix_g, v_norm_ffn_g, v_final_norm_g, v_cv_w_pw1, v_cv_b_pw1, v_cv_w_dw, v_cv_b_dw, v_cv_ln_g, v_cv_ln_b, v_cv_w_pw2, v_cv_b_pw2, v_gdn_w_in, v_gdn_conv_w, v_gdn_a_log, v_gdn_dt_bias, v_gdn_norm_g, v_gdn_w_out, v_mlp_w1, v_mlp_w2):
    env = dict(locals())
    bl, S, _ = x.shape
    T = bl * S
    tm = min(256, S)
    xf = x.reshape((T, D))
    tgt = loss_target.reshape((T, D))

    chip = 2 * lax.axis_index("x") + lax.axis_index("y")
    pref = jnp.stack([chip, lax.axis_index("c")]).astype(jnp.int32)
    big = [cv_w_pw1[0], cv_w_pw2[0], gdn_w_in[0], gdn_w_out[0], mlp_w1, mlp_w2]
    placed = [place_shard(ws, big[ws.group], pref) for ws in WSPECS]
    (w_pw1,), wdw, wcv = gather_first(FIRST, placed, cv_w_dw[0], gdn_conv_w[0])
    gather_a, token_a = gather_start("A", LAYER0, placed, wcv)
    gather_b, token_b = gather_start("B", LAYER1, placed, wcv)
    wdw_p, wcv_p = _pad_rows(wdw, HB_CV), _pad_rows(wcv, HB_SC)
    alog_p = jnp.pad(gdn_a_log, ((0, 0), (0, HD - H)))
    dtb_p = jnp.pad(gdn_dt_bias, ((0, 0), (0, HD - H)))
    g_mix0, g_mix1 = norm_mix_g[0:1] + (token_a[0, 0] + token_b[0, 0]), norm_mix_g[1:2]
    g_ffn0, g_ffn1 = norm_ffn_g[0:1], norm_ffn_g[1:2]
    g_fin = final_norm_g.reshape((1, D))

    hn0 = _rms_fwd("rms_mix0", xf, g_mix0, tm)
    u = mm("cv_pw1", hn0, w_pw1, "NN", F32, epi=lambda acc, b: acc + b, epi_ins=[(cv_b_pw1, "row")])
    dwc, s_act = dwconv_fwd("cv_dwconv", u, 2 * D, wdw_p, cv_b_dw, S, KCV, HB_CV, f_glu, (f_ln_silu, (cv_ln_g, cv_ln_b)), tm)
    w_pw2, w1_0, w2_0 = gather_wait(gather_a, dwc)
    h1, hnf0 = mm("cv_pw2", s_act, w_pw2, "NN", [F32, BF16], epi=lambda acc, b, r, g: _res_rms(acc + b + r, g),
                  epi_ins=[(cv_b_pw2, "row"), (xf, "tile"), (g_ffn0, "row")])
    z1_0 = mm("mlp_up0", hnf0, w1_0, "NN", BF16)
    h2, hn2 = mm("mlp_down0", z1_0, w2_0, "NN", [F32, BF16], a_fn=f_relu2, epi=lambda acc, r, g: _res_rms(acc + r, g),
                 epi_ins=[(h1, "tile"), (g_mix1, "row")])

    w_in_sm, w_out, w1_1, w2_1 = gather_wait(gather_b, h2)
    w_in = jnp.transpose(w_in_sm, (1, 0, 2)).reshape((D, 4 * D + 2 * H))
    w_qkv, w_z = w_in[:, :3 * D], w_in[:, 3 * D:4 * D]
    w_qkvz = w_in[:, :4 * D]
    w_ab = jnp.pad(w_in[:, 4 * D:], ((0, 0), (0, HD - 2 * H)))
    pqkvz = mm("gdn_in", hn2, w_qkvz, "NN", F32)
    pab = mm("gdn_in_ab", hn2, w_ab, "NN", F32)
    cpre = dwconv_fwd("gdn_conv", pqkvz, 3 * D, wcv_p, None, S, KSC, HB_SC, lambda v: v, None, tm)[0]
    gu, gw, gqg, gkg, gqk, gt, geg = gdn_prep_fwd(cpre, pab, alog_p, dtb_p)
    o, sall = gdn_scan_fwd(gu, gw, gqg, gkg, gqk, geg, S)
    on = row_call("gdn_post", lambda oo, zz, ng: ((f_post(oo, zz, ng),), ()), [(o, D, 0), (pqkvz, D, 3)],
                  [gdn_norm_g], [(D, BF16)], [], tm)[0]
    h3, hnf1 = mm("gdn_out", on, w_out, "NN", [F32, BF16], epi=lambda acc, r, g: _res_rms(acc + r, g),
                  epi_ins=[(h2, "tile"), (g_ffn1, "row")])
    z1_1 = mm("mlp_up1", hnf1, w1_1, "NN", BF16)

    def head(acc, res, tt, gg):
        def loss_of(h_, g_):
            return 0.5 * jnp.sum(jnp.mean(jnp.square(f_rms(h_, g_) - tt), axis=-1))
        lv, (dh_, dg_) = jax.value_and_grad(loss_of, (0, 1))(acc + res, gg)
        return (dh_,), (dg_, jnp.full((1, D), lv, F32))

    dh4, dg_fin, loss_row = mm("mlp_down1", z1_1, w2_1, "NN", [F32], a_fn=f_relu2, epi=head,
                               epi_ins=[(h3, "tile"), (tgt, "tile"), (g_fin, "row")], accs=[(1, D), (1, D)])

    dh3, dg_ffn1, _, dw1_1, dw2_1 = _mlp_bwd("1", dh4, h3, g_ffn1, w1_1, w2_1, hnf1, z1_1)
    dw_out = mm_tn("gdn_out_dw", on, dh3)

    def post_bwd(don, oo, zz, ng):
        _, vjp = jax.vjp(f_post, oo, zz, ng)
        do_, dz_, dng_ = vjp(don)
        return (do_, dz_), (dng_,)

    do, dz, dng = mm("gdn_out_dx", dh3, w_out, "NT", [F32, F32], epi=post_bwd,
                     epi_ins=[(o, "tile"), (pqkvz, ("cols", 3)), (gdn_norm_g, "whole")], accs=[(1, HD)])
    du, dw, dqg, dkg, dqk, deg = gdn_scan_bwd(do, gu, gw, gqg, gkg, gqk, geg, sall, S)
    dcpre, dpab, dalog, ddtb = gdn_prep_bwd(cpre, pab, alog_p, dtb_p, gt, du, dw, dqg, dkg, dqk, deg)
    dqkv, dwcv, _ = dwconv_bwd("gdn_conv_bwd", dcpre, pqkvz, 3 * D, wcv_p, S, KSC, HB_SC, lambda v: v, None, tm)
    dhn2 = mm("gdn_in_dx_ab", dpab, w_ab, "NT", F32)
    dhn2 = mm("gdn_in_dx_z", dz, w_z, "NT", F32, epi=lambda acc, r: acc + r, epi_ins=[(dhn2, "tile")])
    dh2, dg_mix1, _ = mm("gdn_in_dx_qkv", dqkv, w_qkv, "NT", [F32],
                         epi=lambda acc, prev, hh, rr, gg: _rms_bwd_epi(acc + prev, hh, rr, gg),
                         epi_ins=[(dhn2, "tile"), (h2, "tile"), (dh3, "tile"), (g_mix1, "row")], accs=[(1, D), (1, D)])
    dw_in = jnp.concatenate([mm_tn("gdn_in_dw_qkv", hn2, dqkv), mm_tn("gdn_in_dw_z", hn2, dz),
                             mm_tn("gdn_in_dw_ab", hn2, dpab)[:, :2 * H]], axis=1)

    dw_in_sm = jnp.transpose(dw_in.reshape((D, 4, D + 4)), (1, 0, 2))
    reduce_b, rtoken_b = reduce_start("B", LAYER1, [dw_in_sm, dw_out, dw1_1, dw2_1], pref)

    dh1, dg_ffn0, db_pw2, dw1_0, dw2_0 = _mlp_bwd("0", dh2, h1, g_ffn0, w1_0, w2_0, hnf0, z1_0, rtoken_b)
    reduce_a, rtoken_a = reduce_start("A", LAYER0[1:], [dw1_0, dw2_0], pref)
    dw_pw2 = mm_tn("cv_pw2_dw", s_act, dh1)

    def ln_bwd(ds, xx, gg, bb, *_):
        _, vjp = jax.vjp(f_ln_silu, xx, gg, bb)
        dx_, dg_, db_ = vjp(ds)
        return (dx_,), (dg_, db_, jnp.sum(dx_, axis=0, keepdims=True))

    ddw, dln_g, dln_b, db_dw = mm("cv_pw2_dx", dh1, w_pw2, "NT", [F32], epi=ln_bwd,
                                  epi_ins=[(dwc, "tile"), (cv_ln_g, "row"), (cv_ln_b, "row"), (rtoken_a, "whole")],
                                  accs=[(1, D)] * 3)
    du_cv, dwdw, db_pw1 = dwconv_bwd("cv_dwconv_bwd", ddw, u, 2 * D, wdw_p, S, KCV, HB_CV, f_glu, _glu_bwd, tm)
    dw_pw1 = mm_tn("cv_pw1_dw", hn0, du_cv)
    grad_x, dg_mix0, _ = mm("cv_pw1_dx", du_cv, w_pw1, "NT", [F32], epi=_rms_bwd_epi,
                            epi_ins=[(xf, "tile"), (dh1, "tile"), (g_mix0, "row")], accs=[(1, D), (1, D)])

    last = FIRST + LAYER0[:1]
    sums_l = chip_sums("0", last, [dw_pw1, dw_pw2], pref)
    plan_l = _plan_reduce_chips(last)
    _, parts_l = xor_exchange("reduce_chips0", sums_l, [], _parts_shapes(last),
                              lambda ins_, ip_, outs_, pos: (plan_l(list(ins_) + list(outs_), pos), []), 3 * len(last))
    sums, parts = [None] * len(WSPECS), [None] * len(WSPECS)
    for sel, sums_s, parts_s in ((last, sums_l, parts_l), reduce_wait(reduce_a, grad_x), reduce_wait(reduce_b, grad_x)):
        for j, i in enumerate(sel):
            sums[i], parts[i] = sums_s[j], parts_s[j]
    g_pw1, g_pw2, g_in, g_out, g_w1, g_w2 = finish_reduce(sums, parts, pref)

    small = jnp.concatenate([
        dg_mix0, dg_mix1, dg_ffn0, dg_ffn1, dg_fin, db_pw1.reshape((2, D)), db_dw, dln_g, dln_b, db_pw2,
        _row1(dalog[:, :H]), _row1(ddtb[:, :H]), _row1(dng), loss_row, jnp.zeros((1, D), F32),
        dwdw, dwcv[:KSC].reshape((3 * KSC, D)), jnp.zeros((NSMALL - 48 - 3 * KSC, D), F32)], axis=0)
    small_all = gather_small(small)

    def pack(a, b, c_, d, e, f, g_, h_, i_, j_, k_):
        return jnp.concatenate([a, b, c_.reshape((1, D)), d.reshape((2, D)), e, f, g_, h_, _row1(i_), _row1(j_), _row1(k_),
                                jnp.zeros((2, D), F32)], axis=0)

    order = lambda p: (p + "norm_mix_g", p + "norm_ffn_g", p + "final_norm_g", p + "cv_b_pw1", p + "cv_b_dw", p + "cv_ln_g",
                       p + "cv_ln_b", p + "cv_b_pw2", p + "gdn_a_log", p + "gdn_dt_bias", p + "gdn_norm_g")
    w16, m16, v16 = (pack(*[env[nm] for nm in order(p)]) for p in ("", "m_", "v_"))

    def small_step(ga, ww, mm_, vv):
        gsum = ga[0]
        for dev in range(1, 8):
            gsum = gsum + ga[dev]
        delta, m2, v2 = f_adamw(ww, gsum[:16], mm_, vv)
        return gsum, delta, m2, v2

    def small_body(ga_ref, w_ref, m_ref, v_ref, g_out, d_out, m_out, v_out):
        gsum, delta, m2, v2 = small_step(ga_ref[...], w_ref[...], m_ref[...], v_ref[...])
        g_out[...] = gsum
        d_out[...] = delta
        m_out[...] = m2
        v_out[...] = v2

    vm = pl.BlockSpec(memory_space=pltpu.VMEM)
    sg, sd, sm, sv = pl.pallas_call(
        small_body, name="adamw_small", in_specs=[vm] * 4, out_specs=[vm] * 4,
        out_shape=[jax.ShapeDtypeStruct((NSMALL, D), F32)] + [jax.ShapeDtypeStruct((16, D), F32)] * 3)(small_all, w16, m16, v16)

    def unpack(b):
        return (b[0:2], b[2:4], b[4], b[5:7].reshape((1, 2 * D)), b[7:8], b[8:9], b[9:10], b[10:11],
                b[11:12, :H], b[12:13, :H], b[13:14, :HD])

    loss = sg[14, 0]
    g_dw = lax.dynamic_slice(sg[16:16 + KCV], (0, chip * (D // 4)), (KCV, D // 4))
    g_cv = lax.dynamic_slice(sg[48:48 + 3 * KSC].reshape((KSC, 3 * D)), (0, chip * (3 * D // 4)), (KSC, 3 * D // 4))

    def adamw(name, w, g, m, v):
        lead = w.shape[:-2]
        if len(lead) == 1 and lead[0] == 1:
            d, m2, v2 = ew_call(name, f_adamw, [w[0], g.reshape(w.shape[1:]), m[0], v[0]], 3)
            return g.reshape(w.shape), d[None], m2[None], v2[None]
        return (g.reshape(w.shape),) + tuple(ew_call(name, f_adamw, [w, g.reshape(w.shape), m, v], 3))

    res = {
        "cv_w_pw1": adamw("adamw_pw1", cv_w_pw1, g_pw1, m_cv_w_pw1, v_cv_w_pw1),
        "cv_w_dw": adamw("adamw_dw", cv_w_dw, g_dw, m_cv_w_dw, v_cv_w_dw),
        "cv_w_pw2": adamw("adamw_pw2", cv_w_pw2, g_pw2, m_cv_w_pw2, v_cv_w_pw2),
        "gdn_w_in": adamw("adamw_win", gdn_w_in, g_in, m_gdn_w_in, v_gdn_w_in),
        "gdn_conv_w": adamw("adamw_cvw", gdn_conv_w, g_cv, m_gdn_conv_w, v_gdn_conv_w),
        "gdn_w_out": adamw("adamw_wout", gdn_w_out, g_out, m_gdn_w_out, v_gdn_w_out),
        "mlp_w1": adamw("adamw_w1", mlp_w1, g_w1, m_mlp_w1, v_mlp_w1),
        "mlp_w2": adamw("adamw_w2", mlp_w2, g_w2, m_mlp_w2, v_mlp_w2),
    }
    names = ("norm_mix_g", "norm_ffn_g", "final_norm_g", "cv_b_pw1", "cv_b_dw", "cv_ln_g", "cv_ln_b", "cv_b_pw2",
             "gdn_a_log", "gdn_dt_bias", "gdn_norm_g")
    for nm, gg, dd, mm_, vv in zip(names, unpack(sg), unpack(sd), unpack(sm), unpack(sv)):
        res[nm] = (gg, dd, mm_, vv)
    weights = ("norm_mix_g", "norm_ffn_g", "final_norm_g", "cv_w_pw1", "cv_b_pw1", "cv_w_dw", "cv_b_dw", "cv_ln_g",
               "cv_ln_b", "cv_w_pw2", "cv_b_pw2", "gdn_w_in", "gdn_conv_w", "gdn_a_log", "gdn_dt_bias", "gdn_norm_g",
               "gdn_w_out", "mlp_w1", "mlp_w2")
    outs = [loss, grad_x.reshape(x.shape)]
    for kind in range(4):
        outs += [res[nm][kind] for nm in weights]
    return tuple(outs)
```

```python
import functools

import jax
import jax.numpy as jnp
from jax import lax
from jax.experimental import pallas as pl
from jax.experimental.pallas import tpu as pltpu

F32, BF16 = jnp.float32, jnp.bfloat16
D = 1024
H = 8
HD = 128
CH = 64
DFF = 4 * D
KCV, HB_CV = 31, 32
KSC, HB_SC = 4, 8
EPS = 1e-6
LR, B1, B2, EPS_A, WD, STEP = 0.001, 0.9, 0.999, 1e-08, 0.01, 10
VMEM_LIMIT = 56 * 1024 * 1024
SUB = 32
NSMALL = 64
MESH = pl.DeviceIdType.MESH


def _cp(*sem):
    return pltpu.CompilerParams(dimension_semantics=sem, vmem_limit_bytes=VMEM_LIMIT)


def f_rms(h, g):
    return h * lax.rsqrt(jnp.mean(h * h, axis=-1, keepdims=True) + EPS) * g


def f_silu(x):
    return x * jax.nn.sigmoid(x)


def f_glu(u):
    return u[:, :D] * jax.nn.sigmoid(u[:, D:])


def f_ln_silu(x, g, b):
    mu = jnp.mean(x, axis=-1, keepdims=True)
    xc = x - mu
    y = xc * lax.rsqrt(jnp.mean(xc * xc, axis=-1, keepdims=True) + EPS)
    return f_silu(y * g + b)


def f_relu2(z):
    r = jnp.maximum(z.astype(F32), 0.0)
    return r * r


def f_post(o, z, ng):
    outs = []
    for h in range(H):
        oh = o[:, h * HD:(h + 1) * HD]
        y = oh * lax.rsqrt(jnp.mean(oh * oh, axis=-1, keepdims=True) + EPS) * ng
        outs.append(y * f_silu(z[:, h * HD:(h + 1) * HD]))
    return jnp.concatenate(outs, axis=1)


def f_adamw(w, g, m, v):
    m2 = B1 * m + (1.0 - B1) * g
    v2 = B2 * v + (1.0 - B2) * (g * g)
    m_hat = m2 / (1.0 - B1 ** STEP)
    v_hat = v2 / (1.0 - B2 ** STEP)
    delta = -LR * (m_hat / (jnp.sqrt(v_hat) + EPS_A) + WD * w)
    return delta, m2, v2


def _dot_raw(a, b, mode):
    dims = {"NN": ((1,), (0,)), "NT": ((1,), (1,)), "TN": ((0,), (0,))}[mode]
    return lax.dot_general(a.astype(BF16), b.astype(BF16), (dims, ((), ())), preferred_element_type=F32)


@functools.partial(jax.custom_vjp, nondiff_argnums=(2,))
def _dot_vjp(a, b, mode):
    return _dot_raw(a, b, mode)


def _dot_fwd(a, b, mode):
    return _dot_raw(a, b, mode), (a, b)


def _dot_bwd(mode, res, dc):
    a, b = res
    if mode == "NN":
        return _dot_vjp(dc, b, "NT"), _dot_vjp(a, dc, "TN")
    if mode == "NT":
        return _dot_vjp(dc, b, "NN"), _dot_vjp(dc, a, "TN")
    return _dot_vjp(b, dc, "NT"), _dot_vjp(a, dc, "NN")


_dot_vjp.defvjp(_dot_fwd, _dot_bwd)


def _split(x):
    xh = x.astype(BF16)
    return xh, (x - xh.astype(F32)).astype(BF16)


def _dot_split(xs, ys):
    (xh, xl), (yh, yl) = xs, ys
    return _dot_raw(xh, yh, "NN") + (_dot_raw(xh, yl, "NN") + _dot_raw(xl, yh, "NN"))


def _tril_inverse(a_list):
    ri = lax.broadcasted_iota(jnp.int32, (CH, CH), 0)
    ci = lax.broadcasted_iota(jnp.int32, (CH, CH), 1)
    eye = (ri == ci).astype(F32)
    ts = None
    for lvl in range(CH.bit_length() - 1):
        same_pair = jnp.right_shift(ri, lvl + 1) == jnp.right_shift(ci, lvl + 1)
        quarter = (jnp.bitwise_and(jnp.right_shift(ri, lvl), 1) == 1) & (jnp.bitwise_and(jnp.right_shift(ci, lvl), 1) == 0)
        offs = [jnp.where(same_pair & quarter, a, 0.0) for a in a_list]
        if ts is None:
            ts = [eye - off for off in offs]
            continue
        tsp = [_split(t) for t in ts]
        mids = [_dot_split(tp, _split(off)) for tp, off in zip(tsp, offs)]
        ts = [t - _dot_split(_split(m), tp) for t, m, tp in zip(ts, mids, tsp)]
    return ts


@jax.custom_vjp
def _stored_solve(a, t, rhs):
    return _dot_raw(t, rhs, "NN")


def _stored_solve_fwd(a, t, rhs):
    sol = _dot_raw(t, rhs, "NN")
    return sol, (t, sol)


def _stored_solve_bwd(res, g):
    t, sol = res
    g_rhs = _dot_vjp(t, g, "TN")
    return -_dot_vjp(g_rhs, sol, "NT"), jnp.zeros_like(t), g_rhs


_stored_solve.defvjp(_stored_solve_fwd, _stored_solve_bwd)


def _lane_pick(row, idx, width):
    sel = lax.broadcasted_iota(jnp.int32, (1, width), 1) == idx
    return jnp.sum(jnp.where(sel, row, 0.0), axis=1, keepdims=True)


def f_prep(cqs, cks, cvs, araws, braws, alogs, dtbs, t_stored, dot):
    ri = lax.broadcasted_iota(jnp.int32, (CH, CH), 0)
    ci = lax.broadcasted_iota(jnp.int32, (CH, CH), 1)
    eye = (ri == ci).astype(F32)
    low = (ri >= ci).astype(F32)
    last = lax.broadcasted_iota(jnp.int32, (CH, 1), 0) == CH - 1
    nh = range(len(cqs))
    qs, ks, vbs, kbs, gcs, decays = [], [], [], [], [], []
    for h in nh:
        q = f_silu(cqs[h])
        qs.append(q * lax.rsqrt(jnp.sum(q * q, axis=-1, keepdims=True) + 1e-6) * (HD ** -0.5))
        k = f_silu(cks[h])
        k = k * lax.rsqrt(jnp.sum(k * k, axis=-1, keepdims=True) + 1e-6)
        ks.append(k)
        beta = jax.nn.sigmoid(braws[h])
        sp_in = araws[h] + dtbs[h]
        softplus = jnp.maximum(sp_in, 0.0) + jnp.log(1.0 + jnp.exp(-jnp.abs(sp_in)))
        g = -jnp.exp(alogs[h]) * softplus
        g_row = jnp.sum(eye * g, axis=0, keepdims=True)
        gc = jnp.sum(low * g_row, axis=1, keepdims=True)
        gc_row = jnp.sum(eye * gc, axis=0, keepdims=True)
        gcs.append(gc)
        decays.append(jnp.exp(jnp.where(ri >= ci, gc - gc_row, -1e30)))
        vbs.append(f_silu(cvs[h]) * beta)
        kbs.append(k * beta)
    kks = [dot(kbs[h], ks[h], "NT") for h in nh]
    a_list = [jnp.where(ri > ci, kks[h] * decays[h], 0.0) for h in nh]
    if t_stored is None:
        ts = _tril_inverse(a_list)
        solve = lambda h, rhs: dot(ts[h], rhs, "NN")
    else:
        ts = t_stored
        solve = lambda h, rhs: _stored_solve(a_list[h], t_stored[h], rhs)
    egcs = [jnp.exp(gc) for gc in gcs]
    us = [solve(h, vbs[h]) for h in nh]
    ws = [solve(h, kbs[h] * egcs[h]) for h in nh]
    qks = [dot(qs[h], ks[h], "NT") * decays[h] for h in nh]
    qgs = [qs[h] * egcs[h] for h in nh]
    gls = [jnp.sum(jnp.where(last, gc, 0.0), axis=0, keepdims=True) for gc in gcs]
    kgs = [ks[h] * jnp.exp(gls[h] - gcs[h]) for h in nh]
    egs = [jnp.exp(gl) * jnp.ones((1, HD), F32) for gl in gls]
    return us, ws, qks, qgs, kgs, egs, ts


def f_scan(ss, us, ws, qgs, kgs, qks, egs, dot):
    nh = range(len(ss))
    ws_s = [dot(ws[h], ss[h], "NN") for h in nh]
    qs_s = [dot(qgs[h], ss[h], "NN") for h in nh]
    vns = [us[h] - ws_s[h] for h in nh]
    os_ = [qs_s[h] + dot(qks[h], vns[h], "NN") for h in nh]
    s2s = [ss[h] * egs[h] + dot(kgs[h], vns[h], "TN") for h in nh]
    return os_, s2s


def row_call(name, fn, rows, pars, out_rows, out_accs, tm):
    T = rows[0][0].shape[0]
    n_r, n_p, n_o = len(rows), len(pars), len(out_rows)
    in_specs = [pl.BlockSpec((tm, w), functools.partial(lambda i, cb: (i, cb), cb=cb)) for (_, w, cb) in rows]
    in_specs += [pl.BlockSpec(p.shape, functools.partial(lambda i, nd: (0,) * nd, nd=p.ndim)) for p in pars]
    out_specs = [pl.BlockSpec((tm, w), lambda i: (i, 0)) for (w, _) in out_rows]
    out_specs += [pl.BlockSpec(s, lambda i: (0, 0)) for s in out_accs]
    out_shape = [jax.ShapeDtypeStruct((T, w), dt) for (w, dt) in out_rows]
    out_shape += [jax.ShapeDtypeStruct(s, F32) for s in out_accs]

    def body(*refs):
        rin, pin = refs[:n_r], refs[n_r:n_r + n_p]
        rout, aout = refs[n_r + n_p:n_r + n_p + n_o], refs[n_r + n_p + n_o:]
        if aout:
            @pl.when(pl.program_id(0) == 0)
            def _():
                for a in aout:
                    a[...] = jnp.zeros(a.shape, F32)
        pv = [p[...] for p in pin]

        def step(r, carry):
            sl = pl.ds(pl.multiple_of(r * SUB, SUB), SUB)
            outs, accs = fn(*[x[sl, :] for x in rin], *pv)
            for o, val in zip(rout, outs):
                o[sl, :] = val.astype(o.dtype)
            for a, val in zip(aout, accs):
                a[...] += val
            return carry

        lax.fori_loop(0, tm // SUB, step, 0)

    return pl.pallas_call(body, name=name, grid=(T // tm,), in_specs=in_specs, out_specs=out_specs,
                          out_shape=out_shape, compiler_params=_cp("arbitrary"))(*[r[0] for r in rows], *pars)


EW_TILE_ELEMS = 256 * 1024


def _ew_rows(R, Cc):
    if R * Cc <= EW_TILE_ELEMS or R % 8:
        return R
    tr = 8
    while tr * 2 * Cc <= EW_TILE_ELEMS and R % (tr * 2) == 0:
        tr *= 2
    return tr


def ew_call(name, fn, ins, n_out):
    shape = ins[0].shape
    lead = shape[:-2]
    R, Cc = shape[-2:]
    tr = _ew_rows(R, Cc)
    grid = lead + (R // tr,)
    nl = len(lead)
    spec = pl.BlockSpec((None,) * nl + (tr, Cc), lambda *idx: idx + (0,))

    def body(*refs):
        outs = fn(*[r[...] for r in refs[:len(ins)]])
        for o, val in zip(refs[len(ins):], outs):
            o[...] = val

    return pl.pallas_call(body, name=name, grid=grid, in_specs=[spec] * len(ins), out_specs=[spec] * n_out,
                          out_shape=[jax.ShapeDtypeStruct(shape, F32)] * n_out,
                          compiler_params=_cp(*(("arbitrary",) * len(grid))))(*ins)


MM_RESIDENT_BYTES = 8 * 1024 * 1024


def mm(name, a, b, mode, out_dtype, a_fn=None, epi=None, epi_ins=(), accs=(), tm=512):
    sub_epi = epi is not None and isinstance(out_dtype, (list, tuple))
    M, K = a.shape
    N = b.shape[1] if mode == "NN" else b.shape[0]
    tn = N if K * N * 2 <= MM_RESIDENT_BYTES else min(N, 1024)
    tm = min(tm if tn <= 1024 else tm // 2, M)
    multi = isinstance(out_dtype, (list, tuple))
    dts = list(out_dtype) if multi else [out_dtype]
    n_e, n_o = len(epi_ins), len(dts)
    in_specs = [pl.BlockSpec((tm, K), lambda j, i: (i, 0)),
                pl.BlockSpec((K, tn), lambda j, i: (0, j)) if mode == "NN" else pl.BlockSpec((tn, K), lambda j, i: (j, 0))]
    row_kinds = []
    for (arr, kind) in epi_ins:
        if kind == "tile" or isinstance(kind, tuple):
            cb = kind[1] if isinstance(kind, tuple) else 0
            in_specs.append(pl.BlockSpec((tm, tn), functools.partial(lambda j, i, cb: (i, cb + j), cb=cb)))
            row_kinds.append(True)
        elif kind == "row":
            in_specs.append(pl.BlockSpec((1, tn), lambda j, i: (0, j)))
            row_kinds.append(False)
        else:
            in_specs.append(pl.BlockSpec(arr.shape, lambda j, i: (0, 0)))
            row_kinds.append(False)

    def body(a_ref, b_ref, *rest):
        e_refs, o_refs, acc_refs = rest[:n_e], rest[n_e:n_e + n_o], rest[n_e + n_o:n_e + n_o + len(accs)]
        av = a_ref[...]
        if a_fn is not None:
            av = a_fn(av)
        res = _dot_raw(av, b_ref[...], mode)
        if epi is None or not sub_epi:
            if epi is not None:
                res = epi(res, *[r[...] for r in e_refs])
            o_refs[0][...] = res.astype(o_refs[0].dtype)
            return
        prod = rest[-1]
        prod[...] = res
        if acc_refs:
            @pl.when((pl.program_id(0) == 0) & (pl.program_id(1) == 0))
            def _():
                for r in acc_refs:
                    r[...] = jnp.zeros(r.shape, F32)
        small = [None if is_rows else r[...] for r, is_rows in zip(e_refs, row_kinds)]

        def step(k, carry):
            sl = pl.ds(pl.multiple_of(k * SUB, SUB), SUB)
            out = epi(prod[sl, :], *[r[sl, :] if is_rows else sm for r, is_rows, sm in zip(e_refs, row_kinds, small)])
            tiles, contribs = out if multi else ((out,), ())
            for r, t in zip(o_refs, tiles):
                r[sl, :] = t.astype(r.dtype)
            for r, t in zip(acc_refs, contribs):
                r[...] += t
            return carry

        lax.fori_loop(0, tm // SUB, step, 0)

    out_specs = [pl.BlockSpec((tm, tn), lambda j, i: (i, j))] * n_o + [pl.BlockSpec(s, lambda j, i: (0, 0)) for s in accs]
    out_shape = [jax.ShapeDtypeStruct((M, N), dt) for dt in dts] + [jax.ShapeDtypeStruct(s, F32) for s in accs]
    res = pl.pallas_call(body, name=name, grid=(N // tn, M // tm), in_specs=in_specs, out_specs=out_specs,
                         out_shape=out_shape, scratch_shapes=[pltpu.VMEM((tm, tn), F32)] if sub_epi else [],
                         compiler_params=_cp("arbitrary", "arbitrary"))(a, b, *[e[0] for e in epi_ins])
    return res if multi else res[0]


def mm_tn(name, a, g, a_fn=None, a_cols=None, tt=1024):
    T = a.shape[0]
    ka, acb = (a.shape[1], 0) if a_cols is None else a_cols
    N = g.shape[1]
    tt = min(tt, T)
    tka, tn = min(ka, 1024), min(N, 1024)
    nkb = ka // tka

    def body(a_ref, g_ref, o_ref):
        @pl.when(pl.program_id(2) == 0)
        def _():
            o_ref[...] = jnp.zeros(o_ref.shape, F32)
        av = a_ref[...]
        if a_fn is not None:
            av = a_fn(av)
        o_ref[...] += _dot_raw(av, g_ref[...], "TN")

    return pl.pallas_call(body, name=name, grid=(nkb, N // tn, T // tt),
                          in_specs=[pl.BlockSpec((tt, tka), lambda ia, j, t: (t, acb * nkb + ia)),
                                    pl.BlockSpec((tt, tn), lambda ia, j, t: (t, j))],
                          out_specs=pl.BlockSpec((tka, tn), lambda ia, j, t: (ia, j)),
                          out_shape=jax.ShapeDtypeStruct((ka, N), F32),
                          compiler_params=_cp("arbitrary", "arbitrary", "arbitrary"))(a, g)


SUBLANES = 8


class _RowShifts:
    def __init__(self, src, shifted, nrows, reuse):
        self.src, self.shifted, self.reuse = src, shifted, reuse
        if reuse:
            for ph in range(1, SUBLANES):
                for r0 in range(0, nrows - SUBLANES, SUB):
                    n = min(SUB, nrows - SUBLANES - r0)
                    shifted[ph - 1, r0:r0 + n, :] = src[r0 + ph:r0 + ph + n, :]

    def window(self, off, cols):
        ph = off % SUBLANES
        if not self.reuse or ph == 0:
            return self.src[off:off + SUB, cols]
        return self.shifted[ph - 1, off - ph:off - ph + SUB, cols]


def _shift_scratch(nrows, C, reuse):
    return [pltpu.VMEM((SUBLANES - 1, nrows - SUBLANES, C), F32)] if reuse else []


def dwconv_fwd(name, x, xw, w_pad, bias, S, K, HB, pre, post, tm):
    T = x.shape[0]
    C = w_pad.shape[1]
    nb, per_seq = tm // HB, S // tm
    has_b, has_post = bias is not None, post is not None
    reuse = K > SUBLANES

    def body(*refs):
        x_ref, xp_ref, w_ref = refs[:3]
        pos = 3
        b_ref = refs[pos] if has_b else None
        pos += has_b
        ppars = refs[pos:pos + (len(post[1]) if has_post else 0)]
        pos += len(ppars)
        c_ref = refs[pos]
        s_ref = refs[pos + 1] if has_post else None
        ext = refs[pos + 1 + has_post]
        first = (pl.program_id(0) % per_seq) == 0
        ext[0:HB, :] = jnp.where(first, 0.0, pre(xp_ref[...]))
        for r in range(tm // SUB):
            ext[HB + r * SUB:HB + (r + 1) * SUB, :] = pre(x_ref[r * SUB:(r + 1) * SUB, :])
        rows_of = _RowShifts(ext, refs[-1] if reuse else None, HB + tm, reuse)
        pv = [p[...] for p in ppars]
        assert not has_post or C == D
        for r in range(tm // SUB):
            for c0 in range(0, C, D):
                cols = slice(c0, c0 + D)
                acc = jnp.zeros((SUB, D), F32)
                if has_b:
                    acc = acc + b_ref[:, cols]
                for k in range(K):
                    acc = acc + w_ref[k:k + 1, cols] * rows_of.window(HB + r * SUB - (K - 1) + k, cols)
                c_ref[r * SUB:(r + 1) * SUB, cols] = acc
                if has_post:
                    s_ref[r * SUB:(r + 1) * SUB, :] = post[0](acc, *pv).astype(BF16)

    ins = [x, x, w_pad] + ([bias] if has_b else []) + (list(post[1]) if has_post else [])
    in_specs = [pl.BlockSpec((tm, xw), lambda i: (i, 0)),
                pl.BlockSpec((HB, xw), lambda i: (jnp.maximum(i * nb - 1, 0), 0)),
                pl.BlockSpec(w_pad.shape, lambda i: (0, 0))]
    in_specs += [pl.BlockSpec(p.shape, lambda i: (0, 0)) for p in ins[3:]]
    out_specs = [pl.BlockSpec((tm, C), lambda i: (i, 0))] * (1 + has_post)
    out_shape = [jax.ShapeDtypeStruct((T, C), F32)] + ([jax.ShapeDtypeStruct((T, C), BF16)] if has_post else [])
    return pl.pallas_call(body, name=name, grid=(T // tm,), in_specs=in_specs, out_specs=out_specs, out_shape=out_shape,
                          scratch_shapes=[pltpu.VMEM((HB + tm, C), F32)] + _shift_scratch(HB + tm, C, reuse),
                          compiler_params=_cp("arbitrary"))(*ins)


def dwconv_bwd(name, g, x, xw, w_pad, S, K, HB, pre, pre_bwd, tm):
    T = g.shape[0]
    C = w_pad.shape[1]
    nb, per_seq = tm // HB, S // tm
    nblk = T // HB

    reuse = K > SUBLANES

    def body(g_ref, gn_ref, x_ref, xp_ref, w_ref, dx_ref, dw_ref, dbx_ref, extg, exta, *shift_refs):
        i = pl.program_id(0)
        first = (i % per_seq) == 0
        last = (i % per_seq) == per_seq - 1

        @pl.when(i == 0)
        def _():
            dw_ref[...] = jnp.zeros(dw_ref.shape, F32)
            dbx_ref[...] = jnp.zeros(dbx_ref.shape, F32)

        extg[tm:tm + HB, :] = jnp.where(last, 0.0, gn_ref[...])
        exta[0:HB, :] = jnp.where(first, 0.0, pre(xp_ref[...]))
        for r in range(tm // SUB):
            extg[r * SUB:(r + 1) * SUB, :] = g_ref[r * SUB:(r + 1) * SUB, :]
            exta[HB + r * SUB:HB + (r + 1) * SUB, :] = pre(x_ref[r * SUB:(r + 1) * SUB, :])
        assert pre_bwd is None or C == D
        g_rows = _RowShifts(extg, shift_refs[0] if reuse else None, tm + HB, reuse)
        a_rows = _RowShifts(exta, shift_refs[1] if reuse else None, HB + tm, reuse)
        for r in range(tm // SUB):
            rows = slice(r * SUB, (r + 1) * SUB)
            for c0 in range(0, C, D):
                cols = slice(c0, c0 + D)
                acc = jnp.zeros((SUB, D), F32)
                for k in range(K):
                    acc = acc + w_ref[k:k + 1, cols] * g_rows.window(r * SUB + (K - 1) - k, cols)
                if pre_bwd is None:
                    dx_ref[rows, cols] = acc
                    dbx_ref[:, cols] += jnp.sum(acc, axis=0, keepdims=True)
                else:
                    dx = pre_bwd(x_ref[rows, :], acc)
                    dx_ref[rows, :] = dx
                    dbx_ref[...] += jnp.sum(dx, axis=0, keepdims=True)
        for k in range(K):
            for c0 in range(0, C, D):
                cols = slice(c0, c0 + D)
                p = jnp.zeros((SUB, D), F32)
                for r in range(tm // SUB):
                    p = p + extg[r * SUB:(r + 1) * SUB, cols] * a_rows.window(HB + r * SUB - (K - 1) + k, cols)
                dw_ref[k:k + 1, cols] += jnp.sum(p, axis=0, keepdims=True)

    in_specs = [pl.BlockSpec((tm, C), lambda i: (i, 0)),
                pl.BlockSpec((HB, C), lambda i: (jnp.minimum((i + 1) * nb, nblk - 1), 0)),
                pl.BlockSpec((tm, xw), lambda i: (i, 0)),
                pl.BlockSpec((HB, xw), lambda i: (jnp.maximum(i * nb - 1, 0), 0)),
                pl.BlockSpec(w_pad.shape, lambda i: (0, 0))]
    out_specs = [pl.BlockSpec((tm, xw), lambda i: (i, 0)), pl.BlockSpec((HB, C), lambda i: (0, 0)),
                 pl.BlockSpec((1, xw), lambda i: (0, 0))]
    out_shape = [jax.ShapeDtypeStruct((T, xw), F32), jax.ShapeDtypeStruct((HB, C), F32), jax.ShapeDtypeStruct((1, xw), F32)]
    return pl.pallas_call(body, name=name, grid=(T // tm,), in_specs=in_specs, out_specs=out_specs, out_shape=out_shape,
                          scratch_shapes=[pltpu.VMEM((tm + HB, C), F32), pltpu.VMEM((HB + tm, C), F32)]
                          + _shift_scratch(tm + HB, C, reuse) * 2,
                          compiler_params=_cp("arbitrary"))(g, g, x, x, w_pad)


def _glu_bwd(u, da):
    u1, sg = u[:, :D], jax.nn.sigmoid(u[:, D:])
    return jnp.concatenate([da * sg, da * u1 * sg * (1.0 - sg)], axis=1)


def _head_cols(ref, h, base=0):
    return ref[:, base + h * HD:base + (h + 1) * HD]


def _prep_inputs(c_ref, ab, al, dt):
    hs = range(H)
    return ([_head_cols(c_ref, h) for h in hs], [_head_cols(c_ref, h, D) for h in hs],
            [_head_cols(c_ref, h, 2 * D) for h in hs], [_lane_pick(ab, h, HD) for h in hs],
            [_lane_pick(ab, H + h, HD) for h in hs], [_lane_pick(al, h, HD) for h in hs],
            [_lane_pick(dt, h, HD) for h in hs])


def gdn_prep_fwd(cpre, pab, alog, dtb):
    T = cpre.shape[0]
    nc = T // CH

    def body(c_ref, ab_ref, al_ref, dt_ref, u_ref, w_ref, qg_ref, kg_ref, qk_ref, t_ref, eg_ref):
        us, ws, qks, qgs, kgs, egs, ts = f_prep(*_prep_inputs(c_ref, ab_ref[...], al_ref[...], dt_ref[...]), None, _dot_raw)
        for h in range(H):
            cols = slice(h * HD, (h + 1) * HD)
            u_ref[:, cols] = us[h]
            w_ref[:, cols] = ws[h].astype(BF16)
            qg_ref[:, cols] = qgs[h].astype(BF16)
            kg_ref[:, cols] = kgs[h].astype(BF16)
            qk_ref[0, h] = qks[h].astype(BF16)
            t_ref[0, h] = ts[h].astype(BF16)
            eg_ref[0, h:h + 1, :] = egs[h]

    row = lambda w: pl.BlockSpec((CH, w), lambda n: (n, 0))
    par = pl.BlockSpec((1, HD), lambda n: (0, 0))
    mat = pl.BlockSpec((1, H, CH, CH), lambda n: (n, 0, 0, 0))
    return pl.pallas_call(
        body, name="gdn_prep_fwd", grid=(nc,), in_specs=[row(3 * D), row(HD), par, par],
        out_specs=[row(D), row(D), row(D), row(D), mat, mat, pl.BlockSpec((1, H, HD), lambda n: (n, 0, 0))],
        out_shape=[jax.ShapeDtypeStruct((T, D), F32)] + [jax.ShapeDtypeStruct((T, D), BF16)] * 3
        + [jax.ShapeDtypeStruct((nc, H, CH, CH), BF16)] * 2 + [jax.ShapeDtypeStruct((nc, H, HD), F32)],
        compiler_params=_cp("arbitrary"))(cpre, pab, alog, dtb)


def gdn_prep_bwd(cpre, pab, alog, dtb, tmat, du, dw, dqg, dkg, dqk, deg):
    T = cpre.shape[0]
    nc = T // CH

    def body(c_ref, ab_ref, al_ref, dt_ref, t_ref, du_ref, dw_ref, dqg_ref, dkg_ref, dqk_ref, deg_ref,
             dc_ref, dab_ref, dal_ref, ddt_ref):
        @pl.when(pl.program_id(0) == 0)
        def _():
            dal_ref[...] = jnp.zeros(dal_ref.shape, F32)
            ddt_ref[...] = jnp.zeros(ddt_ref.shape, F32)

        lane = lax.broadcasted_iota(jnp.int32, (1, HD), 1)
        dab = jnp.zeros((CH, HD), F32)
        dal = jnp.zeros((1, HD), F32)
        ddt = jnp.zeros((1, HD), F32)
        hs = range(H)
        t_st = [t_ref[0, h].astype(F32) for h in hs]

        def fwd(*args):
            return tuple(f_prep(*args, t_st, _dot_vjp)[:6])

        _, vjp = jax.vjp(fwd, *_prep_inputs(c_ref, ab_ref[...], al_ref[...], dt_ref[...]))
        dcqs, dcks, dcvs, dars, dbrs, dals, ddts = vjp((
            [_head_cols(du_ref, h) for h in hs], [_head_cols(dw_ref, h) for h in hs], [dqk_ref[0, h] for h in hs],
            [_head_cols(dqg_ref, h) for h in hs], [_head_cols(dkg_ref, h) for h in hs],
            [deg_ref[0, h:h + 1, :] for h in hs]))
        for h in hs:
            dc_ref[:, h * HD:(h + 1) * HD] = dcqs[h]
            dc_ref[:, D + h * HD:D + (h + 1) * HD] = dcks[h]
            dc_ref[:, 2 * D + h * HD:2 * D + (h + 1) * HD] = dcvs[h]
            dab = dab + jnp.where(lane == h, dars[h], 0.0) + jnp.where(lane == H + h, dbrs[h], 0.0)
            dal = dal + jnp.where(lane == h, dals[h], 0.0)
            ddt = ddt + jnp.where(lane == h, ddts[h], 0.0)
        dab_ref[...] = dab
        dal_ref[...] += dal
        ddt_ref[...] += ddt

    row = lambda w: pl.BlockSpec((CH, w), lambda n: (n, 0))
    par = pl.BlockSpec((1, HD), lambda n: (0, 0))
    mat = pl.BlockSpec((1, H, CH, CH), lambda n: (n, 0, 0, 0))
    vec = pl.BlockSpec((1, H, HD), lambda n: (n, 0, 0))
    return pl.pallas_call(
        body, name="gdn_prep_bwd", grid=(nc,),
        in_specs=[row(3 * D), row(HD), par, par, mat, row(D), row(D), row(D), row(D), mat, vec],
        out_specs=[row(3 * D), row(HD), par, par],
        out_shape=[jax.ShapeDtypeStruct((T, 3 * D), F32), jax.ShapeDtypeStruct((T, HD), F32),
                   jax.ShapeDtypeStruct((1, HD), F32), jax.ShapeDtypeStruct((1, HD), F32)],
        compiler_params=_cp("arbitrary"))(cpre, pab, alog, dtb, tmat, du, dw, dqg, dkg, dqk, deg)


def gdn_scan_fwd(u, w, qg, kg, qk, eg, S):
    T = u.shape[0]
    nc, per_seq = T // CH, S // CH

    def body(u_ref, w_ref, qg_ref, kg_ref, qk_ref, eg_ref, o_ref, sall_ref, s_ref):
        @pl.when(pl.program_id(0) % per_seq == 0)
        def _():
            s_ref[...] = jnp.zeros(s_ref.shape, F32)

        hs = range(H)
        ss = [s_ref[h] for h in hs]
        os_, s2s = f_scan(ss, [_head_cols(u_ref, h) for h in hs], [_head_cols(w_ref, h) for h in hs],
                          [_head_cols(qg_ref, h) for h in hs], [_head_cols(kg_ref, h) for h in hs],
                          [qk_ref[0, h] for h in hs], [eg_ref[0, h:h + 1, :] for h in hs], _dot_raw)
        for h in hs:
            sall_ref[0, h] = ss[h]
            o_ref[:, h * HD:(h + 1) * HD] = os_[h]
            s_ref[h] = s2s[h]

    row = pl.BlockSpec((CH, D), lambda n: (n, 0))
    return pl.pallas_call(
        body, name="gdn_scan_fwd", grid=(nc,),
        in_specs=[row, row, row, row, pl.BlockSpec((1, H, CH, CH), lambda n: (n, 0, 0, 0)),
                  pl.BlockSpec((1, H, HD), lambda n: (n, 0, 0))],
        out_specs=[row, pl.BlockSpec((1, H, HD, HD), lambda n: (n, 0, 0, 0))],
        out_shape=[jax.ShapeDtypeStruct((T, D), F32), jax.ShapeDtypeStruct((nc, H, HD, HD), F32)],
        scratch_shapes=[pltpu.VMEM((H, HD, HD), F32)], compiler_params=_cp("arbitrary"))(u, w, qg, kg, qk, eg)


def gdn_scan_bwd(do, u, w, qg, kg, qk, eg, sall, S):
    T = u.shape[0]
    nc, per_seq = T // CH, S // CH

    def body(do_ref, u_ref, w_ref, qg_ref, kg_ref, qk_ref, eg_ref, sall_ref,
             du_ref, dw_ref, dqg_ref, dkg_ref, dqk_ref, deg_ref, ds_ref):
        n = nc - 1 - pl.program_id(0)

        @pl.when(n % per_seq == per_seq - 1)
        def _():
            ds_ref[...] = jnp.zeros(ds_ref.shape, F32)

        hs = range(H)

        def fwd(*args):
            return f_scan(*args, _dot_vjp)

        _, vjp = jax.vjp(fwd, [sall_ref[0, h] for h in hs], [_head_cols(u_ref, h) for h in hs],
                         [_head_cols(w_ref, h).astype(F32) for h in hs], [_head_cols(qg_ref, h).astype(F32) for h in hs],
                         [_head_cols(kg_ref, h).astype(F32) for h in hs], [qk_ref[0, h].astype(F32) for h in hs],
                         [eg_ref[0, h:h + 1, :] for h in hs])
        dss, dus, dws, dqgs, dkgs, dqks, degs = vjp(([_head_cols(do_ref, h) for h in hs], [ds_ref[h] for h in hs]))
        for h in hs:
            cols = slice(h * HD, (h + 1) * HD)
            du_ref[:, cols] = dus[h]
            dw_ref[:, cols] = dws[h]
            dqg_ref[:, cols] = dqgs[h]
            dkg_ref[:, cols] = dkgs[h]
            dqk_ref[0, h] = dqks[h]
            deg_ref[0, h:h + 1, :] = degs[h]
            ds_ref[h] = dss[h]

    rev = lambda n: (nc - 1 - n, 0)
    row = pl.BlockSpec((CH, D), rev)
    mat = pl.BlockSpec((1, H, CH, CH), lambda n: (nc - 1 - n, 0, 0, 0))
    vec = pl.BlockSpec((1, H, HD), lambda n: (nc - 1 - n, 0, 0))
    return pl.pallas_call(
        body, name="gdn_scan_bwd", grid=(nc,),
        in_specs=[row, row, row, row, row, mat, vec, pl.BlockSpec((1, H, HD, HD), lambda n: (nc - 1 - n, 0, 0, 0))],
        out_specs=[row, row, row, row, mat, vec],
        out_shape=[jax.ShapeDtypeStruct((T, D), F32)] * 4
        + [jax.ShapeDtypeStruct((nc, H, CH, CH), F32), jax.ShapeDtypeStruct((nc, H, HD), F32)],
        scratch_shapes=[pltpu.VMEM((H, HD, HD), F32)], compiler_params=_cp("arbitrary"))(do, u, w, qg, kg, qk, eg, sall)


def xor_exchange(name, ins, inplace, out_shapes, plan, n_remote, n_local=0):
    n_in, n_ip, n_out = len(ins), len(inplace), len(out_shapes)

    def body(*refs):
        in_refs = refs[:n_in]
        ip_refs = refs[n_in + n_ip:n_in + 2 * n_ip]
        out_refs = refs[n_in + 2 * n_ip:n_in + 2 * n_ip + n_out]
        send_sems, recv_sems, loc_sems = refs[n_in + 2 * n_ip + n_out:]
        x, y, c = lax.axis_index("x"), lax.axis_index("y"), lax.axis_index("c")
        remote, local = plan(in_refs, ip_refs, out_refs, (x, y, c))
        assert len(remote) == n_remote and len(local) == n_local
        copies = []
        for k, ((dx, dy, dc), src, dst) in enumerate(remote):
            peer = (1 - x if dx else x, 1 - y if dy else y, 1 - c if dc else c)
            copies.append(pltpu.make_async_remote_copy(src_ref=src, dst_ref=dst, send_sem=send_sems.at[k],
                                                       recv_sem=recv_sems.at[k], device_id=peer, device_id_type=MESH))
        for cp in copies:
            cp.start()
        locs = [pltpu.make_async_copy(src, dst, loc_sems.at[k]) for k, (src, dst) in enumerate(local)]
        for cp in locs:
            cp.start()
        for cp in copies:
            cp.wait()
        for cp in locs:
            cp.wait()

    anyspec = pl.BlockSpec(memory_space=pl.ANY)
    res = pl.pallas_call(
        body, name=name, in_specs=[anyspec] * (n_in + n_ip), out_specs=[anyspec] * (n_ip + n_out),
        out_shape=[jax.ShapeDtypeStruct(a.shape, a.dtype) for a in inplace] + list(out_shapes),
        input_output_aliases={n_in + i: i for i in range(n_ip)},
        scratch_shapes=[pltpu.SemaphoreType.DMA((n_remote,)), pltpu.SemaphoreType.DMA((n_remote,)),
                        pltpu.SemaphoreType.DMA((max(n_local, 1),))],
        )(*ins, *inplace)
    return list(res[:n_ip]), list(res[n_ip:])


HBM_SPEC = pl.BlockSpec(memory_space=pltpu.HBM)
SEM_SPEC = pl.BlockSpec(memory_space=pltpu.SEMAPHORE)


def _flip_copies(plan, refs, send_sems, recv_sems):
    x, y, c = lax.axis_index("x"), lax.axis_index("y"), lax.axis_index("c")
    copies = []
    for k, ((dx, dy, dc), src, dst) in enumerate(plan(refs, (x, y, c))):
        peer = (1 - x if dx else x, 1 - y if dy else y, 1 - c if dc else c)
        copies.append(pltpu.make_async_remote_copy(src_ref=src, dst_ref=dst, send_sem=send_sems.at[k],
                                                   recv_sem=recv_sems.at[k], device_id=peer, device_id_type=MESH))
    return copies


def xor_start(name, arrays, plan, n_remote, after):
    n = len(arrays)

    def body(*refs):
        for cp in _flip_copies(plan, refs[:n], refs[n + 1], refs[n + 2]):
            cp.start()
        refs[-1][...] = jnp.zeros(refs[-1].shape, F32)

    res = pl.pallas_call(
        body, name=name, in_specs=[HBM_SPEC] * n + [pl.BlockSpec(memory_space=pl.ANY)],
        out_shape=(pltpu.SemaphoreType.DMA((n_remote,)), pltpu.SemaphoreType.DMA((n_remote,)),
                   *[pltpu.HBM(a.shape, a.dtype) for a in arrays], jax.ShapeDtypeStruct((8, 128), F32)),
        out_specs=(SEM_SPEC, SEM_SPEC, *([HBM_SPEC] * n), pl.BlockSpec(memory_space=pltpu.VMEM)),
        input_output_aliases={i: 2 + i for i in range(n)},
        compiler_params=pltpu.CompilerParams(has_side_effects=pltpu.SideEffectType.DATAFLOW_SIDE_EFFECTING),
    )(*[pltpu.with_memory_space_constraint(a, pltpu.HBM) for a in arrays], after)
    return res[0], res[1], list(res[2:2 + n]), res[2 + n]


def xor_wait(name, send_sems, recv_sems, arrays, plan, after):
    n = len(arrays)

    def body(*refs):
        for cp in _flip_copies(plan, refs[:n], refs[n], refs[n + 1]):
            cp.wait_send()
            cp.wait_recv()

    return list(pl.pallas_call(
        body, name=name, in_specs=[HBM_SPEC] * n + [SEM_SPEC, SEM_SPEC, pl.BlockSpec(memory_space=pl.ANY)],
        out_shape=[pltpu.HBM(a.shape, a.dtype) for a in arrays], out_specs=[HBM_SPEC] * n,
        input_output_aliases={i: i for i in range(n)},
        compiler_params=pltpu.CompilerParams(has_side_effects=pltpu.SideEffectType.DATAFLOW_SIDE_EFFECTING),
    )(*arrays, send_sems, recv_sems, after))


class WSpec:
    def __init__(self, name, full, sa, ha, group, layer=None, lead=False):
        self.name, self.full, self.sa, self.ha, self.group, self.layer, self.lead = name, full, sa, ha, group, layer, lead
        self.ws = 1 if lead else full[sa] // 4
        self.wh = full[ha] // 2

    def shard_shape(self):
        if self.lead:
            return tuple(n for a, n in enumerate(self.full) if a != self.sa)
        return tuple(self.ws if a == self.sa else n for a, n in enumerate(self.full))

    def half_full_shape(self):
        return tuple(self.wh if a == self.ha else n for a, n in enumerate(self.full))

    def shard_half_shape(self):
        s = list(self.half_full_shape())
        if self.lead:
            del s[self.sa]
        else:
            s[self.sa] = self.ws
        return tuple(s)

    def full_view(self, ref, q=None, h=None):
        idx = []
        for a in range(len(self.full)):
            if a == self.sa and q is not None:
                idx.append(q if self.lead else pl.ds(pl.multiple_of(q * self.ws, self.ws), self.ws))
            elif a == self.ha and h is not None:
                idx.append(pl.ds(pl.multiple_of(h * self.wh, self.wh), self.wh))
            else:
                idx.append(slice(None))
        return ref.at[tuple(idx)]

    def shard_view(self, ref, h):
        idx = [] if self.layer is None else [self.layer]
        for a in range(len(self.full)):
            if self.lead and a == self.sa:
                continue
            idx.append(pl.ds(pl.multiple_of(h * self.wh, self.wh), self.wh) if a == self.ha else slice(None))
        return ref.at[tuple(idx)]

    def rows_cols(self, shard, half):
        rows, cols = self.full[-2:]
        if shard and not self.lead:
            rows, cols = (rows // 4, cols) if self.sa == 0 else (rows, cols // 4)
        if half:
            rows, cols = (rows // 2, cols) if self.ha == len(self.full) - 2 else (rows, cols // 2)
        return rows, cols

    def spec(self, tr, cw, nr, shard=False, half=False, has_lead=False, stacked=False):
        two_d = len(self.full) == 2
        shard_on_cols = two_d and self.sa == 1
        half_on_cols = two_d and self.ha == 1
        layer = self.layer

        def index(*args):
            pref = args[-1]
            i = args[-2]
            r, cblk, pre = i, 0, ()
            if shard:
                if self.lead:
                    pre = (pref[0],)
                elif shard_on_cols:
                    cblk = pref[0]
                else:
                    r = pref[0] * nr + i
            elif has_lead:
                pre = (args[0],)
            if half:
                if half_on_cols:
                    cblk = pref[1]
                else:
                    r = pref[1] * nr + i
            if stacked:
                pre = (layer,) + pre
            return pre + (r, cblk)

        n_pre = int(stacked) + int(self.lead and (shard or has_lead))
        return pl.BlockSpec((None,) * n_pre + (tr, cw), index)


WSPECS = [
    WSpec("cv_w_pw1", (D, 2 * D), 1, 0, 0),
    WSpec("cv_w_pw2", (D, D), 0, 1, 1),
    WSpec("gdn_w_in", (4, D, (4 * D + 2 * H) // 4), 0, 1, 2, lead=True),
    WSpec("gdn_w_out", (D, D), 0, 1, 3),
    WSpec("mlp_w1_0", (D, DFF), 1, 0, 4, layer=0),
    WSpec("mlp_w1_1", (D, DFF), 1, 0, 4, layer=1),
    WSpec("mlp_w2_0", (DFF, D), 0, 1, 5, layer=0),
    WSpec("mlp_w2_1", (DFF, D), 0, 1, 5, layer=1),
]
FLIPS = [(1, 0, 0), (0, 1, 0), (1, 1, 0)]
SIB = (0, 0, 1)


def _chip(x, y):
    return 2 * x + y


def _prefetch_call(name, body, grid, in_specs, out_specs, out_shape, pref, args, aliases=None):
    return pl.pallas_call(
        body, name=name, out_shape=out_shape, input_output_aliases=aliases or {},
        grid_spec=pltpu.PrefetchScalarGridSpec(num_scalar_prefetch=1, grid=grid, in_specs=in_specs, out_specs=out_specs),
        compiler_params=_cp(*(("arbitrary",) * len(grid))))(pref, *args)


def place_shard(ws, shard, pref):
    rows, cols = ws.rows_cols(True, False)
    tr = _ew_rows(rows, cols)
    nr = rows // tr
    stacked = ws.layer is not None
    layer = ws.layer

    def body(_, s_ref, o_ref):
        o_ref[...] = s_ref[...].astype(BF16)

    in_spec = pl.BlockSpec(((None,) if stacked else ()) + (tr, cols),
                           (lambda i, p: (layer, i, 0)) if stacked else (lambda i, p: (i, 0)))
    return _prefetch_call("place_" + ws.name, body, (nr,), [in_spec], ws.spec(tr, cols, nr, shard=True),
                          jax.ShapeDtypeStruct(ws.full, BF16), pref, [shard])


FIRST = [0]
LAYER0 = [1, 4, 6]
LAYER1 = [2, 3, 5, 7]


def _plan_gather_chips(sel):
    def plan(refs, pos):
        x, y, c = pos
        remote = []
        for j, i in enumerate(sel):
            mine = WSPECS[i].full_view(refs[j], _chip(x, y), c)
            remote += [(f, mine, mine) for f in FLIPS]
        return remote
    return plan


def gather_cores(tag, sel, nat):
    def plan(in_refs, ip_refs, out_refs, pos):
        x, y, c = pos
        remote = []
        for j, i in enumerate(sel):
            for (dx, dy, _) in FLIPS:
                got = WSPECS[i].full_view(ip_refs[j], _chip(1 - x if dx else x, 1 - y if dy else y), c)
                remote.append((SIB, got, got))
        return remote, []

    return xor_exchange("gather_cores" + tag, [], nat, [], plan, 3 * len(sel))[0]


def gather_start(tag, sel, placed, after):
    plan = _plan_gather_chips(sel)
    send, recv, arrays, token = xor_start("gather_chips%s_start" % tag, [placed[i] for i in sel], plan, 3 * len(sel), after)
    return (tag, sel, plan, send, recv, arrays), token


def gather_wait(state, after):
    tag, sel, plan, send, recv, arrays = state
    return gather_cores(tag, sel, xor_wait("gather_chips%s_wait" % tag, send, recv, arrays, plan, after))


def reduce_start(tag, sel, grads, pref):
    plan = _plan_reduce_chips(sel)
    sums = chip_sums(tag, sel, grads, pref)
    send, recv, arrays, token = xor_start("reduce_chips%s_start" % tag,
                                          sums + [lax.empty(s.shape, s.dtype) for s in _parts_shapes(sel)], plan,
                                          3 * len(sel), pref)
    return (tag, sel, plan, send, recv, arrays), token


def reduce_wait(state, after):
    tag, sel, plan, send, recv, arrays = state
    arrays = xor_wait("reduce_chips%s_wait" % tag, send, recv, arrays, plan, after)
    return sel, arrays[:len(sel)], arrays[len(sel):]


def gather_first(sel, placed, wdw_shard, wcv_shard):
    plan_w = _plan_gather_chips(sel)

    def plan(in_refs, ip_refs, out_refs, pos):
        x, y, c = pos
        remote, local = plan_w(ip_refs, pos), []
        for j, width in enumerate((D // 4, 3 * D // 4)):
            dst = out_refs[j].at[:, pl.ds(pl.multiple_of(_chip(x, y) * width, 128), width)]
            local.append((in_refs[j], dst))
            remote += [(f, in_refs[j], dst) for f in FLIPS]
        return remote, local

    taps = [jax.ShapeDtypeStruct((KCV, D), F32), jax.ShapeDtypeStruct((KSC, 3 * D), F32)]
    nat, (wdw, wcv) = xor_exchange("gather_chips0", [wdw_shard, wcv_shard], [placed[i] for i in sel], taps, plan,
                                   3 * (len(sel) + 2), 2)
    return gather_cores("0", sel, nat), wdw, wcv


def half_add(ws, g, rsib, pref):
    rows, cols = ws.rows_cols(False, True)
    tr = _ew_rows(rows, cols)
    nr = rows // tr

    def body(_, a_ref, b_ref, o_ref):
        o_ref[...] = (a_ref[...] + b_ref[...]).astype(BF16)

    whole = ws.spec(tr, cols, nr, has_lead=ws.lead)
    return _prefetch_call("reduce_add_" + ws.name, body, (4, nr) if ws.lead else (nr,),
                          [ws.spec(tr, cols, nr, half=True, has_lead=ws.lead), whole], whole,
                          jax.ShapeDtypeStruct(ws.half_full_shape(), BF16), pref, [g, rsib])


def shard_sum(ws, s, parts, buf, pref):
    rows, cols = ws.rows_cols(True, True)
    tr = _ew_rows(rows, cols)
    nr = rows // tr
    stacked = ws.layer is not None

    def body(_, s_ref, p_ref, *rest):
        rest[-1][...] = ((s_ref[...].astype(F32) + p_ref[0].astype(F32)) + p_ref[1].astype(F32)) + p_ref[2].astype(F32)

    in_specs = [ws.spec(tr, cols, nr, shard=True, has_lead=ws.lead), pl.BlockSpec((3, tr, cols), lambda i, p: (0, i, 0))]
    args, aliases = [s, parts], {}
    if buf is not None:
        in_specs.append(pl.BlockSpec(memory_space=pl.ANY))
        args.append(buf)
        aliases = {3: 0}
    shape = ((2,) if stacked else ()) + ws.shard_shape()
    return _prefetch_call("reduce_sum_" + ws.name, body, (nr,), in_specs, ws.spec(tr, cols, nr, half=True, stacked=stacked),
                          jax.ShapeDtypeStruct(shape, F32), pref, args, aliases)


def chip_sums(tag, sel, grads, pref):
    def plan(in_refs, ip_refs, out_refs, pos):
        c = pos[2]
        return [(SIB, WSPECS[i].full_view(in_refs[j], None, 1 - c), out_refs[j]) for j, i in enumerate(sel)], []

    halves = [jax.ShapeDtypeStruct(WSPECS[i].half_full_shape(), F32) for i in sel]
    _, rsib = xor_exchange("reduce_cores" + tag, grads, [], halves, plan, len(sel))
    return [half_add(WSPECS[i], grads[j], rsib[j], pref) for j, i in enumerate(sel)]


def _plan_reduce_chips(sel):
    n = len(sel)

    def plan(refs, pos):
        x, y, c = pos
        remote = []
        for j, i in enumerate(sel):
            for s, (dx, dy, _) in enumerate(FLIPS):
                qq = _chip(1 - x if dx else x, 1 - y if dy else y)
                remote.append(((dx, dy, 0), WSPECS[i].full_view(refs[j], qq), refs[n + j].at[s]))
        return remote
    return plan


def _parts_shapes(sel):
    return [jax.ShapeDtypeStruct((3,) + WSPECS[i].shard_half_shape(), BF16) for i in sel]


def finish_reduce(sums, parts, pref):
    n = len(WSPECS)
    bufs = {}
    for i, ws in enumerate(WSPECS):
        bufs[ws.group] = shard_sum(ws, sums[i], parts[i], bufs.get(ws.group), pref)

    def plan3(in_refs, ip_refs, out_refs, pos):
        c = pos[2]
        remote = []
        for ws in WSPECS:
            mine = ws.shard_view(ip_refs[ws.group], c)
            remote.append((SIB, mine, mine))
        return remote, []

    return xor_exchange("reduce_swap", [], [bufs[g] for g in sorted(bufs)], [], plan3, n)[0]


def gather_small(buf):
    flips = [(dx, dy, dc) for dx in (0, 1) for dy in (0, 1) for dc in (0, 1)][1:]

    def plan(in_refs, ip_refs, out_refs, pos):
        x, y, c = pos
        me = 4 * x + 2 * y + c
        dst = out_refs[0].at[me]
        return [(f, in_refs[0], dst) for f in flips], [(in_refs[0], dst)]

    return xor_exchange("gather_small", [buf], [], [jax.ShapeDtypeStruct((8,) + buf.shape, F32)], plan, 7, 1)[1][0]


def _pad_rows(a, rows):
    return jnp.pad(a, ((0, rows - a.shape[0]), (0, 0)))


def _row1(v):
    v = v.reshape((1, -1))
    return jnp.pad(v, ((0, 0), (0, D - v.shape[1])))


def _rms_fwd(name, h, g, tm):
    return row_call(name, lambda hh, gg: ((f_rms(hh, gg),), ()), [(h, D, 0)], [g], [(D, BF16)], [], tm)[0]


def _res_rms(h, g):
    return (h, f_rms(h, g)), ()


def _rms_bwd_epi(dhn, h, dres, g):
    _, vjp = jax.vjp(f_rms, h, g)
    dh, dg = vjp(dhn)
    dh = dh + dres
    return (dh,), (dg, jnp.sum(dh, axis=0, keepdims=True))


def _mlp_bwd(tag, dh, h, g, w1, w2, hn, z1, token=None):
    dz1 = mm("mlp_down_dx" + tag, dh, w2, "NT", BF16,
             epi=lambda acc, z, *_: acc * (2.0 * jnp.maximum(z.astype(F32), 0.0)),
             epi_ins=[(z1, "tile")] + ([] if token is None else [(token, "whole")]))
    dw2 = mm_tn("mlp_down_dw" + tag, z1, dh, a_fn=f_relu2)
    dh_in, dg, colsum = mm("mlp_up_dx" + tag, dz1, w1, "NT", [F32], epi=_rms_bwd_epi,
                           epi_ins=[(h, "tile"), (dh, "tile"), (g, "row")], accs=[(1, D), (1, D)])
    dw1 = mm_tn("mlp_up_dw" + tag, hn, dz1)
    return dh_in, dg, colsum, dw1, dw2


def kernel(x, norm_mix_g, norm_ffn_g, final_norm_g, cv_w_pw1, cv_b_pw1, cv_w_dw, cv_b_dw, cv_ln_g, cv_ln_b, cv_w_pw2, cv_b_pw2, gdn_w_in, gdn_conv_w, gdn_a_log, gdn_dt_bias, gdn_norm_g, gdn_w_out, mlp_w1, mlp_w2, loss_target, m_norm_mix_g, m_norm_ffn_g, m_final_norm_g, m_cv_w_pw1, m_cv_b_pw1, m_cv_w_dw, m_cv_b_dw, m_cv_ln_g, m_cv_ln_b, m_cv_w_pw2, m_cv_b_pw2, m_gdn_w_in, m_gdn_conv_w, m_gdn_a_log, m_gdn_dt_bias, m_gdn_norm_g, m_gdn_w_out, m_mlp_w1, m_mlp_w2, v_norm_mix_g, v_norm_ffn_g, v_final_norm_g, v_cv_w_pw1, v_cv_b_pw1, v_cv_w_dw, v_cv_b_dw, v_cv_ln_g, v_cv_ln_b, v_cv_w_pw2, v_cv_b_pw2, v_gdn_w_in, v_gdn_conv_w, v_gdn_a_log, v_gdn_dt_bias, v_gdn_norm_g, v_gdn_w_out, v_mlp_w1, v_mlp_w2):
    env = dict(locals())
    bl, S, _ = x.shape
    T = bl * S
    tm = min(256, S)
    xf = x.reshape((T, D))
    tgt = loss_target.reshape((T, D))

    chip = 2 * lax.axis_index("x") + lax.axis_index("y")
    pref = jnp.stack([chip, lax.axis_index("c")]).astype(jnp.int32)
    big = [cv_w_pw1[0], cv_w_pw2[0], gdn_w_in[0], gdn_w_out[0], mlp_w1, mlp_w2]
    placed = [place_shard(ws, big[ws.group], pref) for ws in WSPECS]
    (w_pw1,), wdw, wcv = gather_first(FIRST, placed, cv_w_dw[0], gdn_conv_w[0])
    gather_a, token_a = gather_start("A", LAYER0, placed, wcv)
    gather_b, token_b = gather_start("B", LAYER1, placed, wcv)
    wdw_p, wcv_p = _pad_rows(wdw, HB_CV), _pad_rows(wcv, HB_SC)
    alog_p = jnp.pad(gdn_a_log, ((0, 0), (0, HD - H)))
    dtb_p = jnp.pad(gdn_dt_bias, ((0, 0), (0, HD - H)))
    g_mix0, g_mix1 = norm_mix_g[0:1] + (token_a[0, 0] + token_b[0, 0]), norm_mix_g[1:2]
    g_ffn0, g_ffn1 = norm_ffn_g[0:1], norm_ffn_g[1:2]
    g_fin = final_norm_g.reshape((1, D))

    hn0 = _rms_fwd("rms_mix0", xf, g_mix0, tm)
    u = mm("cv_pw1", hn0, w_pw1, "NN", F32, epi=lambda acc, b: acc + b, epi_ins=[(cv_b_pw1, "row")])
    dwc, s_act = dwconv_fwd("cv_dwconv", u, 2 * D, wdw_p, cv_b_dw, S, KCV, HB_CV, f_glu, (f_ln_silu, (cv_ln_g, cv_ln_b)), tm)
    w_pw2, w1_0, w2_0 = gather_wait(gather_a, dwc)
    h1, hnf0 = mm("cv_pw2", s_act, w_pw2, "NN", [F32, BF16], epi=lambda acc, b, r, g: _res_rms(acc + b + r, g),
                  epi_ins=[(cv_b_pw2, "row"), (xf, "tile"), (g_ffn0, "row")])
    z1_0 = mm("mlp_up0", hnf0, w1_0, "NN", BF16)
    h2, hn2 = mm("mlp_down0", z1_0, w2_0, "NN", [F32, BF16], a_fn=f_relu2, epi=lambda acc, r, g: _res_rms(acc + r, g),
                 epi_ins=[(h1, "tile"), (g_mix1, "row")])

    w_in_sm, w_out, w1_1, w2_1 = gather_wait(gather_b, h2)
    w_in = jnp.transpose(w_in_sm, (1, 0, 2)).reshape((D, 4 * D + 2 * H))
    w_qkv, w_z = w_in[:, :3 * D], w_in[:, 3 * D:4 * D]
    w_qkvz = w_in[:, :4 * D]
    w_ab = jnp.pad(w_in[:, 4 * D:], ((0, 0), (0, HD - 2 * H)))
    pqkvz = mm("gdn_in", hn2, w_qkvz, "NN", F32)
    pab = mm("gdn_in_ab", hn2, w_ab, "NN", F32)
    cpre = dwconv_fwd("gdn_conv", pqkvz, 3 * D, wcv_p, None, S, KSC, HB_SC, lambda v: v, None, tm)[0]
    gu, gw, gqg, gkg, gqk, gt, geg = gdn_prep_fwd(cpre, pab, alog_p, dtb_p)
    o, sall = gdn_scan_fwd(gu, gw, gqg, gkg, gqk, geg, S)
    on = row_call("gdn_post", lambda oo, zz, ng: ((f_post(oo, zz, ng),), ()), [(o, D, 0), (pqkvz, D, 3)],
                  [gdn_norm_g], [(D, BF16)], [], tm)[0]
    h3, hnf1 = mm("gdn_out", on, w_out, "NN", [F32, BF16], epi=lambda acc, r, g: _res_rms(acc + r, g),
                  epi_ins=[(h2, "tile"), (g_ffn1, "row")])
    z1_1 = mm("mlp_up1", hnf1, w1_1, "NN", BF16)

    def head(acc, res, tt, gg):
        def loss_of(h_, g_):
            return 0.5 * jnp.sum(jnp.mean(jnp.square(f_rms(h_, g_) - tt), axis=-1))
        lv, (dh_, dg_) = jax.value_and_grad(loss_of, (0, 1))(acc + res, gg)
        return (dh_,), (dg_, jnp.full((1, D), lv, F32))

    dh4, dg_fin, loss_row = mm("mlp_down1", z1_1, w2_1, "NN", [F32], a_fn=f_relu2, epi=head,
                               epi_ins=[(h3, "tile"), (tgt, "tile"), (g_fin, "row")], accs=[(1, D), (1, D)])

    dh3, dg_ffn1, _, dw1_1, dw2_1 = _mlp_bwd("1", dh4, h3, g_ffn1, w1_1, w2_1, hnf1, z1_1)
    dw_out = mm_tn("gdn_out_dw", on, dh3)

    def post_bwd(don, oo, zz, ng):
        _, vjp = jax.vjp(f_post, oo, zz, ng)
        do_, dz_, dng_ = vjp(don)
        return (do_, dz_), (dng_,)

    do, dz, dng = mm("gdn_out_dx", dh3, w_out, "NT", [F32, F32], epi=post_bwd,
                     epi_ins=[(o, "tile"), (pqkvz, ("cols", 3)), (gdn_norm_g, "whole")], accs=[(1, HD)])
    du, dw, dqg, dkg, dqk, deg = gdn_scan_bwd(do, gu, gw, gqg, gkg, gqk, geg, sall, S)
    dcpre, dpab, dalog, ddtb = gdn_prep_bwd(cpre, pab, alog_p, dtb_p, gt, du, dw, dqg, dkg, dqk, deg)
    dqkv, dwcv, _ = dwconv_bwd("gdn_conv_bwd", dcpre, pqkvz, 3 * D, wcv_p, S, KSC, HB_SC, lambda v: v, None, tm)
    dhn2 = mm("gdn_in_dx_ab", dpab, w_ab, "NT", F32)
    dhn2 = mm("gdn_in_dx_z", dz, w_z, "NT", F32, epi=lambda acc, r: acc + r, epi_ins=[(dhn2, "tile")])
    dh2, dg_mix1, _ = mm("gdn_in_dx_qkv", dqkv, w_qkv, "NT", [F32],
                         epi=lambda acc, prev, hh, rr, gg: _rms_bwd_epi(acc + prev, hh, rr, gg),
                         epi_ins=[(dhn2, "tile"), (h2, "tile"), (dh3, "tile"), (g_mix1, "row")], accs=[(1, D), (1, D)])
    dw_in = jnp.concatenate([mm_tn("gdn_in_dw_qkv", hn2, dqkv), mm_tn("gdn_in_dw_z", hn2, dz),
                             mm_tn("gdn_in_dw_ab", hn2, dpab)[:, :2 * H]], axis=1)

    dw_in_sm = jnp.transpose(dw_in.reshape((D, 4, D + 4)), (1, 0, 2))
    reduce_b, rtoken_b = reduce_start("B", LAYER1, [dw_in_sm, dw_out, dw1_1, dw2_1], pref)

    dh1, dg_ffn0, db_pw2, dw1_0, dw2_0 = _mlp_bwd("0", dh2, h1, g_ffn0, w1_0, w2_0, hnf0, z1_0, rtoken_b)
    reduce_a, rtoken_a = reduce_start("A", LAYER0[1:], [dw1_0, dw2_0], pref)
    dw_pw2 = mm_tn("cv_pw2_dw", s_act, dh1)

    def ln_bwd(ds, xx, gg, bb, *_):
        _, vjp = jax.vjp(f_ln_silu, xx, gg, bb)
        dx_, dg_, db_ = vjp(ds)
        return (dx_,), (dg_, db_, jnp.sum(dx_, axis=0, keepdims=True))

    ddw, dln_g, dln_b, db_dw = mm("cv_pw2_dx", dh1, w_pw2, "NT", [F32], epi=ln_bwd,
                                  epi_ins=[(dwc, "tile"), (cv_ln_g, "row"), (cv_ln_b, "row"), (rtoken_a, "whole")],
                                  accs=[(1, D)] * 3)
    du_cv, dwdw, db_pw1 = dwconv_bwd("cv_dwconv_bwd", ddw, u, 2 * D, wdw_p, S, KCV, HB_CV, f_glu, _glu_bwd, tm)
    dw_pw1 = mm_tn("cv_pw1_dw", hn0, du_cv)
    grad_x, dg_mix0, _ = mm("cv_pw1_dx", du_cv, w_pw1, "NT", [F32], epi=_rms_bwd_epi,
                            epi_ins=[(xf, "tile"), (dh1, "tile"), (g_mix0, "row")], accs=[(1, D), (1, D)])

    last = FIRST + LAYER0[:1]
    sums_l = chip_sums("0", last, [dw_pw1, dw_pw2], pref)
    plan_l = _plan_reduce_chips(last)
    _, parts_l = xor_exchange("reduce_chips0", sums_l, [], _parts_shapes(last),
                              lambda ins_, ip_, outs_, pos: (plan_l(list(ins_) + list(outs_), pos), []), 3 * len(last))
    sums, parts = [None] * len(WSPECS), [None] * len(WSPECS)
    for sel, sums_s, parts_s in ((last, sums_l, parts_l), reduce_wait(reduce_a, grad_x), reduce_wait(reduce_b, grad_x)):
        for j, i in enumerate(sel):
            sums[i], parts[i] = sums_s[j], parts_s[j]
    g_pw1, g_pw2, g_in, g_out, g_w1, g_w2 = finish_reduce(sums, parts, pref)

    small = jnp.concatenate([
        dg_mix0, dg_mix1, dg_ffn0, dg_ffn1, dg_fin, db_pw1.reshape((2, D)), db_dw, dln_g, dln_b, db_pw2,
        _row1(dalog[:, :H]), _row1(ddtb[:, :H]), _row1(dng), loss_row, jnp.zeros((1, D), F32),
        dwdw, dwcv[:KSC].reshape((3 * KSC, D)), jnp.zeros((NSMALL - 48 - 3 * KSC, D), F32)], axis=0)
    small_all = gather_small(small)

    def pack(a, b, c_, d, e, f, g_, h_, i_, j_, k_):
        return jnp.concatenate([a, b, c_.reshape((1, D)), d.reshape((2, D)), e, f, g_, h_, _row1(i_), _row1(j_), _row1(k_),
                                jnp.zeros((2, D), F32)], axis=0)

    order = lambda p: (p + "norm_mix_g", p + "norm_ffn_g", p + "final_norm_g", p + "cv_b_pw1", p + "cv_b_dw", p + "cv_ln_g",
                       p + "cv_ln_b", p + "cv_b_pw2", p + "gdn_a_log", p + "gdn_dt_bias", p + "gdn_norm_g")
    w16, m16, v16 = (pack(*[env[nm] for nm in order(p)]) for p in ("", "m_", "v_"))

    def small_step(ga, ww, mm_, vv):
        gsum = ga[0]
        for dev in range(1, 8):
            gsum = gsum + ga[dev]
        delta, m2, v2 = f_adamw(ww, gsum[:16], mm_, vv)
        return gsum, delta, m2, v2

    def small_body(ga_ref, w_ref, m_ref, v_ref, g_out, d_out, m_out, v_out):
        gsum, delta, m2, v2 = small_step(ga_ref[...], w_ref[...], m_ref[...], v_ref[...])
        g_out[...] = gsum
        d_out[...] = delta
        m_out[...] = m2
        v_out[...] = v2

    vm = pl.BlockSpec(memory_space=pltpu.VMEM)
    sg, sd, sm, sv = pl.pallas_call(
        small_body, name="adamw_small", in_specs=[vm] * 4, out_specs=[vm] * 4,
        out_shape=[jax.ShapeDtypeStruct((NSMALL, D), F32)] + [jax.ShapeDtypeStruct((16, D), F32)] * 3)(small_all, w16, m16, v16)

    def unpack(b):
        return (b[0:2], b[2:4], b[4], b[5:7].reshape((1, 2 * D)), b[7:8], b[8:9], b[9:10], b[10:11],
                b[11:12, :H], b[12:13, :H], b[13:14, :HD])

    loss = sg[14, 0]
    g_dw = lax.dynamic_slice(sg[16:16 + KCV], (0, chip * (D // 4)), (KCV, D // 4))
    g_cv = lax.dynamic_slice(sg[48:48 + 3 * KSC].reshape((KSC, 3 * D)), (0, chip * (3 * D // 4)), (KSC, 3 * D // 4))

    def adamw(name, w, g, m, v):
        lead = w.shape[:-2]
        if len(lead) == 1 and lead[0] == 1:
            d, m2, v2 = ew_call(name, f_adamw, [w[0], g.reshape(w.shape[1:]), m[0], v[0]], 3)
            return g.reshape(w.shape), d[None], m2[None], v2[None]
        return (g.reshape(w.shape),) + tuple(ew_call(name, f_adamw, [w, g.reshape(w.shape), m, v], 3))

    res = {
        "cv_w_pw1": adamw("adamw_pw1", cv_w_pw1, g_pw1, m_cv_w_pw1, v_cv_w_pw1),
        "cv_w_dw": adamw("adamw_dw", cv_w_dw, g_dw, m_cv_w_dw, v_cv_w_dw),
        "cv_w_pw2": adamw("adamw_pw2", cv_w_pw2, g_pw2, m_cv_w_pw2, v_cv_w_pw2),
        "gdn_w_in": adamw("adamw_win", gdn_w_in, g_in, m_gdn_w_in, v_gdn_w_in),
        "gdn_conv_w": adamw("adamw_cvw", gdn_conv_w, g_cv, m_gdn_conv_w, v_gdn_conv_w),
        "gdn_w_out": adamw("adamw_wout", gdn_w_out, g_out, m_gdn_w_out, v_gdn_w_out),
        "mlp_w1": adamw("adamw_w1", mlp_w1, g_w1, m_mlp_w1, v_mlp_w1),
        "mlp_w2": adamw("adamw_w2", mlp_w2, g_w2, m_mlp_w2, v_mlp_w2),
    }
    names = ("norm_mix_g", "norm_ffn_g", "final_norm_g", "cv_b_pw1", "cv_b_dw", "cv_ln_g", "cv_ln_b", "cv_b_pw2",
             "gdn_a_log", "gdn_dt_bias", "gdn_norm_g")
    for nm, gg, dd, mm_, vv in zip(names, unpack(sg), unpack(sd), unpack(sm), unpack(sv)):
        res[nm] = (gg, dd, mm_, vv)
    weights = ("norm_mix_g", "norm_ffn_g", "final_norm_g", "cv_w_pw1", "cv_b_pw1", "cv_w_dw", "cv_b_dw", "cv_ln_g",
               "cv_ln_b", "cv_w_pw2", "cv_b_pw2", "gdn_w_in", "gdn_conv_w", "gdn_a_log", "gdn_dt_bias", "gdn_norm_g",
               "gdn_w_out", "mlp_w1", "mlp_w2")
    outs = [loss, grad_x.reshape(x.shape)]
    for kind in range(4):
        outs += [res[nm][kind] for nm in weights]
    return tuple(outs)
```

```python
import functools

import jax
import jax.numpy as jnp
from jax import lax
from jax.experimental import pallas as pl
from jax.experimental.pallas import tpu as pltpu

F32, BF16 = jnp.float32, jnp.bfloat16
D = 1024
H = 8
HD = 128
CH = 64
DFF = 4 * D
KCV, HB_CV = 31, 32
KSC, HB_SC = 4, 8
EPS = 1e-6
LR, B1, B2, EPS_A, WD, STEP = 0.001, 0.9, 0.999, 1e-08, 0.01, 10
VMEM_LIMIT = 56 * 1024 * 1024
SUB = 32
NSMALL = 64
MESH = pl.DeviceIdType.MESH


def _cp(*sem):
    return pltpu.CompilerParams(dimension_semantics=sem, vmem_limit_bytes=VMEM_LIMIT)


def f_rms(h, g):
    return h * lax.rsqrt(jnp.mean(h * h, axis=-1, keepdims=True) + EPS) * g


def f_silu(x):
    return x * jax.nn.sigmoid(x)


def f_glu(u):
    return u[:, :D] * jax.nn.sigmoid(u[:, D:])


def f_ln_silu(x, g, b):
    mu = jnp.mean(x, axis=-1, keepdims=True)
    xc = x - mu
    y = xc * lax.rsqrt(jnp.mean(xc * xc, axis=-1, keepdims=True) + EPS)
    return f_silu(y * g + b)


def f_relu2(z):
    r = jnp.maximum(z.astype(F32), 0.0)
    return r * r


def f_post(o, z, ng):
    outs = []
    for h in range(H):
        oh = o[:, h * HD:(h + 1) * HD]
        y = oh * lax.rsqrt(jnp.mean(oh * oh, axis=-1, keepdims=True) + EPS) * ng
        outs.append(y * f_silu(z[:, h * HD:(h + 1) * HD]))
    return jnp.concatenate(outs, axis=1)


def f_adamw(w, g, m, v):
    m2 = B1 * m + (1.0 - B1) * g
    v2 = B2 * v + (1.0 - B2) * (g * g)
    m_hat = m2 / (1.0 - B1 ** STEP)
    v_hat = v2 / (1.0 - B2 ** STEP)
    delta = -LR * (m_hat / (jnp.sqrt(v_hat) + EPS_A) + WD * w)
    return delta, m2, v2


def _dot_raw(a, b, mode):
    dims = {"NN": ((1,), (0,)), "NT": ((1,), (1,)), "TN": ((0,), (0,))}[mode]
    return lax.dot_general(a.astype(BF16), b.astype(BF16), (dims, ((), ())), preferred_element_type=F32)


@functools.partial(jax.custom_vjp, nondiff_argnums=(2,))
def _dot_vjp(a, b, mode):
    return _dot_raw(a, b, mode)


def _dot_fwd(a, b, mode):
    return _dot_raw(a, b, mode), (a, b)


def _dot_bwd(mode, res, dc):
    a, b = res
    if mode == "NN":
        return _dot_vjp(dc, b, "NT"), _dot_vjp(a, dc, "TN")
    if mode == "NT":
        return _dot_vjp(dc, b, "NN"), _dot_vjp(dc, a, "TN")
    return _dot_vjp(b, dc, "NT"), _dot_vjp(a, dc, "NN")


_dot_vjp.defvjp(_dot_fwd, _dot_bwd)


def _split(x):
    xh = x.astype(BF16)
    return xh, (x - xh.astype(F32)).astype(BF16)


def _dot_split(xs, ys):
    (xh, xl), (yh, yl) = xs, ys
    return _dot_raw(xh, yh, "NN") + (_dot_raw(xh, yl, "NN") + _dot_raw(xl, yh, "NN"))


def _tril_inverse(a_list):
    ri = lax.broadcasted_iota(jnp.int32, (CH, CH), 0)
    ci = lax.broadcasted_iota(jnp.int32, (CH, CH), 1)
    eye = (ri == ci).astype(F32)
    ts = None
    for lvl in range(CH.bit_length() - 1):
        same_pair = jnp.right_shift(ri, lvl + 1) == jnp.right_shift(ci, lvl + 1)
        quarter = (jnp.bitwise_and(jnp.right_shift(ri, lvl), 1) == 1) & (jnp.bitwise_and(jnp.right_shift(ci, lvl), 1) == 0)
        offs = [jnp.where(same_pair & quarter, a, 0.0) for a in a_list]
        if ts is None:
            ts = [eye - off for off in offs]
            continue
        tsp = [_split(t) for t in ts]
        mids = [_dot_split(tp, _split(off)) for tp, off in zip(tsp, offs)]
        ts = [t - _dot_split(_split(m), tp) for t, m, tp in zip(ts, mids, tsp)]
    return ts


@jax.custom_vjp
def _stored_solve(a, t, rhs):
    return _dot_raw(t, rhs, "NN")


def _stored_solve_fwd(a, t, rhs):
    sol = _dot_raw(t, rhs, "NN")
    return sol, (t, sol)


def _stored_solve_bwd(res, g):
    t, sol = res
    g_rhs = _dot_vjp(t, g, "TN")
    return -_dot_vjp(g_rhs, sol, "NT"), jnp.zeros_like(t), g_rhs


_stored_solve.defvjp(_stored_solve_fwd, _stored_solve_bwd)


def _lane_pick(row, idx, width):
    sel = lax.broadcasted_iota(jnp.int32, (1, width), 1) == idx
    return jnp.sum(jnp.where(sel, row, 0.0), axis=1, keepdims=True)


def f_prep(cqs, cks, cvs, araws, braws, alogs, dtbs, t_stored, dot):
    ri = lax.broadcasted_iota(jnp.int32, (CH, CH), 0)
    ci = lax.broadcasted_iota(jnp.int32, (CH, CH), 1)
    eye = (ri == ci).astype(F32)
    low = (ri >= ci).astype(F32)
    last = lax.broadcasted_iota(jnp.int32, (CH, 1), 0) == CH - 1
    nh = range(len(cqs))
    qs, ks, vbs, kbs, gcs, decays = [], [], [], [], [], []
    for h in nh:
        q = f_silu(cqs[h])
        qs.append(q * lax.rsqrt(jnp.sum(q * q, axis=-1, keepdims=True) + 1e-6) * (HD ** -0.5))
        k = f_silu(cks[h])
        k = k * lax.rsqrt(jnp.sum(k * k, axis=-1, keepdims=True) + 1e-6)
        ks.append(k)
        beta = jax.nn.sigmoid(braws[h])
        sp_in = araws[h] + dtbs[h]
        softplus = jnp.maximum(sp_in, 0.0) + jnp.log(1.0 + jnp.exp(-jnp.abs(sp_in)))
        g = -jnp.exp(alogs[h]) * softplus
        g_row = jnp.sum(eye * g, axis=0, keepdims=True)
        gc = jnp.sum(low * g_row, axis=1, keepdims=True)
        gc_row = jnp.sum(eye * gc, axis=0, keepdims=True)
        gcs.append(gc)
        decays.append(jnp.exp(jnp.where(ri >= ci, gc - gc_row, -1e30)))
        vbs.append(f_silu(cvs[h]) * beta)
        kbs.append(k * beta)
    kks = [dot(kbs[h], ks[h], "NT") for h in nh]
    a_list = [jnp.where(ri > ci, kks[h] * decays[h], 0.0) for h in nh]
    if t_stored is None:
        ts = _tril_inverse(a_list)
        solve = lambda h, rhs: dot(ts[h], rhs, "NN")
    else:
        ts = t_stored
        solve = lambda h, rhs: _stored_solve(a_list[h], t_stored[h], rhs)
    egcs = [jnp.exp(gc) for gc in gcs]
    us = [solve(h, vbs[h]) for h in nh]
    ws = [solve(h, kbs[h] * egcs[h]) for h in nh]
    qks = [dot(qs[h], ks[h], "NT") * decays[h] for h in nh]
    qgs = [qs[h] * egcs[h] for h in nh]
    gls = [jnp.sum(jnp.where(last, gc, 0.0), axis=0, keepdims=True) for gc in gcs]
    kgs = [ks[h] * jnp.exp(gls[h] - gcs[h]) for h in nh]
    egs = [jnp.exp(gl) * jnp.ones((1, HD), F32) for gl in gls]
    return us, ws, qks, qgs, kgs, egs, ts


def f_scan(ss, us, ws, qgs, kgs, qks, egs, dot):
    nh = range(len(ss))
    ws_s = [dot(ws[h], ss[h], "NN") for h in nh]
    qs_s = [dot(qgs[h], ss[h], "NN") for h in nh]
    vns = [us[h] - ws_s[h] for h in nh]
    os_ = [qs_s[h] + dot(qks[h], vns[h], "NN") for h in nh]
    s2s = [ss[h] * egs[h] + dot(kgs[h], vns[h], "TN") for h in nh]
    return os_, s2s


def row_call(name, fn, rows, pars, out_rows, out_accs, tm):
    T = rows[0][0].shape[0]
    n_r, n_p, n_o = len(rows), len(pars), len(out_rows)
    in_specs = [pl.BlockSpec((tm, w), functools.partial(lambda i, cb: (i, cb), cb=cb)) for (_, w, cb) in rows]
    in_specs += [pl.BlockSpec(p.shape, functools.partial(lambda i, nd: (0,) * nd, nd=p.ndim)) for p in pars]
    out_specs = [pl.BlockSpec((tm, w), lambda i: (i, 0)) for (w, _) in out_rows]
    out_specs += [pl.BlockSpec(s, lambda i: (0, 0)) for s in out_accs]
    out_shape = [jax.ShapeDtypeStruct((T, w), dt) for (w, dt) in out_rows]
    out_shape += [jax.ShapeDtypeStruct(s, F32) for s in out_accs]

    def body(*refs):
        rin, pin = refs[:n_r], refs[n_r:n_r + n_p]
        rout, aout = refs[n_r + n_p:n_r + n_p + n_o], refs[n_r + n_p + n_o:]
        if aout:
            @pl.when(pl.program_id(0) == 0)
            def _():
                for a in aout:
                    a[...] = jnp.zeros(a.shape, F32)
        pv = [p[...] for p in pin]

        def step(r, carry):
            sl = pl.ds(pl.multiple_of(r * SUB, SUB), SUB)
            outs, accs = fn(*[x[sl, :] for x in rin], *pv)
            for o, val in zip(rout, outs):
                o[sl, :] = val.astype(o.dtype)
            for a, val in zip(aout, accs):
                a[...] += val
            return carry

        lax.fori_loop(0, tm // SUB, step, 0)

    return pl.pallas_call(body, name=name, grid=(T // tm,), in_specs=in_specs, out_specs=out_specs,
                          out_shape=out_shape, compiler_params=_cp("arbitrary"))(*[r[0] for r in rows], *pars)


EW_TILE_ELEMS = 256 * 1024


def _ew_rows(R, Cc):
    if R * Cc <= EW_TILE_ELEMS or R % 8:
        return R
    tr = 8
    while tr * 2 * Cc <= EW_TILE_ELEMS and R % (tr * 2) == 0:
        tr *= 2
    return tr


def ew_call(name, fn, ins, n_out):
    shape = ins[0].shape
    lead = shape[:-2]
    R, Cc = shape[-2:]
    tr = _ew_rows(R, Cc)
    grid = lead + (R // tr,)
    nl = len(lead)
    spec = pl.BlockSpec((None,) * nl + (tr, Cc), lambda *idx: idx + (0,))

    def body(*refs):
        outs = fn(*[r[...] for r in refs[:len(ins)]])
        for o, val in zip(refs[len(ins):], outs):
            o[...] = val

    return pl.pallas_call(body, name=name, grid=grid, in_specs=[spec] * len(ins), out_specs=[spec] * n_out,
                          out_shape=[jax.ShapeDtypeStruct(shape, F32)] * n_out,
                          compiler_params=_cp(*(("arbitrary",) * len(grid))))(*ins)


MM_RESIDENT_BYTES = 8 * 1024 * 1024


def mm(name, a, b, mode, out_dtype, a_fn=None, epi=None, epi_ins=(), accs=(), tm=512):
    sub_epi = epi is not None and isinstance(out_dtype, (list, tuple))
    M, K = a.shape
    N = b.shape[1] if mode == "NN" else b.shape[0]
    tn = N if K * N * 2 <= MM_RESIDENT_BYTES else min(N, 1024)
    tm = min(tm if tn <= 1024 else tm // 2, M)
    multi = isinstance(out_dtype, (list, tuple))
    dts = list(out_dtype) if multi else [out_dtype]
    n_e, n_o = len(epi_ins), len(dts)
    in_specs = [pl.BlockSpec((tm, K), lambda j, i: (i, 0)),
                pl.BlockSpec((K, tn), lambda j, i: (0, j)) if mode == "NN" else pl.BlockSpec((tn, K), lambda j, i: (j, 0))]
    row_kinds = []
    for (arr, kind) in epi_ins:
        if kind == "tile" or isinstance(kind, tuple):
            cb = kind[1] if isinstance(kind, tuple) else 0
            in_specs.append(pl.BlockSpec((tm, tn), functools.partial(lambda j, i, cb: (i, cb + j), cb=cb)))
            row_kinds.append(True)
        elif kind == "row":
            in_specs.append(pl.BlockSpec((1, tn), lambda j, i: (0, j)))
            row_kinds.append(False)
        else:
            in_specs.append(pl.BlockSpec(arr.shape, lambda j, i: (0, 0)))
            row_kinds.append(False)

    def body(a_ref, b_ref, *rest):
        e_refs, o_refs, acc_refs = rest[:n_e], rest[n_e:n_e + n_o], rest[n_e + n_o:n_e + n_o + len(accs)]
        av = a_ref[...]
        if a_fn is not None:
            av = a_fn(av)
        res = _dot_raw(av, b_ref[...], mode)
        if epi is None or not sub_epi:
            if epi is not None:
                res = epi(res, *[r[...] for r in e_refs])
            o_refs[0][...] = res.astype(o_refs[0].dtype)
            return
        prod = rest[-1]
        prod[...] = res
        if acc_refs:
            @pl.when((pl.program_id(0) == 0) & (pl.program_id(1) == 0))
            def _():
                for r in acc_refs:
                    r[...] = jnp.zeros(r.shape, F32)
        small = [None if is_rows else r[...] for r, is_rows in zip(e_refs, row_kinds)]

        def step(k, carry):
            sl = pl.ds(pl.multiple_of(k * SUB, SUB), SUB)
            out = epi(prod[sl, :], *[r[sl, :] if is_rows else sm for r, is_rows, sm in zip(e_refs, row_kinds, small)])
            tiles, contribs = out if multi else ((out,), ())
            for r, t in zip(o_refs, tiles):
                r[sl, :] = t.astype(r.dtype)
            for r, t in zip(acc_refs, contribs):
                r[...] += t
            return carry

        lax.fori_loop(0, tm // SUB, step, 0)

    out_specs = [pl.BlockSpec((tm, tn), lambda j, i: (i, j))] * n_o + [pl.BlockSpec(s, lambda j, i: (0, 0)) for s in accs]
    out_shape = [jax.ShapeDtypeStruct((M, N), dt) for dt in dts] + [jax.ShapeDtypeStruct(s, F32) for s in accs]
    res = pl.pallas_call(body, name=name, grid=(N // tn, M // tm), in_specs=in_specs, out_specs=out_specs,
                         out_shape=out_shape, scratch_shapes=[pltpu.VMEM((tm, tn), F32)] if sub_epi else [],
                         compiler_params=_cp("arbitrary", "arbitrary"))(a, b, *[e[0] for e in epi_ins])
    return res if multi else res[0]


def mm_tn(name, a, g, a_fn=None, a_cols=None, tt=1024):
    T = a.shape[0]
    ka, acb = (a.shape[1], 0) if a_cols is None else a_cols
    N = g.shape[1]
    tt = min(tt, T)
    tka, tn = min(ka, 1024), min(N, 1024)
    nkb = ka // tka

    def body(a_ref, g_ref, o_ref):
        @pl.when(pl.program_id(2) == 0)
        def _():
            o_ref[...] = jnp.zeros(o_ref.shape, F32)
        av = a_ref[...]
        if a_fn is not None:
            av = a_fn(av)
        o_ref[...] += _dot_raw(av, g_ref[...], "TN")

    return pl.pallas_call(body, name=name, grid=(nkb, N // tn, T // tt),
                          in_specs=[pl.BlockSpec((tt, tka), lambda ia, j, t: (t, acb * nkb + ia)),
                                    pl.BlockSpec((tt, tn), lambda ia, j, t: (t, j))],
                          out_specs=pl.BlockSpec((tka, tn), lambda ia, j, t: (ia, j)),
                          out_shape=jax.ShapeDtypeStruct((ka, N), F32),
                          compiler_params=_cp("arbitrary", "arbitrary", "arbitrary"))(a, g)


SUBLANES = 8


class _RowShifts:
    def __init__(self, src, shifted, nrows, reuse):
        self.src, self.shifted, self.reuse = src, shifted, reuse
        if reuse:
            for ph in range(1, SUBLANES):
                for r0 in range(0, nrows - SUBLANES, SUB):
                    n = min(SUB, nrows - SUBLANES - r0)
                    shifted[ph - 1, r0:r0 + n, :] = src[r0 + ph:r0 + ph + n, :]

    def window(self, off, cols):
        ph = off % SUBLANES
        if not self.reuse or ph == 0:
            return self.src[off:off + SUB, cols]
        return self.shifted[ph - 1, off - ph:off - ph + SUB, cols]


def _shift_scratch(nrows, C, reuse):
    return [pltpu.VMEM((SUBLANES - 1, nrows - SUBLANES, C), F32)] if reuse else []


def dwconv_fwd(name, x, xw, w_pad, bias, S, K, HB, pre, post, tm):
    T = x.shape[0]
    C = w_pad.shape[1]
    nb, per_seq = tm // HB, S // tm
    has_b, has_post = bias is not None, post is not None
    reuse = K > SUBLANES

    def body(*refs):
        x_ref, xp_ref, w_ref = refs[:3]
        pos = 3
        b_ref = refs[pos] if has_b else None
        pos += has_b
        ppars = refs[pos:pos + (len(post[1]) if has_post else 0)]
        pos += len(ppars)
        c_ref = refs[pos]
        s_ref = refs[pos + 1] if has_post else None
        ext = refs[pos + 1 + has_post]
        first = (pl.program_id(0) % per_seq) == 0
        ext[0:HB, :] = jnp.where(first, 0.0, pre(xp_ref[...]))
        for r in range(tm // SUB):
            ext[HB + r * SUB:HB + (r + 1) * SUB, :] = pre(x_ref[r * SUB:(r + 1) * SUB, :])
        rows_of = _RowShifts(ext, refs[-1] if reuse else None, HB + tm, reuse)
        pv = [p[...] for p in ppars]
        assert not has_post or C == D
        for r in range(tm // SUB):
            for c0 in range(0, C, D):
                cols = slice(c0, c0 + D)
                acc = jnp.zeros((SUB, D), F32)
                if has_b:
                    acc = acc + b_ref[:, cols]
                for k in range(K):
                    acc = acc + w_ref[k:k + 1, cols] * rows_of.window(HB + r * SUB - (K - 1) + k, cols)
                c_ref[r * SUB:(r + 1) * SUB, cols] = acc
                if has_post:
                    s_ref[r * SUB:(r + 1) * SUB, :] = post[0](acc, *pv).astype(BF16)

    ins = [x, x, w_pad] + ([bias] if has_b else []) + (list(post[1]) if has_post else [])
    in_specs = [pl.BlockSpec((tm, xw), lambda i: (i, 0)),
                pl.BlockSpec((HB, xw), lambda i: (jnp.maximum(i * nb - 1, 0), 0)),
                pl.BlockSpec(w_pad.shape, lambda i: (0, 0))]
    in_specs += [pl.BlockSpec(p.shape, lambda i: (0, 0)) for p in ins[3:]]
    out_specs = [pl.BlockSpec((tm, C), lambda i: (i, 0))] * (1 + has_post)
    out_shape = [jax.ShapeDtypeStruct((T, C), F32)] + ([jax.ShapeDtypeStruct((T, C), BF16)] if has_post else [])
    return pl.pallas_call(body, name=name, grid=(T // tm,), in_specs=in_specs, out_specs=out_specs, out_shape=out_shape,
                          scratch_shapes=[pltpu.VMEM((HB + tm, C), F32)] + _shift_scratch(HB + tm, C, reuse),
                          compiler_params=_cp("arbitrary"))(*ins)


def dwconv_bwd(name, g, x, xw, w_pad, S, K, HB, pre, pre_bwd, tm):
    T = g.shape[0]
    C = w_pad.shape[1]
    nb, per_seq = tm // HB, S // tm
    nblk = T // HB

    reuse = K > SUBLANES

    def body(g_ref, gn_ref, x_ref, xp_ref, w_ref, dx_ref, dw_ref, dbx_ref, extg, exta, *shift_refs):
        i = pl.program_id(0)
        first = (i % per_seq) == 0
        last = (i % per_seq) == per_seq - 1

        @pl.when(i == 0)
        def _():
            dw_ref[...] = jnp.zeros(dw_ref.shape, F32)
            dbx_ref[...] = jnp.zeros(dbx_ref.shape, F32)

        extg[tm:tm + HB, :] = jnp.where(last, 0.0, gn_ref[...])
        exta[0:HB, :] = jnp.where(first, 0.0, pre(xp_ref[...]))
        for r in range(tm // SUB):
            extg[r * SUB:(r + 1) * SUB, :] = g_ref[r * SUB:(r + 1) * SUB, :]
            exta[HB + r * SUB:HB + (r + 1) * SUB, :] = pre(x_ref[r * SUB:(r + 1) * SUB, :])
        assert pre_bwd is None or C == D
        g_rows = _RowShifts(extg, shift_refs[0] if reuse else None, tm + HB, reuse)
        a_rows = _RowShifts(exta, shift_refs[1] if reuse else None, HB + tm, reuse)
        for r in range(tm // SUB):
            rows = slice(r * SUB, (r + 1) * SUB)
            for c0 in range(0, C, D):
                cols = slice(c0, c0 + D)
                acc = jnp.zeros((SUB, D), F32)
                for k in range(K):
                    acc = acc + w_ref[k:k + 1, cols] * g_rows.window(r * SUB + (K - 1) - k, cols)
                if pre_bwd is None:
                    dx_ref[rows, cols] = acc
                    dbx_ref[:, cols] += jnp.sum(acc, axis=0, keepdims=True)
                else:
                    dx = pre_bwd(x_ref[rows, :], acc)
                    dx_ref[rows, :] = dx
                    dbx_ref[...] += jnp.sum(dx, axis=0, keepdims=True)
        for k in range(K):
            for c0 in range(0, C, D):
                cols = slice(c0, c0 + D)
                p = jnp.zeros((SUB, D), F32)
                for r in range(tm // SUB):
                    p = p + extg[r * SUB:(r + 1) * SUB, cols] * a_rows.window(HB + r * SUB - (K - 1) + k, cols)
                dw_ref[k:k + 1, cols] += jnp.sum(p, axis=0, keepdims=True)

    in_specs = [pl.BlockSpec((tm, C), lambda i: (i, 0)),
                pl.BlockSpec((HB, C), lambda i: (jnp.minimum((i + 1) * nb, nblk - 1), 0)),
                pl.BlockSpec((tm, xw), lambda i: (i, 0)),
                pl.BlockSpec((HB, xw), lambda i: (jnp.maximum(i * nb - 1, 0), 0)),
                pl.BlockSpec(w_pad.shape, lambda i: (0, 0))]
    out_specs = [pl.BlockSpec((tm, xw), lambda i: (i, 0)), pl.BlockSpec((HB, C), lambda i: (0, 0)),
                 pl.BlockSpec((1, xw), lambda i: (0, 0))]
    out_shape = [jax.ShapeDtypeStruct((T, xw), F32), jax.ShapeDtypeStruct((HB, C), F32), jax.ShapeDtypeStruct((1, xw), F32)]
    return pl.pallas_call(body, name=name, grid=(T // tm,), in_specs=in_specs, out_specs=out_specs, out_shape=out_shape,
                          scratch_shapes=[pltpu.VMEM((tm + HB, C), F32), pltpu.VMEM((HB + tm, C), F32)]
                          + _shift_scratch(tm + HB, C, reuse) * 2,
                          compiler_params=_cp("arbitrary"))(g, g, x, x, w_pad)


def _glu_bwd(u, da):
    u1, sg = u[:, :D], jax.nn.sigmoid(u[:, D:])
    return jnp.concatenate([da * sg, da * u1 * sg * (1.0 - sg)], axis=1)


def _head_cols(ref, h, base=0, rows=slice(None)):
    return ref[rows, base + h * HD:base + (h + 1) * HD]


def _prep_inputs(c_ref, ab, al, dt, rows=slice(None)):
    hs = range(H)
    return ([_head_cols(c_ref, h, 0, rows) for h in hs], [_head_cols(c_ref, h, D, rows) for h in hs],
            [_head_cols(c_ref, h, 2 * D, rows) for h in hs], [_lane_pick(ab, h, HD) for h in hs],
            [_lane_pick(ab, H + h, HD) for h in hs], [_lane_pick(al, h, HD) for h in hs],
            [_lane_pick(dt, h, HD) for h in hs])


PREP_CHUNKS = 2


def gdn_prep_fwd(cpre, pab, alog, dtb):
    T = cpre.shape[0]
    nc = T // CH
    G = PREP_CHUNKS

    def body(c_ref, ab_ref, al_ref, dt_ref, u_ref, w_ref, qg_ref, kg_ref, qk_ref, t_ref, eg_ref):
        ins = [[] for _ in range(7)]
        for ci in range(G):
            rows = slice(ci * CH, (ci + 1) * CH)
            for lst, part in zip(ins, _prep_inputs(c_ref, ab_ref[rows, :], al_ref[...], dt_ref[...], rows)):
                lst += part
        us, ws, qks, qgs, kgs, egs, ts = f_prep(*ins, None, _dot_raw)
        for ci in range(G):
            rows = slice(ci * CH, (ci + 1) * CH)
            for h in range(H):
                cols, k = slice(h * HD, (h + 1) * HD), ci * H + h
                u_ref[rows, cols] = us[k]
                w_ref[rows, cols] = ws[k].astype(BF16)
                qg_ref[rows, cols] = qgs[k].astype(BF16)
                kg_ref[rows, cols] = kgs[k].astype(BF16)
                qk_ref[ci, h] = qks[k].astype(BF16)
                t_ref[ci, h] = ts[k].astype(BF16)
                eg_ref[ci, h:h + 1, :] = egs[k]

    row = lambda w: pl.BlockSpec((G * CH, w), lambda n: (n, 0))
    par = pl.BlockSpec((1, HD), lambda n: (0, 0))
    mat = pl.BlockSpec((G, H, CH, CH), lambda n: (n, 0, 0, 0))
    return pl.pallas_call(
        body, name="gdn_prep_fwd", grid=(nc // G,), in_specs=[row(3 * D), row(HD), par, par],
        out_specs=[row(D), row(D), row(D), row(D), mat, mat, pl.BlockSpec((G, H, HD), lambda n: (n, 0, 0))],
        out_shape=[jax.ShapeDtypeStruct((T, D), F32)] + [jax.ShapeDtypeStruct((T, D), BF16)] * 3
        + [jax.ShapeDtypeStruct((nc, H, CH, CH), BF16)] * 2 + [jax.ShapeDtypeStruct((nc, H, HD), F32)],
        compiler_params=_cp("arbitrary"))(cpre, pab, alog, dtb)


def gdn_prep_bwd(cpre, pab, alog, dtb, tmat, du, dw, dqg, dkg, dqk, deg):
    T = cpre.shape[0]
    nc = T // CH
    G = PREP_CHUNKS

    def body(c_ref, ab_ref, al_ref, dt_ref, t_ref, du_ref, dw_ref, dqg_ref, dkg_ref, dqk_ref, deg_ref,
             dc_ref, dab_ref, dal_ref, ddt_ref):
        @pl.when(pl.program_id(0) == 0)
        def _():
            dal_ref[...] = jnp.zeros(dal_ref.shape, F32)
            ddt_ref[...] = jnp.zeros(ddt_ref.shape, F32)

        lane = lax.broadcasted_iota(jnp.int32, (1, HD), 1)
        dal = jnp.zeros((1, HD), F32)
        ddt = jnp.zeros((1, HD), F32)
        hs = range(H)
        chunks = [(ci, slice(ci * CH, (ci + 1) * CH)) for ci in range(G)]
        t_st = [t_ref[ci, h].astype(F32) for ci, _ in chunks for h in hs]
        ins = [[] for _ in range(7)]
        for ci, rows in chunks:
            for lst, part in zip(ins, _prep_inputs(c_ref, ab_ref[rows, :], al_ref[...], dt_ref[...], rows)):
                lst += part

        def fwd(*args):
            return tuple(f_prep(*args, t_st, _dot_vjp)[:6])

        _, vjp = jax.vjp(fwd, *ins)
        dcqs, dcks, dcvs, dars, dbrs, dals, ddts = vjp((
            [_head_cols(du_ref, h, 0, rows) for _, rows in chunks for h in hs],
            [_head_cols(dw_ref, h, 0, rows) for _, rows in chunks for h in hs],
            [dqk_ref[ci, h] for ci, _ in chunks for h in hs],
            [_head_cols(dqg_ref, h, 0, rows) for _, rows in chunks for h in hs],
            [_head_cols(dkg_ref, h, 0, rows) for _, rows in chunks for h in hs],
            [deg_ref[ci, h:h + 1, :] for ci, _ in chunks for h in hs]))
        for ci, rows in chunks:
            dab = jnp.zeros((CH, HD), F32)
            for h in hs:
                k = ci * H + h
                dc_ref[rows, h * HD:(h + 1) * HD] = dcqs[k]
                dc_ref[rows, D + h * HD:D + (h + 1) * HD] = dcks[k]
                dc_ref[rows, 2 * D + h * HD:2 * D + (h + 1) * HD] = dcvs[k]
                dab = dab + jnp.where(lane == h, dars[k], 0.0) + jnp.where(lane == H + h, dbrs[k], 0.0)
                dal = dal + jnp.where(lane == h, dals[k], 0.0)
                ddt = ddt + jnp.where(lane == h, ddts[k], 0.0)
            dab_ref[rows, :] = dab
        dal_ref[...] += dal
        ddt_ref[...] += ddt

    row = lambda w: pl.BlockSpec((G * CH, w), lambda n: (n, 0))
    par = pl.BlockSpec((1, HD), lambda n: (0, 0))
    mat = pl.BlockSpec((G, H, CH, CH), lambda n: (n, 0, 0, 0))
    vec = pl.BlockSpec((G, H, HD), lambda n: (n, 0, 0))
    return pl.pallas_call(
        body, name="gdn_prep_bwd", grid=(nc // G,),
        in_specs=[row(3 * D), row(HD), par, par, mat, row(D), row(D), row(D), row(D), mat, vec],
        out_specs=[row(3 * D), row(HD), par, par],
        out_shape=[jax.ShapeDtypeStruct((T, 3 * D), F32), jax.ShapeDtypeStruct((T, HD), F32),
                   jax.ShapeDtypeStruct((1, HD), F32), jax.ShapeDtypeStruct((1, HD), F32)],
        compiler_params=_cp("arbitrary"))(cpre, pab, alog, dtb, tmat, du, dw, dqg, dkg, dqk, deg)


def gdn_scan_fwd(u, w, qg, kg, qk, eg, S):
    T = u.shape[0]
    nc, per_seq = T // CH, S // CH

    def body(u_ref, w_ref, qg_ref, kg_ref, qk_ref, eg_ref, o_ref, sall_ref, s_ref):
        @pl.when(pl.program_id(0) % per_seq == 0)
        def _():
            s_ref[...] = jnp.zeros(s_ref.shape, F32)

        hs = range(H)
        ss = [s_ref[h] for h in hs]
        os_, s2s = f_scan(ss, [_head_cols(u_ref, h) for h in hs], [_head_cols(w_ref, h) for h in hs],
                          [_head_cols(qg_ref, h) for h in hs], [_head_cols(kg_ref, h) for h in hs],
                          [qk_ref[0, h] for h in hs], [eg_ref[0, h:h + 1, :] for h in hs], _dot_raw)
        for h in hs:
            sall_ref[0, h] = ss[h]
            o_ref[:, h * HD:(h + 1) * HD] = os_[h]
            s_ref[h] = s2s[h]

    row = pl.BlockSpec((CH, D), lambda n: (n, 0))
    return pl.pallas_call(
        body, name="gdn_scan_fwd", grid=(nc,),
        in_specs=[row, row, row, row, pl.BlockSpec((1, H, CH, CH), lambda n: (n, 0, 0, 0)),
                  pl.BlockSpec((1, H, HD), lambda n: (n, 0, 0))],
        out_specs=[row, pl.BlockSpec((1, H, HD, HD), lambda n: (n, 0, 0, 0))],
        out_shape=[jax.ShapeDtypeStruct((T, D), F32), jax.ShapeDtypeStruct((nc, H, HD, HD), F32)],
        scratch_shapes=[pltpu.VMEM((H, HD, HD), F32)], compiler_params=_cp("arbitrary"))(u, w, qg, kg, qk, eg)


def gdn_scan_bwd(do, u, w, qg, kg, qk, eg, sall, S):
    T = u.shape[0]
    nc, per_seq = T // CH, S // CH

    def body(do_ref, u_ref, w_ref, qg_ref, kg_ref, qk_ref, eg_ref, sall_ref,
             du_ref, dw_ref, dqg_ref, dkg_ref, dqk_ref, deg_ref, ds_ref):
        n = nc - 1 - pl.program_id(0)

        @pl.when(n % per_seq == per_seq - 1)
        def _():
            ds_ref[...] = jnp.zeros(ds_ref.shape, F32)

        hs = range(H)

        def fwd(*args):
            return f_scan(*args, _dot_vjp)

        _, vjp = jax.vjp(fwd, [sall_ref[0, h] for h in hs], [_head_cols(u_ref, h) for h in hs],
                         [_head_cols(w_ref, h).astype(F32) for h in hs], [_head_cols(qg_ref, h).astype(F32) for h in hs],
                         [_head_cols(kg_ref, h).astype(F32) for h in hs], [qk_ref[0, h].astype(F32) for h in hs],
                         [eg_ref[0, h:h + 1, :] for h in hs])
        dss, dus, dws, dqgs, dkgs, dqks, degs = vjp(([_head_cols(do_ref, h) for h in hs], [ds_ref[h] for h in hs]))
        for h in hs:
            cols = slice(h * HD, (h + 1) * HD)
            du_ref[:, cols] = dus[h]
            dw_ref[:, cols] = dws[h]
            dqg_ref[:, cols] = dqgs[h]
            dkg_ref[:, cols] = dkgs[h]
            dqk_ref[0, h] = dqks[h]
            deg_ref[0, h:h + 1, :] = degs[h]
            ds_ref[h] = dss[h]

    rev = lambda n: (nc - 1 - n, 0)
    row = pl.BlockSpec((CH, D), rev)
    mat = pl.BlockSpec((1, H, CH, CH), lambda n: (nc - 1 - n, 0, 0, 0))
    vec = pl.BlockSpec((1, H, HD), lambda n: (nc - 1 - n, 0, 0))
    return pl.pallas_call(
        body, name="gdn_scan_bwd", grid=(nc,),
        in_specs=[row, row, row, row, row, mat, vec, pl.BlockSpec((1, H, HD, HD), lambda n: (nc - 1 - n, 0, 0, 0))],
        out_specs=[row, row, row, row, mat, vec],
        out_shape=[jax.ShapeDtypeStruct((T, D), F32)] * 4
        + [jax.ShapeDtypeStruct((nc, H, CH, CH), F32), jax.ShapeDtypeStruct((nc, H, HD), F32)],
        scratch_shapes=[pltpu.VMEM((H, HD, HD), F32)], compiler_params=_cp("arbitrary"))(do, u, w, qg, kg, qk, eg, sall)


def xor_exchange(name, ins, inplace, out_shapes, plan, n_remote, n_local=0):
    n_in, n_ip, n_out = len(ins), len(inplace), len(out_shapes)

    def body(*refs):
        in_refs = refs[:n_in]
        ip_refs = refs[n_in + n_ip:n_in + 2 * n_ip]
        out_refs = refs[n_in + 2 * n_ip:n_in + 2 * n_ip + n_out]
        send_sems, recv_sems, loc_sems = refs[n_in + 2 * n_ip + n_out:]
        x, y, c = lax.axis_index("x"), lax.axis_index("y"), lax.axis_index("c")
        remote, local = plan(in_refs, ip_refs, out_refs, (x, y, c))
        assert len(remote) == n_remote and len(local) == n_local
        copies = []
        for k, ((dx, dy, dc), src, dst) in enumerate(remote):
            peer = (1 - x if dx else x, 1 - y if dy else y, 1 - c if dc else c)
            copies.append(pltpu.make_async_remote_copy(src_ref=src, dst_ref=dst, send_sem=send_sems.at[k],
                                                       recv_sem=recv_sems.at[k], device_id=peer, device_id_type=MESH))
        for cp in copies:
            cp.start()
        locs = [pltpu.make_async_copy(src, dst, loc_sems.at[k]) for k, (src, dst) in enumerate(local)]
        for cp in locs:
            cp.start()
        for cp in copies:
            cp.wait()
        for cp in locs:
            cp.wait()

    anyspec = pl.BlockSpec(memory_space=pl.ANY)
    res = pl.pallas_call(
        body, name=name, in_specs=[anyspec] * (n_in + n_ip), out_specs=[anyspec] * (n_ip + n_out),
        out_shape=[jax.ShapeDtypeStruct(a.shape, a.dtype) for a in inplace] + list(out_shapes),
        input_output_aliases={n_in + i: i for i in range(n_ip)},
        scratch_shapes=[pltpu.SemaphoreType.DMA((n_remote,)), pltpu.SemaphoreType.DMA((n_remote,)),
                        pltpu.SemaphoreType.DMA((max(n_local, 1),))],
        )(*ins, *inplace)
    return list(res[:n_ip]), list(res[n_ip:])


HBM_SPEC = pl.BlockSpec(memory_space=pltpu.HBM)
SEM_SPEC = pl.BlockSpec(memory_space=pltpu.SEMAPHORE)


def _flip_copies(plan, refs, send_sems, recv_sems):
    x, y, c = lax.axis_index("x"), lax.axis_index("y"), lax.axis_index("c")
    copies = []
    for k, ((dx, dy, dc), src, dst) in enumerate(plan(refs, (x, y, c))):
        peer = (1 - x if dx else x, 1 - y if dy else y, 1 - c if dc else c)
        copies.append(pltpu.make_async_remote_copy(src_ref=src, dst_ref=dst, send_sem=send_sems.at[k],
                                                   recv_sem=recv_sems.at[k], device_id=peer, device_id_type=MESH))
    return copies


def xor_start(name, arrays, plan, n_remote, after):
    n = len(arrays)

    def body(*refs):
        for cp in _flip_copies(plan, refs[:n], refs[n + 1], refs[n + 2]):
            cp.start()
        refs[-1][...] = jnp.zeros(refs[-1].shape, F32)

    res = pl.pallas_call(
        body, name=name, in_specs=[HBM_SPEC] * n + [pl.BlockSpec(memory_space=pl.ANY)],
        out_shape=(pltpu.SemaphoreType.DMA((n_remote,)), pltpu.SemaphoreType.DMA((n_remote,)),
                   *[pltpu.HBM(a.shape, a.dtype) for a in arrays], jax.ShapeDtypeStruct((8, 128), F32)),
        out_specs=(SEM_SPEC, SEM_SPEC, *([HBM_SPEC] * n), pl.BlockSpec(memory_space=pltpu.VMEM)),
        input_output_aliases={i: 2 + i for i in range(n)},
        compiler_params=pltpu.CompilerParams(has_side_effects=pltpu.SideEffectType.DATAFLOW_SIDE_EFFECTING),
    )(*[pltpu.with_memory_space_constraint(a, pltpu.HBM) for a in arrays], after)
    return res[0], res[1], list(res[2:2 + n]), res[2 + n]


def xor_wait(name, send_sems, recv_sems, arrays, plan, after):
    n = len(arrays)

    def body(*refs):
        for cp in _flip_copies(plan, refs[:n], refs[n], refs[n + 1]):
            cp.wait_send()
            cp.wait_recv()

    return list(pl.pallas_call(
        body, name=name, in_specs=[HBM_SPEC] * n + [SEM_SPEC, SEM_SPEC, pl.BlockSpec(memory_space=pl.ANY)],
        out_shape=[pltpu.HBM(a.shape, a.dtype) for a in arrays], out_specs=[HBM_SPEC] * n,
        input_output_aliases={i: i for i in range(n)},
        compiler_params=pltpu.CompilerParams(has_side_effects=pltpu.SideEffectType.DATAFLOW_SIDE_EFFECTING),
    )(*arrays, send_sems, recv_sems, after))


class WSpec:
    def __init__(self, name, full, sa, ha, group, layer=None, lead=False):
        self.name, self.full, self.sa, self.ha, self.group, self.layer, self.lead = name, full, sa, ha, group, layer, lead
        self.ws = 1 if lead else full[sa] // 4
        self.wh = full[ha] // 2

    def shard_shape(self):
        if self.lead:
            return tuple(n for a, n in enumerate(self.full) if a != self.sa)
        return tuple(self.ws if a == self.sa else n for a, n in enumerate(self.full))

    def half_full_shape(self):
        return tuple(self.wh if a == self.ha else n for a, n in enumerate(self.full))

    def shard_half_shape(self):
        s = list(self.half_full_shape())
        if self.lead:
            del s[self.sa]
        else:
            s[self.sa] = self.ws
        return tuple(s)

    def full_view(self, ref, q=None, h=None):
        idx = []
        for a in range(len(self.full)):
            if a == self.sa and q is not None:
                idx.append(q if self.lead else pl.ds(pl.multiple_of(q * self.ws, self.ws), self.ws))
            elif a == self.ha and h is not None:
                idx.append(pl.ds(pl.multiple_of(h * self.wh, self.wh), self.wh))
            else:
                idx.append(slice(None))
        return ref.at[tuple(idx)]

    def shard_view(self, ref, h):
        idx = [] if self.layer is None else [self.layer]
        for a in range(len(self.full)):
            if self.lead and a == self.sa:
                continue
            idx.append(pl.ds(pl.multiple_of(h * self.wh, self.wh), self.wh) if a == self.ha else slice(None))
        return ref.at[tuple(idx)]

    def rows_cols(self, shard, half):
        rows, cols = self.full[-2:]
        if shard and not self.lead:
            rows, cols = (rows // 4, cols) if self.sa == 0 else (rows, cols // 4)
        if half:
            rows, cols = (rows // 2, cols) if self.ha == len(self.full) - 2 else (rows, cols // 2)
        return rows, cols

    def spec(self, tr, cw, nr, shard=False, half=False, has_lead=False, stacked=False):
        two_d = len(self.full) == 2
        shard_on_cols = two_d and self.sa == 1
        half_on_cols = two_d and self.ha == 1
        layer = self.layer

        def index(*args):
            pref = args[-1]
            i = args[-2]
            r, cblk, pre = i, 0, ()
            if shard:
                if self.lead:
                    pre = (pref[0],)
                elif shard_on_cols:
                    cblk = pref[0]
                else:
                    r = pref[0] * nr + i
            elif has_lead:
                pre = (args[0],)
            if half:
                if half_on_cols:
                    cblk = pref[1]
                else:
                    r = pref[1] * nr + i
            if stacked:
                pre = (layer,) + pre
            return pre + (r, cblk)

        n_pre = int(stacked) + int(self.lead and (shard or has_lead))
        return pl.BlockSpec((None,) * n_pre + (tr, cw), index)


WSPECS = [
    WSpec("cv_w_pw1", (D, 2 * D), 1, 0, 0),
    WSpec("cv_w_pw2", (D, D), 0, 1, 1),
    WSpec("gdn_w_in", (4, D, (4 * D + 2 * H) // 4), 0, 1, 2, lead=True),
    WSpec("gdn_w_out", (D, D), 0, 1, 3),
    WSpec("mlp_w1_0", (D, DFF), 1, 0, 4, layer=0),
    WSpec("mlp_w1_1", (D, DFF), 1, 0, 4, layer=1),
    WSpec("mlp_w2_0", (DFF, D), 0, 1, 5, layer=0),
    WSpec("mlp_w2_1", (DFF, D), 0, 1, 5, layer=1),
]
FLIPS = [(1, 0, 0), (0, 1, 0), (1, 1, 0)]
SIB = (0, 0, 1)


def _chip(x, y):
    return 2 * x + y


def _prefetch_call(name, body, grid, in_specs, out_specs, out_shape, pref, args, aliases=None):
    return pl.pallas_call(
        body, name=name, out_shape=out_shape, input_output_aliases=aliases or {},
        grid_spec=pltpu.PrefetchScalarGridSpec(num_scalar_prefetch=1, grid=grid, in_specs=in_specs, out_specs=out_specs),
        compiler_params=_cp(*(("arbitrary",) * len(grid))))(pref, *args)


def place_shard(ws, shard, pref):
    rows, cols = ws.rows_cols(True, False)
    tr = _ew_rows(rows, cols)
    nr = rows // tr
    stacked = ws.layer is not None
    layer = ws.layer

    def body(_, s_ref, o_ref):
        o_ref[...] = s_ref[...].astype(BF16)

    in_spec = pl.BlockSpec(((None,) if stacked else ()) + (tr, cols),
                           (lambda i, p: (layer, i, 0)) if stacked else (lambda i, p: (i, 0)))
    return _prefetch_call("place_" + ws.name, body, (nr,), [in_spec], ws.spec(tr, cols, nr, shard=True),
                          jax.ShapeDtypeStruct(ws.full, BF16), pref, [shard])


FIRST = [0]
LAYER0 = [1, 4, 6]
LAYER1 = [2, 3, 5, 7]


def _plan_gather_chips(sel):
    def plan(refs, pos):
        x, y, c = pos
        remote = []
        for j, i in enumerate(sel):
            mine = WSPECS[i].full_view(refs[j], _chip(x, y), c)
            remote += [(f, mine, mine) for f in FLIPS]
        return remote
    return plan


def gather_cores(tag, sel, nat):
    def plan(in_refs, ip_refs, out_refs, pos):
        x, y, c = pos
        remote = []
        for j, i in enumerate(sel):
            for (dx, dy, _) in FLIPS:
                got = WSPECS[i].full_view(ip_refs[j], _chip(1 - x if dx else x, 1 - y if dy else y), c)
                remote.append((SIB, got, got))
        return remote, []

    return xor_exchange("gather_cores" + tag, [], nat, [], plan, 3 * len(sel))[0]


def gather_start(tag, sel, placed, after):
    plan = _plan_gather_chips(sel)
    send, recv, arrays, token = xor_start("gather_chips%s_start" % tag, [placed[i] for i in sel], plan, 3 * len(sel), after)
    return (tag, sel, plan, send, recv, arrays), token


def gather_wait(state, after):
    tag, sel, plan, send, recv, arrays = state
    return gather_cores(tag, sel, xor_wait("gather_chips%s_wait" % tag, send, recv, arrays, plan, after))


def reduce_start(tag, sel, grads, pref):
    plan = _plan_reduce_chips(sel)
    sums = chip_sums(tag, sel, grads, pref)
    send, recv, arrays, token = xor_start("reduce_chips%s_start" % tag,
                                          sums + [lax.empty(s.shape, s.dtype) for s in _parts_shapes(sel)], plan,
                                          3 * len(sel), pref)
    return (tag, sel, plan, send, recv, arrays), token


def reduce_wait(state, after):
    tag, sel, plan, send, recv, arrays = state
    arrays = xor_wait("reduce_chips%s_wait" % tag, send, recv, arrays, plan, after)
    return sel, arrays[:len(sel)], arrays[len(sel):]


def gather_first(sel, placed, wdw_shard, wcv_shard):
    plan_w = _plan_gather_chips(sel)

    def plan(in_refs, ip_refs, out_refs, pos):
        x, y, c = pos
        remote, local = plan_w(ip_refs, pos), []
        for j, width in enumerate((D // 4, 3 * D // 4)):
            dst = out_refs[j].at[:, pl.ds(pl.multiple_of(_chip(x, y) * width, 128), width)]
            local.append((in_refs[j], dst))
            remote += [(f, in_refs[j], dst) for f in FLIPS]
        return remote, local

    taps = [jax.ShapeDtypeStruct((KCV, D), F32), jax.ShapeDtypeStruct((KSC, 3 * D), F32)]
    nat, (wdw, wcv) = xor_exchange("gather_chips0", [wdw_shard, wcv_shard], [placed[i] for i in sel], taps, plan,
                                   3 * (len(sel) + 2), 2)
    return gather_cores("0", sel, nat), wdw, wcv


def half_add(ws, g, rsib, pref):
    rows, cols = ws.rows_cols(False, True)
    tr = _ew_rows(rows, cols)
    nr = rows // tr

    def body(_, a_ref, b_ref, o_ref):
        o_ref[...] = (a_ref[...] + b_ref[...]).astype(BF16)

    whole = ws.spec(tr, cols, nr, has_lead=ws.lead)
    return _prefetch_call("reduce_add_" + ws.name, body, (4, nr) if ws.lead else (nr,),
                          [ws.spec(tr, cols, nr, half=True, has_lead=ws.lead), whole], whole,
                          jax.ShapeDtypeStruct(ws.half_full_shape(), BF16), pref, [g, rsib])


def shard_sum(ws, s, parts, buf, pref):
    rows, cols = ws.rows_cols(True, True)
    tr = _ew_rows(rows, cols)
    nr = rows // tr
    stacked = ws.layer is not None

    def body(_, s_ref, p_ref, *rest):
        rest[-1][...] = ((s_ref[...].astype(F32) + p_ref[0].astype(F32)) + p_ref[1].astype(F32)) + p_ref[2].astype(F32)

    in_specs = [ws.spec(tr, cols, nr, shard=True, has_lead=ws.lead), pl.BlockSpec((3, tr, cols), lambda i, p: (0, i, 0))]
    args, aliases = [s, parts], {}
    if buf is not None:
        in_specs.append(pl.BlockSpec(memory_space=pl.ANY))
        args.append(buf)
        aliases = {3: 0}
    shape = ((2,) if stacked else ()) + ws.shard_shape()
    return _prefetch_call("reduce_sum_" + ws.name, body, (nr,), in_specs, ws.spec(tr, cols, nr, half=True, stacked=stacked),
                          jax.ShapeDtypeStruct(shape, F32), pref, args, aliases)


def chip_sums(tag, sel, grads, pref):
    def plan(in_refs, ip_refs, out_refs, pos):
        c = pos[2]
        return [(SIB, WSPECS[i].full_view(in_refs[j], None, 1 - c), out_refs[j]) for j, i in enumerate(sel)], []

    halves = [jax.ShapeDtypeStruct(WSPECS[i].half_full_shape(), F32) for i in sel]
    _, rsib = xor_exchange("reduce_cores" + tag, grads, [], halves, plan, len(sel))
    return [half_add(WSPECS[i], grads[j], rsib[j], pref) for j, i in enumerate(sel)]


def _plan_reduce_chips(sel):
    n = len(sel)

    def plan(refs, pos):
        x, y, c = pos
        remote = []
        for j, i in enumerate(sel):
            for s, (dx, dy, _) in enumerate(FLIPS):
                qq = _chip(1 - x if dx else x, 1 - y if dy else y)
                remote.append(((dx, dy, 0), WSPECS[i].full_view(refs[j], qq), refs[n + j].at[s]))
        return remote
    return plan


def _parts_shapes(sel):
    return [jax.ShapeDtypeStruct((3,) + WSPECS[i].shard_half_shape(), BF16) for i in sel]


def finish_reduce(sums, parts, pref):
    n = len(WSPECS)
    bufs = {}
    for i, ws in enumerate(WSPECS):
        bufs[ws.group] = shard_sum(ws, sums[i], parts[i], bufs.get(ws.group), pref)

    def plan3(in_refs, ip_refs, out_refs, pos):
        c = pos[2]
        remote = []
        for ws in WSPECS:
            mine = ws.shard_view(ip_refs[ws.group], c)
            remote.append((SIB, mine, mine))
        return remote, []

    return xor_exchange("reduce_swap", [], [bufs[g] for g in sorted(bufs)], [], plan3, n)[0]


def gather_small(buf):
    flips = [(dx, dy, dc) for dx in (0, 1) for dy in (0, 1) for dc in (0, 1)][1:]

    def plan(in_refs, ip_refs, out_refs, pos):
        x, y, c = pos
        me = 4 * x + 2 * y + c
        dst = out_refs[0].at[me]
        return [(f, in_refs[0], dst) for f in flips], [(in_refs[0], dst)]

    return xor_exchange("gather_small", [buf], [], [jax.ShapeDtypeStruct((8,) + buf.shape, F32)], plan, 7, 1)[1][0]


def _pad_rows(a, rows):
    return jnp.pad(a, ((0, rows - a.shape[0]), (0, 0)))


def _row1(v):
    v = v.reshape((1, -1))
    return jnp.pad(v, ((0, 0), (0, D - v.shape[1])))


def _rms_fwd(name, h, g, tm):
    return row_call(name, lambda hh, gg: ((f_rms(hh, gg),), ()), [(h, D, 0)], [g], [(D, BF16)], [], tm)[0]


def _res_rms(h, g):
    return (h, f_rms(h, g)), ()


def _rms_bwd_epi(dhn, h, dres, g):
    _, vjp = jax.vjp(f_rms, h, g)
    dh, dg = vjp(dhn)
    dh = dh + dres
    return (dh,), (dg, jnp.sum(dh, axis=0, keepdims=True))


def _mlp_bwd(tag, dh, h, g, w1, w2, hn, z1, token=None):
    dz1 = mm("mlp_down_dx" + tag, dh, w2, "NT", BF16,
             epi=lambda acc, z, *_: acc * (2.0 * jnp.maximum(z.astype(F32), 0.0)),
             epi_ins=[(z1, "tile")] + ([] if token is None else [(token, "whole")]))
    dw2 = mm_tn("mlp_down_dw" + tag, z1, dh, a_fn=f_relu2)
    dh_in, dg, colsum = mm("mlp_up_dx" + tag, dz1, w1, "NT", [F32], epi=_rms_bwd_epi,
                           epi_ins=[(h, "tile"), (dh, "tile"), (g, "row")], accs=[(1, D), (1, D)])
    dw1 = mm_tn("mlp_up_dw" + tag, hn, dz1)
    return dh_in, dg, colsum, dw1, dw2


def kernel(x, norm_mix_g, norm_ffn_g, final_norm_g, cv_w_pw1, cv_b_pw1, cv_w_dw, cv_b_dw, cv_ln_g, cv_ln_b, cv_w_pw2, cv_b_pw2, gdn_w_in, gdn_conv_w, gdn_a_log, gdn_dt_bias, gdn_norm_g, gdn_w_out, mlp_w1, mlp_w2, loss_target, m_norm_mix_g, m_norm_ffn_g, m_final_norm_g, m_cv_w_pw1, m_cv_b_pw1, m_cv_w_dw, m_cv_b_dw, m_cv_ln_g, m_cv_ln_b, m_cv_w_pw2, m_cv_b_pw2, m_gdn_w_in, m_gdn_conv_w, m_gdn_a_log, m_gdn_dt_bias, m_gdn_norm_g, m_gdn_w_out, m_mlp_w1, m_mlp_w2, v_norm_mix_g, v_norm_ffn_g, v_final_norm_g, v_cv_w_pw1, v_cv_b_pw1, v_cv_w_dw, v_cv_b_dw, v_cv_ln_g, v_cv_ln_b, v_cv_w_pw2, v_cv_b_pw2, v_gdn_w_in, v_gdn_conv_w, v_gdn_a_log, v_gdn_dt_bias, v_gdn_norm_g, v_gdn_w_out, v_mlp_w1, v_mlp_w2):
    env = dict(locals())
    bl, S, _ = x.shape
    T = bl * S
    tm = min(256, S)
    xf = x.reshape((T, D))
    tgt = loss_target.reshape((T, D))

    chip = 2 * lax.axis_index("x") + lax.axis_index("y")
    pref = jnp.stack([chip, lax.axis_index("c")]).astype(jnp.int32)
    big = [cv_w_pw1[0], cv_w_pw2[0], gdn_w_in[0], gdn_w_out[0], mlp_w1, mlp_w2]
    placed = [place_shard(ws, big[ws.group], pref) for ws in WSPECS]
    (w_pw1,), wdw, wcv = gather_first(FIRST, placed, cv_w_dw[0], gdn_conv_w[0])
    gather_a, token_a = gather_start("A", LAYER0, placed, wcv)
    gather_b, token_b = gather_start("B", LAYER1, placed, wcv)
    wdw_p, wcv_p = _pad_rows(wdw, HB_CV), _pad_rows(wcv, HB_SC)
    alog_p = jnp.pad(gdn_a_log, ((0, 0), (0, HD - H)))
    dtb_p = jnp.pad(gdn_dt_bias, ((0, 0), (0, HD - H)))
    g_mix0, g_mix1 = norm_mix_g[0:1] + (token_a[0, 0] + token_b[0, 0]), norm_mix_g[1:2]
    g_ffn0, g_ffn1 = norm_ffn_g[0:1], norm_ffn_g[1:2]
    g_fin = final_norm_g.reshape((1, D))

    hn0 = _rms_fwd("rms_mix0", xf, g_mix0, tm)
    u = mm("cv_pw1", hn0, w_pw1, "NN", F32, epi=lambda acc, b: acc + b, epi_ins=[(cv_b_pw1, "row")])
    dwc, s_act = dwconv_fwd("cv_dwconv", u, 2 * D, wdw_p, cv_b_dw, S, KCV, HB_CV, f_glu, (f_ln_silu, (cv_ln_g, cv_ln_b)), tm)
    w_pw2, w1_0, w2_0 = gather_wait(gather_a, dwc)
    h1, hnf0 = mm("cv_pw2", s_act, w_pw2, "NN", [F32, BF16], epi=lambda acc, b, r, g: _res_rms(acc + b + r, g),
                  epi_ins=[(cv_b_pw2, "row"), (xf, "tile"), (g_ffn0, "row")])
    z1_0 = mm("mlp_up0", hnf0, w1_0, "NN", BF16)
    h2, hn2 = mm("mlp_down0", z1_0, w2_0, "NN", [F32, BF16], a_fn=f_relu2, epi=lambda acc, r, g: _res_rms(acc + r, g),
                 epi_ins=[(h1, "tile"), (g_mix1, "row")])

    w_in_sm, w_out, w1_1, w2_1 = gather_wait(gather_b, h2)
    w_in = jnp.transpose(w_in_sm, (1, 0, 2)).reshape((D, 4 * D + 2 * H))
    w_qkv, w_z = w_in[:, :3 * D], w_in[:, 3 * D:4 * D]
    w_qkvz = w_in[:, :4 * D]
    w_ab = jnp.pad(w_in[:, 4 * D:], ((0, 0), (0, HD - 2 * H)))
    pqkvz = mm("gdn_in", hn2, w_qkvz, "NN", F32)
    pab = mm("gdn_in_ab", hn2, w_ab, "NN", F32)
    cpre = dwconv_fwd("gdn_conv", pqkvz, 3 * D, wcv_p, None, S, KSC, HB_SC, lambda v: v, None, tm)[0]
    gu, gw, gqg, gkg, gqk, gt, geg = gdn_prep_fwd(cpre, pab, alog_p, dtb_p)
    o, sall = gdn_scan_fwd(gu, gw, gqg, gkg, gqk, geg, S)
    on = row_call("gdn_post", lambda oo, zz, ng: ((f_post(oo, zz, ng),), ()), [(o, D, 0), (pqkvz, D, 3)],
                  [gdn_norm_g], [(D, BF16)], [], tm)[0]
    h3, hnf1 = mm("gdn_out", on, w_out, "NN", [F32, BF16], epi=lambda acc, r, g: _res_rms(acc + r, g),
                  epi_ins=[(h2, "tile"), (g_ffn1, "row")])
    z1_1 = mm("mlp_up1", hnf1, w1_1, "NN", BF16)

    def head(acc, res, tt, gg):
        def loss_of(h_, g_):
            return 0.5 * jnp.sum(jnp.mean(jnp.square(f_rms(h_, g_) - tt), axis=-1))
        lv, (dh_, dg_) = jax.value_and_grad(loss_of, (0, 1))(acc + res, gg)
        return (dh_,), (dg_, jnp.full((1, D), lv, F32))

    dh4, dg_fin, loss_row = mm("mlp_down1", z1_1, w2_1, "NN", [F32], a_fn=f_relu2, epi=head,
                               epi_ins=[(h3, "tile"), (tgt, "tile"), (g_fin, "row")], accs=[(1, D), (1, D)])

    dh3, dg_ffn1, _, dw1_1, dw2_1 = _mlp_bwd("1", dh4, h3, g_ffn1, w1_1, w2_1, hnf1, z1_1)
    dw_out = mm_tn("gdn_out_dw", on, dh3)

    def post_bwd(don, oo, zz, ng):
        _, vjp = jax.vjp(f_post, oo, zz, ng)
        do_, dz_, dng_ = vjp(don)
        return (do_, dz_), (dng_,)

    do, dz, dng = mm("gdn_out_dx", dh3, w_out, "NT", [F32, F32], epi=post_bwd,
                     epi_ins=[(o, "tile"), (pqkvz, ("cols", 3)), (gdn_norm_g, "whole")], accs=[(1, HD)])
    du, dw, dqg, dkg, dqk, deg = gdn_scan_bwd(do, gu, gw, gqg, gkg, gqk, geg, sall, S)
    dcpre, dpab, dalog, ddtb = gdn_prep_bwd(cpre, pab, alog_p, dtb_p, gt, du, dw, dqg, dkg, dqk, deg)
    dqkv, dwcv, _ = dwconv_bwd("gdn_conv_bwd", dcpre, pqkvz, 3 * D, wcv_p, S, KSC, HB_SC, lambda v: v, None, tm)
    dhn2 = mm("gdn_in_dx_ab", dpab, w_ab, "NT", F32)
    dhn2 = mm("gdn_in_dx_z", dz, w_z, "NT", F32, epi=lambda acc, r: acc + r, epi_ins=[(dhn2, "tile")])
    dh2, dg_mix1, _ = mm("gdn_in_dx_qkv", dqkv, w_qkv, "NT", [F32],
                         epi=lambda acc, prev, hh, rr, gg: _rms_bwd_epi(acc + prev, hh, rr, gg),
                         epi_ins=[(dhn2, "tile"), (h2, "tile"), (dh3, "tile"), (g_mix1, "row")], accs=[(1, D), (1, D)])
    dw_in = jnp.concatenate([mm_tn("gdn_in_dw_qkv", hn2, dqkv), mm_tn("gdn_in_dw_z", hn2, dz),
                             mm_tn("gdn_in_dw_ab", hn2, dpab)[:, :2 * H]], axis=1)

    dw_in_sm = jnp.transpose(dw_in.reshape((D, 4, D + 4)), (1, 0, 2))
    reduce_b, rtoken_b = reduce_start("B", LAYER1, [dw_in_sm, dw_out, dw1_1, dw2_1], pref)

    dh1, dg_ffn0, db_pw2, dw1_0, dw2_0 = _mlp_bwd("0", dh2, h1, g_ffn0, w1_0, w2_0, hnf0, z1_0, rtoken_b)
    reduce_a, rtoken_a = reduce_start("A", LAYER0[1:], [dw1_0, dw2_0], pref)
    dw_pw2 = mm_tn("cv_pw2_dw", s_act, dh1)

    def ln_bwd(ds, xx, gg, bb, *_):
        _, vjp = jax.vjp(f_ln_silu, xx, gg, bb)
        dx_, dg_, db_ = vjp(ds)
        return (dx_,), (dg_, db_, jnp.sum(dx_, axis=0, keepdims=True))

    ddw, dln_g, dln_b, db_dw = mm("cv_pw2_dx", dh1, w_pw2, "NT", [F32], epi=ln_bwd,
                                  epi_ins=[(dwc, "tile"), (cv_ln_g, "row"), (cv_ln_b, "row"), (rtoken_a, "whole")],
                                  accs=[(1, D)] * 3)
    du_cv, dwdw, db_pw1 = dwconv_bwd("cv_dwconv_bwd", ddw, u, 2 * D, wdw_p, S, KCV, HB_CV, f_glu, _glu_bwd, tm)
    dw_pw1 = mm_tn("cv_pw1_dw", hn0, du_cv)
    grad_x, dg_mix0, _ = mm("cv_pw1_dx", du_cv, w_pw1, "NT", [F32], epi=_rms_bwd_epi,
                            epi_ins=[(xf, "tile"), (dh1, "tile"), (g_mix0, "row")], accs=[(1, D), (1, D)])

    last = FIRST + LAYER0[:1]
    sums_l = chip_sums("0", last, [dw_pw1, dw_pw2], pref)
    plan_l = _plan_reduce_chips(last)
    _, parts_l = xor_exchange("reduce_chips0", sums_l, [], _parts_shapes(last),
                              lambda ins_, ip_, outs_, pos: (plan_l(list(ins_) + list(outs_), pos), []), 3 * len(last))
    sums, parts = [None] * len(WSPECS), [None] * len(WSPECS)
    for sel, sums_s, parts_s in ((last, sums_l, parts_l), reduce_wait(reduce_a, grad_x), reduce_wait(reduce_b, grad_x)):
        for j, i in enumerate(sel):
            sums[i], parts[i] = sums_s[j], parts_s[j]
    g_pw1, g_pw2, g_in, g_out, g_w1, g_w2 = finish_reduce(sums, parts, pref)

    small = jnp.concatenate([
        dg_mix0, dg_mix1, dg_ffn0, dg_ffn1, dg_fin, db_pw1.reshape((2, D)), db_dw, dln_g, dln_b, db_pw2,
        _row1(dalog[:, :H]), _row1(ddtb[:, :H]), _row1(dng), loss_row, jnp.zeros((1, D), F32),
        dwdw, dwcv[:KSC].reshape((3 * KSC, D)), jnp.zeros((NSMALL - 48 - 3 * KSC, D), F32)], axis=0)
    small_all = gather_small(small)

    def pack(a, b, c_, d, e, f, g_, h_, i_, j_, k_):
        return jnp.concatenate([a, b, c_.reshape((1, D)), d.reshape((2, D)), e, f, g_, h_, _row1(i_), _row1(j_), _row1(k_),
                                jnp.zeros((2, D), F32)], axis=0)

    order = lambda p: (p + "norm_mix_g", p + "norm_ffn_g", p + "final_norm_g", p + "cv_b_pw1", p + "cv_b_dw", p + "cv_ln_g",
                       p + "cv_ln_b", p + "cv_b_pw2", p + "gdn_a_log", p + "gdn_dt_bias", p + "gdn_norm_g")
    w16, m16, v16 = (pack(*[env[nm] for nm in order(p)]) for p in ("", "m_", "v_"))

    def small_step(ga, ww, mm_, vv):
        gsum = ga[0]
        for dev in range(1, 8):
            gsum = gsum + ga[dev]
        delta, m2, v2 = f_adamw(ww, gsum[:16], mm_, vv)
        return gsum, delta, m2, v2

    def small_body(ga_ref, w_ref, m_ref, v_ref, g_out, d_out, m_out, v_out):
        gsum, delta, m2, v2 = small_step(ga_ref[...], w_ref[...], m_ref[...], v_ref[...])
        g_out[...] = gsum
        d_out[...] = delta
        m_out[...] = m2
        v_out[...] = v2

    vm = pl.BlockSpec(memory_space=pltpu.VMEM)
    sg, sd, sm, sv = pl.pallas_call(
        small_body, name="adamw_small", in_specs=[vm] * 4, out_specs=[vm] * 4,
        out_shape=[jax.ShapeDtypeStruct((NSMALL, D), F32)] + [jax.ShapeDtypeStruct((16, D), F32)] * 3)(small_all, w16, m16, v16)

    def unpack(b):
        return (b[0:2], b[2:4], b[4], b[5:7].reshape((1, 2 * D)), b[7:8], b[8:9], b[9:10], b[10:11],
                b[11:12, :H], b[12:13, :H], b[13:14, :HD])

    loss = sg[14, 0]
    g_dw = lax.dynamic_slice(sg[16:16 + KCV], (0, chip * (D // 4)), (KCV, D // 4))
    g_cv = lax.dynamic_slice(sg[48:48 + 3 * KSC].reshape((KSC, 3 * D)), (0, chip * (3 * D // 4)), (KSC, 3 * D // 4))

    def adamw(name, w, g, m, v):
        lead = w.shape[:-2]
        if len(lead) == 1 and lead[0] == 1:
            d, m2, v2 = ew_call(name, f_adamw, [w[0], g.reshape(w.shape[1:]), m[0], v[0]], 3)
            return g.reshape(w.shape), d[None], m2[None], v2[None]
        return (g.reshape(w.shape),) + tuple(ew_call(name, f_adamw, [w, g.reshape(w.shape), m, v], 3))

    res = {
        "cv_w_pw1": adamw("adamw_pw1", cv_w_pw1, g_pw1, m_cv_w_pw1, v_cv_w_pw1),
        "cv_w_dw": adamw("adamw_dw", cv_w_dw, g_dw, m_cv_w_dw, v_cv_w_dw),
        "cv_w_pw2": adamw("adamw_pw2", cv_w_pw2, g_pw2, m_cv_w_pw2, v_cv_w_pw2),
        "gdn_w_in": adamw("adamw_win", gdn_w_in, g_in, m_gdn_w_in, v_gdn_w_in),
        "gdn_conv_w": adamw("adamw_cvw", gdn_conv_w, g_cv, m_gdn_conv_w, v_gdn_conv_w),
        "gdn_w_out": adamw("adamw_wout", gdn_w_out, g_out, m_gdn_w_out, v_gdn_w_out),
        "mlp_w1": adamw("adamw_w1", mlp_w1, g_w1, m_mlp_w1, v_mlp_w1),
        "mlp_w2": adamw("adamw_w2", mlp_w2, g_w2, m_mlp_w2, v_mlp_w2),
    }
    names = ("norm_mix_g", "norm_ffn_g", "final_norm_g", "cv_b_pw1", "cv_b_dw", "cv_ln_g", "cv_ln_b", "cv_b_pw2",
             "gdn_a_log", "gdn_dt_bias", "gdn_norm_g")
    for nm, gg, dd, mm_, vv in zip(names, unpack(sg), unpack(sd), unpack(sm), unpack(sv)):
        res[nm] = (gg, dd, mm_, vv)
    weights = ("norm_mix_g", "norm_ffn_g", "final_norm_g", "cv_w_pw1", "cv_b_pw1", "cv_w_dw", "cv_b_dw", "cv_ln_g",
               "cv_ln_b", "cv_w_pw2", "cv_b_pw2", "gdn_w_in", "gdn_conv_w", "gdn_a_log", "gdn_dt_bias", "gdn_norm_g",
               "gdn_w_out", "mlp_w1", "mlp_w2")
    outs = [loss, grad_x.reshape(x.shape)]
    for kind in range(4):
        outs += [res[nm][kind] for nm in weights]
    return tuple(outs)
```

```python
import functools

import jax
import jax.numpy as jnp
from jax import lax
from jax.experimental import pallas as pl
from jax.experimental.pallas import tpu as pltpu

F32, BF16 = jnp.float32, jnp.bfloat16
D = 1024
H = 8
HD = 128
CH = 64
DFF = 4 * D
KCV, HB_CV = 31, 32
KSC, HB_SC = 4, 8
EPS = 1e-6
LR, B1, B2, EPS_A, WD, STEP = 0.001, 0.9, 0.999, 1e-08, 0.01, 10
VMEM_LIMIT = 56 * 1024 * 1024
SUB = 32
NSMALL = 64
MESH = pl.DeviceIdType.MESH


def _cp(*sem):
    return pltpu.CompilerParams(dimension_semantics=sem, vmem_limit_bytes=VMEM_LIMIT)


def f_rms(h, g):
    return h * lax.rsqrt(jnp.mean(h * h, axis=-1, keepdims=True) + EPS) * g


def f_silu(x):
    return x * jax.nn.sigmoid(x)


def f_glu(u):
    return u[:, :D] * jax.nn.sigmoid(u[:, D:])


def f_ln_silu(x, g, b):
    mu = jnp.mean(x, axis=-1, keepdims=True)
    xc = x - mu
    y = xc * lax.rsqrt(jnp.mean(xc * xc, axis=-1, keepdims=True) + EPS)
    return f_silu(y * g + b)


def f_relu2(z):
    r = jnp.maximum(z.astype(F32), 0.0)
    return r * r


def f_post(o, z, ng):
    outs = []
    for h in range(H):
        oh = o[:, h * HD:(h + 1) * HD]
        y = oh * lax.rsqrt(jnp.mean(oh * oh, axis=-1, keepdims=True) + EPS) * ng
        outs.append(y * f_silu(z[:, h * HD:(h + 1) * HD]))
    return jnp.concatenate(outs, axis=1)


def f_adamw(w, g, m, v):
    m2 = B1 * m + (1.0 - B1) * g
    v2 = B2 * v + (1.0 - B2) * (g * g)
    m_hat = m2 / (1.0 - B1 ** STEP)
    v_hat = v2 / (1.0 - B2 ** STEP)
    delta = -LR * (m_hat / (jnp.sqrt(v_hat) + EPS_A) + WD * w)
    return delta, m2, v2


def _dot_raw(a, b, mode):
    dims = {"NN": ((1,), (0,)), "NT": ((1,), (1,)), "TN": ((0,), (0,))}[mode]
    return lax.dot_general(a.astype(BF16), b.astype(BF16), (dims, ((), ())), preferred_element_type=F32)


@functools.partial(jax.custom_vjp, nondiff_argnums=(2,))
def _dot_vjp(a, b, mode):
    return _dot_raw(a, b, mode)


def _dot_fwd(a, b, mode):
    return _dot_raw(a, b, mode), (a, b)


def _dot_bwd(mode, res, dc):
    a, b = res
    if mode == "NN":
        return _dot_vjp(dc, b, "NT"), _dot_vjp(a, dc, "TN")
    if mode == "NT":
        return _dot_vjp(dc, b, "NN"), _dot_vjp(dc, a, "TN")
    return _dot_vjp(b, dc, "NT"), _dot_vjp(a, dc, "NN")


_dot_vjp.defvjp(_dot_fwd, _dot_bwd)


def _split(x):
    xh = x.astype(BF16)
    return xh, (x - xh.astype(F32)).astype(BF16)


def _dot_split(xs, ys):
    (xh, xl), (yh, yl) = xs, ys
    return _dot_raw(xh, yh, "NN") + (_dot_raw(xh, yl, "NN") + _dot_raw(xl, yh, "NN"))


def _tril_inverse(a_list):
    ri = lax.broadcasted_iota(jnp.int32, (CH, CH), 0)
    ci = lax.broadcasted_iota(jnp.int32, (CH, CH), 1)
    eye = (ri == ci).astype(F32)
    ts = None
    for lvl in range(CH.bit_length() - 1):
        same_pair = jnp.right_shift(ri, lvl + 1) == jnp.right_shift(ci, lvl + 1)
        quarter = (jnp.bitwise_and(jnp.right_shift(ri, lvl), 1) == 1) & (jnp.bitwise_and(jnp.right_shift(ci, lvl), 1) == 0)
        offs = [jnp.where(same_pair & quarter, a, 0.0) for a in a_list]
        if ts is None:
            ts = [eye - off for off in offs]
            continue
        tsp = [_split(t) for t in ts]
        mids = [_dot_split(tp, _split(off)) for tp, off in zip(tsp, offs)]
        ts = [t - _dot_split(_split(m), tp) for t, m, tp in zip(ts, mids, tsp)]
    return ts


@jax.custom_vjp
def _stored_solve(a, t, rhs):
    return _dot_raw(t, rhs, "NN")


def _stored_solve_fwd(a, t, rhs):
    sol = _dot_raw(t, rhs, "NN")
    return sol, (t, sol)


def _stored_solve_bwd(res, g):
    t, sol = res
    g_rhs = _dot_vjp(t, g, "TN")
    return -_dot_vjp(g_rhs, sol, "NT"), jnp.zeros_like(t), g_rhs


_stored_solve.defvjp(_stored_solve_fwd, _stored_solve_bwd)


def _lane_pick(row, idx, width):
    sel = lax.broadcasted_iota(jnp.int32, (1, width), 1) == idx
    return jnp.sum(jnp.where(sel, row, 0.0), axis=1, keepdims=True)


def f_prep(cqs, cks, cvs, araws, braws, alogs, dtbs, t_stored, dot):
    ri = lax.broadcasted_iota(jnp.int32, (CH, CH), 0)
    ci = lax.broadcasted_iota(jnp.int32, (CH, CH), 1)
    eye = (ri == ci).astype(F32)
    low = (ri >= ci).astype(F32)
    last = lax.broadcasted_iota(jnp.int32, (CH, 1), 0) == CH - 1
    nh = range(len(cqs))
    qs, ks, vbs, kbs, gcs, decays = [], [], [], [], [], []
    for h in nh:
        q = f_silu(cqs[h])
        qs.append(q * lax.rsqrt(jnp.sum(q * q, axis=-1, keepdims=True) + 1e-6) * (HD ** -0.5))
        k = f_silu(cks[h])
        k = k * lax.rsqrt(jnp.sum(k * k, axis=-1, keepdims=True) + 1e-6)
        ks.append(k)
        beta = jax.nn.sigmoid(braws[h])
        sp_in = araws[h] + dtbs[h]
        softplus = jnp.maximum(sp_in, 0.0) + jnp.log(1.0 + jnp.exp(-jnp.abs(sp_in)))
        g = -jnp.exp(alogs[h]) * softplus
        g_row = jnp.sum(eye * g, axis=0, keepdims=True)
        gc = jnp.sum(low * g_row, axis=1, keepdims=True)
        gc_row = jnp.sum(eye * gc, axis=0, keepdims=True)
        gcs.append(gc)
        decays.append(jnp.exp(jnp.where(ri >= ci, gc - gc_row, -1e30)))
        vbs.append(f_silu(cvs[h]) * beta)
        kbs.append(k * beta)
    kks = [dot(kbs[h], ks[h], "NT") for h in nh]
    a_list = [jnp.where(ri > ci, kks[h] * decays[h], 0.0) for h in nh]
    if t_stored is None:
        ts = _tril_inverse(a_list)
        solve = lambda h, rhs: dot(ts[h], rhs, "NN")
    else:
        ts = t_stored
        solve = lambda h, rhs: _stored_solve(a_list[h], t_stored[h], rhs)
    egcs = [jnp.exp(gc) for gc in gcs]
    us = [solve(h, vbs[h]) for h in nh]
    ws = [solve(h, kbs[h] * egcs[h]) for h in nh]
    qks = [dot(qs[h], ks[h], "NT") * decays[h] for h in nh]
    qgs = [qs[h] * egcs[h] for h in nh]
    gls = [jnp.sum(jnp.where(last, gc, 0.0), axis=0, keepdims=True) for gc in gcs]
    kgs = [ks[h] * jnp.exp(gls[h] - gcs[h]) for h in nh]
    egs = [jnp.exp(gl) * jnp.ones((1, HD), F32) for gl in gls]
    return us, ws, qks, qgs, kgs, egs, ts


def f_scan(ss, us, ws, qgs, kgs, qks, egs, dot):
    nh = range(len(ss))
    ws_s = [dot(ws[h], ss[h], "NN") for h in nh]
    qs_s = [dot(qgs[h], ss[h], "NN") for h in nh]
    vns = [us[h] - ws_s[h] for h in nh]
    os_ = [qs_s[h] + dot(qks[h], vns[h], "NN") for h in nh]
    s2s = [ss[h] * egs[h] + dot(kgs[h], vns[h], "TN") for h in nh]
    return os_, s2s


def row_call(name, fn, rows, pars, out_rows, out_accs, tm):
    T = rows[0][0].shape[0]
    n_r, n_p, n_o = len(rows), len(pars), len(out_rows)
    in_specs = [pl.BlockSpec((tm, w), functools.partial(lambda i, cb: (i, cb), cb=cb)) for (_, w, cb) in rows]
    in_specs += [pl.BlockSpec(p.shape, functools.partial(lambda i, nd: (0,) * nd, nd=p.ndim)) for p in pars]
    out_specs = [pl.BlockSpec((tm, w), lambda i: (i, 0)) for (w, _) in out_rows]
    out_specs += [pl.BlockSpec(s, lambda i: (0, 0)) for s in out_accs]
    out_shape = [jax.ShapeDtypeStruct((T, w), dt) for (w, dt) in out_rows]
    out_shape += [jax.ShapeDtypeStruct(s, F32) for s in out_accs]

    def body(*refs):
        rin, pin = refs[:n_r], refs[n_r:n_r + n_p]
        rout, aout = refs[n_r + n_p:n_r + n_p + n_o], refs[n_r + n_p + n_o:]
        if aout:
            @pl.when(pl.program_id(0) == 0)
            def _():
                for a in aout:
                    a[...] = jnp.zeros(a.shape, F32)
        pv = [p[...] for p in pin]

        def step(r, carry):
            sl = pl.ds(pl.multiple_of(r * SUB, SUB), SUB)
            outs, accs = fn(*[x[sl, :] for x in rin], *pv)
            for o, val in zip(rout, outs):
                o[sl, :] = val.astype(o.dtype)
            for a, val in zip(aout, accs):
                a[...] += val
            return carry

        lax.fori_loop(0, tm // SUB, step, 0)

    return pl.pallas_call(body, name=name, grid=(T // tm,), in_specs=in_specs, out_specs=out_specs,
                          out_shape=out_shape, compiler_params=_cp("arbitrary"))(*[r[0] for r in rows], *pars)


EW_TILE_ELEMS = 256 * 1024


def _ew_rows(R, Cc):
    if R * Cc <= EW_TILE_ELEMS or R % 8:
        return R
    tr = 8
    while tr * 2 * Cc <= EW_TILE_ELEMS and R % (tr * 2) == 0:
        tr *= 2
    return tr


def ew_call(name, fn, ins, n_out):
    shape = ins[0].shape
    lead = shape[:-2]
    R, Cc = shape[-2:]
    tr = _ew_rows(R, Cc)
    grid = lead + (R // tr,)
    nl = len(lead)
    spec = pl.BlockSpec((None,) * nl + (tr, Cc), lambda *idx: idx + (0,))

    def body(*refs):
        outs = fn(*[r[...] for r in refs[:len(ins)]])
        for o, val in zip(refs[len(ins):], outs):
            o[...] = val

    return pl.pallas_call(body, name=name, grid=grid, in_specs=[spec] * len(ins), out_specs=[spec] * n_out,
                          out_shape=[jax.ShapeDtypeStruct(shape, F32)] * n_out,
                          compiler_params=_cp(*(("arbitrary",) * len(grid))))(*ins)


MM_RESIDENT_BYTES = 8 * 1024 * 1024


def mm(name, a, b, mode, out_dtype, a_fn=None, epi=None, epi_ins=(), accs=(), tm=512):
    sub_epi = epi is not None and isinstance(out_dtype, (list, tuple))
    M, K = a.shape
    N = b.shape[1] if mode == "NN" else b.shape[0]
    tn = N if K * N * 2 <= MM_RESIDENT_BYTES else min(N, 1024)
    tm = min(tm if tn <= 1024 else tm // 2, M)
    multi = isinstance(out_dtype, (list, tuple))
    dts = list(out_dtype) if multi else [out_dtype]
    n_e, n_o = len(epi_ins), len(dts)
    in_specs = [pl.BlockSpec((tm, K), lambda j, i: (i, 0)),
                pl.BlockSpec((K, tn), lambda j, i: (0, j)) if mode == "NN" else pl.BlockSpec((tn, K), lambda j, i: (j, 0))]
    row_kinds = []
    for (arr, kind) in epi_ins:
        if kind == "tile" or isinstance(kind, tuple):
            cb = kind[1] if isinstance(kind, tuple) else 0
            in_specs.append(pl.BlockSpec((tm, tn), functools.partial(lambda j, i, cb: (i, cb + j), cb=cb)))
            row_kinds.append(True)
        elif kind == "row":
            in_specs.append(pl.BlockSpec((1, tn), lambda j, i: (0, j)))
            row_kinds.append(False)
        else:
            in_specs.append(pl.BlockSpec(arr.shape, lambda j, i: (0, 0)))
            row_kinds.append(False)

    def body(a_ref, b_ref, *rest):
        e_refs, o_refs, acc_refs = rest[:n_e], rest[n_e:n_e + n_o], rest[n_e + n_o:n_e + n_o + len(accs)]
        av = a_ref[...]
        if a_fn is not None:
            av = a_fn(av)
        res = _dot_raw(av, b_ref[...], mode)
        if epi is None or not sub_epi:
            if epi is not None:
                res = epi(res, *[r[...] for r in e_refs])
            o_refs[0][...] = res.astype(o_refs[0].dtype)
            return
        prod = rest[-1]
        prod[...] = res
        if acc_refs:
            @pl.when((pl.program_id(0) == 0) & (pl.program_id(1) == 0))
            def _():
                for r in acc_refs:
                    r[...] = jnp.zeros(r.shape, F32)
        small = [None if is_rows else r[...] for r, is_rows in zip(e_refs, row_kinds)]

        def step(k, carry):
            sl = pl.ds(pl.multiple_of(k * SUB, SUB), SUB)
            out = epi(prod[sl, :], *[r[sl, :] if is_rows else sm for r, is_rows, sm in zip(e_refs, row_kinds, small)])
            tiles, contribs = out if multi else ((out,), ())
            for r, t in zip(o_refs, tiles):
                r[sl, :] = t.astype(r.dtype)
            for r, t in zip(acc_refs, contribs):
                r[...] += t
            return carry

        lax.fori_loop(0, tm // SUB, step, 0, unroll=2)

    out_specs = [pl.BlockSpec((tm, tn), lambda j, i: (i, j))] * n_o + [pl.BlockSpec(s, lambda j, i: (0, 0)) for s in accs]
    out_shape = [jax.ShapeDtypeStruct((M, N), dt) for dt in dts] + [jax.ShapeDtypeStruct(s, F32) for s in accs]
    res = pl.pallas_call(body, name=name, grid=(N // tn, M // tm), in_specs=in_specs, out_specs=out_specs,
                         out_shape=out_shape, scratch_shapes=[pltpu.VMEM((tm, tn), F32)] if sub_epi else [],
                         compiler_params=_cp("arbitrary", "arbitrary"))(a, b, *[e[0] for e in epi_ins])
    return res if multi else res[0]


def mm_tn(name, a, g, a_fn=None, a_cols=None, tt=1024):
    T = a.shape[0]
    ka, acb = (a.shape[1], 0) if a_cols is None else a_cols
    N = g.shape[1]
    tt = min(tt, T)
    tka, tn = min(ka, 1024), min(N, 1024)
    nkb = ka // tka

    def body(a_ref, g_ref, o_ref):
        @pl.when(pl.program_id(2) == 0)
        def _():
            o_ref[...] = jnp.zeros(o_ref.shape, F32)
        av = a_ref[...]
        if a_fn is not None:
            av = a_fn(av)
        o_ref[...] += _dot_raw(av, g_ref[...], "TN")

    return pl.pallas_call(body, name=name, grid=(nkb, N // tn, T // tt),
                          in_specs=[pl.BlockSpec((tt, tka), lambda ia, j, t: (t, acb * nkb + ia)),
                                    pl.BlockSpec((tt, tn), lambda ia, j, t: (t, j))],
                          out_specs=pl.BlockSpec((tka, tn), lambda ia, j, t: (ia, j)),
                          out_shape=jax.ShapeDtypeStruct((ka, N), F32),
                          compiler_params=_cp("arbitrary", "arbitrary", "arbitrary"))(a, g)


SUBLANES = 8


class _RowShifts:
    def __init__(self, src, shifted, nrows, reuse):
        self.src, self.shifted, self.reuse = src, shifted, reuse
        if reuse:
            for ph in range(1, SUBLANES):
                for r0 in range(0, nrows - SUBLANES, SUB):
                    n = min(SUB, nrows - SUBLANES - r0)
                    shifted[ph - 1, r0:r0 + n, :] = src[r0 + ph:r0 + ph + n, :]

    def window(self, off, cols):
        ph = off % SUBLANES
        if not self.reuse or ph == 0:
            return self.src[off:off + SUB, cols]
        return self.shifted[ph - 1, off - ph:off - ph + SUB, cols]


def _shift_scratch(nrows, C, reuse):
    return [pltpu.VMEM((SUBLANES - 1, nrows - SUBLANES, C), F32)] if reuse else []


def dwconv_fwd(name, x, xw, w_pad, bias, S, K, HB, pre, post, tm):
    T = x.shape[0]
    C = w_pad.shape[1]
    nb, per_seq = tm // HB, S // tm
    has_b, has_post = bias is not None, post is not None
    reuse = K > SUBLANES

    def body(*refs):
        x_ref, xp_ref, w_ref = refs[:3]
        pos = 3
        b_ref = refs[pos] if has_b else None
        pos += has_b
        ppars = refs[pos:pos + (len(post[1]) if has_post else 0)]
        pos += len(ppars)
        c_ref = refs[pos]
        s_ref = refs[pos + 1] if has_post else None
        ext = refs[pos + 1 + has_post]
        first = (pl.program_id(0) % per_seq) == 0
        ext[0:HB, :] = jnp.where(first, 0.0, pre(xp_ref[...]))
        for r in range(tm // SUB):
            ext[HB + r * SUB:HB + (r + 1) * SUB, :] = pre(x_ref[r * SUB:(r + 1) * SUB, :])
        rows_of = _RowShifts(ext, refs[-1] if reuse else None, HB + tm, reuse)
        pv = [p[...] for p in ppars]
        assert not has_post or C == D
        for r in range(tm // SUB):
            for c0 in range(0, C, D):
                cols = slice(c0, c0 + D)
                acc = jnp.zeros((SUB, D), F32)
                if has_b:
                    acc = acc + b_ref[:, cols]
                for k in range(K):
                    acc = acc + w_ref[k:k + 1, cols] * rows_of.window(HB + r * SUB - (K - 1) + k, cols)
                c_ref[r * SUB:(r + 1) * SUB, cols] = acc
                if has_post:
                    s_ref[r * SUB:(r + 1) * SUB, :] = post[0](acc, *pv).astype(BF16)

    ins = [x, x, w_pad] + ([bias] if has_b else []) + (list(post[1]) if has_post else [])
    in_specs = [pl.BlockSpec((tm, xw), lambda i: (i, 0)),
                pl.BlockSpec((HB, xw), lambda i: (jnp.maximum(i * nb - 1, 0), 0)),
                pl.BlockSpec(w_pad.shape, lambda i: (0, 0))]
    in_specs += [pl.BlockSpec(p.shape, lambda i: (0, 0)) for p in ins[3:]]
    out_specs = [pl.BlockSpec((tm, C), lambda i: (i, 0))] * (1 + has_post)
    out_shape = [jax.ShapeDtypeStruct((T, C), F32)] + ([jax.ShapeDtypeStruct((T, C), BF16)] if has_post else [])
    return pl.pallas_call(body, name=name, grid=(T // tm,), in_specs=in_specs, out_specs=out_specs, out_shape=out_shape,
                          scratch_shapes=[pltpu.VMEM((HB + tm, C), F32)] + _shift_scratch(HB + tm, C, reuse),
                          compiler_params=_cp("arbitrary"))(*ins)


def dwconv_bwd(name, g, x, xw, w_pad, S, K, HB, pre, pre_bwd, tm):
    T = g.shape[0]
    C = w_pad.shape[1]
    nb, per_seq = tm // HB, S // tm
    nblk = T // HB

    reuse = K > SUBLANES

    def body(g_ref, gn_ref, x_ref, xp_ref, w_ref, dx_ref, dw_ref, dbx_ref, extg, exta, *shift_refs):
        i = pl.program_id(0)
        first = (i % per_seq) == 0
        last = (i % per_seq) == per_seq - 1

        @pl.when(i == 0)
        def _():
            dw_ref[...] = jnp.zeros(dw_ref.shape, F32)
            dbx_ref[...] = jnp.zeros(dbx_ref.shape, F32)

        extg[tm:tm + HB, :] = jnp.where(last, 0.0, gn_ref[...])
        exta[0:HB, :] = jnp.where(first, 0.0, pre(xp_ref[...]))
        for r in range(tm // SUB):
            extg[r * SUB:(r + 1) * SUB, :] = g_ref[r * SUB:(r + 1) * SUB, :]
            exta[HB + r * SUB:HB + (r + 1) * SUB, :] = pre(x_ref[r * SUB:(r + 1) * SUB, :])
        assert pre_bwd is None or C == D
        g_rows = _RowShifts(extg, shift_refs[0] if reuse else None, tm + HB, reuse)
        a_rows = _RowShifts(exta, shift_refs[1] if reuse else None, HB + tm, reuse)
        for r in range(tm // SUB):
            rows = slice(r * SUB, (r + 1) * SUB)
            for c0 in range(0, C, D):
                cols = slice(c0, c0 + D)
                acc = jnp.zeros((SUB, D), F32)
                for k in range(K):
                    acc = acc + w_ref[k:k + 1, cols] * g_rows.window(r * SUB + (K - 1) - k, cols)
                if pre_bwd is None:
                    dx_ref[rows, cols] = acc
                    dbx_ref[:, cols] += jnp.sum(acc, axis=0, keepdims=True)
                else:
                    dx = pre_bwd(x_ref[rows, :], acc)
                    dx_ref[rows, :] = dx
                    dbx_ref[...] += jnp.sum(dx, axis=0, keepdims=True)
        for k in range(K):
            for c0 in range(0, C, D):
                cols = slice(c0, c0 + D)
                p = jnp.zeros((SUB, D), F32)
                for r in range(tm // SUB):
                    p = p + extg[r * SUB:(r + 1) * SUB, cols] * a_rows.window(HB + r * SUB - (K - 1) + k, cols)
                dw_ref[k:k + 1, cols] += jnp.sum(p, axis=0, keepdims=True)

    in_specs = [pl.BlockSpec((tm, C), lambda i: (i, 0)),
                pl.BlockSpec((HB, C), lambda i: (jnp.minimum((i + 1) * nb, nblk - 1), 0)),
                pl.BlockSpec((tm, xw), lambda i: (i, 0)),
                pl.BlockSpec((HB, xw), lambda i: (jnp.maximum(i * nb - 1, 0), 0)),
                pl.BlockSpec(w_pad.shape, lambda i: (0, 0))]
    out_specs = [pl.BlockSpec((tm, xw), lambda i: (i, 0)), pl.BlockSpec((HB, C), lambda i: (0, 0)),
                 pl.BlockSpec((1, xw), lambda i: (0, 0))]
    out_shape = [jax.ShapeDtypeStruct((T, xw), F32), jax.ShapeDtypeStruct((HB, C), F32), jax.ShapeDtypeStruct((1, xw), F32)]
    return pl.pallas_call(body, name=name, grid=(T // tm,), in_specs=in_specs, out_specs=out_specs, out_shape=out_shape,
                          scratch_shapes=[pltpu.VMEM((tm + HB, C), F32), pltpu.VMEM((HB + tm, C), F32)]
                          + _shift_scratch(tm + HB, C, reuse) * 2,
                          compiler_params=_cp("arbitrary"))(g, g, x, x, w_pad)


def _glu_bwd(u, da):
    u1, sg = u[:, :D], jax.nn.sigmoid(u[:, D:])
    return jnp.concatenate([da * sg, da * u1 * sg * (1.0 - sg)], axis=1)


def _head_cols(ref, h, base=0, rows=slice(None)):
    return ref[rows, base + h * HD:base + (h + 1) * HD]


def _prep_inputs(c_ref, ab, al, dt, rows=slice(None)):
    hs = range(H)
    return ([_head_cols(c_ref, h, 0, rows) for h in hs], [_head_cols(c_ref, h, D, rows) for h in hs],
            [_head_cols(c_ref, h, 2 * D, rows) for h in hs], [_lane_pick(ab, h, HD) for h in hs],
            [_lane_pick(ab, H + h, HD) for h in hs], [_lane_pick(al, h, HD) for h in hs],
            [_lane_pick(dt, h, HD) for h in hs])


PREP_CHUNKS = 2


def gdn_prep_fwd(cpre, pab, alog, dtb):
    T = cpre.shape[0]
    nc = T // CH
    G = PREP_CHUNKS

    def body(c_ref, ab_ref, al_ref, dt_ref, u_ref, w_ref, qg_ref, kg_ref, qk_ref, t_ref, eg_ref):
        ins = [[] for _ in range(7)]
        for ci in range(G):
            rows = slice(ci * CH, (ci + 1) * CH)
            for lst, part in zip(ins, _prep_inputs(c_ref, ab_ref[rows, :], al_ref[...], dt_ref[...], rows)):
                lst += part
        us, ws, qks, qgs, kgs, egs, ts = f_prep(*ins, None, _dot_raw)
        for ci in range(G):
            rows = slice(ci * CH, (ci + 1) * CH)
            for h in range(H):
                cols, k = slice(h * HD, (h + 1) * HD), ci * H + h
                u_ref[rows, cols] = us[k]
                w_ref[rows, cols] = ws[k].astype(BF16)
                qg_ref[rows, cols] = qgs[k].astype(BF16)
                kg_ref[rows, cols] = kgs[k].astype(BF16)
                qk_ref[ci, h] = qks[k].astype(BF16)
                t_ref[ci, h] = ts[k].astype(BF16)
                eg_ref[ci, h:h + 1, :] = egs[k]

    row = lambda w: pl.BlockSpec((G * CH, w), lambda n: (n, 0))
    par = pl.BlockSpec((1, HD), lambda n: (0, 0))
    mat = pl.BlockSpec((G, H, CH, CH), lambda n: (n, 0, 0, 0))
    return pl.pallas_call(
        body, name="gdn_prep_fwd", grid=(nc // G,), in_specs=[row(3 * D), row(HD), par, par],
        out_specs=[row(D), row(D), row(D), row(D), mat, mat, pl.BlockSpec((G, H, HD), lambda n: (n, 0, 0))],
        out_shape=[jax.ShapeDtypeStruct((T, D), F32)] + [jax.ShapeDtypeStruct((T, D), BF16)] * 3
        + [jax.ShapeDtypeStruct((nc, H, CH, CH), BF16)] * 2 + [jax.ShapeDtypeStruct((nc, H, HD), F32)],
        compiler_params=_cp("arbitrary"))(cpre, pab, alog, dtb)


def gdn_prep_bwd(cpre, pab, alog, dtb, tmat, du, dw, dqg, dkg, dqk, deg):
    T = cpre.shape[0]
    nc = T // CH
    G = PREP_CHUNKS

    def body(c_ref, ab_ref, al_ref, dt_ref, t_ref, du_ref, dw_ref, dqg_ref, dkg_ref, dqk_ref, deg_ref,
             dc_ref, dab_ref, dal_ref, ddt_ref):
        @pl.when(pl.program_id(0) == 0)
        def _():
            dal_ref[...] = jnp.zeros(dal_ref.shape, F32)
            ddt_ref[...] = jnp.zeros(ddt_ref.shape, F32)

        lane = lax.broadcasted_iota(jnp.int32, (1, HD), 1)
        dal = jnp.zeros((1, HD), F32)
        ddt = jnp.zeros((1, HD), F32)
        hs = range(H)
        chunks = [(ci, slice(ci * CH, (ci + 1) * CH)) for ci in range(G)]
        t_st = [t_ref[ci, h].astype(F32) for ci, _ in chunks for h in hs]
        ins = [[] for _ in range(7)]
        for ci, rows in chunks:
            for lst, part in zip(ins, _prep_inputs(c_ref, ab_ref[rows, :], al_ref[...], dt_ref[...], rows)):
                lst += part

        def fwd(*args):
            return tuple(f_prep(*args, t_st, _dot_vjp)[:6])

        _, vjp = jax.vjp(fwd, *ins)
        dcqs, dcks, dcvs, dars, dbrs, dals, ddts = vjp((
            [_head_cols(du_ref, h, 0, rows) for _, rows in chunks for h in hs],
            [_head_cols(dw_ref, h, 0, rows) for _, rows in chunks for h in hs],
            [dqk_ref[ci, h] for ci, _ in chunks for h in hs],
            [_head_cols(dqg_ref, h, 0, rows) for _, rows in chunks for h in hs],
            [_head_cols(dkg_ref, h, 0, rows) for _, rows in chunks for h in hs],
            [deg_ref[ci, h:h + 1, :] for ci, _ in chunks for h in hs]))
        for ci, rows in chunks:
            dab = jnp.zeros((CH, HD), F32)
            for h in hs:
                k = ci * H + h
                dc_ref[rows, h * HD:(h + 1) * HD] = dcqs[k]
                dc_ref[rows, D + h * HD:D + (h + 1) * HD] = dcks[k]
                dc_ref[rows, 2 * D + h * HD:2 * D + (h + 1) * HD] = dcvs[k]
                dab = dab + jnp.where(lane == h, dars[k], 0.0) + jnp.where(lane == H + h, dbrs[k], 0.0)
                dal = dal + jnp.where(lane == h, dals[k], 0.0)
                ddt = ddt + jnp.where(lane == h, ddts[k], 0.0)
            dab_ref[rows, :] = dab
        dal_ref[...] += dal
        ddt_ref[...] += ddt

    row = lambda w: pl.BlockSpec((G * CH, w), lambda n: (n, 0))
    par = pl.BlockSpec((1, HD), lambda n: (0, 0))
    mat = pl.BlockSpec((G, H, CH, CH), lambda n: (n, 0, 0, 0))
    vec = pl.BlockSpec((G, H, HD), lambda n: (n, 0, 0))
    return pl.pallas_call(
        body, name="gdn_prep_bwd", grid=(nc // G,),
        in_specs=[row(3 * D), row(HD), par, par, mat, row(D), row(D), row(D), row(D), mat, vec],
        out_specs=[row(3 * D), row(HD), par, par],
        out_shape=[jax.ShapeDtypeStruct((T, 3 * D), F32), jax.ShapeDtypeStruct((T, HD), F32),
                   jax.ShapeDtypeStruct((1, HD), F32), jax.ShapeDtypeStruct((1, HD), F32)],
        compiler_params=_cp("arbitrary"))(cpre, pab, alog, dtb, tmat, du, dw, dqg, dkg, dqk, deg)


def gdn_scan_fwd(u, w, qg, kg, qk, eg, S):
    T = u.shape[0]
    nc, per_seq = T // CH, S // CH

    def body(u_ref, w_ref, qg_ref, kg_ref, qk_ref, eg_ref, o_ref, sall_ref, s_ref):
        @pl.when(pl.program_id(0) % per_seq == 0)
        def _():
            s_ref[...] = jnp.zeros(s_ref.shape, F32)

        hs = range(H)
        ss = [s_ref[h] for h in hs]
        os_, s2s = f_scan(ss, [_head_cols(u_ref, h) for h in hs], [_head_cols(w_ref, h) for h in hs],
                          [_head_cols(qg_ref, h) for h in hs], [_head_cols(kg_ref, h) for h in hs],
                          [qk_ref[0, h] for h in hs], [eg_ref[0, h:h + 1, :] for h in hs], _dot_raw)
        for h in hs:
            sall_ref[0, h] = ss[h]
            o_ref[:, h * HD:(h + 1) * HD] = os_[h]
            s_ref[h] = s2s[h]

    row = pl.BlockSpec((CH, D), lambda n: (n, 0))
    return pl.pallas_call(
        body, name="gdn_scan_fwd", grid=(nc,),
        in_specs=[row, row, row, row, pl.BlockSpec((1, H, CH, CH), lambda n: (n, 0, 0, 0)),
                  pl.BlockSpec((1, H, HD), lambda n: (n, 0, 0))],
        out_specs=[row, pl.BlockSpec((1, H, HD, HD), lambda n: (n, 0, 0, 0))],
        out_shape=[jax.ShapeDtypeStruct((T, D), F32), jax.ShapeDtypeStruct((nc, H, HD, HD), F32)],
        scratch_shapes=[pltpu.VMEM((H, HD, HD), F32)], compiler_params=_cp("arbitrary"))(u, w, qg, kg, qk, eg)


def gdn_scan_bwd(do, u, w, qg, kg, qk, eg, sall, S):
    T = u.shape[0]
    nc, per_seq = T // CH, S // CH

    def body(do_ref, u_ref, w_ref, qg_ref, kg_ref, qk_ref, eg_ref, sall_ref,
             du_ref, dw_ref, dqg_ref, dkg_ref, dqk_ref, deg_ref, ds_ref):
        n = nc - 1 - pl.program_id(0)

        @pl.when(n % per_seq == per_seq - 1)
        def _():
            ds_ref[...] = jnp.zeros(ds_ref.shape, F32)

        hs = range(H)

        def fwd(*args):
            return f_scan(*args, _dot_vjp)

        _, vjp = jax.vjp(fwd, [sall_ref[0, h] for h in hs], [_head_cols(u_ref, h) for h in hs],
                         [_head_cols(w_ref, h).astype(F32) for h in hs], [_head_cols(qg_ref, h).astype(F32) for h in hs],
                         [_head_cols(kg_ref, h).astype(F32) for h in hs], [qk_ref[0, h].astype(F32) for h in hs],
                         [eg_ref[0, h:h + 1, :] for h in hs])
        dss, dus, dws, dqgs, dkgs, dqks, degs = vjp(([_head_cols(do_ref, h) for h in hs], [ds_ref[h] for h in hs]))
        for h in hs:
            cols = slice(h * HD, (h + 1) * HD)
            du_ref[:, cols] = dus[h]
            dw_ref[:, cols] = dws[h]
            dqg_ref[:, cols] = dqgs[h]
            dkg_ref[:, cols] = dkgs[h]
            dqk_ref[0, h] = dqks[h]
            deg_ref[0, h:h + 1, :] = degs[h]
            ds_ref[h] = dss[h]

    rev = lambda n: (nc - 1 - n, 0)
    row = pl.BlockSpec((CH, D), rev)
    mat = pl.BlockSpec((1, H, CH, CH), lambda n: (nc - 1 - n, 0, 0, 0))
    vec = pl.BlockSpec((1, H, HD), lambda n: (nc - 1 - n, 0, 0))
    return pl.pallas_call(
        body, name="gdn_scan_bwd", grid=(nc,),
        in_specs=[row, row, row, row, row, mat, vec, pl.BlockSpec((1, H, HD, HD), lambda n: (nc - 1 - n, 0, 0, 0))],
        out_specs=[row, row, row, row, mat, vec],
        out_shape=[jax.ShapeDtypeStruct((T, D), F32)] * 4
        + [jax.ShapeDtypeStruct((nc, H, CH, CH), F32), jax.ShapeDtypeStruct((nc, H, HD), F32)],
        scratch_shapes=[pltpu.VMEM((H, HD, HD), F32)], compiler_params=_cp("arbitrary"))(do, u, w, qg, kg, qk, eg, sall)


def xor_exchange(name, ins, inplace, out_shapes, plan, n_remote, n_local=0):
    n_in, n_ip, n_out = len(ins), len(inplace), len(out_shapes)

    def body(*refs):
        in_refs = refs[:n_in]
        ip_refs = refs[n_in + n_ip:n_in + 2 * n_ip]
        out_refs = refs[n_in + 2 * n_ip:n_in + 2 * n_ip + n_out]
        send_sems, recv_sems, loc_sems = refs[n_in + 2 * n_ip + n_out:]
        x, y, c = lax.axis_index("x"), lax.axis_index("y"), lax.axis_index("c")
        remote, local = plan(in_refs, ip_refs, out_refs, (x, y, c))
        assert len(remote) == n_remote and len(local) == n_local
        copies = []
        for k, ((dx, dy, dc), src, dst) in enumerate(remote):
            peer = (1 - x if dx else x, 1 - y if dy else y, 1 - c if dc else c)
            copies.append(pltpu.make_async_remote_copy(src_ref=src, dst_ref=dst, send_sem=send_sems.at[k],
                                                       recv_sem=recv_sems.at[k], device_id=peer, device_id_type=MESH))
        for cp in copies:
            cp.start()
        locs = [pltpu.make_async_copy(src, dst, loc_sems.at[k]) for k, (src, dst) in enumerate(local)]
        for cp in locs:
            cp.start()
        for cp in copies:
            cp.wait()
        for cp in locs:
            cp.wait()

    anyspec = pl.BlockSpec(memory_space=pl.ANY)
    res = pl.pallas_call(
        body, name=name, in_specs=[anyspec] * (n_in + n_ip), out_specs=[anyspec] * (n_ip + n_out),
        out_shape=[jax.ShapeDtypeStruct(a.shape, a.dtype) for a in inplace] + list(out_shapes),
        input_output_aliases={n_in + i: i for i in range(n_ip)},
        scratch_shapes=[pltpu.SemaphoreType.DMA((n_remote,)), pltpu.SemaphoreType.DMA((n_remote,)),
                        pltpu.SemaphoreType.DMA((max(n_local, 1),))],
        )(*ins, *inplace)
    return list(res[:n_ip]), list(res[n_ip:])


HBM_SPEC = pl.BlockSpec(memory_space=pltpu.HBM)
SEM_SPEC = pl.BlockSpec(memory_space=pltpu.SEMAPHORE)


def _flip_copies(plan, refs, send_sems, recv_sems):
    x, y, c = lax.axis_index("x"), lax.axis_index("y"), lax.axis_index("c")
    copies = []
    for k, ((dx, dy, dc), src, dst) in enumerate(plan(refs, (x, y, c))):
        peer = (1 - x if dx else x, 1 - y if dy else y, 1 - c if dc else c)
        copies.append(pltpu.make_async_remote_copy(src_ref=src, dst_ref=dst, send_sem=send_sems.at[k],
                                                   recv_sem=recv_sems.at[k], device_id=peer, device_id_type=MESH))
    return copies


def xor_start(name, arrays, plan, n_remote, after):
    n = len(arrays)

    def body(*refs):
        for cp in _flip_copies(plan, refs[:n], refs[n + 1], refs[n + 2]):
            cp.start()
        refs[-1][...] = jnp.zeros(refs[-1].shape, F32)

    res = pl.pallas_call(
        body, name=name, in_specs=[HBM_SPEC] * n + [pl.BlockSpec(memory_space=pl.ANY)],
        out_shape=(pltpu.SemaphoreType.DMA((n_remote,)), pltpu.SemaphoreType.DMA((n_remote,)),
                   *[pltpu.HBM(a.shape, a.dtype) for a in arrays], jax.ShapeDtypeStruct((8, 128), F32)),
        out_specs=(SEM_SPEC, SEM_SPEC, *([HBM_SPEC] * n), pl.BlockSpec(memory_space=pltpu.VMEM)),
        input_output_aliases={i: 2 + i for i in range(n)},
        compiler_params=pltpu.CompilerParams(has_side_effects=pltpu.SideEffectType.DATAFLOW_SIDE_EFFECTING),
    )(*[pltpu.with_memory_space_constraint(a, pltpu.HBM) for a in arrays], after)
    return res[0], res[1], list(res[2:2 + n]), res[2 + n]


def xor_wait(name, send_sems, recv_sems, arrays, plan, after):
    n = len(arrays)

    def body(*refs):
        for cp in _flip_copies(plan, refs[:n], refs[n], refs[n + 1]):
            cp.wait_send()
            cp.wait_recv()

    return list(pl.pallas_call(
        body, name=name, in_specs=[HBM_SPEC] * n + [SEM_SPEC, SEM_SPEC, pl.BlockSpec(memory_space=pl.ANY)],
        out_shape=[pltpu.HBM(a.shape, a.dtype) for a in arrays], out_specs=[HBM_SPEC] * n,
        input_output_aliases={i: i for i in range(n)},
        compiler_params=pltpu.CompilerParams(has_side_effects=pltpu.SideEffectType.DATAFLOW_SIDE_EFFECTING),
    )(*arrays, send_sems, recv_sems, after))


class WSpec:
    def __init__(self, name, full, sa, ha, group, layer=None, lead=False):
        self.name, self.full, self.sa, self.ha, self.group, self.layer, self.lead = name, full, sa, ha, group, layer, lead
        self.ws = 1 if lead else full[sa] // 4
        self.wh = full[ha] // 2

    def shard_shape(self):
        if self.lead:
            return tuple(n for a, n in enumerate(self.full) if a != self.sa)
        return tuple(self.ws if a == self.sa else n for a, n in enumerate(self.full))

    def half_full_shape(self):
        return tuple(self.wh if a == self.ha else n for a, n in enumerate(self.full))

    def shard_half_shape(self):
        s = list(self.half_full_shape())
        if self.lead:
            del s[self.sa]
        else:
            s[self.sa] = self.ws
        return tuple(s)

    def full_view(self, ref, q=None, h=None):
        idx = []
        for a in range(len(self.full)):
            if a == self.sa and q is not None:
                idx.append(q if self.lead else pl.ds(pl.multiple_of(q * self.ws, self.ws), self.ws))
            elif a == self.ha and h is not None:
                idx.append(pl.ds(pl.multiple_of(h * self.wh, self.wh), self.wh))
            else:
                idx.append(slice(None))
        return ref.at[tuple(idx)]

    def shard_view(self, ref, h):
        idx = [] if self.layer is None else [self.layer]
        for a in range(len(self.full)):
            if self.lead and a == self.sa:
                continue
            idx.append(pl.ds(pl.multiple_of(h * self.wh, self.wh), self.wh) if a == self.ha else slice(None))
        return ref.at[tuple(idx)]

    def rows_cols(self, shard, half):
        rows, cols = self.full[-2:]
        if shard and not self.lead:
            rows, cols = (rows // 4, cols) if self.sa == 0 else (rows, cols // 4)
        if half:
            rows, cols = (rows // 2, cols) if self.ha == len(self.full) - 2 else (rows, cols // 2)
        return rows, cols

    def spec(self, tr, cw, nr, shard=False, half=False, has_lead=False, stacked=False):
        two_d = len(self.full) == 2
        shard_on_cols = two_d and self.sa == 1
        half_on_cols = two_d and self.ha == 1
        layer = self.layer

        def index(*args):
            pref = args[-1]
            i = args[-2]
            r, cblk, pre = i, 0, ()
            if shard:
                if self.lead:
                    pre = (pref[0],)
                elif shard_on_cols:
                    cblk = pref[0]
                else:
                    r = pref[0] * nr + i
            elif has_lead:
                pre = (args[0],)
            if half:
                if half_on_cols:
                    cblk = pref[1]
                else:
                    r = pref[1] * nr + i
            if stacked:
                pre = (layer,) + pre
            return pre + (r, cblk)

        n_pre = int(stacked) + int(self.lead and (shard or has_lead))
        return pl.BlockSpec((None,) * n_pre + (tr, cw), index)


WSPECS = [
    WSpec("cv_w_pw1", (D, 2 * D), 1, 0, 0),
    WSpec("cv_w_pw2", (D, D), 0, 1, 1),
    WSpec("gdn_w_in", (4, D, (4 * D + 2 * H) // 4), 0, 1, 2, lead=True),
    WSpec("gdn_w_out", (D, D), 0, 1, 3),
    WSpec("mlp_w1_0", (D, DFF), 1, 0, 4, layer=0),
    WSpec("mlp_w1_1", (D, DFF), 1, 0, 4, layer=1),
    WSpec("mlp_w2_0", (DFF, D), 0, 1, 5, layer=0),
    WSpec("mlp_w2_1", (DFF, D), 0, 1, 5, layer=1),
]
FLIPS = [(1, 0, 0), (0, 1, 0), (1, 1, 0)]
SIB = (0, 0, 1)


def _chip(x, y):
    return 2 * x + y


def _prefetch_call(name, body, grid, in_specs, out_specs, out_shape, pref, args, aliases=None):
    return pl.pallas_call(
        body, name=name, out_shape=out_shape, input_output_aliases=aliases or {},
        grid_spec=pltpu.PrefetchScalarGridSpec(num_scalar_prefetch=1, grid=grid, in_specs=in_specs, out_specs=out_specs),
        compiler_params=_cp(*(("arbitrary",) * len(grid))))(pref, *args)


def place_shard(ws, shard, pref):
    rows, cols = ws.rows_cols(True, False)
    tr = _ew_rows(rows, cols)
    nr = rows // tr
    stacked = ws.layer is not None
    layer = ws.layer

    def body(_, s_ref, o_ref):
        o_ref[...] = s_ref[...].astype(BF16)

    in_spec = pl.BlockSpec(((None,) if stacked else ()) + (tr, cols),
                           (lambda i, p: (layer, i, 0)) if stacked else (lambda i, p: (i, 0)))
    return _prefetch_call("place_" + ws.name, body, (nr,), [in_spec], ws.spec(tr, cols, nr, shard=True),
                          jax.ShapeDtypeStruct(ws.full, BF16), pref, [shard])


FIRST = [0]
LAYER0 = [1, 4, 6]
LAYER1 = [2, 3, 5, 7]


def _plan_gather_chips(sel):
    def plan(refs, pos):
        x, y, c = pos
        remote = []
        for j, i in enumerate(sel):
            mine = WSPECS[i].full_view(refs[j], _chip(x, y), c)
            remote += [(f, mine, mine) for f in FLIPS]
        return remote
    return plan


def gather_cores(tag, sel, nat):
    def plan(in_refs, ip_refs, out_refs, pos):
        x, y, c = pos
        remote = []
        for j, i in enumerate(sel):
            for (dx, dy, _) in FLIPS:
                got = WSPECS[i].full_view(ip_refs[j], _chip(1 - x if dx else x, 1 - y if dy else y), c)
                remote.append((SIB, got, got))
        return remote, []

    return xor_exchange("gather_cores" + tag, [], nat, [], plan, 3 * len(sel))[0]


def gather_start(tag, sel, placed, after):
    plan = _plan_gather_chips(sel)
    send, recv, arrays, token = xor_start("gather_chips%s_start" % tag, [placed[i] for i in sel], plan, 3 * len(sel), after)
    return (tag, sel, plan, send, recv, arrays), token


def gather_wait(state, after):
    tag, sel, plan, send, recv, arrays = state
    return gather_cores(tag, sel, xor_wait("gather_chips%s_wait" % tag, send, recv, arrays, plan, after))


def reduce_start(tag, sel, grads, pref):
    plan = _plan_reduce_chips(sel)
    sums = chip_sums(tag, sel, grads, pref)
    send, recv, arrays, token = xor_start("reduce_chips%s_start" % tag,
                                          sums + [lax.empty(s.shape, s.dtype) for s in _parts_shapes(sel)], plan,
                                          3 * len(sel), pref)
    return (tag, sel, plan, send, recv, arrays), token


def reduce_wait(state, after):
    tag, sel, plan, send, recv, arrays = state
    arrays = xor_wait("reduce_chips%s_wait" % tag, send, recv, arrays, plan, after)
    return sel, arrays[:len(sel)], arrays[len(sel):]


def gather_first(sel, placed, wdw_shard, wcv_shard):
    plan_w = _plan_gather_chips(sel)

    def plan(in_refs, ip_refs, out_refs, pos):
        x, y, c = pos
        remote, local = plan_w(ip_refs, pos), []
        for j, width in enumerate((D // 4, 3 * D // 4)):
            dst = out_refs[j].at[:, pl.ds(pl.multiple_of(_chip(x, y) * width, 128), width)]
            local.append((in_refs[j], dst))
            remote += [(f, in_refs[j], dst) for f in FLIPS]
        return remote, local

    taps = [jax.ShapeDtypeStruct((KCV, D), F32), jax.ShapeDtypeStruct((KSC, 3 * D), F32)]
    nat, (wdw, wcv) = xor_exchange("gather_chips0", [wdw_shard, wcv_shard], [placed[i] for i in sel], taps, plan,
                                   3 * (len(sel) + 2), 2)
    return gather_cores("0", sel, nat), wdw, wcv


def half_add(ws, g, rsib, pref):
    rows, cols = ws.rows_cols(False, True)
    tr = _ew_rows(rows, cols)
    nr = rows // tr

    def body(_, a_ref, b_ref, o_ref):
        o_ref[...] = (a_ref[...] + b_ref[...]).astype(BF16)

    whole = ws.spec(tr, cols, nr, has_lead=ws.lead)
    return _prefetch_call("reduce_add_" + ws.name, body, (4, nr) if ws.lead else (nr,),
                          [ws.spec(tr, cols, nr, half=True, has_lead=ws.lead), whole], whole,
                          jax.ShapeDtypeStruct(ws.half_full_shape(), BF16), pref, [g, rsib])


def shard_sum(ws, s, parts, buf, pref):
    rows, cols = ws.rows_cols(True, True)
    tr = _ew_rows(rows, cols)
    nr = rows // tr
    stacked = ws.layer is not None

    def body(_, s_ref, p_ref, *rest):
        rest[-1][...] = ((s_ref[...].astype(F32) + p_ref[0].astype(F32)) + p_ref[1].astype(F32)) + p_ref[2].astype(F32)

    in_specs = [ws.spec(tr, cols, nr, shard=True, has_lead=ws.lead), pl.BlockSpec((3, tr, cols), lambda i, p: (0, i, 0))]
    args, aliases = [s, parts], {}
    if buf is not None:
        in_specs.append(pl.BlockSpec(memory_space=pl.ANY))
        args.append(buf)
        aliases = {3: 0}
    shape = ((2,) if stacked else ()) + ws.shard_shape()
    return _prefetch_call("reduce_sum_" + ws.name, body, (nr,), in_specs, ws.spec(tr, cols, nr, half=True, stacked=stacked),
                          jax.ShapeDtypeStruct(shape, F32), pref, args, aliases)


def chip_sums(tag, sel, grads, pref):
    def plan(in_refs, ip_refs, out_refs, pos):
        c = pos[2]
        return [(SIB, WSPECS[i].full_view(in_refs[j], None, 1 - c), out_refs[j]) for j, i in enumerate(sel)], []

    halves = [jax.ShapeDtypeStruct(WSPECS[i].half_full_shape(), F32) for i in sel]
    _, rsib = xor_exchange("reduce_cores" + tag, grads, [], halves, plan, len(sel))
    return [half_add(WSPECS[i], grads[j], rsib[j], pref) for j, i in enumerate(sel)]


def _plan_reduce_chips(sel):
    n = len(sel)

    def plan(refs, pos):
        x, y, c = pos
        remote = []
        for j, i in enumerate(sel):
            for s, (dx, dy, _) in enumerate(FLIPS):
                qq = _chip(1 - x if dx else x, 1 - y if dy else y)
                remote.append(((dx, dy, 0), WSPECS[i].full_view(refs[j], qq), refs[n + j].at[s]))
        return remote
    return plan


def _parts_shapes(sel):
    return [jax.ShapeDtypeStruct((3,) + WSPECS[i].shard_half_shape(), BF16) for i in sel]


def finish_reduce(sums, parts, pref):
    n = len(WSPECS)
    bufs = {}
    for i, ws in enumerate(WSPECS):
        bufs[ws.group] = shard_sum(ws, sums[i], parts[i], bufs.get(ws.group), pref)

    def plan3(in_refs, ip_refs, out_refs, pos):
        c = pos[2]
        remote = []
        for ws in WSPECS:
            mine = ws.shard_view(ip_refs[ws.group], c)
            remote.append((SIB, mine, mine))
        return remote, []

    return xor_exchange("reduce_swap", [], [bufs[g] for g in sorted(bufs)], [], plan3, n)[0]


def gather_small(buf):
    flips = [(dx, dy, dc) for dx in (0, 1) for dy in (0, 1) for dc in (0, 1)][1:]

    def plan(in_refs, ip_refs, out_refs, pos):
        x, y, c = pos
        me = 4 * x + 2 * y + c
        dst = out_refs[0].at[me]
        return [(f, in_refs[0], dst) for f in flips], [(in_refs[0], dst)]

    return xor_exchange("gather_small", [buf], [], [jax.ShapeDtypeStruct((8,) + buf.shape, F32)], plan, 7, 1)[1][0]


def _pad_rows(a, rows):
    return jnp.pad(a, ((0, rows - a.shape[0]), (0, 0)))


def _row1(v):
    v = v.reshape((1, -1))
    return jnp.pad(v, ((0, 0), (0, D - v.shape[1])))


def _rms_fwd(name, h, g, tm):
    return row_call(name, lambda hh, gg: ((f_rms(hh, gg),), ()), [(h, D, 0)], [g], [(D, BF16)], [], tm)[0]


def _res_rms(h, g):
    return (h, f_rms(h, g)), ()


def _rms_bwd_epi(dhn, h, dres, g):
    _, vjp = jax.vjp(f_rms, h, g)
    dh, dg = vjp(dhn)
    dh = dh + dres
    return (dh,), (dg, jnp.sum(dh, axis=0, keepdims=True))


def _mlp_bwd(tag, dh, h, g, w1, w2, hn, z1, token=None):
    dz1 = mm("mlp_down_dx" + tag, dh, w2, "NT", BF16,
             epi=lambda acc, z, *_: acc * (2.0 * jnp.maximum(z.astype(F32), 0.0)),
             epi_ins=[(z1, "tile")] + ([] if token is None else [(token, "whole")]))
    dw2 = mm_tn("mlp_down_dw" + tag, z1, dh, a_fn=f_relu2)
    dh_in, dg, colsum = mm("mlp_up_dx" + tag, dz1, w1, "NT", [F32], epi=_rms_bwd_epi,
                           epi_ins=[(h, "tile"), (dh, "tile"), (g, "row")], accs=[(1, D), (1, D)])
    dw1 = mm_tn("mlp_up_dw" + tag, hn, dz1)
    return dh_in, dg, colsum, dw1, dw2


def kernel(x, norm_mix_g, norm_ffn_g, final_norm_g, cv_w_pw1, cv_b_pw1, cv_w_dw, cv_b_dw, cv_ln_g, cv_ln_b, cv_w_pw2, cv_b_pw2, gdn_w_in, gdn_conv_w, gdn_a_log, gdn_dt_bias, gdn_norm_g, gdn_w_out, mlp_w1, mlp_w2, loss_target, m_norm_mix_g, m_norm_ffn_g, m_final_norm_g, m_cv_w_pw1, m_cv_b_pw1, m_cv_w_dw, m_cv_b_dw, m_cv_ln_g, m_cv_ln_b, m_cv_w_pw2, m_cv_b_pw2, m_gdn_w_in, m_gdn_conv_w, m_gdn_a_log, m_gdn_dt_bias, m_gdn_norm_g, m_gdn_w_out, m_mlp_w1, m_mlp_w2, v_norm_mix_g, v_norm_ffn_g, v_final_norm_g, v_cv_w_pw1, v_cv_b_pw1, v_cv_w_dw, v_cv_b_dw, v_cv_ln_g, v_cv_ln_b, v_cv_w_pw2, v_cv_b_pw2, v_gdn_w_in, v_gdn_conv_w, v_gdn_a_log, v_gdn_dt_bias, v_gdn_norm_g, v_gdn_w_out, v_mlp_w1, v_mlp_w2):
    env = dict(locals())
    bl, S, _ = x.shape
    T = bl * S
    tm = min(256, S)
    xf = x.reshape((T, D))
    tgt = loss_target.reshape((T, D))

    chip = 2 * lax.axis_index("x") + lax.axis_index("y")
    pref = jnp.stack([chip, lax.axis_index("c")]).astype(jnp.int32)
    big = [cv_w_pw1[0], cv_w_pw2[0], gdn_w_in[0], gdn_w_out[0], mlp_w1, mlp_w2]
    placed = [place_shard(ws, big[ws.group], pref) for ws in WSPECS]
    (w_pw1,), wdw, wcv = gather_first(FIRST, placed, cv_w_dw[0], gdn_conv_w[0])
    gather_a, token_a = gather_start("A", LAYER0, placed, wcv)
    gather_b, token_b = gather_start("B", LAYER1, placed, wcv)
    wdw_p, wcv_p = _pad_rows(wdw, HB_CV), _pad_rows(wcv, HB_SC)
    alog_p = jnp.pad(gdn_a_log, ((0, 0), (0, HD - H)))
    dtb_p = jnp.pad(gdn_dt_bias, ((0, 0), (0, HD - H)))
    g_mix0, g_mix1 = norm_mix_g[0:1] + (token_a[0, 0] + token_b[0, 0]), norm_mix_g[1:2]
    g_ffn0, g_ffn1 = norm_ffn_g[0:1], norm_ffn_g[1:2]
    g_fin = final_norm_g.reshape((1, D))

    hn0 = _rms_fwd("rms_mix0", xf, g_mix0, tm)
    u = mm("cv_pw1", hn0, w_pw1, "NN", F32, epi=lambda acc, b: acc + b, epi_ins=[(cv_b_pw1, "row")])
    dwc, s_act = dwconv_fwd("cv_dwconv", u, 2 * D, wdw_p, cv_b_dw, S, KCV, HB_CV, f_glu, (f_ln_silu, (cv_ln_g, cv_ln_b)), tm)
    w_pw2, w1_0, w2_0 = gather_wait(gather_a, dwc)
    h1, hnf0 = mm("cv_pw2", s_act, w_pw2, "NN", [F32, BF16], epi=lambda acc, b, r, g: _res_rms(acc + b + r, g),
                  epi_ins=[(cv_b_pw2, "row"), (xf, "tile"), (g_ffn0, "row")])
    z1_0 = mm("mlp_up0", hnf0, w1_0, "NN", BF16)
    h2, hn2 = mm("mlp_down0", z1_0, w2_0, "NN", [F32, BF16], a_fn=f_relu2, epi=lambda acc, r, g: _res_rms(acc + r, g),
                 epi_ins=[(h1, "tile"), (g_mix1, "row")])

    w_in_sm, w_out, w1_1, w2_1 = gather_wait(gather_b, h2)
    w_in = jnp.transpose(w_in_sm, (1, 0, 2)).reshape((D, 4 * D + 2 * H))
    w_qkv, w_z = w_in[:, :3 * D], w_in[:, 3 * D:4 * D]
    w_qkvz = w_in[:, :4 * D]
    w_ab = jnp.pad(w_in[:, 4 * D:], ((0, 0), (0, HD - 2 * H)))
    pqkvz = mm("gdn_in", hn2, w_qkvz, "NN", F32)
    pab = mm("gdn_in_ab", hn2, w_ab, "NN", F32)
    cpre = dwconv_fwd("gdn_conv", pqkvz, 3 * D, wcv_p, None, S, KSC, HB_SC, lambda v: v, None, tm)[0]
    gu, gw, gqg, gkg, gqk, gt, geg = gdn_prep_fwd(cpre, pab, alog_p, dtb_p)
    o, sall = gdn_scan_fwd(gu, gw, gqg, gkg, gqk, geg, S)
    on = row_call("gdn_post", lambda oo, zz, ng: ((f_post(oo, zz, ng),), ()), [(o, D, 0), (pqkvz, D, 3)],
                  [gdn_norm_g], [(D, BF16)], [], tm)[0]
    h3, hnf1 = mm("gdn_out", on, w_out, "NN", [F32, BF16], epi=lambda acc, r, g: _res_rms(acc + r, g),
                  epi_ins=[(h2, "tile"), (g_ffn1, "row")])
    z1_1 = mm("mlp_up1", hnf1, w1_1, "NN", BF16)

    def head(acc, res, tt, gg):
        def loss_of(h_, g_):
            return 0.5 * jnp.sum(jnp.mean(jnp.square(f_rms(h_, g_) - tt), axis=-1))
        lv, (dh_, dg_) = jax.value_and_grad(loss_of, (0, 1))(acc + res, gg)
        return (dh_,), (dg_, jnp.full((1, D), lv, F32))

    dh4, dg_fin, loss_row = mm("mlp_down1", z1_1, w2_1, "NN", [F32], a_fn=f_relu2, epi=head,
                               epi_ins=[(h3, "tile"), (tgt, "tile"), (g_fin, "row")], accs=[(1, D), (1, D)])

    dh3, dg_ffn1, _, dw1_1, dw2_1 = _mlp_bwd("1", dh4, h3, g_ffn1, w1_1, w2_1, hnf1, z1_1)
    dw_out = mm_tn("gdn_out_dw", on, dh3)

    def post_bwd(don, oo, zz, ng):
        _, vjp = jax.vjp(f_post, oo, zz, ng)
        do_, dz_, dng_ = vjp(don)
        return (do_, dz_), (dng_,)

    do, dz, dng = mm("gdn_out_dx", dh3, w_out, "NT", [F32, F32], epi=post_bwd,
                     epi_ins=[(o, "tile"), (pqkvz, ("cols", 3)), (gdn_norm_g, "whole")], accs=[(1, HD)])
    du, dw, dqg, dkg, dqk, deg = gdn_scan_bwd(do, gu, gw, gqg, gkg, gqk, geg, sall, S)
    dcpre, dpab, dalog, ddtb = gdn_prep_bwd(cpre, pab, alog_p, dtb_p, gt, du, dw, dqg, dkg, dqk, deg)
    dqkv, dwcv, _ = dwconv_bwd("gdn_conv_bwd", dcpre, pqkvz, 3 * D, wcv_p, S, KSC, HB_SC, lambda v: v, None, tm)
    dhn2 = mm("gdn_in_dx_ab", dpab, w_ab, "NT", F32)
    dhn2 = mm("gdn_in_dx_z", dz, w_z, "NT", F32, epi=lambda acc, r: acc + r, epi_ins=[(dhn2, "tile")])
    dh2, dg_mix1, _ = mm("gdn_in_dx_qkv", dqkv, w_qkv, "NT", [F32],
                         epi=lambda acc, prev, hh, rr, gg: _rms_bwd_epi(acc + prev, hh, rr, gg),
                         epi_ins=[(dhn2, "tile"), (h2, "tile"), (dh3, "tile"), (g_mix1, "row")], accs=[(1, D), (1, D)])
    dw_in = jnp.concatenate([mm_tn("gdn_in_dw_qkv", hn2, dqkv), mm_tn("gdn_in_dw_z", hn2, dz),
                             mm_tn("gdn_in_dw_ab", hn2, dpab)[:, :2 * H]], axis=1)

    dw_in_sm = jnp.transpose(dw_in.reshape((D, 4, D + 4)), (1, 0, 2))
    reduce_b, rtoken_b = reduce_start("B", LAYER1, [dw_in_sm, dw_out, dw1_1, dw2_1], pref)

    dh1, dg_ffn0, db_pw2, dw1_0, dw2_0 = _mlp_bwd("0", dh2, h1, g_ffn0, w1_0, w2_0, hnf0, z1_0, rtoken_b)
    reduce_a, rtoken_a = reduce_start("A", LAYER0[1:], [dw1_0, dw2_0], pref)
    dw_pw2 = mm_tn("cv_pw2_dw", s_act, dh1)

    def ln_bwd(ds, xx, gg, bb, *_):
        _, vjp = jax.vjp(f_ln_silu, xx, gg, bb)
        dx_, dg_, db_ = vjp(ds)
        return (dx_,), (dg_, db_, jnp.sum(dx_, axis=0, keepdims=True))

    ddw, dln_g, dln_b, db_dw = mm("cv_pw2_dx", dh1, w_pw2, "NT", [F32], epi=ln_bwd,
                                  epi_ins=[(dwc, "tile"), (cv_ln_g, "row"), (cv_ln_b, "row"), (rtoken_a, "whole")],
                                  accs=[(1, D)] * 3)
    du_cv, dwdw, db_pw1 = dwconv_bwd("cv_dwconv_bwd", ddw, u, 2 * D, wdw_p, S, KCV, HB_CV, f_glu, _glu_bwd, tm)
    dw_pw1 = mm_tn("cv_pw1_dw", hn0, du_cv)
    grad_x, dg_mix0, _ = mm("cv_pw1_dx", du_cv, w_pw1, "NT", [F32], epi=_rms_bwd_epi,
                            epi_ins=[(xf, "tile"), (dh1, "tile"), (g_mix0, "row")], accs=[(1, D), (1, D)])

    last = FIRST + LAYER0[:1]
    sums_l = chip_sums("0", last, [dw_pw1, dw_pw2], pref)
    plan_l = _plan_reduce_chips(last)
    _, parts_l = xor_exchange("reduce_chips0", sums_l, [], _parts_shapes(last),
                              lambda ins_, ip_, outs_, pos: (plan_l(list(ins_) + list(outs_), pos), []), 3 * len(last))
    sums, parts = [None] * len(WSPECS), [None] * len(WSPECS)
    for sel, sums_s, parts_s in ((last, sums_l, parts_l), reduce_wait(reduce_a, grad_x), reduce_wait(reduce_b, grad_x)):
        for j, i in enumerate(sel):
            sums[i], parts[i] = sums_s[j], parts_s[j]
    g_pw1, g_pw2, g_in, g_out, g_w1, g_w2 = finish_reduce(sums, parts, pref)

    small = jnp.concatenate([
        dg_mix0, dg_mix1, dg_ffn0, dg_ffn1, dg_fin, db_pw1.reshape((2, D)), db_dw, dln_g, dln_b, db_pw2,
        _row1(dalog[:, :H]), _row1(ddtb[:, :H]), _row1(dng), loss_row, jnp.zeros((1, D), F32),
        dwdw, dwcv[:KSC].reshape((3 * KSC, D)), jnp.zeros((NSMALL - 48 - 3 * KSC, D), F32)], axis=0)
    small_all = gather_small(small)

    def pack(a, b, c_, d, e, f, g_, h_, i_, j_, k_):
        return jnp.concatenate([a, b, c_.reshape((1, D)), d.reshape((2, D)), e, f, g_, h_, _row1(i_), _row1(j_), _row1(k_),
                                jnp.zeros((2, D), F32)], axis=0)

    order = lambda p: (p + "norm_mix_g", p + "norm_ffn_g", p + "final_norm_g", p + "cv_b_pw1", p + "cv_b_dw", p + "cv_ln_g",
                       p + "cv_ln_b", p + "cv_b_pw2", p + "gdn_a_log", p + "gdn_dt_bias", p + "gdn_norm_g")
    w16, m16, v16 = (pack(*[env[nm] for nm in order(p)]) for p in ("", "m_", "v_"))

    def small_step(ga, ww, mm_, vv):
        gsum = ga[0]
        for dev in range(1, 8):
            gsum = gsum + ga[dev]
        delta, m2, v2 = f_adamw(ww, gsum[:16], mm_, vv)
        return gsum, delta, m2, v2

    def small_body(ga_ref, w_ref, m_ref, v_ref, g_out, d_out, m_out, v_out):
        gsum, delta, m2, v2 = small_step(ga_ref[...], w_ref[...], m_ref[...], v_ref[...])
        g_out[...] = gsum
        d_out[...] = delta
        m_out[...] = m2
        v_out[...] = v2

    vm = pl.BlockSpec(memory_space=pltpu.VMEM)
    sg, sd, sm, sv = pl.pallas_call(
        small_body, name="adamw_small", in_specs=[vm] * 4, out_specs=[vm] * 4,
        out_shape=[jax.ShapeDtypeStruct((NSMALL, D), F32)] + [jax.ShapeDtypeStruct((16, D), F32)] * 3)(small_all, w16, m16, v16)

    def unpack(b):
        return (b[0:2], b[2:4], b[4], b[5:7].reshape((1, 2 * D)), b[7:8], b[8:9], b[9:10], b[10:11],
                b[11:12, :H], b[12:13, :H], b[13:14, :HD])

    loss = sg[14, 0]
    g_dw = lax.dynamic_slice(sg[16:16 + KCV], (0, chip * (D // 4)), (KCV, D // 4))
    g_cv = lax.dynamic_slice(sg[48:48 + 3 * KSC].reshape((KSC, 3 * D)), (0, chip * (3 * D // 4)), (KSC, 3 * D // 4))

    def adamw(name, w, g, m, v):
        lead = w.shape[:-2]
        if len(lead) == 1 and lead[0] == 1:
            d, m2, v2 = ew_call(name, f_adamw, [w[0], g.reshape(w.shape[1:]), m[0], v[0]], 3)
            return g.reshape(w.shape), d[None], m2[None], v2[None]
        return (g.reshape(w.shape),) + tuple(ew_call(name, f_adamw, [w, g.reshape(w.shape), m, v], 3))

    res = {
        "cv_w_pw1": adamw("adamw_pw1", cv_w_pw1, g_pw1, m_cv_w_pw1, v_cv_w_pw1),
        "cv_w_dw": adamw("adamw_dw", cv_w_dw, g_dw, m_cv_w_dw, v_cv_w_dw),
        "cv_w_pw2": adamw("adamw_pw2", cv_w_pw2, g_pw2, m_cv_w_pw2, v_cv_w_pw2),
        "gdn_w_in": adamw("adamw_win", gdn_w_in, g_in, m_gdn_w_in, v_gdn_w_in),
        "gdn_conv_w": adamw("adamw_cvw", gdn_conv_w, g_cv, m_gdn_conv_w, v_gdn_conv_w),
        "gdn_w_out": adamw("adamw_wout", gdn_w_out, g_out, m_gdn_w_out, v_gdn_w_out),
        "mlp_w1": adamw("adamw_w1", mlp_w1, g_w1, m_mlp_w1, v_mlp_w1),
        "mlp_w2": adamw("adamw_w2", mlp_w2, g_w2, m_mlp_w2, v_mlp_w2),
    }
    names = ("norm_mix_g", "norm_ffn_g", "final_norm_g", "cv_b_pw1", "cv_b_dw", "cv_ln_g", "cv_ln_b", "cv_b_pw2",
             "gdn_a_log", "gdn_dt_bias", "gdn_norm_g")
    for nm, gg, dd, mm_, vv in zip(names, unpack(sg), unpack(sd), unpack(sm), unpack(sv)):
        res[nm] = (gg, dd, mm_, vv)
    weights = ("norm_mix_g", "norm_ffn_g", "final_norm_g", "cv_w_pw1", "cv_b_pw1", "cv_w_dw", "cv_b_dw", "cv_ln_g",
               "cv_ln_b", "cv_w_pw2", "cv_b_pw2", "gdn_w_in", "gdn_conv_w", "gdn_a_log", "gdn_dt_bias", "gdn_norm_g",
               "gdn_w_out", "mlp_w1", "mlp_w2")
    outs = [loss, grad_x.reshape(x.shape)]
    for kind in range(4):
        outs += [res[nm][kind] for nm in weights]
    return tuple(outs)
```

```python
import functools

import jax
import jax.numpy as jnp
from jax import lax
from jax.experimental import pallas as pl
from jax.experimental.pallas import tpu as pltpu

F32, BF16 = jnp.float32, jnp.bfloat16
D = 1024
H = 8
HD = 128
CH = 64
DFF = 4 * D
KCV, HB_CV = 31, 32
KSC, HB_SC = 4, 8
EPS = 1e-6
LR, B1, B2, EPS_A, WD, STEP = 0.001, 0.9, 0.999, 1e-08, 0.01, 10
VMEM_LIMIT = 56 * 1024 * 1024
SUB = 32
NSMALL = 64
MESH = pl.DeviceIdType.MESH


def _cp(*sem):
    return pltpu.CompilerParams(dimension_semantics=sem, vmem_limit_bytes=VMEM_LIMIT)


def f_rms(h, g):
    return h * lax.rsqrt(jnp.mean(h * h, axis=-1, keepdims=True) + EPS) * g


def f_silu(x):
    return x * jax.nn.sigmoid(x)


def f_glu(u):
    return u[:, :D] * jax.nn.sigmoid(u[:, D:])


def f_ln_silu(x, g, b):
    mu = jnp.mean(x, axis=-1, keepdims=True)
    xc = x - mu
    y = xc * lax.rsqrt(jnp.mean(xc * xc, axis=-1, keepdims=True) + EPS)
    return f_silu(y * g + b)


def f_relu2(z):
    r = jnp.maximum(z.astype(F32), 0.0)
    return r * r


def f_post(o, z, ng):
    outs = []
    for h in range(H):
        oh = o[:, h * HD:(h + 1) * HD]
        y = oh * lax.rsqrt(jnp.mean(oh * oh, axis=-1, keepdims=True) + EPS) * ng
        outs.append(y * f_silu(z[:, h * HD:(h + 1) * HD]))
    return jnp.concatenate(outs, axis=1)


def f_adamw(w, g, m, v):
    m2 = B1 * m + (1.0 - B1) * g
    v2 = B2 * v + (1.0 - B2) * (g * g)
    m_hat = m2 / (1.0 - B1 ** STEP)
    v_hat = v2 / (1.0 - B2 ** STEP)
    delta = -LR * (m_hat / (jnp.sqrt(v_hat) + EPS_A) + WD * w)
    return delta, m2, v2


def _dot_raw(a, b, mode):
    dims = {"NN": ((1,), (0,)), "NT": ((1,), (1,)), "TN": ((0,), (0,))}[mode]
    return lax.dot_general(a.astype(BF16), b.astype(BF16), (dims, ((), ())), preferred_element_type=F32)


@functools.partial(jax.custom_vjp, nondiff_argnums=(2,))
def _dot_vjp(a, b, mode):
    return _dot_raw(a, b, mode)


def _dot_fwd(a, b, mode):
    return _dot_raw(a, b, mode), (a, b)


def _dot_bwd(mode, res, dc):
    a, b = res
    if mode == "NN":
        return _dot_vjp(dc, b, "NT"), _dot_vjp(a, dc, "TN")
    if mode == "NT":
        return _dot_vjp(dc, b, "NN"), _dot_vjp(dc, a, "TN")
    return _dot_vjp(b, dc, "NT"), _dot_vjp(a, dc, "NN")


_dot_vjp.defvjp(_dot_fwd, _dot_bwd)


def _split(x):
    xh = x.astype(BF16)
    return xh, (x - xh.astype(F32)).astype(BF16)


def _dot_split(xs, ys):
    (xh, xl), (yh, yl) = xs, ys
    return _dot_raw(xh, yh, "NN") + (_dot_raw(xh, yl, "NN") + _dot_raw(xl, yh, "NN"))


def _tril_inverse(a_list):
    ri = lax.broadcasted_iota(jnp.int32, (CH, CH), 0)
    ci = lax.broadcasted_iota(jnp.int32, (CH, CH), 1)
    eye = (ri == ci).astype(F32)
    ts = None
    for lvl in range(CH.bit_length() - 1):
        same_pair = jnp.right_shift(ri, lvl + 1) == jnp.right_shift(ci, lvl + 1)
        quarter = (jnp.bitwise_and(jnp.right_shift(ri, lvl), 1) == 1) & (jnp.bitwise_and(jnp.right_shift(ci, lvl), 1) == 0)
        offs = [jnp.where(same_pair & quarter, a, 0.0) for a in a_list]
        if ts is None:
            ts = [eye - off for off in offs]
            continue
        tsp = [_split(t) for t in ts]
        mids = [_dot_split(tp, _split(off)) for tp, off in zip(tsp, offs)]
        ts = [t - _dot_split(_split(m), tp) for t, m, tp in zip(ts, mids, tsp)]
    return ts


@jax.custom_vjp
def _stored_solve(a, t, rhs):
    return _dot_raw(t, rhs, "NN")


def _stored_solve_fwd(a, t, rhs):
    sol = _dot_raw(t, rhs, "NN")
    return sol, (t, sol)


def _stored_solve_bwd(res, g):
    t, sol = res
    g_rhs = _dot_vjp(t, g, "TN")
    return -_dot_vjp(g_rhs, sol, "NT"), jnp.zeros_like(t), g_rhs


_stored_solve.defvjp(_stored_solve_fwd, _stored_solve_bwd)


def _lane_pick(row, idx, width):
    sel = lax.broadcasted_iota(jnp.int32, (1, width), 1) == idx
    return jnp.sum(jnp.where(sel, row, 0.0), axis=1, keepdims=True)


def f_prep(cqs, cks, cvs, araws, braws, alogs, dtbs, t_stored, dot):
    ri = lax.broadcasted_iota(jnp.int32, (CH, CH), 0)
    ci = lax.broadcasted_iota(jnp.int32, (CH, CH), 1)
    eye = (ri == ci).astype(F32)
    low = (ri >= ci).astype(F32)
    last = lax.broadcasted_iota(jnp.int32, (CH, 1), 0) == CH - 1
    nh = range(len(cqs))
    qs, ks, vbs, kbs, gcs, decays = [], [], [], [], [], []
    for h in nh:
        q = f_silu(cqs[h])
        qs.append(q * lax.rsqrt(jnp.sum(q * q, axis=-1, keepdims=True) + 1e-6) * (HD ** -0.5))
        k = f_silu(cks[h])
        k = k * lax.rsqrt(jnp.sum(k * k, axis=-1, keepdims=True) + 1e-6)
        ks.append(k)
        beta = jax.nn.sigmoid(braws[h])
        sp_in = araws[h] + dtbs[h]
        softplus = jnp.maximum(sp_in, 0.0) + jnp.log(1.0 + jnp.exp(-jnp.abs(sp_in)))
        g = -jnp.exp(alogs[h]) * softplus
        g_row = jnp.sum(eye * g, axis=0, keepdims=True)
        gc = jnp.sum(low * g_row, axis=1, keepdims=True)
        gc_row = jnp.sum(eye * gc, axis=0, keepdims=True)
        gcs.append(gc)
        decays.append(jnp.exp(jnp.where(ri >= ci, gc - gc_row, -1e30)))
        vbs.append(f_silu(cvs[h]) * beta)
        kbs.append(k * beta)
    kks = [dot(kbs[h], ks[h], "NT") for h in nh]
    a_list = [jnp.where(ri > ci, kks[h] * decays[h], 0.0) for h in nh]
    if t_stored is None:
        ts = _tril_inverse(a_list)
        solve = lambda h, rhs: dot(ts[h], rhs, "NN")
    else:
        ts = t_stored
        solve = lambda h, rhs: _stored_solve(a_list[h], t_stored[h], rhs)
    egcs = [jnp.exp(gc) for gc in gcs]
    us = [solve(h, vbs[h]) for h in nh]
    ws = [solve(h, kbs[h] * egcs[h]) for h in nh]
    qks = [dot(qs[h], ks[h], "NT") * decays[h] for h in nh]
    qgs = [qs[h] * egcs[h] for h in nh]
    gls = [jnp.sum(jnp.where(last, gc, 0.0), axis=0, keepdims=True) for gc in gcs]
    kgs = [ks[h] * jnp.exp(gls[h] - gcs[h]) for h in nh]
    egs = [jnp.exp(gl) * jnp.ones((1, HD), F32) for gl in gls]
    return us, ws, qks, qgs, kgs, egs, ts


def f_scan(ss, us, ws, qgs, kgs, qks, egs, dot):
    nh = range(len(ss))
    ws_s = [dot(ws[h], ss[h], "NN") for h in nh]
    qs_s = [dot(qgs[h], ss[h], "NN") for h in nh]
    vns = [us[h] - ws_s[h] for h in nh]
    os_ = [qs_s[h] + dot(qks[h], vns[h], "NN") for h in nh]
    s2s = [ss[h] * egs[h] + dot(kgs[h], vns[h], "TN") for h in nh]
    return os_, s2s


def row_call(name, fn, rows, pars, out_rows, out_accs, tm):
    T = rows[0][0].shape[0]
    n_r, n_p, n_o = len(rows), len(pars), len(out_rows)
    in_specs = [pl.BlockSpec((tm, w), functools.partial(lambda i, cb: (i, cb), cb=cb)) for (_, w, cb) in rows]
    in_specs += [pl.BlockSpec(p.shape, functools.partial(lambda i, nd: (0,) * nd, nd=p.ndim)) for p in pars]
    out_specs = [pl.BlockSpec((tm, w), lambda i: (i, 0)) for (w, _) in out_rows]
    out_specs += [pl.BlockSpec(s, lambda i: (0, 0)) for s in out_accs]
    out_shape = [jax.ShapeDtypeStruct((T, w), dt) for (w, dt) in out_rows]
    out_shape += [jax.ShapeDtypeStruct(s, F32) for s in out_accs]

    def body(*refs):
        rin, pin = refs[:n_r], refs[n_r:n_r + n_p]
        rout, aout = refs[n_r + n_p:n_r + n_p + n_o], refs[n_r + n_p + n_o:]
        if aout:
            @pl.when(pl.program_id(0) == 0)
            def _():
                for a in aout:
                    a[...] = jnp.zeros(a.shape, F32)
        pv = [p[...] for p in pin]

        def step(r, carry):
            sl = pl.ds(pl.multiple_of(r * SUB, SUB), SUB)
            outs, accs = fn(*[x[sl, :] for x in rin], *pv)
            for o, val in zip(rout, outs):
                o[sl, :] = val.astype(o.dtype)
            for a, val in zip(aout, accs):
                a[...] += val
            return carry

        lax.fori_loop(0, tm // SUB, step, 0)

    return pl.pallas_call(body, name=name, grid=(T // tm,), in_specs=in_specs, out_specs=out_specs,
                          out_shape=out_shape, compiler_params=_cp("arbitrary"))(*[r[0] for r in rows], *pars)


EW_TILE_ELEMS = 256 * 1024


def _ew_rows(R, Cc):
    if R * Cc <= EW_TILE_ELEMS or R % 8:
        return R
    tr = 8
    while tr * 2 * Cc <= EW_TILE_ELEMS and R % (tr * 2) == 0:
        tr *= 2
    return tr


def ew_call(name, fn, ins, n_out):
    shape = ins[0].shape
    lead = shape[:-2]
    R, Cc = shape[-2:]
    tr = _ew_rows(R, Cc)
    grid = lead + (R // tr,)
    nl = len(lead)
    spec = pl.BlockSpec((None,) * nl + (tr, Cc), lambda *idx: idx + (0,))

    def body(*refs):
        outs = fn(*[r[...] for r in refs[:len(ins)]])
        for o, val in zip(refs[len(ins):], outs):
            o[...] = val

    return pl.pallas_call(body, name=name, grid=grid, in_specs=[spec] * len(ins), out_specs=[spec] * n_out,
                          out_shape=[jax.ShapeDtypeStruct(shape, F32)] * n_out,
                          compiler_params=_cp(*(("arbitrary",) * len(grid))))(*ins)


MM_RESIDENT_BYTES = 8 * 1024 * 1024


def mm(name, a, b, mode, out_dtype, a_fn=None, epi=None, epi_ins=(), accs=(), tm=512):
    sub_epi = epi is not None and isinstance(out_dtype, (list, tuple))
    M, K = a.shape
    N = b.shape[1] if mode == "NN" else b.shape[0]
    tn = N if K * N * 2 <= MM_RESIDENT_BYTES else min(N, 1024)
    tm = min(tm if tn <= 1024 else tm // 2, M)
    multi = isinstance(out_dtype, (list, tuple))
    dts = list(out_dtype) if multi else [out_dtype]
    n_e, n_o = len(epi_ins), len(dts)
    in_specs = [pl.BlockSpec((tm, K), lambda j, i: (i, 0)),
                pl.BlockSpec((K, tn), lambda j, i: (0, j)) if mode == "NN" else pl.BlockSpec((tn, K), lambda j, i: (j, 0))]
    row_kinds = []
    for (arr, kind) in epi_ins:
        if kind == "tile" or isinstance(kind, tuple):
            cb = kind[1] if isinstance(kind, tuple) else 0
            in_specs.append(pl.BlockSpec((tm, tn), functools.partial(lambda j, i, cb: (i, cb + j), cb=cb)))
            row_kinds.append(True)
        elif kind == "row":
            in_specs.append(pl.BlockSpec((1, tn), lambda j, i: (0, j)))
            row_kinds.append(False)
        else:
            in_specs.append(pl.BlockSpec(arr.shape, lambda j, i: (0, 0)))
            row_kinds.append(False)

    def body(a_ref, b_ref, *rest):
        e_refs, o_refs, acc_refs = rest[:n_e], rest[n_e:n_e + n_o], rest[n_e + n_o:n_e + n_o + len(accs)]
        av = a_ref[...]
        if a_fn is not None:
            av = a_fn(av)
        res = _dot_raw(av, b_ref[...], mode)
        if epi is None or not sub_epi:
            if epi is not None:
                res = epi(res, *[r[...] for r in e_refs])
            o_refs[0][...] = res.astype(o_refs[0].dtype)
            return
        prod = rest[-1]
        prod[...] = res
        if acc_refs:
            @pl.when((pl.program_id(0) == 0) & (pl.program_id(1) == 0))
            def _():
                for r in acc_refs:
                    r[...] = jnp.zeros(r.shape, F32)
        small = [None if is_rows else r[...] for r, is_rows in zip(e_refs, row_kinds)]

        def step(k, carry):
            sl = pl.ds(pl.multiple_of(k * SUB, SUB), SUB)
            out = epi(prod[sl, :], *[r[sl, :] if is_rows else sm for r, is_rows, sm in zip(e_refs, row_kinds, small)])
            tiles, contribs = out if multi else ((out,), ())
            for r, t in zip(o_refs, tiles):
                r[sl, :] = t.astype(r.dtype)
            for r, t in zip(acc_refs, contribs):
                r[...] += t
            return carry

        lax.fori_loop(0, tm // SUB, step, 0, unroll=4)

    out_specs = [pl.BlockSpec((tm, tn), lambda j, i: (i, j))] * n_o + [pl.BlockSpec(s, lambda j, i: (0, 0)) for s in accs]
    out_shape = [jax.ShapeDtypeStruct((M, N), dt) for dt in dts] + [jax.ShapeDtypeStruct(s, F32) for s in accs]
    res = pl.pallas_call(body, name=name, grid=(N // tn, M // tm), in_specs=in_specs, out_specs=out_specs,
                         out_shape=out_shape, scratch_shapes=[pltpu.VMEM((tm, tn), F32)] if sub_epi else [],
                         compiler_params=_cp("arbitrary", "arbitrary"))(a, b, *[e[0] for e in epi_ins])
    return res if multi else res[0]


def mm_tn(name, a, g, a_fn=None, a_cols=None, tt=1024):
    T = a.shape[0]
    ka, acb = (a.shape[1], 0) if a_cols is None else a_cols
    N = g.shape[1]
    tt = min(tt, T)
    tka, tn = min(ka, 1024), min(N, 1024)
    nkb = ka // tka

    def body(a_ref, g_ref, o_ref):
        @pl.when(pl.program_id(2) == 0)
        def _():
            o_ref[...] = jnp.zeros(o_ref.shape, F32)
        av = a_ref[...]
        if a_fn is not None:
            av = a_fn(av)
        o_ref[...] += _dot_raw(av, g_ref[...], "TN")

    return pl.pallas_call(body, name=name, grid=(nkb, N // tn, T // tt),
                          in_specs=[pl.BlockSpec((tt, tka), lambda ia, j, t: (t, acb * nkb + ia)),
                                    pl.BlockSpec((tt, tn), lambda ia, j, t: (t, j))],
                          out_specs=pl.BlockSpec((tka, tn), lambda ia, j, t: (ia, j)),
                          out_shape=jax.ShapeDtypeStruct((ka, N), F32),
                          compiler_params=_cp("arbitrary", "arbitrary", "arbitrary"))(a, g)


SUBLANES = 8


class _RowShifts:
    def __init__(self, src, shifted, nrows, reuse):
        self.src, self.shifted, self.reuse = src, shifted, reuse
        if reuse:
            for ph in range(1, SUBLANES):
                for r0 in range(0, nrows - SUBLANES, SUB):
                    n = min(SUB, nrows - SUBLANES - r0)
                    shifted[ph - 1, r0:r0 + n, :] = src[r0 + ph:r0 + ph + n, :]

    def window(self, off, cols):
        ph = off % SUBLANES
        if not self.reuse or ph == 0:
            return self.src[off:off + SUB, cols]
        return self.shifted[ph - 1, off - ph:off - ph + SUB, cols]


def _shift_scratch(nrows, C, reuse):
    return [pltpu.VMEM((SUBLANES - 1, nrows - SUBLANES, C), F32)] if reuse else []


def dwconv_fwd(name, x, xw, w_pad, bias, S, K, HB, pre, post, tm):
    T = x.shape[0]
    C = w_pad.shape[1]
    nb, per_seq = tm // HB, S // tm
    has_b, has_post = bias is not None, post is not None
    reuse = K > SUBLANES

    def body(*refs):
        x_ref, xp_ref, w_ref = refs[:3]
        pos = 3
        b_ref = refs[pos] if has_b else None
        pos += has_b
        ppars = refs[pos:pos + (len(post[1]) if has_post else 0)]
        pos += len(ppars)
        c_ref = refs[pos]
        s_ref = refs[pos + 1] if has_post else None
        ext = refs[pos + 1 + has_post]
        first = (pl.program_id(0) % per_seq) == 0
        ext[0:HB, :] = jnp.where(first, 0.0, pre(xp_ref[...]))
        for r in range(tm // SUB):
            ext[HB + r * SUB:HB + (r + 1) * SUB, :] = pre(x_ref[r * SUB:(r + 1) * SUB, :])
        rows_of = _RowShifts(ext, refs[-1] if reuse else None, HB + tm, reuse)
        pv = [p[...] for p in ppars]
        assert not has_post or C == D
        for r in range(tm // SUB):
            for c0 in range(0, C, D):
                cols = slice(c0, c0 + D)
                acc = jnp.zeros((SUB, D), F32)
                if has_b:
                    acc = acc + b_ref[:, cols]
                for k in range(K):
                    acc = acc + w_ref[k:k + 1, cols] * rows_of.window(HB + r * SUB - (K - 1) + k, cols)
                c_ref[r * SUB:(r + 1) * SUB, cols] = acc
                if has_post:
                    s_ref[r * SUB:(r + 1) * SUB, :] = post[0](acc, *pv).astype(BF16)

    ins = [x, x, w_pad] + ([bias] if has_b else []) + (list(post[1]) if has_post else [])
    in_specs = [pl.BlockSpec((tm, xw), lambda i: (i, 0)),
                pl.BlockSpec((HB, xw), lambda i: (jnp.maximum(i * nb - 1, 0), 0)),
                pl.BlockSpec(w_pad.shape, lambda i: (0, 0))]
    in_specs += [pl.BlockSpec(p.shape, lambda i: (0, 0)) for p in ins[3:]]
    out_specs = [pl.BlockSpec((tm, C), lambda i: (i, 0))] * (1 + has_post)
    out_shape = [jax.ShapeDtypeStruct((T, C), F32)] + ([jax.ShapeDtypeStruct((T, C), BF16)] if has_post else [])
    return pl.pallas_call(body, name=name, grid=(T // tm,), in_specs=in_specs, out_specs=out_specs, out_shape=out_shape,
                          scratch_shapes=[pltpu.VMEM((HB + tm, C), F32)] + _shift_scratch(HB + tm, C, reuse),
                          compiler_params=_cp("arbitrary"))(*ins)


def dwconv_bwd(name, g, x, xw, w_pad, S, K, HB, pre, pre_bwd, tm):
    T = g.shape[0]
    C = w_pad.shape[1]
    nb, per_seq = tm // HB, S // tm
    nblk = T // HB

    reuse = K > SUBLANES

    def body(g_ref, gn_ref, x_ref, xp_ref, w_ref, dx_ref, dw_ref, dbx_ref, extg, exta, *shift_refs):
        i = pl.program_id(0)
        first = (i % per_seq) == 0
        last = (i % per_seq) == per_seq - 1

        @pl.when(i == 0)
        def _():
            dw_ref[...] = jnp.zeros(dw_ref.shape, F32)
            dbx_ref[...] = jnp.zeros(dbx_ref.shape, F32)

        extg[tm:tm + HB, :] = jnp.where(last, 0.0, gn_ref[...])
        exta[0:HB, :] = jnp.where(first, 0.0, pre(xp_ref[...]))
        for r in range(tm // SUB):
            extg[r * SUB:(r + 1) * SUB, :] = g_ref[r * SUB:(r + 1) * SUB, :]
            exta[HB + r * SUB:HB + (r + 1) * SUB, :] = pre(x_ref[r * SUB:(r + 1) * SUB, :])
        assert pre_bwd is None or C == D
        g_rows = _RowShifts(extg, shift_refs[0] if reuse else None, tm + HB, reuse)
        a_rows = _RowShifts(exta, shift_refs[1] if reuse else None, HB + tm, reuse)
        for r in range(tm // SUB):
            rows = slice(r * SUB, (r + 1) * SUB)
            for c0 in range(0, C, D):
                cols = slice(c0, c0 + D)
                acc = jnp.zeros((SUB, D), F32)
                for k in range(K):
                    acc = acc + w_ref[k:k + 1, cols] * g_rows.window(r * SUB + (K - 1) - k, cols)
                if pre_bwd is None:
                    dx_ref[rows, cols] = acc
                    dbx_ref[:, cols] += jnp.sum(acc, axis=0, keepdims=True)
                else:
                    dx = pre_bwd(x_ref[rows, :], acc)
                    dx_ref[rows, :] = dx
                    dbx_ref[...] += jnp.sum(dx, axis=0, keepdims=True)
        for k in range(K):
            for c0 in range(0, C, D):
                cols = slice(c0, c0 + D)
                p = jnp.zeros((SUB, D), F32)
                for r in range(tm // SUB):
                    p = p + extg[r * SUB:(r + 1) * SUB, cols] * a_rows.window(HB + r * SUB - (K - 1) + k, cols)
                dw_ref[k:k + 1, cols] += jnp.sum(p, axis=0, keepdims=True)

    in_specs = [pl.BlockSpec((tm, C), lambda i: (i, 0)),
                pl.BlockSpec((HB, C), lambda i: (jnp.minimum((i + 1) * nb, nblk - 1), 0)),
                pl.BlockSpec((tm, xw), lambda i: (i, 0)),
                pl.BlockSpec((HB, xw), lambda i: (jnp.maximum(i * nb - 1, 0), 0)),
                pl.BlockSpec(w_pad.shape, lambda i: (0, 0))]
    out_specs = [pl.BlockSpec((tm, xw), lambda i: (i, 0)), pl.BlockSpec((HB, C), lambda i: (0, 0)),
                 pl.BlockSpec((1, xw), lambda i: (0, 0))]
    out_shape = [jax.ShapeDtypeStruct((T, xw), F32), jax.ShapeDtypeStruct((HB, C), F32), jax.ShapeDtypeStruct((1, xw), F32)]
    return pl.pallas_call(body, name=name, grid=(T // tm,), in_specs=in_specs, out_specs=out_specs, out_shape=out_shape,
                          scratch_shapes=[pltpu.VMEM((tm + HB, C), F32), pltpu.VMEM((HB + tm, C), F32)]
                          + _shift_scratch(tm + HB, C, reuse) * 2,
                          compiler_params=_cp("arbitrary"))(g, g, x, x, w_pad)


def _glu_bwd(u, da):
    u1, sg = u[:, :D], jax.nn.sigmoid(u[:, D:])
    return jnp.concatenate([da * sg, da * u1 * sg * (1.0 - sg)], axis=1)


def _head_cols(ref, h, base=0, rows=slice(None)):
    return ref[rows, base + h * HD:base + (h + 1) * HD]


def _prep_inputs(c_ref, ab, al, dt, rows=slice(None)):
    hs = range(H)
    return ([_head_cols(c_ref, h, 0, rows) for h in hs], [_head_cols(c_ref, h, D, rows) for h in hs],
            [_head_cols(c_ref, h, 2 * D, rows) for h in hs], [_lane_pick(ab, h, HD) for h in hs],
            [_lane_pick(ab, H + h, HD) for h in hs], [_lane_pick(al, h, HD) for h in hs],
            [_lane_pick(dt, h, HD) for h in hs])


PREP_CHUNKS = 2


def gdn_prep_fwd(cpre, pab, alog, dtb):
    T = cpre.shape[0]
    nc = T // CH
    G = PREP_CHUNKS

    def body(c_ref, ab_ref, al_ref, dt_ref, u_ref, w_ref, qg_ref, kg_ref, qk_ref, t_ref, eg_ref):
        ins = [[] for _ in range(7)]
        for ci in range(G):
            rows = slice(ci * CH, (ci + 1) * CH)
            for lst, part in zip(ins, _prep_inputs(c_ref, ab_ref[rows, :], al_ref[...], dt_ref[...], rows)):
                lst += part
        us, ws, qks, qgs, kgs, egs, ts = f_prep(*ins, None, _dot_raw)
        for ci in range(G):
            rows = slice(ci * CH, (ci + 1) * CH)
            for h in range(H):
                cols, k = slice(h * HD, (h + 1) * HD), ci * H + h
                u_ref[rows, cols] = us[k]
                w_ref[rows, cols] = ws[k].astype(BF16)
                qg_ref[rows, cols] = qgs[k].astype(BF16)
                kg_ref[rows, cols] = kgs[k].astype(BF16)
                qk_ref[ci, h] = qks[k].astype(BF16)
                t_ref[ci, h] = ts[k].astype(BF16)
                eg_ref[ci, h:h + 1, :] = egs[k]

    row = lambda w: pl.BlockSpec((G * CH, w), lambda n: (n, 0))
    par = pl.BlockSpec((1, HD), lambda n: (0, 0))
    mat = pl.BlockSpec((G, H, CH, CH), lambda n: (n, 0, 0, 0))
    return pl.pallas_call(
        body, name="gdn_prep_fwd", grid=(nc // G,), in_specs=[row(3 * D), row(HD), par, par],
        out_specs=[row(D), row(D), row(D), row(D), mat, mat, pl.BlockSpec((G, H, HD), lambda n: (n, 0, 0))],
        out_shape=[jax.ShapeDtypeStruct((T, D), F32)] + [jax.ShapeDtypeStruct((T, D), BF16)] * 3
        + [jax.ShapeDtypeStruct((nc, H, CH, CH), BF16)] * 2 + [jax.ShapeDtypeStruct((nc, H, HD), F32)],
        compiler_params=_cp("arbitrary"))(cpre, pab, alog, dtb)


def gdn_prep_bwd(cpre, pab, alog, dtb, tmat, du, dw, dqg, dkg, dqk, deg):
    T = cpre.shape[0]
    nc = T // CH
    G = PREP_CHUNKS

    def body(c_ref, ab_ref, al_ref, dt_ref, t_ref, du_ref, dw_ref, dqg_ref, dkg_ref, dqk_ref, deg_ref,
             dc_ref, dab_ref, dal_ref, ddt_ref):
        @pl.when(pl.program_id(0) == 0)
        def _():
            dal_ref[...] = jnp.zeros(dal_ref.shape, F32)
            ddt_ref[...] = jnp.zeros(ddt_ref.shape, F32)

        lane = lax.broadcasted_iota(jnp.int32, (1, HD), 1)
        dal = jnp.zeros((1, HD), F32)
        ddt = jnp.zeros((1, HD), F32)
        hs = range(H)
        chunks = [(ci, slice(ci * CH, (ci + 1) * CH)) for ci in range(G)]
        t_st = [t_ref[ci, h].astype(F32) for ci, _ in chunks for h in hs]
        ins = [[] for _ in range(7)]
        for ci, rows in chunks:
            for lst, part in zip(ins, _prep_inputs(c_ref, ab_ref[rows, :], al_ref[...], dt_ref[...], rows)):
                lst += part

        def fwd(*args):
            return tuple(f_prep(*args, t_st, _dot_vjp)[:6])

        _, vjp = jax.vjp(fwd, *ins)
        dcqs, dcks, dcvs, dars, dbrs, dals, ddts = vjp((
            [_head_cols(du_ref, h, 0, rows) for _, rows in chunks for h in hs],
            [_head_cols(dw_ref, h, 0, rows) for _, rows in chunks for h in hs],
            [dqk_ref[ci, h] for ci, _ in chunks for h in hs],
            [_head_cols(dqg_ref, h, 0, rows) for _, rows in chunks for h in hs],
            [_head_cols(dkg_ref, h, 0, rows) for _, rows in chunks for h in hs],
            [deg_ref[ci, h:h + 1, :] for ci, _ in chunks for h in hs]))
        for ci, rows in chunks:
            dab = jnp.zeros((CH, HD), F32)
            for h in hs:
                k = ci * H + h
                dc_ref[rows, h * HD:(h + 1) * HD] = dcqs[k]
                dc_ref[rows, D + h * HD:D + (h + 1) * HD] = dcks[k]
                dc_ref[rows, 2 * D + h * HD:2 * D + (h + 1) * HD] = dcvs[k]
                dab = dab + jnp.where(lane == h, dars[k], 0.0) + jnp.where(lane == H + h, dbrs[k], 0.0)
                dal = dal + jnp.where(lane == h, dals[k], 0.0)
                ddt = ddt + jnp.where(lane == h, ddts[k], 0.0)
            dab_ref[rows, :] = dab
        dal_ref[...] += dal
        ddt_ref[...] += ddt

    row = lambda w: pl.BlockSpec((G * CH, w), lambda n: (n, 0))
    par = pl.BlockSpec((1, HD), lambda n: (0, 0))
    mat = pl.BlockSpec((G, H, CH, CH), lambda n: (n, 0, 0, 0))
    vec = pl.BlockSpec((G, H, HD), lambda n: (n, 0, 0))
    return pl.pallas_call(
        body, name="gdn_prep_bwd", grid=(nc // G,),
        in_specs=[row(3 * D), row(HD), par, par, mat, row(D), row(D), row(D), row(D), mat, vec],
        out_specs=[row(3 * D), row(HD), par, par],
        out_shape=[jax.ShapeDtypeStruct((T, 3 * D), F32), jax.ShapeDtypeStruct((T, HD), F32),
                   jax.ShapeDtypeStruct((1, HD), F32), jax.ShapeDtypeStruct((1, HD), F32)],
        compiler_params=_cp("arbitrary"))(cpre, pab, alog, dtb, tmat, du, dw, dqg, dkg, dqk, deg)


def gdn_scan_fwd(u, w, qg, kg, qk, eg, S):
    T = u.shape[0]
    nc, per_seq = T // CH, S // CH

    def body(u_ref, w_ref, qg_ref, kg_ref, qk_ref, eg_ref, o_ref, sall_ref, s_ref):
        @pl.when(pl.program_id(0) % per_seq == 0)
        def _():
            s_ref[...] = jnp.zeros(s_ref.shape, F32)

        hs = range(H)
        ss = [s_ref[h] for h in hs]
        os_, s2s = f_scan(ss, [_head_cols(u_ref, h) for h in hs], [_head_cols(w_ref, h) for h in hs],
                          [_head_cols(qg_ref, h) for h in hs], [_head_cols(kg_ref, h) for h in hs],
                          [qk_ref[0, h] for h in hs], [eg_ref[0, h:h + 1, :] for h in hs], _dot_raw)
        for h in hs:
            sall_ref[0, h] = ss[h]
            o_ref[:, h * HD:(h + 1) * HD] = os_[h]
            s_ref[h] = s2s[h]

    row = pl.BlockSpec((CH, D), lambda n: (n, 0))
    return pl.pallas_call(
        body, name="gdn_scan_fwd", grid=(nc,),
        in_specs=[row, row, row, row, pl.BlockSpec((1, H, CH, CH), lambda n: (n, 0, 0, 0)),
                  pl.BlockSpec((1, H, HD), lambda n: (n, 0, 0))],
        out_specs=[row, pl.BlockSpec((1, H, HD, HD), lambda n: (n, 0, 0, 0))],
        out_shape=[jax.ShapeDtypeStruct((T, D), F32), jax.ShapeDtypeStruct((nc, H, HD, HD), F32)],
        scratch_shapes=[pltpu.VMEM((H, HD, HD), F32)], compiler_params=_cp("arbitrary"))(u, w, qg, kg, qk, eg)


def gdn_scan_bwd(do, u, w, qg, kg, qk, eg, sall, S):
    T = u.shape[0]
    nc, per_seq = T // CH, S // CH

    def body(do_ref, u_ref, w_ref, qg_ref, kg_ref, qk_ref, eg_ref, sall_ref,
             du_ref, dw_ref, dqg_ref, dkg_ref, dqk_ref, deg_ref, ds_ref):
        n = nc - 1 - pl.program_id(0)

        @pl.when(n % per_seq == per_seq - 1)
        def _():
            ds_ref[...] = jnp.zeros(ds_ref.shape, F32)

        hs = range(H)

        def fwd(*args):
            return f_scan(*args, _dot_vjp)

        _, vjp = jax.vjp(fwd, [sall_ref[0, h] for h in hs], [_head_cols(u_ref, h) for h in hs],
                         [_head_cols(w_ref, h).astype(F32) for h in hs], [_head_cols(qg_ref, h).astype(F32) for h in hs],
                         [_head_cols(kg_ref, h).astype(F32) for h in hs], [qk_ref[0, h].astype(F32) for h in hs],
                         [eg_ref[0, h:h + 1, :] for h in hs])
        dss, dus, dws, dqgs, dkgs, dqks, degs = vjp(([_head_cols(do_ref, h) for h in hs], [ds_ref[h] for h in hs]))
        for h in hs:
            cols = slice(h * HD, (h + 1) * HD)
            du_ref[:, cols] = dus[h]
            dw_ref[:, cols] = dws[h]
            dqg_ref[:, cols] = dqgs[h]
            dkg_ref[:, cols] = dkgs[h]
            dqk_ref[0, h] = dqks[h]
            deg_ref[0, h:h + 1, :] = degs[h]
            ds_ref[h] = dss[h]

    rev = lambda n: (nc - 1 - n, 0)
    row = pl.BlockSpec((CH, D), rev)
    mat = pl.BlockSpec((1, H, CH, CH), lambda n: (nc - 1 - n, 0, 0, 0))
    vec = pl.BlockSpec((1, H, HD), lambda n: (nc - 1 - n, 0, 0))
    return pl.pallas_call(
        body, name="gdn_scan_bwd", grid=(nc,),
        in_specs=[row, row, row, row, row, mat, vec, pl.BlockSpec((1, H, HD, HD), lambda n: (nc - 1 - n, 0, 0, 0))],
        out_specs=[row, row, row, row, mat, vec],
        out_shape=[jax.ShapeDtypeStruct((T, D), F32)] * 4
        + [jax.ShapeDtypeStruct((nc, H, CH, CH), F32), jax.ShapeDtypeStruct((nc, H, HD), F32)],
        scratch_shapes=[pltpu.VMEM((H, HD, HD), F32)], compiler_params=_cp("arbitrary"))(do, u, w, qg, kg, qk, eg, sall)


def xor_exchange(name, ins, inplace, out_shapes, plan, n_remote, n_local=0):
    n_in, n_ip, n_out = len(ins), len(inplace), len(out_shapes)

    def body(*refs):
        in_refs = refs[:n_in]
        ip_refs = refs[n_in + n_ip:n_in + 2 * n_ip]
        out_refs = refs[n_in + 2 * n_ip:n_in + 2 * n_ip + n_out]
        send_sems, recv_sems, loc_sems = refs[n_in + 2 * n_ip + n_out:]
        x, y, c = lax.axis_index("x"), lax.axis_index("y"), lax.axis_index("c")
        remote, local = plan(in_refs, ip_refs, out_refs, (x, y, c))
        assert len(remote) == n_remote and len(local) == n_local
        copies = []
        for k, ((dx, dy, dc), src, dst) in enumerate(remote):
            peer = (1 - x if dx else x, 1 - y if dy else y, 1 - c if dc else c)
            copies.append(pltpu.make_async_remote_copy(src_ref=src, dst_ref=dst, send_sem=send_sems.at[k],
                                                       recv_sem=recv_sems.at[k], device_id=peer, device_id_type=MESH))
        for cp in copies:
            cp.start()
        locs = [pltpu.make_async_copy(src, dst, loc_sems.at[k]) for k, (src, dst) in enumerate(local)]
        for cp in locs:
            cp.start()
        for cp in copies:
            cp.wait()
        for cp in locs:
            cp.wait()

    anyspec = pl.BlockSpec(memory_space=pl.ANY)
    res = pl.pallas_call(
        body, name=name, in_specs=[anyspec] * (n_in + n_ip), out_specs=[anyspec] * (n_ip + n_out),
        out_shape=[jax.ShapeDtypeStruct(a.shape, a.dtype) for a in inplace] + list(out_shapes),
        input_output_aliases={n_in + i: i for i in range(n_ip)},
        scratch_shapes=[pltpu.SemaphoreType.DMA((n_remote,)), pltpu.SemaphoreType.DMA((n_remote,)),
                        pltpu.SemaphoreType.DMA((max(n_local, 1),))],
        )(*ins, *inplace)
    return list(res[:n_ip]), list(res[n_ip:])


HBM_SPEC = pl.BlockSpec(memory_space=pltpu.HBM)
SEM_SPEC = pl.BlockSpec(memory_space=pltpu.SEMAPHORE)


def _flip_copies(plan, refs, send_sems, recv_sems):
    x, y, c = lax.axis_index("x"), lax.axis_index("y"), lax.axis_index("c")
    copies = []
    for k, ((dx, dy, dc), src, dst) in enumerate(plan(refs, (x, y, c))):
        peer = (1 - x if dx else x, 1 - y if dy else y, 1 - c if dc else c)
        copies.append(pltpu.make_async_remote_copy(src_ref=src, dst_ref=dst, send_sem=send_sems.at[k],
                                                   recv_sem=recv_sems.at[k], device_id=peer, device_id_type=MESH))
    return copies


def xor_start(name, arrays, plan, n_remote, after):
    n = len(arrays)

    def body(*refs):
        for cp in _flip_copies(plan, refs[:n], refs[n + 1], refs[n + 2]):
            cp.start()
        refs[-1][...] = jnp.zeros(refs[-1].shape, F32)

    res = pl.pallas_call(
        body, name=name, in_specs=[HBM_SPEC] * n + [pl.BlockSpec(memory_space=pl.ANY)],
        out_shape=(pltpu.SemaphoreType.DMA((n_remote,)), pltpu.SemaphoreType.DMA((n_remote,)),
                   *[pltpu.HBM(a.shape, a.dtype) for a in arrays], jax.ShapeDtypeStruct((8, 128), F32)),
        out_specs=(SEM_SPEC, SEM_SPEC, *([HBM_SPEC] * n), pl.BlockSpec(memory_space=pltpu.VMEM)),
        input_output_aliases={i: 2 + i for i in range(n)},
        compiler_params=pltpu.CompilerParams(has_side_effects=pltpu.SideEffectType.DATAFLOW_SIDE_EFFECTING),
    )(*[pltpu.with_memory_space_constraint(a, pltpu.HBM) for a in arrays], after)
    return res[0], res[1], list(res[2:2 + n]), res[2 + n]


def xor_wait(name, send_sems, recv_sems, arrays, plan, after):
    n = len(arrays)

    def body(*refs):
        for cp in _flip_copies(plan, refs[:n], refs[n], refs[n + 1]):
            cp.wait_send()
            cp.wait_recv()

    return list(pl.pallas_call(
        body, name=name, in_specs=[HBM_SPEC] * n + [SEM_SPEC, SEM_SPEC, pl.BlockSpec(memory_space=pl.ANY)],
        out_shape=[pltpu.HBM(a.shape, a.dtype) for a in arrays], out_specs=[HBM_SPEC] * n,
        input_output_aliases={i: i for i in range(n)},
        compiler_params=pltpu.CompilerParams(has_side_effects=pltpu.SideEffectType.DATAFLOW_SIDE_EFFECTING),
    )(*arrays, send_sems, recv_sems, after))


class WSpec:
    def __init__(self, name, full, sa, ha, group, layer=None, lead=False):
        self.name, self.full, self.sa, self.ha, self.group, self.layer, self.lead = name, full, sa, ha, group, layer, lead
        self.ws = 1 if lead else full[sa] // 4
        self.wh = full[ha] // 2

    def shard_shape(self):
        if self.lead:
            return tuple(n for a, n in enumerate(self.full) if a != self.sa)
        return tuple(self.ws if a == self.sa else n for a, n in enumerate(self.full))

    def half_full_shape(self):
        return tuple(self.wh if a == self.ha else n for a, n in enumerate(self.full))

    def shard_half_shape(self):
        s = list(self.half_full_shape())
        if self.lead:
            del s[self.sa]
        else:
            s[self.sa] = self.ws
        return tuple(s)

    def full_view(self, ref, q=None, h=None):
        idx = []
        for a in range(len(self.full)):
            if a == self.sa and q is not None:
                idx.append(q if self.lead else pl.ds(pl.multiple_of(q * self.ws, self.ws), self.ws))
            elif a == self.ha and h is not None:
                idx.append(pl.ds(pl.multiple_of(h * self.wh, self.wh), self.wh))
            else:
                idx.append(slice(None))
        return ref.at[tuple(idx)]

    def shard_view(self, ref, h):
        idx = [] if self.layer is None else [self.layer]
        for a in range(len(self.full)):
            if self.lead and a == self.sa:
                continue
            idx.append(pl.ds(pl.multiple_of(h * self.wh, self.wh), self.wh) if a == self.ha else slice(None))
        return ref.at[tuple(idx)]

    def rows_cols(self, shard, half):
        rows, cols = self.full[-2:]
        if shard and not self.lead:
            rows, cols = (rows // 4, cols) if self.sa == 0 else (rows, cols // 4)
        if half:
            rows, cols = (rows // 2, cols) if self.ha == len(self.full) - 2 else (rows, cols // 2)
        return rows, cols

    def spec(self, tr, cw, nr, shard=False, half=False, has_lead=False, stacked=False):
        two_d = len(self.full) == 2
        shard_on_cols = two_d and self.sa == 1
        half_on_cols = two_d and self.ha == 1
        layer = self.layer

        def index(*args):
            pref = args[-1]
            i = args[-2]
            r, cblk, pre = i, 0, ()
            if shard:
                if self.lead:
                    pre = (pref[0],)
                elif shard_on_cols:
                    cblk = pref[0]
                else:
                    r = pref[0] * nr + i
            elif has_lead:
                pre = (args[0],)
            if half:
                if half_on_cols:
                    cblk = pref[1]
                else:
                    r = pref[1] * nr + i
            if stacked:
                pre = (layer,) + pre
            return pre + (r, cblk)

        n_pre = int(stacked) + int(self.lead and (shard or has_lead))
        return pl.BlockSpec((None,) * n_pre + (tr, cw), index)


WSPECS = [
    WSpec("cv_w_pw1", (D, 2 * D), 1, 0, 0),
    WSpec("cv_w_pw2", (D, D), 0, 1, 1),
    WSpec("gdn_w_in", (4, D, (4 * D + 2 * H) // 4), 0, 1, 2, lead=True),
    WSpec("gdn_w_out", (D, D), 0, 1, 3),
    WSpec("mlp_w1_0", (D, DFF), 1, 0, 4, layer=0),
    WSpec("mlp_w1_1", (D, DFF), 1, 0, 4, layer=1),
    WSpec("mlp_w2_0", (DFF, D), 0, 1, 5, layer=0),
    WSpec("mlp_w2_1", (DFF, D), 0, 1, 5, layer=1),
]
FLIPS = [(1, 0, 0), (0, 1, 0), (1, 1, 0)]
SIB = (0, 0, 1)


def _chip(x, y):
    return 2 * x + y


def _prefetch_call(name, body, grid, in_specs, out_specs, out_shape, pref, args, aliases=None):
    return pl.pallas_call(
        body, name=name, out_shape=out_shape, input_output_aliases=aliases or {},
        grid_spec=pltpu.PrefetchScalarGridSpec(num_scalar_prefetch=1, grid=grid, in_specs=in_specs, out_specs=out_specs),
        compiler_params=_cp(*(("arbitrary",) * len(grid))))(pref, *args)


def place_shard(ws, shard, pref):
    rows, cols = ws.rows_cols(True, False)
    tr = _ew_rows(rows, cols)
    nr = rows // tr
    stacked = ws.layer is not None
    layer = ws.layer

    def body(_, s_ref, o_ref):
        o_ref[...] = s_ref[...].astype(BF16)

    in_spec = pl.BlockSpec(((None,) if stacked else ()) + (tr, cols),
                           (lambda i, p: (layer, i, 0)) if stacked else (lambda i, p: (i, 0)))
    return _prefetch_call("place_" + ws.name, body, (nr,), [in_spec], ws.spec(tr, cols, nr, shard=True),
                          jax.ShapeDtypeStruct(ws.full, BF16), pref, [shard])


FIRST = [0]
LAYER0 = [1, 4, 6]
LAYER1 = [2, 3, 5, 7]


def _plan_gather_chips(sel):
    def plan(refs, pos):
        x, y, c = pos
        remote = []
        for j, i in enumerate(sel):
            mine = WSPECS[i].full_view(refs[j], _chip(x, y), c)
            remote += [(f, mine, mine) for f in FLIPS]
        return remote
    return plan


def gather_cores(tag, sel, nat):
    def plan(in_refs, ip_refs, out_refs, pos):
        x, y, c = pos
        remote = []
        for j, i in enumerate(sel):
            for (dx, dy, _) in FLIPS:
                got = WSPECS[i].full_view(ip_refs[j], _chip(1 - x if dx else x, 1 - y if dy else y), c)
                remote.append((SIB, got, got))
        return remote, []

    return xor_exchange("gather_cores" + tag, [], nat, [], plan, 3 * len(sel))[0]


def gather_start(tag, sel, placed, after):
    plan = _plan_gather_chips(sel)
    send, recv, arrays, token = xor_start("gather_chips%s_start" % tag, [placed[i] for i in sel], plan, 3 * len(sel), after)
    return (tag, sel, plan, send, recv, arrays), token


def gather_wait(state, after):
    tag, sel, plan, send, recv, arrays = state
    return gather_cores(tag, sel, xor_wait("gather_chips%s_wait" % tag, send, recv, arrays, plan, after))


def reduce_start(tag, sel, grads, pref):
    plan = _plan_reduce_chips(sel)
    sums = chip_sums(tag, sel, grads, pref)
    send, recv, arrays, token = xor_start("reduce_chips%s_start" % tag,
                                          sums + [lax.empty(s.shape, s.dtype) for s in _parts_shapes(sel)], plan,
                                          3 * len(sel), pref)
    return (tag, sel, plan, send, recv, arrays), token


def reduce_wait(state, after):
    tag, sel, plan, send, recv, arrays = state
    arrays = xor_wait("reduce_chips%s_wait" % tag, send, recv, arrays, plan, after)
    return sel, arrays[:len(sel)], arrays[len(sel):]


def gather_first(sel, placed, wdw_shard, wcv_shard):
    plan_w = _plan_gather_chips(sel)

    def plan(in_refs, ip_refs, out_refs, pos):
        x, y, c = pos
        remote, local = plan_w(ip_refs, pos), []
        for j, width in enumerate((D // 4, 3 * D // 4)):
            dst = out_refs[j].at[:, pl.ds(pl.multiple_of(_chip(x, y) * width, 128), width)]
            local.append((in_refs[j], dst))
            remote += [(f, in_refs[j], dst) for f in FLIPS]
        return remote, local

    taps = [jax.ShapeDtypeStruct((KCV, D), F32), jax.ShapeDtypeStruct((KSC, 3 * D), F32)]
    nat, (wdw, wcv) = xor_exchange("gather_chips0", [wdw_shard, wcv_shard], [placed[i] for i in sel], taps, plan,
                                   3 * (len(sel) + 2), 2)
    return gather_cores("0", sel, nat), wdw, wcv


def half_add(ws, g, rsib, pref):
    rows, cols = ws.rows_cols(False, True)
    tr = _ew_rows(rows, cols)
    nr = rows // tr

    def body(_, a_ref, b_ref, o_ref):
        o_ref[...] = (a_ref[...] + b_ref[...]).astype(BF16)

    whole = ws.spec(tr, cols, nr, has_lead=ws.lead)
    return _prefetch_call("reduce_add_" + ws.name, body, (4, nr) if ws.lead else (nr,),
                          [ws.spec(tr, cols, nr, half=True, has_lead=ws.lead), whole], whole,
                          jax.ShapeDtypeStruct(ws.half_full_shape(), BF16), pref, [g, rsib])


def shard_sum(ws, s, parts, buf, pref):
    rows, cols = ws.rows_cols(True, True)
    tr = _ew_rows(rows, cols)
    nr = rows // tr
    stacked = ws.layer is not None

    def body(_, s_ref, p_ref, *rest):
        rest[-1][...] = ((s_ref[...].astype(F32) + p_ref[0].astype(F32)) + p_ref[1].astype(F32)) + p_ref[2].astype(F32)

    in_specs = [ws.spec(tr, cols, nr, shard=True, has_lead=ws.lead), pl.BlockSpec((3, tr, cols), lambda i, p: (0, i, 0))]
    args, aliases = [s, parts], {}
    if buf is not None:
        in_specs.append(pl.BlockSpec(memory_space=pl.ANY))
        args.append(buf)
        aliases = {3: 0}
    shape = ((2,) if stacked else ()) + ws.shard_shape()
    return _prefetch_call("reduce_sum_" + ws.name, body, (nr,), in_specs, ws.spec(tr, cols, nr, half=True, stacked=stacked),
                          jax.ShapeDtypeStruct(shape, F32), pref, args, aliases)


def chip_sums(tag, sel, grads, pref):
    def plan(in_refs, ip_refs, out_refs, pos):
        c = pos[2]
        return [(SIB, WSPECS[i].full_view(in_refs[j], None, 1 - c), out_refs[j]) for j, i in enumerate(sel)], []

    halves = [jax.ShapeDtypeStruct(WSPECS[i].half_full_shape(), F32) for i in sel]
    _, rsib = xor_exchange("reduce_cores" + tag, grads, [], halves, plan, len(sel))
    return [half_add(WSPECS[i], grads[j], rsib[j], pref) for j, i in enumerate(sel)]


def _plan_reduce_chips(sel):
    n = len(sel)

    def plan(refs, pos):
        x, y, c = pos
        remote = []
        for j, i in enumerate(sel):
            for s, (dx, dy, _) in enumerate(FLIPS):
                qq = _chip(1 - x if dx else x, 1 - y if dy else y)
                remote.append(((dx, dy, 0), WSPECS[i].full_view(refs[j], qq), refs[n + j].at[s]))
        return remote
    return plan


def _parts_shapes(sel):
    return [jax.ShapeDtypeStruct((3,) + WSPECS[i].shard_half_shape(), BF16) for i in sel]


def finish_reduce(sums, parts, pref):
    n = len(WSPECS)
    bufs = {}
    for i, ws in enumerate(WSPECS):
        bufs[ws.group] = shard_sum(ws, sums[i], parts[i], bufs.get(ws.group), pref)

    def plan3(in_refs, ip_refs, out_refs, pos):
        c = pos[2]
        remote = []
        for ws in WSPECS:
            mine = ws.shard_view(ip_refs[ws.group], c)
            remote.append((SIB, mine, mine))
        return remote, []

    return xor_exchange("reduce_swap", [], [bufs[g] for g in sorted(bufs)], [], plan3, n)[0]


def gather_small(buf):
    flips = [(dx, dy, dc) for dx in (0, 1) for dy in (0, 1) for dc in (0, 1)][1:]

    def plan(in_refs, ip_refs, out_refs, pos):
        x, y, c = pos
        me = 4 * x + 2 * y + c
        dst = out_refs[0].at[me]
        return [(f, in_refs[0], dst) for f in flips], [(in_refs[0], dst)]

    return xor_exchange("gather_small", [buf], [], [jax.ShapeDtypeStruct((8,) + buf.shape, F32)], plan, 7, 1)[1][0]


def _pad_rows(a, rows):
    return jnp.pad(a, ((0, rows - a.shape[0]), (0, 0)))


def _row1(v):
    v = v.reshape((1, -1))
    return jnp.pad(v, ((0, 0), (0, D - v.shape[1])))


def _rms_fwd(name, h, g, tm):
    return row_call(name, lambda hh, gg: ((f_rms(hh, gg),), ()), [(h, D, 0)], [g], [(D, BF16)], [], tm)[0]


def _res_rms(h, g):
    return (h, f_rms(h, g)), ()


def _rms_bwd_epi(dhn, h, dres, g):
    _, vjp = jax.vjp(f_rms, h, g)
    dh, dg = vjp(dhn)
    dh = dh + dres
    return (dh,), (dg, jnp.sum(dh, axis=0, keepdims=True))


def _mlp_bwd(tag, dh, h, g, w1, w2, hn, z1, token=None):
    dz1 = mm("mlp_down_dx" + tag, dh, w2, "NT", BF16,
             epi=lambda acc, z, *_: acc * (2.0 * jnp.maximum(z.astype(F32), 0.0)),
             epi_ins=[(z1, "tile")] + ([] if token is None else [(token, "whole")]))
    dw2 = mm_tn("mlp_down_dw" + tag, z1, dh, a_fn=f_relu2)
    dh_in, dg, colsum = mm("mlp_up_dx" + tag, dz1, w1, "NT", [F32], epi=_rms_bwd_epi,
                           epi_ins=[(h, "tile"), (dh, "tile"), (g, "row")], accs=[(1, D), (1, D)])
    dw1 = mm_tn("mlp_up_dw" + tag, hn, dz1)
    return dh_in, dg, colsum, dw1, dw2


def kernel(x, norm_mix_g, norm_ffn_g, final_norm_g, cv_w_pw1, cv_b_pw1, cv_w_dw, cv_b_dw, cv_ln_g, cv_ln_b, cv_w_pw2, cv_b_pw2, gdn_w_in, gdn_conv_w, gdn_a_log, gdn_dt_bias, gdn_norm_g, gdn_w_out, mlp_w1, mlp_w2, loss_target, m_norm_mix_g, m_norm_ffn_g, m_final_norm_g, m_cv_w_pw1, m_cv_b_pw1, m_cv_w_dw, m_cv_b_dw, m_cv_ln_g, m_cv_ln_b, m_cv_w_pw2, m_cv_b_pw2, m_gdn_w_in, m_gdn_conv_w, m_gdn_a_log, m_gdn_dt_bias, m_gdn_norm_g, m_gdn_w_out, m_mlp_w1, m_mlp_w2, v_norm_mix_g, v_norm_ffn_g, v_final_norm_g, v_cv_w_pw1, v_cv_b_pw1, v_cv_w_dw, v_cv_b_dw, v_cv_ln_g, v_cv_ln_b, v_cv_w_pw2, v_cv_b_pw2, v_gdn_w_in, v_gdn_conv_w, v_gdn_a_log, v_gdn_dt_bias, v_gdn_norm_g, v_gdn_w_out, v_mlp_w1, v_mlp_w2):
    env = dict(locals())
    bl, S, _ = x.shape
    T = bl * S
    tm = min(256, S)
    xf = x.reshape((T, D))
    tgt = loss_target.reshape((T, D))

    chip = 2 * lax.axis_index("x") + lax.axis_index("y")
    pref = jnp.stack([chip, lax.axis_index("c")]).astype(jnp.int32)
    big = [cv_w_pw1[0], cv_w_pw2[0], gdn_w_in[0], gdn_w_out[0], mlp_w1, mlp_w2]
    placed = [place_shard(ws, big[ws.group], pref) for ws in WSPECS]
    (w_pw1,), wdw, wcv = gather_first(FIRST, placed, cv_w_dw[0], gdn_conv_w[0])
    gather_a, token_a = gather_start("A", LAYER0, placed, wcv)
    gather_b, token_b = gather_start("B", LAYER1, placed, wcv)
    wdw_p, wcv_p = _pad_rows(wdw, HB_CV), _pad_rows(wcv, HB_SC)
    alog_p = jnp.pad(gdn_a_log, ((0, 0), (0, HD - H)))
    dtb_p = jnp.pad(gdn_dt_bias, ((0, 0), (0, HD - H)))
    g_mix0, g_mix1 = norm_mix_g[0:1] + (token_a[0, 0] + token_b[0, 0]), norm_mix_g[1:2]
    g_ffn0, g_ffn1 = norm_ffn_g[0:1], norm_ffn_g[1:2]
    g_fin = final_norm_g.reshape((1, D))

    hn0 = _rms_fwd("rms_mix0", xf, g_mix0, tm)
    u = mm("cv_pw1", hn0, w_pw1, "NN", F32, epi=lambda acc, b: acc + b, epi_ins=[(cv_b_pw1, "row")])
    dwc, s_act = dwconv_fwd("cv_dwconv", u, 2 * D, wdw_p, cv_b_dw, S, KCV, HB_CV, f_glu, (f_ln_silu, (cv_ln_g, cv_ln_b)), tm)
    w_pw2, w1_0, w2_0 = gather_wait(gather_a, dwc)
    h1, hnf0 = mm("cv_pw2", s_act, w_pw2, "NN", [F32, BF16], epi=lambda acc, b, r, g: _res_rms(acc + b + r, g),
                  epi_ins=[(cv_b_pw2, "row"), (xf, "tile"), (g_ffn0, "row")])
    z1_0 = mm("mlp_up0", hnf0, w1_0, "NN", BF16)
    h2, hn2 = mm("mlp_down0", z1_0, w2_0, "NN", [F32, BF16], a_fn=f_relu2, epi=lambda acc, r, g: _res_rms(acc + r, g),
                 epi_ins=[(h1, "tile"), (g_mix1, "row")])

    w_in_sm, w_out, w1_1, w2_1 = gather_wait(gather_b, h2)
    w_in = jnp.transpose(w_in_sm, (1, 0, 2)).reshape((D, 4 * D + 2 * H))
    w_qkv, w_z = w_in[:, :3 * D], w_in[:, 3 * D:4 * D]
    w_qkvz = w_in[:, :4 * D]
    w_ab = jnp.pad(w_in[:, 4 * D:], ((0, 0), (0, HD - 2 * H)))
    pqkvz = mm("gdn_in", hn2, w_qkvz, "NN", F32)
    pab = mm("gdn_in_ab", hn2, w_ab, "NN", F32)
    cpre = dwconv_fwd("gdn_conv", pqkvz, 3 * D, wcv_p, None, S, KSC, HB_SC, lambda v: v, None, tm)[0]
    gu, gw, gqg, gkg, gqk, gt, geg = gdn_prep_fwd(cpre, pab, alog_p, dtb_p)
    o, sall = gdn_scan_fwd(gu, gw, gqg, gkg, gqk, geg, S)
    on = row_call("gdn_post", lambda oo, zz, ng: ((f_post(oo, zz, ng),), ()), [(o, D, 0), (pqkvz, D, 3)],
                  [gdn_norm_g], [(D, BF16)], [], tm)[0]
    h3, hnf1 = mm("gdn_out", on, w_out, "NN", [F32, BF16], epi=lambda acc, r, g: _res_rms(acc + r, g),
                  epi_ins=[(h2, "tile"), (g_ffn1, "row")])
    z1_1 = mm("mlp_up1", hnf1, w1_1, "NN", BF16)

    def head(acc, res, tt, gg):
        def loss_of(h_, g_):
            return 0.5 * jnp.sum(jnp.mean(jnp.square(f_rms(h_, g_) - tt), axis=-1))
        lv, (dh_, dg_) = jax.value_and_grad(loss_of, (0, 1))(acc + res, gg)
        return (dh_,), (dg_, jnp.full((1, D), lv, F32))

    dh4, dg_fin, loss_row = mm("mlp_down1", z1_1, w2_1, "NN", [F32], a_fn=f_relu2, epi=head,
                               epi_ins=[(h3, "tile"), (tgt, "tile"), (g_fin, "row")], accs=[(1, D), (1, D)])

    dh3, dg_ffn1, _, dw1_1, dw2_1 = _mlp_bwd("1", dh4, h3, g_ffn1, w1_1, w2_1, hnf1, z1_1)
    dw_out = mm_tn("gdn_out_dw", on, dh3)

    def post_bwd(don, oo, zz, ng):
        _, vjp = jax.vjp(f_post, oo, zz, ng)
        do_, dz_, dng_ = vjp(don)
        return (do_, dz_), (dng_,)

    do, dz, dng = mm("gdn_out_dx", dh3, w_out, "NT", [F32, F32], epi=post_bwd,
                     epi_ins=[(o, "tile"), (pqkvz, ("cols", 3)), (gdn_norm_g, "whole")], accs=[(1, HD)])
    du, dw, dqg, dkg, dqk, deg = gdn_scan_bwd(do, gu, gw, gqg, gkg, gqk, geg, sall, S)
    dcpre, dpab, dalog, ddtb = gdn_prep_bwd(cpre, pab, alog_p, dtb_p, gt, du, dw, dqg, dkg, dqk, deg)
    dqkv, dwcv, _ = dwconv_bwd("gdn_conv_bwd", dcpre, pqkvz, 3 * D, wcv_p, S, KSC, HB_SC, lambda v: v, None, tm)
    dhn2 = mm("gdn_in_dx_ab", dpab, w_ab, "NT", F32)
    dhn2 = mm("gdn_in_dx_z", dz, w_z, "NT", F32, epi=lambda acc, r: acc + r, epi_ins=[(dhn2, "tile")])
    dh2, dg_mix1, _ = mm("gdn_in_dx_qkv", dqkv, w_qkv, "NT", [F32],
                         epi=lambda acc, prev, hh, rr, gg: _rms_bwd_epi(acc + prev, hh, rr, gg),
                         epi_ins=[(dhn2, "tile"), (h2, "tile"), (dh3, "tile"), (g_mix1, "row")], accs=[(1, D), (1, D)])
    dw_in = jnp.concatenate([mm_tn("gdn_in_dw_qkv", hn2, dqkv), mm_tn("gdn_in_dw_z", hn2, dz),
                             mm_tn("gdn_in_dw_ab", hn2, dpab)[:, :2 * H]], axis=1)

    dw_in_sm = jnp.transpose(dw_in.reshape((D, 4, D + 4)), (1, 0, 2))
    reduce_b, rtoken_b = reduce_start("B", LAYER1, [dw_in_sm, dw_out, dw1_1, dw2_1], pref)

    dh1, dg_ffn0, db_pw2, dw1_0, dw2_0 = _mlp_bwd("0", dh2, h1, g_ffn0, w1_0, w2_0, hnf0, z1_0, rtoken_b)
    reduce_a, rtoken_a = reduce_start("A", LAYER0[1:], [dw1_0, dw2_0], pref)
    dw_pw2 = mm_tn("cv_pw2_dw", s_act, dh1)

    def ln_bwd(ds, xx, gg, bb, *_):
        _, vjp = jax.vjp(f_ln_silu, xx, gg, bb)
        dx_, dg_, db_ = vjp(ds)
        return (dx_,), (dg_, db_, jnp.sum(dx_, axis=0, keepdims=True))

    ddw, dln_g, dln_b, db_dw = mm("cv_pw2_dx", dh1, w_pw2, "NT", [F32], epi=ln_bwd,
                                  epi_ins=[(dwc, "tile"), (cv_ln_g, "row"), (cv_ln_b, "row"), (rtoken_a, "whole")],
                                  accs=[(1, D)] * 3)
    du_cv, dwdw, db_pw1 = dwconv_bwd("cv_dwconv_bwd", ddw, u, 2 * D, wdw_p, S, KCV, HB_CV, f_glu, _glu_bwd, tm)
    dw_pw1 = mm_tn("cv_pw1_dw", hn0, du_cv)
    grad_x, dg_mix0, _ = mm("cv_pw1_dx", du_cv, w_pw1, "NT", [F32], epi=_rms_bwd_epi,
                            epi_ins=[(xf, "tile"), (dh1, "tile"), (g_mix0, "row")], accs=[(1, D), (1, D)])

    last = FIRST + LAYER0[:1]
    sums_l = chip_sums("0", last, [dw_pw1, dw_pw2], pref)
    plan_l = _plan_reduce_chips(last)
    _, parts_l = xor_exchange("reduce_chips0", sums_l, [], _parts_shapes(last),
                              lambda ins_, ip_, outs_, pos: (plan_l(list(ins_) + list(outs_), pos), []), 3 * len(last))
    sums, parts = [None] * len(WSPECS), [None] * len(WSPECS)
    for sel, sums_s, parts_s in ((last, sums_l, parts_l), reduce_wait(reduce_a, grad_x), reduce_wait(reduce_b, grad_x)):
        for j, i in enumerate(sel):
            sums[i], parts[i] = sums_s[j], parts_s[j]
    g_pw1, g_pw2, g_in, g_out, g_w1, g_w2 = finish_reduce(sums, parts, pref)

    small = jnp.concatenate([
        dg_mix0, dg_mix1, dg_ffn0, dg_ffn1, dg_fin, db_pw1.reshape((2, D)), db_dw, dln_g, dln_b, db_pw2,
        _row1(dalog[:, :H]), _row1(ddtb[:, :H]), _row1(dng), loss_row, jnp.zeros((1, D), F32),
        dwdw, dwcv[:KSC].reshape((3 * KSC, D)), jnp.zeros((NSMALL - 48 - 3 * KSC, D), F32)], axis=0)
    small_all = gather_small(small)

    def pack(a, b, c_, d, e, f, g_, h_, i_, j_, k_):
        return jnp.concatenate([a, b, c_.reshape((1, D)), d.reshape((2, D)), e, f, g_, h_, _row1(i_), _row1(j_), _row1(k_),
                                jnp.zeros((2, D), F32)], axis=0)

    order = lambda p: (p + "norm_mix_g", p + "norm_ffn_g", p + "final_norm_g", p + "cv_b_pw1", p + "cv_b_dw", p + "cv_ln_g",
                       p + "cv_ln_b", p + "cv_b_pw2", p + "gdn_a_log", p + "gdn_dt_bias", p + "gdn_norm_g")
    w16, m16, v16 = (pack(*[env[nm] for nm in order(p)]) for p in ("", "m_", "v_"))

    def small_step(ga, ww, mm_, vv):
        gsum = ga[0]
        for dev in range(1, 8):
            gsum = gsum + ga[dev]
        delta, m2, v2 = f_adamw(ww, gsum[:16], mm_, vv)
        return gsum, delta, m2, v2

    def small_body(ga_ref, w_ref, m_ref, v_ref, g_out, d_out, m_out, v_out):
        gsum, delta, m2, v2 = small_step(ga_ref[...], w_ref[...], m_ref[...], v_ref[...])
        g_out[...] = gsum
        d_out[...] = delta
        m_out[...] = m2
        v_out[...] = v2

    vm = pl.BlockSpec(memory_space=pltpu.VMEM)
    sg, sd, sm, sv = pl.pallas_call(
        small_body, name="adamw_small", in_specs=[vm] * 4, out_specs=[vm] * 4,
        out_shape=[jax.ShapeDtypeStruct((NSMALL, D), F32)] + [jax.ShapeDtypeStruct((16, D), F32)] * 3)(small_all, w16, m16, v16)

    def unpack(b):
        return (b[0:2], b[2:4], b[4], b[5:7].reshape((1, 2 * D)), b[7:8], b[8:9], b[9:10], b[10:11],
                b[11:12, :H], b[12:13, :H], b[13:14, :HD])

    loss = sg[14, 0]
    g_dw = lax.dynamic_slice(sg[16:16 + KCV], (0, chip * (D // 4)), (KCV, D // 4))
    g_cv = lax.dynamic_slice(sg[48:48 + 3 * KSC].reshape((KSC, 3 * D)), (0, chip * (3 * D // 4)), (KSC, 3 * D // 4))

    def adamw(name, w, g, m, v):
        lead = w.shape[:-2]
        if len(lead) == 1 and lead[0] == 1:
            d, m2, v2 = ew_call(name, f_adamw, [w[0], g.reshape(w.shape[1:]), m[0], v[0]], 3)
            return g.reshape(w.shape), d[None], m2[None], v2[None]
        return (g.reshape(w.shape),) + tuple(ew_call(name, f_adamw, [w, g.reshape(w.shape), m, v], 3))

    res = {
        "cv_w_pw1": adamw("adamw_pw1", cv_w_pw1, g_pw1, m_cv_w_pw1, v_cv_w_pw1),
        "cv_w_dw": adamw("adamw_dw", cv_w_dw, g_dw, m_cv_w_dw, v_cv_w_dw),
        "cv_w_pw2": adamw("adamw_pw2", cv_w_pw2, g_pw2, m_cv_w_pw2, v_cv_w_pw2),
        "gdn_w_in": adamw("adamw_win", gdn_w_in, g_in, m_gdn_w_in, v_gdn_w_in),
        "gdn_conv_w": adamw("adamw_cvw", gdn_conv_w, g_cv, m_gdn_conv_w, v_gdn_conv_w),
        "gdn_w_out": adamw("adamw_wout", gdn_w_out, g_out, m_gdn_w_out, v_gdn_w_out),
        "mlp_w1": adamw("adamw_w1", mlp_w1, g_w1, m_mlp_w1, v_mlp_w1),
        "mlp_w2": adamw("adamw_w2", mlp_w2, g_w2, m_mlp_w2, v_mlp_w2),
    }
    names = ("norm_mix_g", "norm_ffn_g", "final_norm_g", "cv_b_pw1", "cv_b_dw", "cv_ln_g", "cv_ln_b", "cv_b_pw2",
             "gdn_a_log", "gdn_dt_bias", "gdn_norm_g")
    for nm, gg, dd, mm_, vv in zip(names, unpack(sg), unpack(sd), unpack(sm), unpack(sv)):
        res[nm] = (gg, dd, mm_, vv)
    weights = ("norm_mix_g", "norm_ffn_g", "final_norm_g", "cv_w_pw1", "cv_b_pw1", "cv_w_dw", "cv_b_dw", "cv_ln_g",
               "cv_ln_b", "cv_w_pw2", "cv_b_pw2", "gdn_w_in", "gdn_conv_w", "gdn_a_log", "gdn_dt_bias", "gdn_norm_g",
               "gdn_w_out", "mlp_w1", "mlp_w2")
    outs = [loss, grad_x.reshape(x.shape)]
    for kind in range(4):
        outs += [res[nm][kind] for nm in weights]
    return tuple(outs)
```
